```python
import math, functools
import jax, jax.numpy as jnp
from jax import lax
import numpy as np

D_MODEL = 1024
BATCH = 8
SEQ = 2048
DEPTH = 1
DEC_BATCH = 128
DEC_SEQ = 4
PAST_LEN = 2048
PAGE_SIZE = 128

D_INNER = 2 * D_MODEL
SSM_HEAD_DIM = 64
SSM_HEADS = D_INNER // SSM_HEAD_DIM
SSM_GROUPS = 4
SSM_STATE = 128
CONV_W = 4
CONV_DIM = D_INNER + 2 * SSM_GROUPS * SSM_STATE
SSD_CHUNK = 128
N_HEADS = 16
HEAD_DIM = 64
N_KV_HEADS = 4
Q_PER_KV = N_HEADS // N_KV_HEADS
IDX_HEADS = 8
IDX_DIM = 64
TOPK_MAX = 256
Q_BLOCK = 128
N_BUCKETS = 32
MAX_DISTANCE = 128
D_FF = 2816
ALPHA = (2 * DEPTH) ** 0.25
BETA = (8 * DEPTH) ** -0.25
LN_EPS = 1e-5
IN_SPLITS = (D_INNER, CONV_DIM, SSM_HEADS, N_HEADS * HEAD_DIM, N_KV_HEADS * HEAD_DIM,
             N_KV_HEADS * HEAD_DIM, IDX_HEADS * IDX_DIM, IDX_DIM, IDX_HEADS, 2 * D_MODEL)
N_IN = sum(IN_SPLITS)

kernel_name = 'hybrid_ssd_dsa_macaron_step'


def layer_norm(x, g, b):
    xf = x.astype(jnp.float32)
    mu = xf.mean(-1, keepdims=True)
    var = jnp.square(xf - mu).mean(-1, keepdims=True)
    return ((xf - mu) * lax.rsqrt(var + LN_EPS)).astype(x.dtype) * g + b


def swiglu(x, wi, wo):
    gate, up = jnp.split(x @ wi, 2, axis=-1)
    return (jax.nn.silu(gate) * up) @ wo


def causal_dwconv(u, prefix, w, b):
    up = jnp.concatenate([prefix.astype(u.dtype), u], axis=1)
    y = lax.conv_general_dilated(up, w[:, None, :].astype(up.dtype), (1,), 'VALID',
                                 dimension_numbers=('NWC', 'WIO', 'NWC'),
                                 feature_group_count=u.shape[-1])
    return jax.nn.silu(y + b), up[:, -(CONV_W - 1):]


def ssd_scan(x, dt, A, Bm, Cm, h0):
    b, l = x.shape[:2]
    q = math.gcd(l, SSD_CHUNK)
    nc = l // q
    r = SSM_HEADS // SSM_GROUPS
    x = x.reshape(b, nc, q, SSM_GROUPS, r, SSM_HEAD_DIM)
    dt = dt.reshape(b, nc, q, SSM_GROUPS, r)
    Bm = Bm.reshape(b, nc, q, SSM_GROUPS, SSM_STATE)
    Cm = Cm.reshape(b, nc, q, SSM_GROUPS, SSM_STATE)
    cum = jnp.cumsum(dt * A.reshape(SSM_GROUPS, r), axis=2)
    causal = jnp.tril(jnp.ones((q, q), dtype=bool))[:, :, None, None]
    seg = jnp.exp(jnp.where(causal, cum[:, :, :, None] - cum[:, :, None, :], -jnp.inf))
    cb = jnp.einsum('bctgn,bcsgn->bctsg', Cm, Bm)
    mix = cb[..., None] * seg * dt[:, :, None]
    y_diag = jnp.einsum('bctsgr,bcsgrp->bctgrp', mix, x)
    decay_end = jnp.exp(cum[:, :, -1:] - cum)
    states = jnp.einsum('bcsgrp,bcsgn->bcgrpn', x * (decay_end * dt)[..., None], Bm)
    chunk_decay = jnp.exp(cum[:, :, -1])

    def step(h, inp):
        dec, st = inp
        return dec[..., None, None] * h + st, h

    h_init = h0.astype(jnp.float32).reshape(b, SSM_GROUPS, r, SSM_HEAD_DIM, SSM_STATE)
    h_last, h_prev = lax.scan(step, h_init, (jnp.moveaxis(chunk_decay, 1, 0), jnp.moveaxis(states, 1, 0)))
    h_prev = jnp.moveaxis(h_prev, 0, 1)
    y_off = jnp.einsum('bctgn,bcgrpn->bctgrp', Cm, h_prev) * jnp.exp(cum)[..., None]
    y = (y_diag + y_off).reshape(b, l, SSM_HEADS, SSM_HEAD_DIM)
    return y, h_last.reshape(b, SSM_HEADS, SSM_HEAD_DIM, SSM_STATE)


def mamba2_branch(z, xbc, dt_raw, conv_prefix, h0, conv_w, conv_b, dt_bias, a_log, d_skip, norm_g):
    b, l = z.shape[:2]
    xbc, conv_state = causal_dwconv(xbc, conv_prefix, conv_w, conv_b)
    gn = SSM_GROUPS * SSM_STATE
    xs = xbc[..., :D_INNER].reshape(b, l, SSM_HEADS, SSM_HEAD_DIM)
    Bm = xbc[..., D_INNER:D_INNER + gn].reshape(b, l, SSM_GROUPS, SSM_STATE)
    Cm = xbc[..., D_INNER + gn:].reshape(b, l, SSM_GROUPS, SSM_STATE)
    dt = jax.nn.softplus((dt_raw + dt_bias).astype(jnp.float32))
    A = -jnp.exp(a_log.astype(jnp.float32))
    y, h = ssd_scan(xs, dt, A, Bm, Cm, h0)
    y = (y + d_skip[:, None] * xs).astype(z.dtype).reshape(b, l, D_INNER)
    u = (y * jax.nn.silu(z)).astype(jnp.float32).reshape(b, l, SSM_GROUPS, D_INNER // SSM_GROUPS)
    u = u * lax.rsqrt(jnp.mean(u * u, -1, keepdims=True) + LN_EPS)
    return u.reshape(b, l, D_INNER).astype(z.dtype) * norm_g, conv_state, h.astype(z.dtype)


def indexer_scores(qi, wi, ki):
    s = jnp.einsum('bthd,bsd->bths', qi, ki).astype(jnp.float32) * (IDX_DIM ** -0.5)
    return jnp.einsum('bths,bth->bts', jax.nn.relu(s), wi.astype(jnp.float32)) * (IDX_HEADS ** -0.5)


def t5_bucket(rel):
    n = jnp.maximum(rel, 0)
    max_exact = N_BUCKETS // 2
    nf = jnp.maximum(n, 1).astype(jnp.float32)
    large = max_exact + (jnp.log(nf / max_exact) / math.log(MAX_DISTANCE / max_exact)
                         * (N_BUCKETS - max_exact)).astype(jnp.int32)
    large = jnp.minimum(large, N_BUCKETS - 1)
    return jnp.where(n < max_exact, n, large)


def sparse_softmax(q, ksel, vsel, kpos, qpos, bias_table):
    b, t = q.shape[:2]
    kk = ksel.shape[2]
    qg = q.reshape(b, t, N_KV_HEADS, Q_PER_KV, HEAD_DIM)
    logits = jnp.einsum('btgqd,btkgd->btgqk', qg, ksel).astype(jnp.float32) * (HEAD_DIM ** -0.5)
    rel = qpos[..., None] - kpos
    bias = bias_table[t5_bucket(rel)].reshape(b, t, kk, N_KV_HEADS, Q_PER_KV).transpose(0, 1, 3, 4, 2)
    logits = jnp.where((rel >= 0)[:, :, None, None, :], logits + bias.astype(jnp.float32), -jnp.inf)
    p = jax.nn.softmax(logits, axis=-1).astype(vsel.dtype)
    out = jnp.einsum('btgqk,btkgd->btgqd', p, vsel)
    return out.reshape(b, t, N_HEADS * HEAD_DIM)


def gather_rows(a, idx):
    return jax.vmap(lambda ab, ib: ab[ib])(a, idx)


def prompt_attend(q, k, v, qi, ki, wi, bias_table):
    b, s = q.shape[:2]
    topk = min(TOPK_MAX, s // 4)
    qb = math.gcd(s, Q_BLOCK)
    nblk = s // qb
    pos = jnp.arange(s)

    def blocks(a):
        return jnp.moveaxis(a.reshape((b, nblk, qb) + a.shape[2:]), 1, 0)

    def one_block(args):
        q_b, qi_b, wi_b, qpos_b = args
        sc = indexer_scores(qi_b, wi_b, ki)
        sc = jnp.where(pos[None, None, :] <= qpos_b[None, :, None], sc, -jnp.inf)
        _, idx = lax.top_k(sc, topk)
        return sparse_softmax(q_b, gather_rows(k, idx), gather_rows(v, idx), idx,
                              jnp.broadcast_to(qpos_b, (b, qb)), bias_table)

    out = lax.map(one_block, (blocks(q), blocks(qi), blocks(wi), pos.reshape(nblk, qb)))
    return jnp.moveaxis(out, 0, 1).reshape(b, s, N_HEADS * HEAD_DIM)


def sample_attend(q, k, v, qi, ki, wi, bias_table, ck, cv, cki, page_table):
    db, t = q.shape[:2]
    page = ck.shape[1]
    past = page_table.shape[1] * page
    total = past + t
    topk = min(TOPK_MAX, total // 4)
    ki_all = jnp.concatenate([cki[page_table].reshape(db, past, IDX_DIM), ki.astype(cki.dtype)], axis=1)
    qpos = past + jnp.arange(t)
    sc = indexer_scores(qi, wi, ki_all)
    sc = jnp.where(jnp.arange(total)[None, None, :] <= qpos[None, :, None], sc, -jnp.inf)
    _, idx = lax.top_k(sc, topk)
    is_past = (idx < past)[..., None, None]
    pi = jnp.minimum(idx, past - 1)
    phys = jax.vmap(lambda pt, j: pt[j])(page_table, pi // page)
    slot = pi % page
    ni = jnp.clip(idx - past, 0, t - 1)
    ksel = jnp.where(is_past, ck[phys, slot], gather_rows(k, ni).astype(ck.dtype))
    vsel = jnp.where(is_past, cv[phys, slot], gather_rows(v, ni).astype(cv.dtype))
    return sparse_softmax(q, ksel, vsel, idx, jnp.broadcast_to(qpos, (db, t)), bias_table)


def trunk_layer(x, conv_prefix, h0, attend, w_in, conv_w, conv_b, dt_bias, a_log, d_skip, ssm_norm_g,
                w_branch_ssm, w_branch_attn, w_out, ffn1_wi, ffn1_wo, ffn2_wi, ffn2_wo,
                ln1_g, ln1_b, ln2_g, ln2_b, ln3_g, ln3_b):
    b, l = x.shape[:2]
    h = layer_norm(ALPHA * x + 0.5 * swiglu(x, ffn1_wi, ffn1_wo), ln1_g, ln1_b)
    offs = np.cumsum(IN_SPLITS)[:-1].tolist()
    z, xbc, dt_raw, q, k, v, qi, ki, wi, gates = jnp.split(h @ w_in, offs, axis=-1)
    y_ssm, conv_state, ssm_state = mamba2_branch(z, xbc, dt_raw, conv_prefix, h0, conv_w, conv_b,
                                                 dt_bias, a_log, d_skip, ssm_norm_g)
    q = q.reshape(b, l, N_HEADS, HEAD_DIM)
    k = k.reshape(b, l, N_KV_HEADS, HEAD_DIM)
    v = v.reshape(b, l, N_KV_HEADS, HEAD_DIM)
    qi = qi.reshape(b, l, IDX_HEADS, IDX_DIM)
    y_att = attend(q, k, v, qi, ki, wi)
    g_ssm, g_att = jnp.split(jax.nn.sigmoid(gates), 2, axis=-1)
    merged = g_ssm * (y_ssm @ w_branch_ssm) + g_att * (y_att @ w_branch_attn)
    h = layer_norm(ALPHA * h + merged @ w_out, ln2_g, ln2_b)
    h = layer_norm(ALPHA * h + 0.5 * swiglu(h, ffn2_wi, ffn2_wo), ln3_g, ln3_b)
    return h, k, v, ki, conv_state, ssm_state


def setup_inputs(seed: int = 0) -> dict:
    key = jax.random.key(seed)
    ks = jax.random.split(key, 32)
    n_pages = PAST_LEN // PAGE_SIZE
    n_used = DEC_BATCH * n_pages
    n_pool = n_used + max(1, n_used // 4)

    def nrm(k, shape, scale):
        return jax.random.normal(k, shape, jnp.float32) * scale

    x_prompt = nrm(ks[0], (BATCH, SEQ, D_MODEL), 1.0)
    x_sample = nrm(ks[1], (DEC_BATCH, DEC_SEQ, D_MODEL), 1.0)
    cache_k = nrm(ks[2], (DEPTH, n_pool, PAGE_SIZE, N_KV_HEADS, HEAD_DIM), 1.0)
    cache_v = nrm(ks[3], (DEPTH, n_pool, PAGE_SIZE, N_KV_HEADS, HEAD_DIM), BETA)
    cache_kidx = nrm(ks[4], (DEPTH, n_pool, PAGE_SIZE, IDX_DIM), 1.0)
    state_conv = nrm(ks[5], (DEPTH, DEC_BATCH, CONV_W - 1, CONV_DIM), 1.0)
    state_ssm = nrm(ks[6], (DEPTH, DEC_BATCH, SSM_HEADS, SSM_HEAD_DIM, SSM_STATE), 0.5)
    page_table = jax.random.permutation(ks[7], n_pool)[:n_used].reshape(DEC_BATCH, n_pages).astype(jnp.int32)
    bias_table = nrm(ks[8], (N_BUCKETS, N_HEADS), 0.3)
    w_in = nrm(ks[9], (DEPTH, D_MODEL, N_IN), D_MODEL ** -0.5)
    v_start = sum(IN_SPLITS[:5])
    w_in = w_in.at[:, :, v_start:v_start + N_KV_HEADS * HEAD_DIM].multiply(BETA)
    conv_w = nrm(ks[10], (DEPTH, CONV_W, CONV_DIM), CONV_W ** -0.5)
    conv_b = nrm(ks[11], (DEPTH, CONV_DIM), 0.01)
    dt0 = jnp.exp(jax.random.uniform(ks[12], (DEPTH, SSM_HEADS), jnp.float32, math.log(1e-3), math.log(1e-1)))
    dt_bias = dt0 + jnp.log(-jnp.expm1(-dt0))
    a_log = jnp.log(jax.random.uniform(ks[13], (DEPTH, SSM_HEADS), jnp.float32, 1.0, 16.0))
    d_skip = 1.0 + nrm(ks[14], (DEPTH, SSM_HEADS), 0.01)
    ssm_norm_g = 1.0 + nrm(ks[15], (DEPTH, D_INNER), 0.01)
    w_branch_ssm = nrm(ks[16], (DEPTH, D_INNER, D_MODEL), D_INNER ** -0.5)
    w_branch_attn = nrm(ks[17], (DEPTH, N_HEADS * HEAD_DIM, D_MODEL), (N_HEADS * HEAD_DIM) ** -0.5)
    w_out = nrm(ks[18], (DEPTH, D_MODEL, D_MODEL), BETA * D_MODEL ** -0.5)
    ffn1_wi = nrm(ks[19], (DEPTH, D_MODEL, 2 * D_FF), D_MODEL ** -0.5)
    ffn1_wo = nrm(ks[20], (DEPTH, D_FF, D_MODEL), BETA * D_FF ** -0.5)
    ffn2_wi = nrm(ks[21], (DEPTH, D_MODEL, 2 * D_FF), D_MODEL ** -0.5)
    ffn2_wo = nrm(ks[22], (DEPTH, D_FF, D_MODEL), BETA * D_FF ** -0.5)
    ln1_g = 1.0 + nrm(ks[23], (DEPTH, D_MODEL), 0.01)
    ln1_b = nrm(ks[24], (DEPTH, D_MODEL), 0.01)
    ln2_g = 1.0 + nrm(ks[25], (DEPTH, D_MODEL), 0.01)
    ln2_b = nrm(ks[26], (DEPTH, D_MODEL), 0.01)
    ln3_g = 1.0 + nrm(ks[27], (DEPTH, D_MODEL), 0.01)
    ln3_b = nrm(ks[28], (DEPTH, D_MODEL), 0.01)
    return {'x_prompt': x_prompt, 'x_sample': x_sample, 'cache_k': cache_k, 'cache_v': cache_v,
            'cache_kidx': cache_kidx, 'state_conv': state_conv, 'state_ssm': state_ssm,
            'page_table': page_table, 'bias_table': bias_table, 'w_in': w_in, 'conv_w': conv_w,
            'conv_b': conv_b, 'dt_bias': dt_bias, 'a_log': a_log, 'd_skip': d_skip,
            'ssm_norm_g': ssm_norm_g, 'w_branch_ssm': w_branch_ssm, 'w_branch_attn': w_branch_attn,
            'w_out': w_out, 'ffn1_wi': ffn1_wi, 'ffn1_wo': ffn1_wo, 'ffn2_wi': ffn2_wi, 'ffn2_wo': ffn2_wo,
            'ln1_g': ln1_g, 'ln1_b': ln1_b, 'ln2_g': ln2_g, 'ln2_b': ln2_b, 'ln3_g': ln3_g, 'ln3_b': ln3_b}


def reference(x_prompt, x_sample, cache_k, cache_v, cache_kidx, state_conv, state_ssm, page_table,
              bias_table, w_in, conv_w, conv_b, dt_bias, a_log, d_skip, ssm_norm_g, w_branch_ssm,
              w_branch_attn, w_out, ffn1_wi, ffn1_wo, ffn2_wi, ffn2_wo,
              ln1_g, ln1_b, ln2_g, ln2_b, ln3_g, ln3_b):
    yp, ys = x_prompt, x_sample
    outs_p, outs_s = [], []
    prompt_mix = functools.partial(prompt_attend, bias_table=bias_table)
    for i in range(DEPTH):
        lw = (w_in[i], conv_w[i], conv_b[i], dt_bias[i], a_log[i], d_skip[i], ssm_norm_g[i],
              w_branch_ssm[i], w_branch_attn[i], w_out[i], ffn1_wi[i], ffn1_wo[i], ffn2_wi[i], ffn2_wo[i],
              ln1_g[i], ln1_b[i], ln2_g[i], ln2_b[i], ln3_g[i], ln3_b[i])
        bp = yp.shape[0]
        conv0 = jnp.zeros((bp, CONV_W - 1, CONV_DIM), yp.dtype)
        h0 = jnp.zeros((bp, SSM_HEADS, SSM_HEAD_DIM, SSM_STATE), jnp.float32)
        yp, *sp = trunk_layer(yp, conv0, h0, prompt_mix, *lw)
        sample_mix = functools.partial(sample_attend, bias_table=bias_table, ck=cache_k[i], cv=cache_v[i],
                                       cki=cache_kidx[i], page_table=page_table)
        ys, *ss = trunk_layer(ys, state_conv[i], state_ssm[i], sample_mix, *lw)
        outs_p.append(sp)
        outs_s.append(ss)
    k_p, v_p, kidx_p, conv_p, ssm_p = [jnp.stack(a) for a in zip(*outs_p)]
    k_s, v_s, kidx_s, conv_s, ssm_s = [jnp.stack(a) for a in zip(*outs_s)]
    return (yp, ys, k_p, v_p, kidx_p, conv_p, ssm_p, k_s, v_s, kidx_s, conv_s, ssm_s)
```

```python
import functools
import math

import numpy as np
import jax
import jax.numpy as jnp
from jax import lax
from jax.experimental import pallas as pl
from jax.experimental.pallas import tpu as pltpu

F32 = jnp.float32
BF16 = jnp.bfloat16
I32 = jnp.int32

D_MODEL = 1024
D_INNER = 2 * D_MODEL
SSM_HEAD_DIM = 64
SSM_HEADS = D_INNER // SSM_HEAD_DIM
SSM_GROUPS = 4
SSM_STATE = 128
CONV_W = 4
CONV_DIM = D_INNER + 2 * SSM_GROUPS * SSM_STATE
SSD_CHUNK = 128
N_HEADS = 16
HEAD_DIM = 64
N_KV_HEADS = 4
Q_PER_KV = N_HEADS // N_KV_HEADS
IDX_HEADS = 8
IDX_DIM = 64
TOPK_MAX = 256
N_BUCKETS = 32
MAX_DISTANCE = 128
D_FF = 2816
DEPTH = 1
ALPHA = (2 * DEPTH) ** 0.25
LN_EPS = 1e-5
IN_SPLITS = (D_INNER, CONV_DIM, SSM_HEADS, N_HEADS * HEAD_DIM, N_KV_HEADS * HEAD_DIM,
             N_KV_HEADS * HEAD_DIM, IDX_HEADS * IDX_DIM, IDX_DIM, IDX_HEADS, 2 * D_MODEL)

LANES = 128
SUBLANES = 8
VMEM_LIMIT_BYTES = 56 * 1024 * 1024

GN = SSM_GROUPS * SSM_STATE
COL_Z = 0
COL_X = COL_Z + D_INNER
COL_G = COL_X + D_INNER
COL_BC = COL_G + 2 * D_MODEL
COL_Q = COL_BC + 2 * GN
COL_K = COL_Q + N_HEADS * HEAD_DIM
COL_V = COL_K + N_KV_HEADS * HEAD_DIM
COL_QI = COL_V + N_KV_HEADS * HEAD_DIM
COL_SM = COL_QI + IDX_HEADS * IDX_DIM
COL_KI = COL_SM + LANES
PROJ_TN = 512
PROJ_N = -(-(COL_KI + LANES) // PROJ_TN) * PROJ_TN
SM_WI = SSM_HEADS

INT_MIN = -(2 ** 31)
NEG_INF = float("-inf")


def _cparams(sem):
    return pltpu.CompilerParams(dimension_semantics=sem, vmem_limit_bytes=VMEM_LIMIT_BYTES)


def _layer_norm(r, g, b):
    mu = jnp.mean(r, axis=-1, keepdims=True)
    d = r - mu
    var = jnp.mean(d * d, axis=-1, keepdims=True)
    return d * lax.rsqrt(var + LN_EPS) * g + b


def _silu(x):
    return x * jax.nn.sigmoid(x)


def _softplus(x):
    return jnp.maximum(x, 0.0) + jnp.log1p(jnp.exp(-jnp.abs(x)))


def _dot(a, b):
    return jnp.dot(a, b, preferred_element_type=F32)


def _dot_nt(a, b):
    return lax.dot_general(a, b, (((1,), (1,)), ((), ())), preferred_element_type=F32)


def _dot_tn(a, b):
    return lax.dot_general(a, b, (((0,), (0,)), ((), ())), preferred_element_type=F32)


def _split3(v):
    hi = v.astype(BF16)
    r1 = v - hi.astype(F32)
    mid = r1.astype(BF16)
    lo = (r1 - mid.astype(F32)).astype(BF16)
    return hi, mid, lo


def _dot_exact_sel(v, sel_bf16):
    hi, mid, lo = _split3(v)
    return (_dot(hi, sel_bf16) + _dot(mid, sel_bf16)) + _dot(lo, sel_bf16)


def _dot_nt_x3(a, b):
    ah = a.astype(BF16)
    al = (a - ah.astype(F32)).astype(BF16)
    bh = b.astype(BF16)
    bl = (b - bh.astype(F32)).astype(BF16)
    return (_dot_nt(ah, bl) + _dot_nt(al, bh)) + _dot_nt(ah, bh)


def _ffn_ln_body(x_ref, wg_ref, wu_ref, wo_ref, g_ref, b_ref, *rest, emit_bf16):
    if emit_bf16:
        o_ref, ob_ref, acc_ref, xb_ref = rest
    else:
        o_ref, acc_ref, xb_ref = rest
    j = pl.program_id(1)

    @pl.when(j == 0)
    def _():
        xb_ref[...] = x_ref[...].astype(BF16)
        acc_ref[...] = jnp.zeros_like(acc_ref)

    xb = xb_ref[...]
    gate = _dot(xb, wg_ref[...])
    up = _dot(xb, wu_ref[...])
    act = (_silu(gate) * up).astype(BF16)
    acc_ref[...] += _dot(act, wo_ref[...])

    @pl.when(j == pl.num_programs(1) - 1)
    def _():
        y = _layer_norm(ALPHA * x_ref[...] + 0.5 * acc_ref[...], g_ref[...], b_ref[...])
        o_ref[...] = y
        if emit_bf16:
            ob_ref[...] = y.astype(BF16)


def ffn_ln(x, wi_bf, wo_bf, g, b, *, emit_bf16, tm=512, tf=1408):
    m, d = x.shape
    dff = wo_bf.shape[0]
    tm = min(tm, m)
    nf = dff // tf
    assert m % tm == 0 and dff % tf == 0
    out_shape = [jax.ShapeDtypeStruct((m, d), F32)]
    out_specs = [pl.BlockSpec((tm, d), lambda i, j: (i, 0))]
    if emit_bf16:
        out_shape.append(jax.ShapeDtypeStruct((m, d), BF16))
        out_specs.append(pl.BlockSpec((tm, d), lambda i, j: (i, 0)))
    res = pl.pallas_call(
        functools.partial(_ffn_ln_body, emit_bf16=emit_bf16),
        grid=(m // tm, nf),
        in_specs=[
            pl.BlockSpec((tm, d), lambda i, j: (i, 0)),
            pl.BlockSpec((d, tf), lambda i, j: (0, j)),
            pl.BlockSpec((d, tf), lambda i, j: (0, j + nf)),
            pl.BlockSpec((tf, d), lambda i, j: (j, 0)),
            pl.BlockSpec((1, d), lambda i, j: (0, 0)),
            pl.BlockSpec((1, d), lambda i, j: (0, 0)),
        ],
        out_specs=out_specs,
        out_shape=out_shape,
        scratch_shapes=[pltpu.VMEM((tm, d), F32), pltpu.VMEM((tm, d), BF16)],
        compiler_params=_cparams(("parallel", "arbitrary")),
        name="ffn_ln",
    )(x, wi_bf, wi_bf, wo_bf, g, b)
    return res if emit_bf16 else res[0]


def _matmul_body(x_ref, w_ref, o_ref):
    o_ref[...] = _dot(x_ref[...], w_ref[...])


def matmul(x_bf, w_bf, *, tm, tn):
    m, k = x_bf.shape
    n = w_bf.shape[1]
    tm = min(tm, m)
    assert m % tm == 0 and n % tn == 0
    return pl.pallas_call(
        _matmul_body,
        grid=(m // tm, n // tn),
        in_specs=[pl.BlockSpec((tm, k), lambda i, j: (i, 0)),
                  pl.BlockSpec((k, tn), lambda i, j: (0, j))],
        out_specs=pl.BlockSpec((tm, tn), lambda i, j: (i, j)),
        out_shape=jax.ShapeDtypeStruct((m, n), F32),
        compiler_params=_cparams(("parallel", "arbitrary")),
        name="in_proj",
    )(x_bf, w_bf)


def pack_w_in(w_in):
    offs = np.cumsum((0,) + IN_SPLITS)
    z, xbc, dt, q, k, v, qi, ki, wi, gates = [w_in[:, offs[i]:offs[i + 1]] for i in range(len(IN_SPLITS))]
    d = w_in.shape[0]
    zeros = lambda n: jnp.zeros((d, n), w_in.dtype)
    small = jnp.concatenate([dt, wi, zeros(LANES - SSM_HEADS - IDX_HEADS)], axis=1)
    kib = jnp.concatenate([ki, zeros(LANES - IDX_DIM)], axis=1)
    packed = jnp.concatenate([z, xbc[:, :D_INNER], gates, xbc[:, D_INNER:], q, k, v, qi, small, kib,
                              zeros(PROJ_N - COL_KI - LANES)], axis=1)
    return packed.astype(BF16)


def _ssd_prompt_body(z_ref, xr_ref, bc_ref, sm_ref, cwx_ref, cwbc_ref, cbx_ref, cbbc_ref,
                     dtb_ref, alog_ref, dskip_ref, ng_ref,
                     y_ref, cs_ref, hs_ref,
                     xbuf, bcbuf, h_scr, cum_b, w_b, xc_scr, bcc_scr, y_scr, cb_scr, cum_t, dt_t):
    c = pl.program_id(1)
    nc = pl.num_programs(1)
    L = SSD_CHUNK
    halo = SUBLANES
    n_pairs = SSM_HEADS // 2
    pairs_per_group = n_pairs // SSM_GROUPS

    @pl.when(c == 0)
    def _():
        xbuf[0:halo, :] = jnp.zeros((halo, D_INNER), F32)
        bcbuf[0:halo, :] = jnp.zeros((halo, 2 * GN), F32)
        h_scr[...] = jnp.zeros_like(h_scr)

    xbuf[halo:halo + L, :] = xr_ref[...]
    bcbuf[halo:halo + L, :] = bc_ref[...]

    def conv(buf, w_ref, b_ref):
        acc = b_ref[...] + w_ref[0:1, :] * buf[pl.ds(halo - (CONV_W - 1), L), :]
        for k in range(1, CONV_W):
            acc = acc + w_ref[k:k + 1, :] * buf[pl.ds(halo - (CONV_W - 1) + k, L), :]
        return _silu(acc)

    xc_scr[...] = conv(xbuf, cwx_ref, cbx_ref)
    bcc_scr[...] = conv(bcbuf, cwbc_ref, cbbc_ref)
    xbuf[0:halo, :] = xbuf[L:L + halo, :]
    bcbuf[0:halo, :] = bcbuf[L:L + halo, :]

    dt = _softplus(sm_ref[...] + dtb_ref[...])
    a = -jnp.exp(alog_ref[...])
    row = lax.broadcasted_iota(I32, (L, L), 0)
    col = lax.broadcasted_iota(I32, (L, L), 1)
    tril = (row >= col).astype(F32)
    cum = jnp.dot(tril, dt * a, precision=lax.Precision.HIGHEST, preferred_element_type=F32)
    w = jnp.exp(cum[L - 1:L, :] - cum) * dt
    cum_t[...] = cum.T
    dt_t[...] = dt.T
    for h in range(SSM_HEADS):
        cum_b[h] = jnp.broadcast_to(cum[:, h:h + 1], (L, LANES))
        w_b[h] = jnp.broadcast_to(w[:, h:h + 1], (L, LANES))
    for g in range(SSM_GROUPS):
        bg = bcc_scr[:, g * SSM_STATE:(g + 1) * SSM_STATE].astype(BF16)
        cg = bcc_scr[:, GN + g * SSM_STATE:GN + (g + 1) * SSM_STATE].astype(BF16)
        cb_scr[g] = _dot_nt(cg, bg)

    causal = row >= col
    lane = lax.broadcasted_iota(I32, (L, LANES), 1)
    lo_half = lane < SSM_HEAD_DIM
    sub = lax.broadcasted_iota(I32, (2 * SSM_HEAD_DIM, SSM_STATE), 0)
    lo_rows = sub < SSM_HEAD_DIM

    def pair_body(i, carry):
        g = i // pairs_per_group
        h0 = 2 * i
        h1 = h0 + 1
        off = pl.multiple_of(i * LANES, LANES)
        boff = pl.multiple_of(g * SSM_STATE, SSM_STATE)
        xp = xc_scr[:, pl.ds(off, LANES)]
        bg = bcc_scr[:, pl.ds(boff, SSM_STATE)].astype(BF16)
        cg = bcc_scr[:, pl.ds(GN + boff, SSM_STATE)].astype(BF16)
        cb = cb_scr[g]
        c0 = cum_b[h0]
        c1 = cum_b[h1]

        def mix(ct, h):
            seg = jnp.exp(jnp.where(causal, ct - cum_t[pl.ds(h, 1), :], NEG_INF))
            return (cb * seg * dt_t[pl.ds(h, 1), :]).astype(BF16)

        zero = jnp.zeros_like(xp)
        yd = (_dot(mix(c0, h0), jnp.where(lo_half, xp, zero).astype(BF16))
              + _dot(mix(c1, h1), jnp.where(lo_half, zero, xp).astype(BF16)))
        hp = h_scr[i]
        yo = _dot_nt(cg, hp.astype(BF16)) * jnp.where(lo_half, jnp.exp(c0), jnp.exp(c1))
        y_scr[:, pl.ds(off, LANES)] = yd + yo + dskip_ref[:, pl.ds(off, LANES)] * xp
        xw = (xp * jnp.where(lo_half, w_b[h0], w_b[h1])).astype(BF16)
        st = _dot_tn(xw, bg)
        dec = jnp.where(lo_rows, jnp.exp(c0[L - 1:L, :]), jnp.exp(c1[L - 1:L, :]))
        h_scr[i] = dec * hp + st
        return carry

    lax.fori_loop(0, n_pairs, pair_body, 0)

    gw = D_INNER // SSM_GROUPS
    for g in range(SSM_GROUPS):
        sl = slice(g * gw, (g + 1) * gw)
        u = y_scr[:, sl] * _silu(z_ref[:, sl])
        ms = jnp.mean(u * u, axis=-1, keepdims=True)
        y_ref[:, sl] = (u * lax.rsqrt(ms + LN_EPS) * ng_ref[:, sl]).astype(y_ref.dtype)

    @pl.when(c == nc - 1)
    def _():
        cs_ref[0, :, 0:D_INNER] = xr_ref[L - (CONV_W - 1):L, :]
        cs_ref[0, :, D_INNER:CONV_DIM] = bc_ref[L - (CONV_W - 1):L, :]
        for i in range(n_pairs):
            hs_ref[0, i * LANES:(i + 1) * LANES, :] = h_scr[i]


def ssd_prompt(proj, batch, seq, conv_w, conv_b, dtb_pad, alog_pad, dskip_e, norm_g):
    L = SSD_CHUNK
    nc = seq // L
    assert seq % L == 0
    blk = lambda width, colblk: pl.BlockSpec((L, width), lambda b, c: (b * nc + c, colblk))
    full = lambda shape: pl.BlockSpec(shape, lambda b, c: tuple(0 for _ in shape))
    n_pairs = SSM_HEADS // 2
    return pl.pallas_call(
        _ssd_prompt_body,
        grid=(batch, nc),
        in_specs=[
            blk(D_INNER, COL_Z // D_INNER),
            blk(D_INNER, COL_X // D_INNER),
            blk(2 * GN, COL_BC // (2 * GN)),
            blk(LANES, COL_SM // LANES),
            full((CONV_W, D_INNER)), full((CONV_W, 2 * GN)), full((1, D_INNER)), full((1, 2 * GN)),
            full((1, LANES)), full((1, LANES)), full((1, D_INNER)), full((1, D_INNER)),
        ],
        out_specs=[
            pl.BlockSpec((L, D_INNER), lambda b, c: (b * nc + c, 0)),
            pl.BlockSpec((1, CONV_W - 1, CONV_DIM), lambda b, c: (b, 0, 0)),
            pl.BlockSpec((1, SSM_HEADS * SSM_HEAD_DIM, SSM_STATE), lambda b, c: (b, 0, 0)),
        ],
        out_shape=[
            jax.ShapeDtypeStruct((batch * seq, D_INNER), BF16),
            jax.ShapeDtypeStruct((batch, CONV_W - 1, CONV_DIM), F32),
            jax.ShapeDtypeStruct((batch, SSM_HEADS * SSM_HEAD_DIM, SSM_STATE), F32),
        ],
        scratch_shapes=[
            pltpu.VMEM((L + 2 * SUBLANES, D_INNER), F32),
            pltpu.VMEM((L + 2 * SUBLANES, 2 * GN), F32),
            pltpu.VMEM((n_pairs, 2 * SSM_HEAD_DIM, SSM_STATE), F32),
            pltpu.VMEM((SSM_HEADS, L, LANES), F32),
            pltpu.VMEM((SSM_HEADS, L, LANES), F32),
            pltpu.VMEM((L, D_INNER), F32),
            pltpu.VMEM((L, 2 * GN), F32),
            pltpu.VMEM((L, D_INNER), F32),
            pltpu.VMEM((SSM_GROUPS, L, L), F32),
            pltpu.VMEM((LANES, L), F32),
            pltpu.VMEM((LANES, L), F32),
        ],
        compiler_params=_cparams(("parallel", "arbitrary")),
        name="ssd_prompt",
    )(proj, proj, proj, proj, conv_w[:, :D_INNER], conv_w[:, D_INNER:], conv_b[:, :D_INNER], conv_b[:, D_INNER:],
      dtb_pad, alog_pad, dskip_e, norm_g)


def _ssd_sample_body(u_ref, z_ref, sm_ref, h0_ref, cw_ref, cb_ref, dtb_ref, alog_ref, dskip_ref, ng_ref, exp_ref,
                     y_ref, cs_ref, hn_ref, *, t_new):
    T = t_new
    n_pairs = SSM_HEADS // 2
    pairs_per_group = n_pairs // SSM_GROUPS
    acc = cb_ref[...] + cw_ref[0:1, :] * u_ref[0, 0:T, :]
    for k in range(1, CONV_W):
        acc = acc + cw_ref[k:k + 1, :] * u_ref[0, k:k + T, :]
    xbc = _silu(acc)
    cs_ref[0] = u_ref[0, T:T + CONV_W - 1, :]
    x = xbc[:, :D_INNER]

    dt = _softplus(sm_ref[0] + dtb_ref[...])
    da = dt * (-jnp.exp(alog_ref[...]))
    rows = [da[0:1, :]]
    for t in range(1, T):
        rows.append(rows[-1] + da[t:t + 1, :])
    cum = jnp.concatenate(rows, axis=0)
    both = _dot_exact_sel(jnp.concatenate([cum, dt], axis=0), exp_ref[...])
    cum_e = both[0:T, :]
    dt_e = both[T:2 * T, :]
    t_idx = lax.broadcasted_iota(I32, (T, GN), 0)

    gw = D_INNER // SSM_GROUPS
    y_parts = []
    for g in range(SSM_GROUPS):
        sl = slice(g * gw, (g + 1) * gw)
        bg = xbc[:, D_INNER + g * SSM_STATE:D_INNER + (g + 1) * SSM_STATE]
        cg = xbc[:, D_INNER + GN + g * SSM_STATE:D_INNER + GN + (g + 1) * SSM_STATE]
        cbg = _dot_nt(cg.astype(BF16), bg.astype(BF16))
        yg = dskip_ref[:, sl] * x[:, sl]
        for s in range(T):
            seg = jnp.exp(jnp.where(t_idx >= s, cum_e[:, sl] - cum_e[s:s + 1, sl], NEG_INF))
            coef = seg * dt_e[s:s + 1, sl] * jnp.broadcast_to(cbg[:, s:s + 1], (T, gw))
            yg = yg + coef * x[s:s + 1, sl]
        y_parts.append(yg)

    xw = x * dt_e * jnp.exp(cum_e[T - 1:T, :] - cum_e)
    ecum = jnp.exp(cum_e)
    sub = lax.broadcasted_iota(I32, (2 * SSM_HEAD_DIM, SSM_STATE), 0)
    lo_rows = sub < SSM_HEAD_DIM
    yo_parts = []
    for i in range(n_pairs):
        g = i // pairs_per_group
        sl = slice(i * LANES, (i + 1) * LANES)
        bg = xbc[:, D_INNER + g * SSM_STATE:D_INNER + (g + 1) * SSM_STATE].astype(BF16)
        cg = xbc[:, D_INNER + GN + g * SSM_STATE:D_INNER + GN + (g + 1) * SSM_STATE].astype(BF16)
        hp = h0_ref[0, i * LANES:(i + 1) * LANES, :]
        yo_parts.append(_dot_nt(cg, hp.astype(BF16)) * ecum[:, sl])
        st = _dot_tn(xw[:, sl].astype(BF16), bg)
        d0 = jnp.exp(cum[T - 1:T, 2 * i:2 * i + 1])
        d1 = jnp.exp(cum[T - 1:T, 2 * i + 1:2 * i + 2])
        dec = jnp.where(lo_rows, jnp.broadcast_to(d0, sub.shape), jnp.broadcast_to(d1, sub.shape))
        hn_ref[0, i * LANES:(i + 1) * LANES, :] = dec * hp + st

    for g in range(SSM_GROUPS):
        sl = slice(g * gw, (g + 1) * gw)
        yo = jnp.concatenate(yo_parts[g * pairs_per_group:(g + 1) * pairs_per_group], axis=1)
        u = (y_parts[g] + yo) * _silu(z_ref[0, :, sl])
        ms = jnp.mean(u * u, axis=-1, keepdims=True)
        y_ref[0, :, sl] = (u * lax.rsqrt(ms + LN_EPS) * ng_ref[:, sl]).astype(y_ref.dtype)


def ssd_sample(u_cat, proj3, h0, conv_w, conv_b, dtb_pad, alog_pad, dskip_e, norm_g, expand):
    nb, t_new, _ = proj3.shape
    full = lambda shape: pl.BlockSpec(shape, lambda b: tuple(0 for _ in shape))
    hp = SSM_HEADS * SSM_HEAD_DIM
    return pl.pallas_call(
        functools.partial(_ssd_sample_body, t_new=t_new),
        grid=(nb,),
        in_specs=[
            pl.BlockSpec((1, t_new + CONV_W - 1, CONV_DIM), lambda b: (b, 0, 0)),
            pl.BlockSpec((1, t_new, D_INNER), lambda b: (b, 0, COL_Z // D_INNER)),
            pl.BlockSpec((1, t_new, LANES), lambda b: (b, 0, COL_SM // LANES)),
            pl.BlockSpec((1, hp, SSM_STATE), lambda b: (b, 0, 0)),
            full((CONV_W, CONV_DIM)), full((1, CONV_DIM)), full((1, LANES)), full((1, LANES)),
            full((1, D_INNER)), full((1, D_INNER)), full((LANES, D_INNER)),
        ],
        out_specs=[
            pl.BlockSpec((1, t_new, D_INNER), lambda b: (b, 0, 0)),
            pl.BlockSpec((1, CONV_W - 1, CONV_DIM), lambda b: (b, 0, 0)),
            pl.BlockSpec((1, hp, SSM_STATE), lambda b: (b, 0, 0)),
        ],
        out_shape=[
            jax.ShapeDtypeStruct((nb, t_new, D_INNER), BF16),
            jax.ShapeDtypeStruct((nb, CONV_W - 1, CONV_DIM), F32),
            jax.ShapeDtypeStruct((nb, hp, SSM_STATE), F32),
        ],
        compiler_params=_cparams(("parallel",)),
        name="ssd_sample",
    )(u_cat, proj3, proj3, h0, conv_w, conv_b, dtb_pad, alog_pad, dskip_e, norm_g, expand)


def _sortable_key(score):
    score = jnp.where(score == 0.0, 0.0, score)
    bits = lax.bitcast_convert_type(score, I32)
    return bits ^ ((bits >> 31) & jnp.int32(0x7FFFFFFF))


def _kth_largest_key(key_ref, n_rows, k):
    def count_ge(cand):
        return jnp.sum((key_ref[0:n_rows, :] >= cand).astype(I32), axis=0, keepdims=True)

    r0 = jnp.where(count_ge(jnp.zeros((1, LANES), I32)) >= k, jnp.int32(0), jnp.int32(INT_MIN))
    r0 = jnp.broadcast_to(r0, (1, LANES))

    def body(it, r):
        cand = r + (jnp.int32(1) << (30 - it))
        return jnp.where(count_ge(cand) >= k, cand, r)

    return lax.fori_loop(0, 31, body, r0)


def _tie_exact_mask(key_ref, msk_ref, n_rows, k, valid_fn):
    thr = _kth_largest_key(key_ref, n_rows, k)
    n_gt = jnp.sum((key_ref[0:n_rows, :] > thr).astype(I32), axis=0, keepdims=True)
    need = (k - n_gt).astype(F32)
    run = jnp.zeros((1, LANES), F32)
    r0 = 0
    while r0 < n_rows:
        rows = min(LANES, n_rows - r0)
        kc = key_ref[r0:r0 + rows, :]
        eq = kc == thr
        eqf = eq.astype(F32)
        ri = lax.broadcasted_iota(I32, (rows, rows), 0)
        ci = lax.broadcasted_iota(I32, (rows, rows), 1)
        strict = (ri > ci).astype(BF16)
        before = _dot(strict, eqf.astype(BF16)) + run
        sel = (kc > thr) | (eq & (before < need))
        sel = sel & valid_fn(r0, rows)
        msk_ref[r0:r0 + rows, :] = jnp.where(sel, 0.0, NEG_INF)
        run = run + jnp.sum(eqf, axis=0, keepdims=True)
        r0 += rows


def _bias_tiles(bucket_ref, tab_ref, bias_scr, n_tiles):
    for d in range(n_tiles):
        bk = bucket_ref[d]
        for h in range(N_HEADS):
            acc = jnp.zeros(bk.shape, F32)
            for b in range(N_BUCKETS):
                acc = jnp.where(bk == b, tab_ref[b, h], acc)
            bias_scr[h, d] = acc


def _attn_prompt_body(tab_ref, q_ref, k_ref, v_ref, qi_ref, ki_ref, sm_ref, bucket_ref,
                      o_ref, key_scr, msk_scr, lg_scr, ot_scr, bias_scr, *, nk, topk):
    b = pl.program_id(0)
    j = pl.program_id(1)
    QB = LANES

    @pl.when((b == 0) & (j == 0))
    def _():
        _bias_tiles(bucket_ref, tab_ref, bias_scr, 2)

    ki = ki_ref[0:nk, 0:IDX_DIM]
    wt = sm_ref[...].T
    score = jnp.zeros((nk, QB), F32)
    for h in range(IDX_HEADS):
        s = _dot_nt_x3(ki, qi_ref[:, h * IDX_DIM:(h + 1) * IDX_DIM]) * (IDX_DIM ** -0.5)
        score = score + jnp.maximum(s, 0.0) * wt[SM_WI + h:SM_WI + h + 1, :]
    score = score * (IDX_HEADS ** -0.5)
    kpos = lax.broadcasted_iota(I32, (nk, QB), 0)
    qpos = j * QB + lax.broadcasted_iota(I32, (nk, QB), 1)
    score = jnp.where(kpos <= qpos, score, NEG_INF)
    key_scr[...] = _sortable_key(score)

    def valid(r0, rows):
        kp = r0 + lax.broadcasted_iota(I32, (rows, QB), 0)
        qp = j * QB + lax.broadcasted_iota(I32, (rows, QB), 1)
        return kp <= qp

    _tie_exact_mask(key_scr, msk_scr, nk, topk, valid)

    jm1 = jnp.maximum(j - 1, 0)
    off0 = pl.multiple_of(j * QB, QB)
    off1 = pl.multiple_of(jm1 * QB, QB)
    for g in range(N_KV_HEADS):
        kg = k_ref[0:nk, g * HEAD_DIM:(g + 1) * HEAD_DIM].astype(BF16)
        vg = v_ref[0:nk, g * HEAD_DIM:(g + 1) * HEAD_DIM].astype(BF16)
        k0 = k_ref[pl.ds(off0, QB), g * HEAD_DIM:(g + 1) * HEAD_DIM].astype(BF16)
        k1 = k_ref[pl.ds(off1, QB), g * HEAD_DIM:(g + 1) * HEAD_DIM].astype(BF16)
        for r in range(Q_PER_KV):
            h = g * Q_PER_KV + r
            qh = (q_ref[:, h * HEAD_DIM:(h + 1) * HEAD_DIM] * (HEAD_DIM ** -0.5)).astype(BF16)
            lg_scr[...] = _dot_nt(kg, qh) + tab_ref[N_BUCKETS - 1, h]

            @pl.when(j > 0)
            def _():
                lg_scr[pl.ds(off1, QB), :] = _dot_nt(k1, qh) + bias_scr[h, 1]

            lg_scr[pl.ds(off0, QB), :] = _dot_nt(k0, qh) + bias_scr[h, 0]
            lg = lg_scr[...] + msk_scr[...]
            m = jnp.max(lg, axis=0, keepdims=True)
            p = jnp.exp(lg - m)
            l = jnp.sum(p, axis=0, keepdims=True)
            ot_scr[h * HEAD_DIM:(h + 1) * HEAD_DIM, :] = _dot_tn(vg, p.astype(BF16)) * (1.0 / l)
    o_ref[...] = ot_scr[...].T.astype(o_ref.dtype)


def _t5_bucket_np(rel):
    n = np.maximum(rel, 0)
    max_exact = N_BUCKETS // 2
    nf = np.maximum(n, 1).astype(np.float32)
    large = max_exact + (np.log(nf / max_exact) / math.log(MAX_DISTANCE / max_exact)
                         * (N_BUCKETS - max_exact)).astype(np.int32)
    large = np.minimum(large, N_BUCKETS - 1)
    return np.where(n < max_exact, n, large).astype(np.int32)


def attn_prompt(proj, bias_table, batch, seq):
    QB = LANES
    nq = seq // QB
    assert seq % QB == 0
    topk = min(TOPK_MAX, seq // 4)
    nk = seq
    ts = np.arange(QB)[:, None]
    tq = np.arange(QB)[None, :]
    bucket = jnp.asarray(np.stack([_t5_bucket_np(tq - ts), _t5_bucket_np(QB + tq - ts)]))
    rowblk = lambda width, colblk: pl.BlockSpec((QB, width), lambda b, j: (b * nq + j, colblk))
    seqblk = lambda width, colblk: pl.BlockSpec((seq, width), lambda b, j: (b, colblk))
    kvw = N_KV_HEADS * HEAD_DIM
    return pl.pallas_call(
        functools.partial(_attn_prompt_body, nk=nk, topk=topk),
        grid=(batch, nq),
        in_specs=[
            pl.BlockSpec(memory_space=pltpu.SMEM),
            rowblk(N_HEADS * HEAD_DIM, COL_Q // (N_HEADS * HEAD_DIM)),
            seqblk(kvw, COL_K // kvw),
            seqblk(kvw, COL_V // kvw),
            rowblk(IDX_HEADS * IDX_DIM, COL_QI // (IDX_HEADS * IDX_DIM)),
            seqblk(LANES, COL_KI // LANES),
            rowblk(LANES, COL_SM // LANES),
            pl.BlockSpec((2, QB, QB), lambda b, j: (0, 0, 0)),
        ],
        out_specs=pl.BlockSpec((QB, N_HEADS * HEAD_DIM), lambda b, j: (b * nq + j, 0)),
        out_shape=jax.ShapeDtypeStruct((batch * seq, N_HEADS * HEAD_DIM), BF16),
        scratch_shapes=[
            pltpu.VMEM((nk, QB), I32),
            pltpu.VMEM((nk, QB), F32),
            pltpu.VMEM((nk, QB), F32),
            pltpu.VMEM((N_HEADS * HEAD_DIM, QB), F32),
            pltpu.VMEM((N_HEADS, 2, QB, QB), F32),
        ],
        compiler_params=_cparams(("arbitrary", "arbitrary")),
        name="attn_prompt",
    )(bias_table, proj, proj, proj, proj, proj, proj, bucket)


def _attn_sample_body(pt_ref, q_ref, qi_ref, w_ref, kn_ref, vn_ref, kin_ref, tabx_ref, bucket_ref, gsum_ref, xpand_ref,
                      *rest, n_pages, page, t_new, topk):
    ck = rest[0:n_pages]
    cv = rest[n_pages:2 * n_pages]
    cki = rest[2 * n_pages:3 * n_pages]
    o_ref, sc_scr, key_scr, msk_scr, bias_scr = rest[3 * n_pages:]
    past = n_pages * page
    tail = SUBLANES
    nk = past + tail
    n_qh = t_new * N_HEADS
    n_ih = t_new * IDX_HEADS

    @pl.when(pl.program_id(0) == 0)
    def _():
        bk = bucket_ref[...]
        acc = jnp.zeros(bk.shape, F32)
        for b in range(N_BUCKETS):
            acc = jnp.where(bk == b, tabx_ref[b:b + 1, :], acc)
        bias_scr[...] = acc

    qi = qi_ref[0]
    wrow = w_ref[0]
    gsum = gsum_ref[...]

    def idx_scores(kblk):
        s = jnp.maximum(_dot_nt_x3(kblk, qi) * (IDX_DIM ** -0.5), 0.0) * wrow
        return _dot_exact_sel(s, gsum) * (IDX_HEADS ** -0.5)

    for c in range(n_pages):
        sc_scr[c * page:(c + 1) * page, :] = idx_scores(cki[c][0])
    kin = jnp.concatenate([kin_ref[0], jnp.zeros((tail - t_new, IDX_DIM), F32)], axis=0)
    sc_scr[past:nk, :] = idx_scores(kin)

    def valid(r0, rows):
        kp = r0 + lax.broadcasted_iota(I32, (rows, LANES), 0)
        qp = past + lax.broadcasted_iota(I32, (rows, LANES), 1)
        return kp <= qp

    score = jnp.where(valid(0, nk), sc_scr[...], NEG_INF)
    key_scr[...] = _sortable_key(score)
    _tie_exact_mask(key_scr, msk_scr, nk, topk, valid)
    self_ = (msk_scr[...] == 0.0).astype(BF16)
    msk_scr[...] = jnp.where(_dot(self_, xpand_ref[...]) > 0.5, 0.0, NEG_INF)

    q = (q_ref[0] * (HEAD_DIM ** -0.5)).astype(BF16)
    lane = lax.broadcasted_iota(I32, (1, LANES), 1)
    grp_of_lane = (lane % N_HEADS) // Q_PER_KV
    kn = jnp.concatenate([kn_ref[0], jnp.zeros((tail - t_new, N_KV_HEADS * HEAD_DIM), F32)], axis=0)
    vn = jnp.concatenate([vn_ref[0], jnp.zeros((tail - t_new, N_KV_HEADS * HEAD_DIM), F32)], axis=0)

    def logits(kblk):
        out = None
        for g in range(N_KV_HEADS):
            lg = _dot_nt(kblk[:, g * HEAD_DIM:(g + 1) * HEAD_DIM].astype(BF16), q)
            lg = jnp.concatenate([lg, jnp.zeros((lg.shape[0], LANES - n_qh), F32)], axis=1)
            out = lg if out is None else jnp.where(grp_of_lane == g, lg, out)
        return out

    far = tabx_ref[N_BUCKETS - 1:N_BUCKETS, :]
    for c in range(n_pages - 1):
        sc_scr[c * page:(c + 1) * page, :] = logits(ck[c][0]) + far
    c = n_pages - 1
    sc_scr[c * page:(c + 1) * page, :] = logits(ck[c][0]) + bias_scr[0:page, :]
    sc_scr[past:nk, :] = logits(kn) + bias_scr[page:page + tail, :]
    lg = sc_scr[...] + msk_scr[...]
    m = jnp.max(lg, axis=0, keepdims=True)
    p = jnp.exp(lg - m)
    p = (p * (1.0 / jnp.sum(p, axis=0, keepdims=True))).astype(BF16)
    sub = lax.broadcasted_iota(I32, (LANES, 1), 0)
    grp_of_row = (sub % N_HEADS) // Q_PER_KV
    out = jnp.zeros((LANES, HEAD_DIM), F32)
    for g in range(N_KV_HEADS):
        sl = slice(g * HEAD_DIM, (g + 1) * HEAD_DIM)
        acc = _dot_tn(p[past:nk, :], vn[:, sl].astype(BF16))
        for c in range(n_pages):
            acc = acc + _dot_tn(p[c * page:(c + 1) * page, :], cv[c][0][:, sl].astype(BF16))
        out = jnp.where(grp_of_row == g, acc, out)
    o_ref[0] = out[0:n_qh, :].astype(o_ref.dtype)


def attn_sample(proj_s, cache_k, cache_v, cache_kidx, page_table, bias_table, t_new):
    nb, n_pages = page_table.shape
    n_pool, page = cache_k.shape[0], cache_k.shape[1]
    past = n_pages * page
    total = past + t_new
    topk = min(TOPK_MAX, total // 4)
    kvw = N_KV_HEADS * HEAD_DIM
    n_qh = t_new * N_HEADS
    n_ih = t_new * IDX_HEADS
    tail = SUBLANES
    assert t_new <= tail and n_qh <= LANES and past >= MAX_DISTANCE + t_new
    ck = cache_k.reshape(n_pool, page, kvw)
    cv = cache_v.reshape(n_pool, page, kvw)
    q3 = proj_s[:, COL_Q:COL_Q + N_HEADS * HEAD_DIM].reshape(nb, n_qh, HEAD_DIM)
    qi3 = proj_s[:, COL_QI:COL_QI + IDX_HEADS * IDX_DIM].reshape(nb, n_ih, IDX_DIM)
    w3 = proj_s[:, COL_SM + SM_WI:COL_SM + SM_WI + IDX_HEADS].reshape(nb, 1, n_ih)
    kn3 = proj_s[:, COL_K:COL_K + kvw].reshape(nb, t_new, kvw)
    vn3 = proj_s[:, COL_V:COL_V + kvw].reshape(nb, t_new, kvw)
    kin3 = proj_s[:, COL_KI:COL_KI + IDX_DIM].reshape(nb, t_new, IDX_DIM)
    lane = np.arange(LANES)
    tok = np.minimum(lane // N_HEADS, t_new - 1)
    tabx = jnp.tile(bias_table, (1, LANES // N_HEADS))
    kpos = np.concatenate([past - page + np.arange(page), past + np.arange(tail)])[:, None]
    bucket = jnp.asarray(_t5_bucket_np(past + tok[None, :] - kpos))
    gsum = np.zeros((n_ih, LANES), np.float32)
    gsum[np.arange(n_ih), np.arange(n_ih) // IDX_HEADS] = 1.0
    xpand = np.zeros((LANES, LANES), np.float32)
    xpand[np.arange(n_qh) // N_HEADS, np.arange(n_qh)] = 1.0
    per_seq = lambda shape: pl.BlockSpec((1,) + shape, lambda b, pt: (b, 0, 0))
    const = lambda shape: pl.BlockSpec(shape, lambda b, pt: tuple(0 for _ in shape))
    page_spec = lambda width, c: pl.BlockSpec((1, page, width), lambda b, pt, c=c: (pt[b, c], 0, 0))
    grid_spec = pltpu.PrefetchScalarGridSpec(
        num_scalar_prefetch=1,
        grid=(nb,),
        in_specs=[per_seq((n_qh, HEAD_DIM)), per_seq((n_ih, IDX_DIM)), per_seq((1, n_ih)),
                  per_seq((t_new, kvw)), per_seq((t_new, kvw)), per_seq((t_new, IDX_DIM)),
                  const((N_BUCKETS, LANES)), const((page + tail, LANES)), const((n_ih, LANES)), const((LANES, LANES))]
                 + [page_spec(kvw, c) for c in range(n_pages)]
                 + [page_spec(kvw, c) for c in range(n_pages)]
                 + [page_spec(IDX_DIM, c) for c in range(n_pages)],
        out_specs=pl.BlockSpec((1, n_qh, HEAD_DIM), lambda b, pt: (b, 0, 0)),
        scratch_shapes=[
            pltpu.VMEM((past + tail, LANES), F32),
            pltpu.VMEM((past + tail, LANES), I32),
            pltpu.VMEM((past + tail, LANES), F32),
            pltpu.VMEM((page + tail, LANES), F32),
        ],
    )
    out = pl.pallas_call(
        functools.partial(_attn_sample_body, n_pages=n_pages, page=page, t_new=t_new, topk=topk),
        grid_spec=grid_spec,
        out_shape=jax.ShapeDtypeStruct((nb, n_qh, HEAD_DIM), BF16),
        compiler_params=_cparams(("arbitrary",)),
        name="attn_sample",
    )(page_table, q3, qi3, w3, kn3, vn3, kin3, tabx, bucket, jnp.asarray(gsum, BF16), jnp.asarray(xpand, BF16),
      *([ck] * n_pages), *([cv] * n_pages), *([cache_kidx] * n_pages))
    return out.reshape(nb * t_new, N_HEADS * HEAD_DIM)


def _merge_body(ys_ref, ya_ref, g_ref, h_ref, wbs_ref, wba_ref, wo_ref, lg_ref, lb_ref, o_ref):
    a = _dot(ys_ref[...], wbs_ref[...])
    b = _dot(ya_ref[...], wba_ref[...])
    gates = jax.nn.sigmoid(g_ref[...])
    merged = gates[:, :D_MODEL] * a + gates[:, D_MODEL:] * b
    r = ALPHA * h_ref[...] + _dot(merged.astype(BF16), wo_ref[...])
    o_ref[...] = _layer_norm(r, lg_ref[...], lb_ref[...])


def merge_out(y_ssm, y_att, proj, h1, wbs, wba, wo, g, b, *, tm=512):
    m = h1.shape[0]
    tm = min(tm, m)
    assert m % tm == 0
    row = lambda width, colblk=0: pl.BlockSpec((tm, width), lambda i: (i, colblk))
    full = lambda shape: pl.BlockSpec(shape, lambda i: tuple(0 for _ in shape))
    return pl.pallas_call(
        _merge_body,
        grid=(m // tm,),
        in_specs=[row(D_INNER), row(N_HEADS * HEAD_DIM), row(2 * D_MODEL, COL_G // (2 * D_MODEL)), row(D_MODEL),
                  full(wbs.shape), full(wba.shape), full(wo.shape), full((1, D_MODEL)), full((1, D_MODEL))],
        out_specs=row(D_MODEL),
        out_shape=jax.ShapeDtypeStruct((m, D_MODEL), F32),
        compiler_params=_cparams(("parallel",)),
        name="merge_out",
    )(y_ssm, y_att, proj, h1, wbs, wba, wo, g, b)


def kernel(x_prompt, x_sample, cache_k, cache_v, cache_kidx, state_conv, state_ssm, page_table, bias_table, w_in, conv_w, conv_b, dt_bias, a_log, d_skip, ssm_norm_g, w_branch_ssm, w_branch_attn, w_out, ffn1_wi, ffn1_wo, ffn2_wi, ffn2_wo, ln1_g, ln1_b, ln2_g, ln2_b, ln3_g, ln3_b):
    assert w_in.shape[0] == DEPTH
    batch, seq, d = x_prompt.shape
    nb, t_new, _ = x_sample.shape
    xs = (x_prompt.reshape(batch * seq, d), x_sample.reshape(nb * t_new, d))
    outs_p, outs_s = [], []
    for i in range(DEPTH):
        w_pack = pack_w_in(w_in[i])
        f1i, f1o = ffn1_wi[i].astype(BF16), ffn1_wo[i].astype(BF16)
        f2i, f2o = ffn2_wi[i].astype(BF16), ffn2_wo[i].astype(BF16)
        wbs, wba, wo = w_branch_ssm[i].astype(BF16), w_branch_attn[i].astype(BF16), w_out[i].astype(BF16)
        row = lambda v: v.reshape(1, -1)
        pad_heads = lambda v: jnp.concatenate([v, jnp.zeros((LANES - SSM_HEADS,), v.dtype)]).reshape(1, LANES)
        dtb_pad, alog_pad = pad_heads(dt_bias[i]), pad_heads(a_log[i])
        dskip_e = jnp.repeat(d_skip[i], SSM_HEAD_DIM).reshape(1, D_INNER)
        ng = row(ssm_norm_g[i])
        cw, cb = conv_w[i], row(conv_b[i])

        h1, h1b, proj = [], [], []
        for x in xs:
            hf, hb = ffn_ln(x, f1i, f1o, row(ln1_g[i]), row(ln1_b[i]), emit_bf16=True)
            h1.append(hf)
            proj.append(matmul(hb, w_pack, tm=2048, tn=PROJ_TN))
        proj_p, proj_s = proj

        ys_p, conv_p, ssm_p = ssd_prompt(proj_p, batch, seq, cw, cb, dtb_pad, alog_pad, dskip_e, ng)
        ya_p = attn_prompt(proj_p, bias_table, batch, seq)

        raw_s = jnp.concatenate([proj_s[:, COL_X:COL_X + D_INNER], proj_s[:, COL_BC:COL_BC + 2 * GN]], axis=1)
        u_cat = jnp.concatenate([state_conv[i], raw_s.reshape(nb, t_new, CONV_DIM)], axis=1)
        expand = np.zeros((LANES, D_INNER), np.float32)
        expand[np.arange(D_INNER) // SSM_HEAD_DIM, np.arange(D_INNER)] = 1.0
        ys_s, conv_s, ssm_s = ssd_sample(u_cat, proj_s.reshape(nb, t_new, PROJ_N),
                                         state_ssm[i].reshape(nb, SSM_HEADS * SSM_HEAD_DIM, SSM_STATE),
                                         cw, cb, dtb_pad, alog_pad, dskip_e, ng, jnp.asarray(expand, BF16))
        ya_s = attn_sample(proj_s, cache_k[i], cache_v[i], cache_kidx[i], page_table, bias_table, t_new)

        new_xs = []
        for hf, pj, ys, ya in ((h1[0], proj_p, ys_p, ya_p), (h1[1], proj_s, ys_s.reshape(nb * t_new, D_INNER), ya_s)):
            h2 = merge_out(ys, ya, pj, hf, wbs, wba, wo, row(ln2_g[i]), row(ln2_b[i]))
            new_xs.append(ffn_ln(h2, f2i, f2o, row(ln3_g[i]), row(ln3_b[i]), emit_bf16=False))
        xs = tuple(new_xs)

        kvw = N_KV_HEADS * HEAD_DIM
        shp = lambda nb_, l_: (nb_, l_, N_KV_HEADS, HEAD_DIM)
        outs_p.append((proj_p[:, COL_K:COL_K + kvw].reshape(shp(batch, seq)),
                       proj_p[:, COL_V:COL_V + kvw].reshape(shp(batch, seq)),
                       proj_p[:, COL_KI:COL_KI + IDX_DIM].reshape(batch, seq, IDX_DIM),
                       conv_p, ssm_p.reshape(batch, SSM_HEADS, SSM_HEAD_DIM, SSM_STATE)))
        outs_s.append((proj_s[:, COL_K:COL_K + kvw].reshape(shp(nb, t_new)),
                       proj_s[:, COL_V:COL_V + kvw].reshape(shp(nb, t_new)),
                       proj_s[:, COL_KI:COL_KI + IDX_DIM].reshape(nb, t_new, IDX_DIM),
                       conv_s, ssm_s.reshape(nb, SSM_HEADS, SSM_HEAD_DIM, SSM_STATE)))
    k_p, v_p, kidx_p, conv_pp, ssm_pp = [jnp.stack(a) for a in zip(*outs_p)]
    k_s, v_s, kidx_s, conv_ss, ssm_ss = [jnp.stack(a) for a in zip(*outs_s)]
    return (xs[0].reshape(batch, seq, d), xs[1].reshape(nb, t_new, d),
            k_p, v_p, kidx_p, conv_pp, ssm_pp, k_s, v_s, kidx_s, conv_ss, ssm_ss)
```

```python
import functools
import math

import numpy as np
import jax
import jax.numpy as jnp
from jax import lax
from jax.experimental import pallas as pl
from jax.experimental.pallas import tpu as pltpu

F32 = jnp.float32
BF16 = jnp.bfloat16
I32 = jnp.int32

D_MODEL = 1024
D_INNER = 2 * D_MODEL
SSM_HEAD_DIM = 64
SSM_HEADS = D_INNER // SSM_HEAD_DIM
SSM_GROUPS = 4
SSM_STATE = 128
CONV_W = 4
CONV_DIM = D_INNER + 2 * SSM_GROUPS * SSM_STATE
SSD_CHUNK = 128
N_HEADS = 16
HEAD_DIM = 64
N_KV_HEADS = 4
Q_PER_KV = N_HEADS // N_KV_HEADS
IDX_HEADS = 8
IDX_DIM = 64
TOPK_MAX = 256
N_BUCKETS = 32
MAX_DISTANCE = 128
D_FF = 2816
DEPTH = 1
ALPHA = (2 * DEPTH) ** 0.25
LN_EPS = 1e-5
IN_SPLITS = (D_INNER, CONV_DIM, SSM_HEADS, N_HEADS * HEAD_DIM, N_KV_HEADS * HEAD_DIM,
             N_KV_HEADS * HEAD_DIM, IDX_HEADS * IDX_DIM, IDX_DIM, IDX_HEADS, 2 * D_MODEL)

LANES = 128
SUBLANES = 8
VMEM_LIMIT_BYTES = 56 * 1024 * 1024

GN = SSM_GROUPS * SSM_STATE
COL_Z = 0
COL_X = COL_Z + D_INNER
COL_G = COL_X + D_INNER
COL_BC = COL_G + 2 * D_MODEL
COL_Q = COL_BC + 2 * GN
COL_K = COL_Q + N_HEADS * HEAD_DIM
COL_V = COL_K + N_KV_HEADS * HEAD_DIM
COL_QI = COL_V + N_KV_HEADS * HEAD_DIM
COL_SM = COL_QI + IDX_HEADS * IDX_DIM
COL_KI = COL_SM + LANES
PROJ_TN = 512
PROJ_N = -(-(COL_KI + LANES) // PROJ_TN) * PROJ_TN
SM_WI = SSM_HEADS

INT_MIN = -(2 ** 31)
NEG_INF = float("-inf")
PROMPT_KEY_GROUPS = 8
SAMPLE_SEQS_PER_STEP = 2
RADIX_BITS = 4


def _cparams(sem):
    return pltpu.CompilerParams(dimension_semantics=sem, vmem_limit_bytes=VMEM_LIMIT_BYTES)


def _layer_norm(r, g, b):
    mu = jnp.mean(r, axis=-1, keepdims=True)
    d = r - mu
    var = jnp.mean(d * d, axis=-1, keepdims=True)
    return d * lax.rsqrt(var + LN_EPS) * g + b


def _silu(x):
    return x * jax.nn.sigmoid(x)


def _softplus(x):
    return jnp.maximum(x, 0.0) + jnp.log1p(jnp.exp(-jnp.abs(x)))


def _dot(a, b):
    return jnp.dot(a, b, preferred_element_type=F32)


def _dot_nt(a, b):
    return lax.dot_general(a, b, (((1,), (1,)), ((), ())), preferred_element_type=F32)


def _dot_tn(a, b):
    return lax.dot_general(a, b, (((0,), (0,)), ((), ())), preferred_element_type=F32)


def _split3(v):
    hi = v.astype(BF16)
    r1 = v - hi.astype(F32)
    mid = r1.astype(BF16)
    lo = (r1 - mid.astype(F32)).astype(BF16)
    return hi, mid, lo


def _dot_exact_sel(v, sel_bf16):
    hi, mid, lo = _split3(v)
    return (_dot(hi, sel_bf16) + _dot(mid, sel_bf16)) + _dot(lo, sel_bf16)


def _sel_dot_exact(sel_bf16, v):
    hi, mid, lo = _split3(v)
    return (_dot(sel_bf16, hi) + _dot(sel_bf16, mid)) + _dot(sel_bf16, lo)


def _hi_lo(a):
    ah = a.astype(BF16)
    return ah, (a - ah.astype(F32)).astype(BF16)


def _dot_nt_x3(a, b):
    ah, al = _hi_lo(a)
    bh, bl = _hi_lo(b)
    return (_dot_nt(ah, bl) + _dot_nt(al, bh)) + _dot_nt(ah, bh)


def _dot_x3(a, b):
    ah, al = _hi_lo(a)
    bh, bl = _hi_lo(b)
    return (_dot(ah, bl) + _dot(al, bh)) + _dot(ah, bh)


def _col_reduce(x, op2, op):
    parts = []
    for r0 in range(0, x.shape[0], LANES):
        y = x[r0:r0 + LANES]
        n = y.shape[0]
        while n > SUBLANES and n % (2 * SUBLANES) == 0:
            n //= 2
            y = op2(y[:n], y[n:])
        parts.append(y)
    while len(parts) > 1:
        parts = [op2(parts[i], parts[i + 1]) if i + 1 < len(parts) else parts[i] for i in range(0, len(parts), 2)]
    return op(parts[0], axis=0, keepdims=True)


def _col_sum(x):
    return _col_reduce(x, jnp.add, jnp.sum)


def _col_max(x):
    return _col_reduce(x, jnp.maximum, jnp.max)


def _ffn_ln_body(x_ref, wg_ref, wu_ref, wo_ref, g_ref, b_ref, *rest, emit_bf16):
    if emit_bf16:
        o_ref, ob_ref, acc_ref, xb_ref = rest
    else:
        o_ref, acc_ref, xb_ref = rest
    j = pl.program_id(1)

    @pl.when(j == 0)
    def _():
        xb_ref[...] = x_ref[...].astype(BF16)
        acc_ref[...] = jnp.zeros_like(acc_ref)

    xb = xb_ref[...]
    gate = _dot(xb, wg_ref[...])
    up = _dot(xb, wu_ref[...])
    act = (_silu(gate) * up).astype(BF16)
    acc_ref[...] += _dot(act, wo_ref[...])

    @pl.when(j == pl.num_programs(1) - 1)
    def _():
        y = _layer_norm(ALPHA * x_ref[...] + 0.5 * acc_ref[...], g_ref[...], b_ref[...])
        o_ref[...] = y
        if emit_bf16:
            ob_ref[...] = y.astype(BF16)


def ffn_ln(x, wi_bf, wo_bf, g, b, *, emit_bf16, tm=512, tf=1408):
    m, d = x.shape
    dff = wo_bf.shape[0]
    tm = min(tm, m)
    nf = dff // tf
    assert m % tm == 0 and dff % tf == 0
    out_shape = [jax.ShapeDtypeStruct((m, d), F32)]
    out_specs = [pl.BlockSpec((tm, d), lambda i, j: (i, 0))]
    if emit_bf16:
        out_shape.append(jax.ShapeDtypeStruct((m, d), BF16))
        out_specs.append(pl.BlockSpec((tm, d), lambda i, j: (i, 0)))
    res = pl.pallas_call(
        functools.partial(_ffn_ln_body, emit_bf16=emit_bf16),
        grid=(m // tm, nf),
        in_specs=[
            pl.BlockSpec((tm, d), lambda i, j: (i, 0)),
            pl.BlockSpec((d, tf), lambda i, j: (0, j)),
            pl.BlockSpec((d, tf), lambda i, j: (0, j + nf)),
            pl.BlockSpec((tf, d), lambda i, j: (j, 0)),
            pl.BlockSpec((1, d), lambda i, j: (0, 0)),
            pl.BlockSpec((1, d), lambda i, j: (0, 0)),
        ],
        out_specs=out_specs,
        out_shape=out_shape,
        scratch_shapes=[pltpu.VMEM((tm, d), F32), pltpu.VMEM((tm, d), BF16)],
        compiler_params=_cparams(("parallel", "arbitrary")),
        name="ffn_ln",
    )(x, wi_bf, wi_bf, wo_bf, g, b)
    return res if emit_bf16 else res[0]


def _matmul_body(x_ref, w_ref, o_ref):
    o_ref[...] = _dot(x_ref[...], w_ref[...])


def matmul(x_bf, w_bf, *, tm, tn):
    m, k = x_bf.shape
    n = w_bf.shape[1]
    tm = min(tm, m)
    assert m % tm == 0 and n % tn == 0
    return pl.pallas_call(
        _matmul_body,
        grid=(m // tm, n // tn),
        in_specs=[pl.BlockSpec((tm, k), lambda i, j: (i, 0)),
                  pl.BlockSpec((k, tn), lambda i, j: (0, j))],
        out_specs=pl.BlockSpec((tm, tn), lambda i, j: (i, j)),
        out_shape=jax.ShapeDtypeStruct((m, n), F32),
        compiler_params=_cparams(("parallel", "arbitrary")),
        name="in_proj",
    )(x_bf, w_bf)


def pack_w_in(w_in):
    offs = np.cumsum((0,) + IN_SPLITS)
    z, xbc, dt, q, k, v, qi, ki, wi, gates = [w_in[:, offs[i]:offs[i + 1]] for i in range(len(IN_SPLITS))]
    d = w_in.shape[0]
    zeros = lambda n: jnp.zeros((d, n), w_in.dtype)
    small = jnp.concatenate([dt, wi, zeros(LANES - SSM_HEADS - IDX_HEADS)], axis=1)
    kib = jnp.concatenate([ki, zeros(LANES - IDX_DIM)], axis=1)
    packed = jnp.concatenate([z, xbc[:, :D_INNER], gates, xbc[:, D_INNER:], q, k, v, qi, small, kib,
                              zeros(PROJ_N - COL_KI - LANES)], axis=1)
    return packed.astype(BF16)


def _ssd_prompt_body(z_ref, xr_ref, bc_ref, sm_ref, cwx_ref, cwbc_ref, cbx_ref, cbbc_ref,
                     dtb_ref, alog_ref, dskip_ref, ng_ref,
                     y_ref, cs_ref, hs_ref,
                     xbuf, bcbuf, h_scr, cum_b, w_b, xc_scr, bcc_scr, y_scr, cb_scr, cum_t, dt_t):
    c = pl.program_id(1)
    nc = pl.num_programs(1)
    L = SSD_CHUNK
    halo = SUBLANES
    n_pairs = SSM_HEADS // 2
    pairs_per_group = n_pairs // SSM_GROUPS

    @pl.when(c == 0)
    def _():
        xbuf[0:halo, :] = jnp.zeros((halo, D_INNER), F32)
        bcbuf[0:halo, :] = jnp.zeros((halo, 2 * GN), F32)
        h_scr[...] = jnp.zeros_like(h_scr)

    xbuf[halo:halo + L, :] = xr_ref[...]
    bcbuf[halo:halo + L, :] = bc_ref[...]

    def conv(buf, w_ref, b_ref):
        acc = b_ref[...] + w_ref[0:1, :] * buf[pl.ds(halo - (CONV_W - 1), L), :]
        for k in range(1, CONV_W):
            acc = acc + w_ref[k:k + 1, :] * buf[pl.ds(halo - (CONV_W - 1) + k, L), :]
        return _silu(acc)

    xc_scr[...] = conv(xbuf, cwx_ref, cbx_ref)
    bcc_scr[...] = conv(bcbuf, cwbc_ref, cbbc_ref)
    xbuf[0:halo, :] = xbuf[L:L + halo, :]
    bcbuf[0:halo, :] = bcbuf[L:L + halo, :]

    dt = _softplus(sm_ref[...] + dtb_ref[...])
    a = -jnp.exp(alog_ref[...])
    row = lax.broadcasted_iota(I32, (L, L), 0)
    col = lax.broadcasted_iota(I32, (L, L), 1)
    tril = (row >= col).astype(F32)
    cum = jnp.dot(tril, dt * a, precision=lax.Precision.HIGHEST, preferred_element_type=F32)
    w = jnp.exp(cum[L - 1:L, :] - cum) * dt
    cum_t[...] = cum.T
    dt_t[...] = dt.T
    for h in range(SSM_HEADS):
        cum_b[h] = jnp.broadcast_to(cum[:, h:h + 1], (L, LANES))
        w_b[h] = jnp.broadcast_to(w[:, h:h + 1], (L, LANES))
    for g in range(SSM_GROUPS):
        bg = bcc_scr[:, g * SSM_STATE:(g + 1) * SSM_STATE].astype(BF16)
        cg = bcc_scr[:, GN + g * SSM_STATE:GN + (g + 1) * SSM_STATE].astype(BF16)
        cb_scr[g] = _dot_nt(cg, bg)

    causal = row >= col
    lane = lax.broadcasted_iota(I32, (L, LANES), 1)
    lo_half = lane < SSM_HEAD_DIM
    sub = lax.broadcasted_iota(I32, (2 * SSM_HEAD_DIM, SSM_STATE), 0)
    lo_rows = sub < SSM_HEAD_DIM

    def pair_body(i, carry):
        g = i // pairs_per_group
        h0 = 2 * i
        h1 = h0 + 1
        off = pl.multiple_of(i * LANES, LANES)
        boff = pl.multiple_of(g * SSM_STATE, SSM_STATE)
        xp = xc_scr[:, pl.ds(off, LANES)]
        bg = bcc_scr[:, pl.ds(boff, SSM_STATE)].astype(BF16)
        cg = bcc_scr[:, pl.ds(GN + boff, SSM_STATE)].astype(BF16)
        cb = cb_scr[g]
        c0 = cum_b[h0]
        c1 = cum_b[h1]

        def mix(ct, h):
            seg = jnp.exp(jnp.where(causal, ct - cum_t[pl.ds(h, 1), :], NEG_INF))
            return (cb * seg * dt_t[pl.ds(h, 1), :]).astype(BF16)

        zero = jnp.zeros_like(xp)
        yd = (_dot(mix(c0, h0), jnp.where(lo_half, xp, zero).astype(BF16))
              + _dot(mix(c1, h1), jnp.where(lo_half, zero, xp).astype(BF16)))
        hp = h_scr[i]
        yo = _dot_nt(cg, hp.astype(BF16)) * jnp.where(lo_half, jnp.exp(c0), jnp.exp(c1))
        y_scr[:, pl.ds(off, LANES)] = yd + yo + dskip_ref[:, pl.ds(off, LANES)] * xp
        xw = (xp * jnp.where(lo_half, w_b[h0], w_b[h1])).astype(BF16)
        st = _dot_tn(xw, bg)
        dec = jnp.where(lo_rows, jnp.exp(c0[L - 1:L, :]), jnp.exp(c1[L - 1:L, :]))
        h_scr[i] = dec * hp + st
        return carry

    lax.fori_loop(0, n_pairs, pair_body, 0)

    gw = D_INNER // SSM_GROUPS
    for g in range(SSM_GROUPS):
        sl = slice(g * gw, (g + 1) * gw)
        u = y_scr[:, sl] * _silu(z_ref[:, sl])
        ms = jnp.mean(u * u, axis=-1, keepdims=True)
        y_ref[:, sl] = (u * lax.rsqrt(ms + LN_EPS) * ng_ref[:, sl]).astype(y_ref.dtype)

    @pl.when(c == nc - 1)
    def _():
        cs_ref[0, :, 0:D_INNER] = xr_ref[L - (CONV_W - 1):L, :]
        cs_ref[0, :, D_INNER:CONV_DIM] = bc_ref[L - (CONV_W - 1):L, :]
        for i in range(n_pairs):
            hs_ref[0, i * LANES:(i + 1) * LANES, :] = h_scr[i]


def ssd_prompt(proj, batch, seq, conv_w, conv_b, dtb_pad, alog_pad, dskip_e, norm_g):
    L = SSD_CHUNK
    nc = seq // L
    assert seq % L == 0
    blk = lambda width, colblk: pl.BlockSpec((L, width), lambda b, c: (b * nc + c, colblk))
    full = lambda shape: pl.BlockSpec(shape, lambda b, c: tuple(0 for _ in shape))
    n_pairs = SSM_HEADS // 2
    return pl.pallas_call(
        _ssd_prompt_body,
        grid=(batch, nc),
        in_specs=[
            blk(D_INNER, COL_Z // D_INNER),
            blk(D_INNER, COL_X // D_INNER),
            blk(2 * GN, COL_BC // (2 * GN)),
            blk(LANES, COL_SM // LANES),
            full((CONV_W, D_INNER)), full((CONV_W, 2 * GN)), full((1, D_INNER)), full((1, 2 * GN)),
            full((1, LANES)), full((1, LANES)), full((1, D_INNER)), full((1, D_INNER)),
        ],
        out_specs=[
            pl.BlockSpec((L, D_INNER), lambda b, c: (b * nc + c, 0)),
            pl.BlockSpec((1, CONV_W - 1, CONV_DIM), lambda b, c: (b, 0, 0)),
            pl.BlockSpec((1, SSM_HEADS * SSM_HEAD_DIM, SSM_STATE), lambda b, c: (b, 0, 0)),
        ],
        out_shape=[
            jax.ShapeDtypeStruct((batch * seq, D_INNER), BF16),
            jax.ShapeDtypeStruct((batch, CONV_W - 1, CONV_DIM), F32),
            jax.ShapeDtypeStruct((batch, SSM_HEADS * SSM_HEAD_DIM, SSM_STATE), F32),
        ],
        scratch_shapes=[
            pltpu.VMEM((L + 2 * SUBLANES, D_INNER), F32),
            pltpu.VMEM((L + 2 * SUBLANES, 2 * GN), F32),
            pltpu.VMEM((n_pairs, 2 * SSM_HEAD_DIM, SSM_STATE), F32),
            pltpu.VMEM((SSM_HEADS, L, LANES), F32),
            pltpu.VMEM((SSM_HEADS, L, LANES), F32),
            pltpu.VMEM((L, D_INNER), F32),
            pltpu.VMEM((L, 2 * GN), F32),
            pltpu.VMEM((L, D_INNER), F32),
            pltpu.VMEM((SSM_GROUPS, L, L), F32),
            pltpu.VMEM((LANES, L), F32),
            pltpu.VMEM((LANES, L), F32),
        ],
        compiler_params=_cparams(("parallel", "arbitrary")),
        name="ssd_prompt",
    )(proj, proj, proj, proj, conv_w[:, :D_INNER], conv_w[:, D_INNER:], conv_b[:, :D_INNER], conv_b[:, D_INNER:],
      dtb_pad, alog_pad, dskip_e, norm_g)


def _ssd_sample_body(u_ref, z_ref, sm_ref, h0_ref, cw_ref, cb_ref, dtb_ref, alog_ref, dskip_ref, ng_ref, exp_ref,
                     y_ref, cs_ref, hn_ref, *, t_new):
    T = t_new
    n_pairs = SSM_HEADS // 2
    pairs_per_group = n_pairs // SSM_GROUPS
    acc = cb_ref[...] + cw_ref[0:1, :] * u_ref[0, 0:T, :]
    for k in range(1, CONV_W):
        acc = acc + cw_ref[k:k + 1, :] * u_ref[0, k:k + T, :]
    xbc = _silu(acc)
    cs_ref[0] = u_ref[0, T:T + CONV_W - 1, :]
    x = xbc[:, :D_INNER]

    dt = _softplus(sm_ref[0] + dtb_ref[...])
    da = dt * (-jnp.exp(alog_ref[...]))
    rows = [da[0:1, :]]
    for t in range(1, T):
        rows.append(rows[-1] + da[t:t + 1, :])
    cum = jnp.concatenate(rows, axis=0)
    both = _dot_exact_sel(jnp.concatenate([cum, dt], axis=0), exp_ref[...])
    cum_e = both[0:T, :]
    dt_e = both[T:2 * T, :]
    t_idx = lax.broadcasted_iota(I32, (T, GN), 0)

    gw = D_INNER // SSM_GROUPS
    y_parts = []
    for g in range(SSM_GROUPS):
        sl = slice(g * gw, (g + 1) * gw)
        bg = xbc[:, D_INNER + g * SSM_STATE:D_INNER + (g + 1) * SSM_STATE]
        cg = xbc[:, D_INNER + GN + g * SSM_STATE:D_INNER + GN + (g + 1) * SSM_STATE]
        cbg = _dot_nt(cg.astype(BF16), bg.astype(BF16))
        yg = dskip_ref[:, sl] * x[:, sl]
        for s in range(T):
            seg = jnp.exp(jnp.where(t_idx >= s, cum_e[:, sl] - cum_e[s:s + 1, sl], NEG_INF))
            coef = seg * dt_e[s:s + 1, sl] * jnp.broadcast_to(cbg[:, s:s + 1], (T, gw))
            yg = yg + coef * x[s:s + 1, sl]
        y_parts.append(yg)

    xw = x * dt_e * jnp.exp(cum_e[T - 1:T, :] - cum_e)
    ecum = jnp.exp(cum_e)
    sub = lax.broadcasted_iota(I32, (2 * SSM_HEAD_DIM, SSM_STATE), 0)
    lo_rows = sub < SSM_HEAD_DIM
    yo_parts = []
    for i in range(n_pairs):
        g = i // pairs_per_group
        sl = slice(i * LANES, (i + 1) * LANES)
        bg = xbc[:, D_INNER + g * SSM_STATE:D_INNER + (g + 1) * SSM_STATE].astype(BF16)
        cg = xbc[:, D_INNER + GN + g * SSM_STATE:D_INNER + GN + (g + 1) * SSM_STATE].astype(BF16)
        hp = h0_ref[0, i * LANES:(i + 1) * LANES, :]
        yo_parts.append(_dot_nt(cg, hp.astype(BF16)) * ecum[:, sl])
        st = _dot_tn(xw[:, sl].astype(BF16), bg)
        d0 = jnp.exp(cum[T - 1:T, 2 * i:2 * i + 1])
        d1 = jnp.exp(cum[T - 1:T, 2 * i + 1:2 * i + 2])
        dec = jnp.where(lo_rows, jnp.broadcast_to(d0, sub.shape), jnp.broadcast_to(d1, sub.shape))
        hn_ref[0, i * LANES:(i + 1) * LANES, :] = dec * hp + st

    for g in range(SSM_GROUPS):
        sl = slice(g * gw, (g + 1) * gw)
        yo = jnp.concatenate(yo_parts[g * pairs_per_group:(g + 1) * pairs_per_group], axis=1)
        u = (y_parts[g] + yo) * _silu(z_ref[0, :, sl])
        ms = jnp.mean(u * u, axis=-1, keepdims=True)
        y_ref[0, :, sl] = (u * lax.rsqrt(ms + LN_EPS) * ng_ref[:, sl]).astype(y_ref.dtype)


def ssd_sample(u_cat, proj3, h0, conv_w, conv_b, dtb_pad, alog_pad, dskip_e, norm_g, expand):
    nb, t_new, _ = proj3.shape
    full = lambda shape: pl.BlockSpec(shape, lambda b: tuple(0 for _ in shape))
    hp = SSM_HEADS * SSM_HEAD_DIM
    return pl.pallas_call(
        functools.partial(_ssd_sample_body, t_new=t_new),
        grid=(nb,),
        in_specs=[
            pl.BlockSpec((1, t_new + CONV_W - 1, CONV_DIM), lambda b: (b, 0, 0)),
            pl.BlockSpec((1, t_new, D_INNER), lambda b: (b, 0, COL_Z // D_INNER)),
            pl.BlockSpec((1, t_new, LANES), lambda b: (b, 0, COL_SM // LANES)),
            pl.BlockSpec((1, hp, SSM_STATE), lambda b: (b, 0, 0)),
            full((CONV_W, CONV_DIM)), full((1, CONV_DIM)), full((1, LANES)), full((1, LANES)),
            full((1, D_INNER)), full((1, D_INNER)), full((LANES, D_INNER)),
        ],
        out_specs=[
            pl.BlockSpec((1, t_new, D_INNER), lambda b: (b, 0, 0)),
            pl.BlockSpec((1, CONV_W - 1, CONV_DIM), lambda b: (b, 0, 0)),
            pl.BlockSpec((1, hp, SSM_STATE), lambda b: (b, 0, 0)),
        ],
        out_shape=[
            jax.ShapeDtypeStruct((nb, t_new, D_INNER), BF16),
            jax.ShapeDtypeStruct((nb, CONV_W - 1, CONV_DIM), F32),
            jax.ShapeDtypeStruct((nb, hp, SSM_STATE), F32),
        ],
        compiler_params=_cparams(("parallel",)),
        name="ssd_sample",
    )(u_cat, proj3, proj3, h0, conv_w, conv_b, dtb_pad, alog_pad, dskip_e, norm_g, expand)


def _sortable_key(score):
    score = jnp.where(score == 0.0, 0.0, score)
    bits = lax.bitcast_convert_type(score, I32)
    return bits ^ ((bits >> 31) & jnp.int32(0x7FFFFFFF))


def _t5_bucket_np(rel):
    n = np.maximum(rel, 0)
    max_exact = N_BUCKETS // 2
    nf = np.maximum(n, 1).astype(np.float32)
    large = max_exact + (np.log(nf / max_exact) / math.log(MAX_DISTANCE / max_exact)
                         * (N_BUCKETS - max_exact)).astype(np.int32)
    large = np.minimum(large, N_BUCKETS - 1)
    return np.where(n < max_exact, n, large).astype(np.int32)


def _kth_largest_cols(key_ref, n_rows, k):
    def count_ge(cand):
        return _col_sum(jnp.where(key_ref[0:n_rows, :] >= cand, 1.0, 0.0))

    r0 = jnp.where(count_ge(jnp.zeros((1, LANES), I32)) >= k, jnp.int32(0), jnp.int32(INT_MIN))
    r0 = jnp.broadcast_to(r0, (1, LANES))

    def body(it, r):
        cand = r + (jnp.int32(1) << (30 - it))
        return jnp.where(count_ge(cand) >= k, cand, r)

    return lax.fori_loop(0, 31, body, r0)


def _topk_mask_cols(key_ref, msk_ref, n_rows, k, valid_fn):
    thr = _kth_largest_cols(key_ref, n_rows, k)
    n_gt = _col_sum(jnp.where(key_ref[0:n_rows, :] > thr, 1.0, 0.0))
    need = k - n_gt
    run = jnp.zeros((1, LANES), F32)
    ri = lax.broadcasted_iota(I32, (LANES, LANES), 0)
    ci = lax.broadcasted_iota(I32, (LANES, LANES), 1)
    strict = (ri > ci).astype(BF16)
    for r0 in range(0, n_rows, LANES):
        kc = key_ref[r0:r0 + LANES, :]
        eq = kc == thr
        eqf = eq.astype(F32)
        before = _dot(strict, eqf.astype(BF16)) + run
        sel = ((kc > thr) | (eq & (before < need))) & valid_fn(r0, LANES)
        msk_ref[r0:r0 + LANES, :] = jnp.where(sel, 0.0, NEG_INF)
        run = run + jnp.sum(eqf, axis=0, keepdims=True)


def _kth_largest_rows(key_ref, k):
    rows = key_ref.shape[0]
    digits = 2 ** RADIX_BITS
    n_rounds = 32 // RADIX_BITS

    def body(rnd, r):
        shift = (32 - RADIX_BITS) - RADIX_BITS * rnd
        n_ok = jnp.zeros((rows, 1), I32)
        for i in range(1, digits):
            cand = r + (jnp.int32(i) << shift)
            cnt = jnp.sum((key_ref[...] >= cand).astype(I32), axis=1, keepdims=True)
            n_ok = n_ok + (cnt >= k).astype(I32)
        return r + (n_ok << shift)

    return lax.fori_loop(0, n_rounds, body, jnp.full((rows, 1), INT_MIN, I32))


def _topk_select_rows(key_ref, sel_ref, k, valid_fn):
    n_keys = key_ref.shape[1]
    thr = _kth_largest_rows(key_ref, k)
    n_gt = jnp.sum((key_ref[...] > thr).astype(I32), axis=1, keepdims=True)
    need = (k - n_gt).astype(F32)
    run = jnp.zeros((key_ref.shape[0], 1), F32)
    ri = lax.broadcasted_iota(I32, (LANES, LANES), 0)
    ci = lax.broadcasted_iota(I32, (LANES, LANES), 1)
    strict = (ri < ci).astype(BF16)
    for l0 in range(0, n_keys, LANES):
        kc = key_ref[:, l0:l0 + LANES]
        eq = kc == thr
        eqf = eq.astype(F32)
        before = _dot(eqf.astype(BF16), strict) + run
        sel = ((kc > thr) | (eq & (before < need))) & valid_fn(l0)
        sel_ref[:, l0:l0 + LANES] = sel.astype(F32).astype(BF16)
        run = run + jnp.sum(eqf, axis=1, keepdims=True)


def _attn_prompt_body(tab_ref, q_ref, k_ref, v_ref, qi_ref, ki_ref, sm_ref, bucket_ref,
                      o_ref, key_scr, msk_scr, lg_scr, ot_scr, bias_scr, *, nk, topk, j0):
    b = pl.program_id(0)
    j = j0 + pl.program_id(1)
    QB = LANES

    @pl.when((b == 0) & (pl.program_id(1) == 0))
    def _():
        for d in range(2):
            bk = bucket_ref[d]
            for h in range(N_HEADS):
                acc = jnp.zeros(bk.shape, F32)
                for bb in range(N_BUCKETS):
                    acc = jnp.where(bk == bb, tab_ref[bb, h], acc)
                bias_scr[h, d] = acc - tab_ref[N_BUCKETS - 1, h]

    ki = ki_ref[0:nk, 0:IDX_DIM]
    kih, kil = _hi_lo(ki)
    wt = sm_ref[...].T
    score = jnp.zeros((nk, QB), F32)
    for h in range(IDX_HEADS):
        qh, ql = _hi_lo(qi_ref[:, h * IDX_DIM:(h + 1) * IDX_DIM])
        s = ((_dot_nt(kih, ql) + _dot_nt(kil, qh)) + _dot_nt(kih, qh)) * (IDX_DIM ** -0.5)
        score = score + jnp.maximum(s, 0.0) * wt[SM_WI + h:SM_WI + h + 1, :]
    score = score * (IDX_HEADS ** -0.5)
    kpos = lax.broadcasted_iota(I32, (nk, QB), 0)
    qpos = j * QB + lax.broadcasted_iota(I32, (nk, QB), 1)
    score = jnp.where(kpos <= qpos, score, NEG_INF)
    key_scr[...] = _sortable_key(score)

    def valid(r0, rows):
        kp = r0 + lax.broadcasted_iota(I32, (rows, QB), 0)
        qp = j * QB + lax.broadcasted_iota(I32, (rows, QB), 1)
        return kp <= qp

    _topk_mask_cols(key_scr, msk_scr, nk, topk, valid)

    jm1 = jnp.maximum(j - 1, 0)
    off0 = pl.multiple_of(j * QB, QB)
    off1 = pl.multiple_of(jm1 * QB, QB)
    for g in range(N_KV_HEADS):
        kg = k_ref[0:nk, g * HEAD_DIM:(g + 1) * HEAD_DIM].astype(BF16)
        vg = v_ref[0:nk, g * HEAD_DIM:(g + 1) * HEAD_DIM].astype(BF16)
        k0 = k_ref[pl.ds(off0, QB), g * HEAD_DIM:(g + 1) * HEAD_DIM].astype(BF16)
        k1 = k_ref[pl.ds(off1, QB), g * HEAD_DIM:(g + 1) * HEAD_DIM].astype(BF16)
        for r in range(Q_PER_KV):
            h = g * Q_PER_KV + r
            qh = (q_ref[:, h * HEAD_DIM:(h + 1) * HEAD_DIM] * (HEAD_DIM ** -0.5)).astype(BF16)
            lg_scr[...] = _dot_nt(kg, qh) + msk_scr[...]

            @pl.when(j > 0)
            def _():
                lg_scr[pl.ds(off1, QB), :] = (_dot_nt(k1, qh) + bias_scr[h, 1]) + msk_scr[pl.ds(off1, QB), :]

            lg_scr[pl.ds(off0, QB), :] = (_dot_nt(k0, qh) + bias_scr[h, 0]) + msk_scr[pl.ds(off0, QB), :]
            lg = lg_scr[...]
            m = _col_max(lg)
            p = jnp.exp(lg - m)
            l = _col_sum(p)
            ot_scr[h * HEAD_DIM:(h + 1) * HEAD_DIM, :] = _dot_tn(vg, p.astype(BF16)) * (1.0 / l)
    o_ref[0] = ot_scr[...].T.astype(o_ref.dtype)


def attn_prompt(proj, bias_table, batch, seq):
    QB = LANES
    nq = seq // QB
    assert seq % QB == 0
    topk = min(TOPK_MAX, seq // 4)
    ts = np.arange(QB)[:, None]
    tq = np.arange(QB)[None, :]
    bucket = jnp.asarray(np.stack([_t5_bucket_np(tq - ts), _t5_bucket_np(QB + tq - ts)]))
    n_groups = math.gcd(nq, PROMPT_KEY_GROUPS)
    per_group = nq // n_groups
    kvw = N_KV_HEADS * HEAD_DIM
    outs = []
    for grp in range(n_groups):
        j0 = grp * per_group
        nk = (j0 + per_group) * QB
        rowblk = lambda width, colblk: pl.BlockSpec((QB, width), lambda b, j: (b * nq + j0 + j, colblk))
        seqblk = lambda width, colblk: pl.BlockSpec((seq, width), lambda b, j: (b, colblk))
        outs.append(pl.pallas_call(
            functools.partial(_attn_prompt_body, nk=nk, topk=topk, j0=j0),
            grid=(batch, per_group),
            in_specs=[
                pl.BlockSpec(memory_space=pltpu.SMEM),
                rowblk(N_HEADS * HEAD_DIM, COL_Q // (N_HEADS * HEAD_DIM)),
                seqblk(kvw, COL_K // kvw),
                seqblk(kvw, COL_V // kvw),
                rowblk(IDX_HEADS * IDX_DIM, COL_QI // (IDX_HEADS * IDX_DIM)),
                seqblk(LANES, COL_KI // LANES),
                rowblk(LANES, COL_SM // LANES),
                pl.BlockSpec((2, QB, QB), lambda b, j: (0, 0, 0)),
            ],
            out_specs=pl.BlockSpec((1, QB, N_HEADS * HEAD_DIM), lambda b, j: (b, j, 0)),
            out_shape=jax.ShapeDtypeStruct((batch, per_group * QB, N_HEADS * HEAD_DIM), BF16),
            scratch_shapes=[
                pltpu.VMEM((nk, QB), I32),
                pltpu.VMEM((nk, QB), F32),
                pltpu.VMEM((nk, QB), F32),
                pltpu.VMEM((N_HEADS * HEAD_DIM, QB), F32),
                pltpu.VMEM((N_HEADS, 2, QB, QB), F32),
            ],
            compiler_params=_cparams(("arbitrary", "arbitrary")),
            name=f"attn_prompt_g{grp}",
        )(bias_table, proj, proj, proj, proj, proj, proj, bucket))
    return jnp.concatenate(outs, axis=1).reshape(batch * seq, N_HEADS * HEAD_DIM)


def _attn_sample_body(pt_ref, q_ref, qi_ref, w_ref, kn_ref, vn_ref, kin_ref, tabt_ref, bucket_ref, gsum_ref, xpand_ref,
                      *rest, n_seq, n_pages, page, t_new, topk):
    n_pg = n_seq * n_pages
    ck = rest[0:n_pg]
    cv = rest[n_pg:2 * n_pg]
    cki = rest[2 * n_pg:3 * n_pg]
    o_ref, sc_scr, key_scr, sel_scr, lg_scr, p_scr, bias_scr = rest[3 * n_pg:]
    past = n_pages * page
    n_tiles = n_pages + 1
    rows_q = t_new * N_HEADS
    rows_g = t_new * Q_PER_KV

    @pl.when(pl.program_id(0) == 0)
    def _():
        bk = bucket_ref[...]
        acc = jnp.zeros(bk.shape, F32)
        for bb in range(N_BUCKETS):
            acc = jnp.where(bk == bb, tabt_ref[:, bb:bb + 1], acc)
        bias_scr[...] = acc - tabt_ref[:, N_BUCKETS - 1:N_BUCKETS]

    for s in range(n_seq):
        qh, ql = _hi_lo(qi_ref[s])
        wcol = w_ref[s]
        gsum = gsum_ref[s]

        def idx_tile(kt_h, kt_l, nt):
            dot = _dot_nt if nt else _dot
            sc = (dot(qh, kt_l) + dot(ql, kt_h)) + dot(qh, kt_h)
            sc = jnp.maximum(sc * (IDX_DIM ** -0.5), 0.0) * wcol
            return _sel_dot_exact(gsum, sc) * (IDX_HEADS ** -0.5)

        for c in range(n_tiles):
            if c < n_pages:
                tile = idx_tile(*_hi_lo(cki[s * n_pages + c][0]), False)
            else:
                tile = idx_tile(*_hi_lo(kin_ref[s]), True)
            sl = slice(c * page, (c + 1) * page)
            sc_scr[:, sl] = tile if s == 0 else sc_scr[:, sl] + tile

    n_keys = n_tiles * page
    n_rows = n_seq * t_new

    def valid(l0):
        kp = l0 + lax.broadcasted_iota(I32, (n_rows, LANES), 1)
        row = lax.broadcasted_iota(I32, (n_rows, LANES), 0)
        tok = row
        for s in range(1, n_seq):
            tok = jnp.where(row >= s * t_new, row - s * t_new, tok)
        return kp <= past + tok

    for c in range(n_tiles):
        sl = slice(c * page, (c + 1) * page)
        key_scr[:, sl] = _sortable_key(jnp.where(valid(c * page), sc_scr[:, sl], NEG_INF))
    _topk_select_rows(key_scr, sel_scr, topk, valid)

    for s in range(n_seq):
        xp = xpand_ref[s]
        for g in range(N_KV_HEADS):
            rs = slice(g * rows_g, (g + 1) * rows_g)
            qg = (q_ref[s, rs, :] * (HEAD_DIM ** -0.5)).astype(BF16)
            for c in range(n_tiles):
                sl = slice(c * page, (c + 1) * page)
                if c < n_pages:
                    lg = _dot(qg, ck[s * n_pages + c][0, g].astype(BF16))
                else:
                    lg = _dot_nt(qg, kn_ref[s, :, g * HEAD_DIM:(g + 1) * HEAD_DIM].astype(BF16))
                if c >= n_pages - 1:
                    lg = lg + bias_scr[rs, (c - n_pages + 1) * page:(c - n_pages + 2) * page]
                lg_scr[rs, sl] = lg
        for c in range(n_tiles):
            sl = slice(c * page, (c + 1) * page)
            picked = _dot(xp, sel_scr[:, sl]) > 0.5
            lg_scr[:, sl] = jnp.where(picked, lg_scr[:, sl], NEG_INF)
        lg = lg_scr[...]
        m = jnp.max(lg, axis=1, keepdims=True)
        p = jnp.exp(lg - m)
        p_scr[...] = (p * (1.0 / jnp.sum(p, axis=1, keepdims=True))).astype(BF16)
        for g in range(N_KV_HEADS):
            rs = slice(g * rows_g, (g + 1) * rows_g)
            acc = _dot(p_scr[rs, past:n_keys], vn_ref[s, :, g * HEAD_DIM:(g + 1) * HEAD_DIM].astype(BF16))
            for c in range(n_pages):
                acc = acc + _dot_nt(p_scr[rs, c * page:(c + 1) * page], cv[s * n_pages + c][0, g].astype(BF16))
            o_ref[s, rs, :] = acc.astype(o_ref.dtype)


def attn_sample(proj_s, cache_k, cache_v, cache_kidx, page_table, bias_table, t_new):
    nb, n_pages = page_table.shape
    n_pool, page = cache_k.shape[0], cache_k.shape[1]
    past = n_pages * page
    total = past + t_new
    topk = min(TOPK_MAX, total // 4)
    kvw = N_KV_HEADS * HEAD_DIM
    n_seq = SAMPLE_SEQS_PER_STEP if nb % SAMPLE_SEQS_PER_STEP == 0 else 1
    rows_q = t_new * N_HEADS
    rows_i = t_new * IDX_HEADS
    n_rows = n_seq * t_new
    n_keys = past + page
    assert page == LANES and page >= MAX_DISTANCE and t_new <= page
    ck_t = jnp.transpose(cache_k, (0, 2, 3, 1))
    cv_t = jnp.transpose(cache_v, (0, 2, 3, 1))
    cki_t = jnp.transpose(cache_kidx, (0, 2, 1))
    seg = lambda col, width: proj_s[:, col:col + width]
    q3 = seg(COL_Q, N_HEADS * HEAD_DIM).reshape(nb, t_new, N_KV_HEADS, Q_PER_KV, HEAD_DIM)
    q3 = q3.transpose(0, 2, 1, 3, 4).reshape(nb, rows_q, HEAD_DIM)
    qi3 = seg(COL_QI, IDX_HEADS * IDX_DIM).reshape(nb, rows_i, IDX_DIM)
    w3 = seg(COL_SM + SM_WI, IDX_HEADS).reshape(nb, rows_i, 1)
    pad_rows = lambda a: jnp.pad(a.reshape(nb, t_new, -1), ((0, 0), (0, page - t_new), (0, 0)))
    kn3, vn3, kin3 = pad_rows(seg(COL_K, kvw)), pad_rows(seg(COL_V, kvw)), pad_rows(seg(COL_KI, IDX_DIM))
    row = np.arange(rows_q)
    row_tok = (row // Q_PER_KV) % t_new
    row_head = (row // (t_new * Q_PER_KV)) * Q_PER_KV + row % Q_PER_KV
    tab_t = jnp.pad(bias_table.T[row_head], ((0, 0), (0, LANES - N_BUCKETS)))
    kpos = np.concatenate([past - page + np.arange(page), past + np.arange(page)])[None, :]
    bucket = jnp.asarray(_t5_bucket_np(past + row_tok[:, None] - kpos))
    gsum = np.zeros((n_seq, n_rows, rows_i), np.float32)
    xpand = np.zeros((n_seq, rows_q, n_rows), np.float32)
    for s in range(n_seq):
        gsum[s, s * t_new + np.arange(rows_i) // IDX_HEADS, np.arange(rows_i)] = 1.0
        xpand[s, row, s * t_new + row_tok] = 1.0
    per_step = lambda shape: pl.BlockSpec((n_seq,) + shape, lambda b, pt: (b,) + tuple(0 for _ in shape))
    const = lambda shape: pl.BlockSpec(shape, lambda b, pt: tuple(0 for _ in shape))

    def page_specs(shape):
        return [pl.BlockSpec((1,) + shape, lambda b, pt, s=s, c=c: (pt[b * n_seq + s, c],) + tuple(0 for _ in shape))
                for s in range(n_seq) for c in range(n_pages)]

    grid_spec = pltpu.PrefetchScalarGridSpec(
        num_scalar_prefetch=1,
        grid=(nb // n_seq,),
        in_specs=[per_step((rows_q, HEAD_DIM)), per_step((rows_i, IDX_DIM)), per_step((rows_i, 1)),
                  per_step((page, kvw)), per_step((page, kvw)), per_step((page, IDX_DIM)),
                  const((rows_q, LANES)), const((rows_q, 2 * page)),
                  const((n_seq, n_rows, rows_i)), const((n_seq, rows_q, n_rows))]
                 + page_specs((N_KV_HEADS, HEAD_DIM, page)) + page_specs((N_KV_HEADS, HEAD_DIM, page))
                 + page_specs((IDX_DIM, page)),
        out_specs=pl.BlockSpec((n_seq, rows_q, HEAD_DIM), lambda b, pt: (b, 0, 0)),
        scratch_shapes=[
            pltpu.VMEM((n_rows, n_keys), F32),
            pltpu.VMEM((n_rows, n_keys), I32),
            pltpu.VMEM((n_rows, n_keys), BF16),
            pltpu.VMEM((rows_q, n_keys), F32),
            pltpu.VMEM((rows_q, n_keys), BF16),
            pltpu.VMEM((rows_q, 2 * page), F32),
        ],
    )
    n_pg = n_seq * n_pages
    out = pl.pallas_call(
        functools.partial(_attn_sample_body, n_seq=n_seq, n_pages=n_pages, page=page, t_new=t_new, topk=topk),
        grid_spec=grid_spec,
        out_shape=jax.ShapeDtypeStruct((nb, rows_q, HEAD_DIM), BF16),
        compiler_params=_cparams(("arbitrary",)),
        name="attn_sample",
    )(page_table, q3, qi3, w3, kn3, vn3, kin3, tab_t, bucket, jnp.asarray(gsum, BF16), jnp.asarray(xpand, BF16),
      *([ck_t] * n_pg), *([cv_t] * n_pg), *([cki_t] * n_pg))
    out = out.reshape(nb, N_KV_HEADS, t_new, Q_PER_KV, HEAD_DIM).transpose(0, 2, 1, 3, 4)
    return out.reshape(nb * t_new, N_HEADS * HEAD_DIM)


def _merge_body(ys_ref, ya_ref, g_ref, h_ref, wbs_ref, wba_ref, wo_ref, lg_ref, lb_ref, o_ref):
    a = _dot(ys_ref[...], wbs_ref[...])
    b = _dot(ya_ref[...], wba_ref[...])
    gates = jax.nn.sigmoid(g_ref[...])
    merged = gates[:, :D_MODEL] * a + gates[:, D_MODEL:] * b
    r = ALPHA * h_ref[...] + _dot(merged.astype(BF16), wo_ref[...])
    o_ref[...] = _layer_norm(r, lg_ref[...], lb_ref[...])


def merge_out(y_ssm, y_att, proj, h1, wbs, wba, wo, g, b, *, tm=512):
    m = h1.shape[0]
    tm = min(tm, m)
    assert m % tm == 0
    row = lambda width, colblk=0: pl.BlockSpec((tm, width), lambda i: (i, colblk))
    full = lambda shape: pl.BlockSpec(shape, lambda i: tuple(0 for _ in shape))
    return pl.pallas_call(
        _merge_body,
        grid=(m // tm,),
        in_specs=[row(D_INNER), row(N_HEADS * HEAD_DIM), row(2 * D_MODEL, COL_G // (2 * D_MODEL)), row(D_MODEL),
                  full(wbs.shape), full(wba.shape), full(wo.shape), full((1, D_MODEL)), full((1, D_MODEL))],
        out_specs=row(D_MODEL),
        out_shape=jax.ShapeDtypeStruct((m, D_MODEL), F32),
        compiler_params=_cparams(("parallel",)),
        name="merge_out",
    )(y_ssm, y_att, proj, h1, wbs, wba, wo, g, b)


def kernel(x_prompt, x_sample, cache_k, cache_v, cache_kidx, state_conv, state_ssm, page_table, bias_table, w_in, conv_w, conv_b, dt_bias, a_log, d_skip, ssm_norm_g, w_branch_ssm, w_branch_attn, w_out, ffn1_wi, ffn1_wo, ffn2_wi, ffn2_wo, ln1_g, ln1_b, ln2_g, ln2_b, ln3_g, ln3_b):
    assert w_in.shape[0] == DEPTH
    batch, seq, d = x_prompt.shape
    nb, t_new, _ = x_sample.shape
    xs = (x_prompt.reshape(batch * seq, d), x_sample.reshape(nb * t_new, d))
    outs_p, outs_s = [], []
    for i in range(DEPTH):
        w_pack = pack_w_in(w_in[i])
        f1i, f1o = ffn1_wi[i].astype(BF16), ffn1_wo[i].astype(BF16)
        f2i, f2o = ffn2_wi[i].astype(BF16), ffn2_wo[i].astype(BF16)
        wbs, wba, wo = w_branch_ssm[i].astype(BF16), w_branch_attn[i].astype(BF16), w_out[i].astype(BF16)
        row = lambda v: v.reshape(1, -1)
        pad_heads = lambda v: jnp.concatenate([v, jnp.zeros((LANES - SSM_HEADS,), v.dtype)]).reshape(1, LANES)
        dtb_pad, alog_pad = pad_heads(dt_bias[i]), pad_heads(a_log[i])
        dskip_e = jnp.repeat(d_skip[i], SSM_HEAD_DIM).reshape(1, D_INNER)
        ng = row(ssm_norm_g[i])
        cw, cb = conv_w[i], row(conv_b[i])

        h1, proj = [], []
        for x in xs:
            hf, hb = ffn_ln(x, f1i, f1o, row(ln1_g[i]), row(ln1_b[i]), emit_bf16=True)
            h1.append(hf)
            proj.append(matmul(hb, w_pack, tm=2048, tn=PROJ_TN))
        proj_p, proj_s = proj

        ys_p, conv_p, ssm_p = ssd_prompt(proj_p, batch, seq, cw, cb, dtb_pad, alog_pad, dskip_e, ng)
        ya_p = attn_prompt(proj_p, bias_table, batch, seq)

        raw_s = jnp.concatenate([proj_s[:, COL_X:COL_X + D_INNER], proj_s[:, COL_BC:COL_BC + 2 * GN]], axis=1)
        u_cat = jnp.concatenate([state_conv[i], raw_s.reshape(nb, t_new, CONV_DIM)], axis=1)
        expand = np.zeros((LANES, D_INNER), np.float32)
        expand[np.arange(D_INNER) // SSM_HEAD_DIM, np.arange(D_INNER)] = 1.0
        ys_s, conv_s, ssm_s = ssd_sample(u_cat, proj_s.reshape(nb, t_new, PROJ_N),
                                         state_ssm[i].reshape(nb, SSM_HEADS * SSM_HEAD_DIM, SSM_STATE),
                                         cw, cb, dtb_pad, alog_pad, dskip_e, ng, jnp.asarray(expand, BF16))
        ya_s = attn_sample(proj_s, cache_k[i], cache_v[i], cache_kidx[i], page_table, bias_table, t_new)

        new_xs = []
        for hf, pj, ys, ya in ((h1[0], proj_p, ys_p, ya_p), (h1[1], proj_s, ys_s.reshape(nb * t_new, D_INNER), ya_s)):
            h2 = merge_out(ys, ya, pj, hf, wbs, wba, wo, row(ln2_g[i]), row(ln2_b[i]))
            new_xs.append(ffn_ln(h2, f2i, f2o, row(ln3_g[i]), row(ln3_b[i]), emit_bf16=False))
        xs = tuple(new_xs)

        kvw = N_KV_HEADS * HEAD_DIM
        shp = lambda nb_, l_: (nb_, l_, N_KV_HEADS, HEAD_DIM)
        outs_p.append((proj_p[:, COL_K:COL_K + kvw].reshape(shp(batch, seq)),
                       proj_p[:, COL_V:COL_V + kvw].reshape(shp(batch, seq)),
                       proj_p[:, COL_KI:COL_KI + IDX_DIM].reshape(batch, seq, IDX_DIM),
                       conv_p, ssm_p.reshape(batch, SSM_HEADS, SSM_HEAD_DIM, SSM_STATE)))
        outs_s.append((proj_s[:, COL_K:COL_K + kvw].reshape(shp(nb, t_new)),
                       proj_s[:, COL_V:COL_V + kvw].reshape(shp(nb, t_new)),
                       proj_s[:, COL_KI:COL_KI + IDX_DIM].reshape(nb, t_new, IDX_DIM),
                       conv_s, ssm_s.reshape(nb, SSM_HEADS, SSM_HEAD_DIM, SSM_STATE)))
    k_p, v_p, kidx_p, conv_pp, ssm_pp = [jnp.stack(a) for a in zip(*outs_p)]
    k_s, v_s, kidx_s, conv_ss, ssm_ss = [jnp.stack(a) for a in zip(*outs_s)]
    return (xs[0].reshape(batch, seq, d), xs[1].reshape(nb, t_new, d),
            k_p, v_p, kidx_p, conv_pp, ssm_pp, k_s, v_s, kidx_s, conv_ss, ssm_ss)
```

```python
import functools
import math

import numpy as np
import jax
import jax.numpy as jnp
from jax import lax
from jax.experimental import pallas as pl
from jax.experimental.pallas import tpu as pltpu

F32 = jnp.float32
BF16 = jnp.bfloat16
I32 = jnp.int32

D_MODEL = 1024
D_INNER = 2 * D_MODEL
SSM_HEAD_DIM = 64
SSM_HEADS = D_INNER // SSM_HEAD_DIM
SSM_GROUPS = 4
SSM_STATE = 128
CONV_W = 4
CONV_DIM = D_INNER + 2 * SSM_GROUPS * SSM_STATE
SSD_CHUNK = 128
N_HEADS = 16
HEAD_DIM = 64
N_KV_HEADS = 4
Q_PER_KV = N_HEADS // N_KV_HEADS
IDX_HEADS = 8
IDX_DIM = 64
TOPK_MAX = 256
N_BUCKETS = 32
MAX_DISTANCE = 128
D_FF = 2816
DEPTH = 1
ALPHA = (2 * DEPTH) ** 0.25
LN_EPS = 1e-5
IN_SPLITS = (D_INNER, CONV_DIM, SSM_HEADS, N_HEADS * HEAD_DIM, N_KV_HEADS * HEAD_DIM,
             N_KV_HEADS * HEAD_DIM, IDX_HEADS * IDX_DIM, IDX_DIM, IDX_HEADS, 2 * D_MODEL)

LANES = 128
SUBLANES = 8
VMEM_LIMIT_BYTES = 56 * 1024 * 1024

GN = SSM_GROUPS * SSM_STATE
COL_Z = 0
COL_X = COL_Z + D_INNER
COL_G = COL_X + D_INNER
COL_BC = COL_G + 2 * D_MODEL
COL_Q = COL_BC + 2 * GN
COL_K = COL_Q + N_HEADS * HEAD_DIM
COL_V = COL_K + N_KV_HEADS * HEAD_DIM
COL_QI = COL_V + N_KV_HEADS * HEAD_DIM
COL_SM = COL_QI + IDX_HEADS * IDX_DIM
COL_KI = COL_SM + LANES
PROJ_TN = 512
PROJ_N = -(-(COL_KI + LANES) // PROJ_TN) * PROJ_TN
SM_WI = SSM_HEADS

INT_MIN = -(2 ** 31)
NEG_INF = float("-inf")
PROMPT_KEY_GROUPS = 8
SAMPLE_SEQS_PER_STEP = 2
RADIX_BITS = 4


def _cparams(sem):
    return pltpu.CompilerParams(dimension_semantics=sem, vmem_limit_bytes=VMEM_LIMIT_BYTES)


def _layer_norm(r, g, b):
    mu = jnp.mean(r, axis=-1, keepdims=True)
    d = r - mu
    var = jnp.mean(d * d, axis=-1, keepdims=True)
    return d * lax.rsqrt(var + LN_EPS) * g + b


def _silu(x):
    return x * jax.nn.sigmoid(x)


def _softplus(x):
    return jnp.maximum(x, 0.0) + jnp.log1p(jnp.exp(-jnp.abs(x)))


def _dot(a, b):
    return jnp.dot(a, b, preferred_element_type=F32)


def _dot_nt(a, b):
    return lax.dot_general(a, b, (((1,), (1,)), ((), ())), preferred_element_type=F32)


def _dot_tn(a, b):
    return lax.dot_general(a, b, (((0,), (0,)), ((), ())), preferred_element_type=F32)


def _split3(v):
    hi = v.astype(BF16)
    r1 = v - hi.astype(F32)
    mid = r1.astype(BF16)
    lo = (r1 - mid.astype(F32)).astype(BF16)
    return hi, mid, lo


def _dot_exact_sel(v, sel_bf16):
    hi, mid, lo = _split3(v)
    return (_dot(hi, sel_bf16) + _dot(mid, sel_bf16)) + _dot(lo, sel_bf16)


def _sel_dot_exact(sel_bf16, v):
    hi, mid, lo = _split3(v)
    return (_dot(sel_bf16, hi) + _dot(sel_bf16, mid)) + _dot(sel_bf16, lo)


def _hi_lo(a):
    ah = a.astype(BF16)
    return ah, (a - ah.astype(F32)).astype(BF16)


def _dot_nt_x3(a, b):
    ah, al = _hi_lo(a)
    bh, bl = _hi_lo(b)
    return (_dot_nt(ah, bl) + _dot_nt(al, bh)) + _dot_nt(ah, bh)


def _dot_x3(a, b):
    ah, al = _hi_lo(a)
    bh, bl = _hi_lo(b)
    return (_dot(ah, bl) + _dot(al, bh)) + _dot(ah, bh)


def _col_reduce(x, op2, op):
    parts = []
    for r0 in range(0, x.shape[0], LANES):
        y = x[r0:r0 + LANES]
        n = y.shape[0]
        while n > SUBLANES and n % (2 * SUBLANES) == 0:
            n //= 2
            y = op2(y[:n], y[n:])
        parts.append(y)
    while len(parts) > 1:
        parts = [op2(parts[i], parts[i + 1]) if i + 1 < len(parts) else parts[i] for i in range(0, len(parts), 2)]
    return op(parts[0], axis=0, keepdims=True)


def _col_sum(x):
    return _col_reduce(x, jnp.add, jnp.sum)


def _col_max(x):
    return _col_reduce(x, jnp.maximum, jnp.max)


def _ffn_ln_body(x_ref, wg_ref, wu_ref, wo_ref, g_ref, b_ref, *rest, emit_bf16):
    if emit_bf16:
        o_ref, ob_ref, acc_ref, xb_ref = rest
    else:
        o_ref, acc_ref, xb_ref = rest
    j = pl.program_id(1)

    @pl.when(j == 0)
    def _():
        xb_ref[...] = x_ref[...].astype(BF16)
        acc_ref[...] = jnp.zeros_like(acc_ref)

    xb = xb_ref[...]
    gate = _dot(xb, wg_ref[...])
    up = _dot(xb, wu_ref[...])
    act = (_silu(gate) * up).astype(BF16)
    acc_ref[...] += _dot(act, wo_ref[...])

    @pl.when(j == pl.num_programs(1) - 1)
    def _():
        y = _layer_norm(ALPHA * x_ref[...] + 0.5 * acc_ref[...], g_ref[...], b_ref[...])
        o_ref[...] = y
        if emit_bf16:
            ob_ref[...] = y.astype(BF16)


def ffn_ln(x, wi_bf, wo_bf, g, b, *, emit_bf16, tm=512, tf=1408):
    m, d = x.shape
    dff = wo_bf.shape[0]
    tm = min(tm, m)
    nf = dff // tf
    assert m % tm == 0 and dff % tf == 0
    out_shape = [jax.ShapeDtypeStruct((m, d), F32)]
    out_specs = [pl.BlockSpec((tm, d), lambda i, j: (i, 0))]
    if emit_bf16:
        out_shape.append(jax.ShapeDtypeStruct((m, d), BF16))
        out_specs.append(pl.BlockSpec((tm, d), lambda i, j: (i, 0)))
    res = pl.pallas_call(
        functools.partial(_ffn_ln_body, emit_bf16=emit_bf16),
        grid=(m // tm, nf),
        in_specs=[
            pl.BlockSpec((tm, d), lambda i, j: (i, 0)),
            pl.BlockSpec((d, tf), lambda i, j: (0, j)),
            pl.BlockSpec((d, tf), lambda i, j: (0, j + nf)),
            pl.BlockSpec((tf, d), lambda i, j: (j, 0)),
            pl.BlockSpec((1, d), lambda i, j: (0, 0)),
            pl.BlockSpec((1, d), lambda i, j: (0, 0)),
        ],
        out_specs=out_specs,
        out_shape=out_shape,
        scratch_shapes=[pltpu.VMEM((tm, d), F32), pltpu.VMEM((tm, d), BF16)],
        compiler_params=_cparams(("parallel", "arbitrary")),
        name="ffn_ln",
    )(x, wi_bf, wi_bf, wo_bf, g, b)
    return res if emit_bf16 else res[0]


def _matmul_body(x_ref, w_ref, o_ref):
    o_ref[...] = _dot(x_ref[...], w_ref[...])


def matmul(x_bf, w_bf, *, tm, tn):
    m, k = x_bf.shape
    n = w_bf.shape[1]
    tm = min(tm, m)
    assert m % tm == 0 and n % tn == 0
    return pl.pallas_call(
        _matmul_body,
        grid=(m // tm, n // tn),
        in_specs=[pl.BlockSpec((tm, k), lambda i, j: (i, 0)),
                  pl.BlockSpec((k, tn), lambda i, j: (0, j))],
        out_specs=pl.BlockSpec((tm, tn), lambda i, j: (i, j)),
        out_shape=jax.ShapeDtypeStruct((m, n), F32),
        compiler_params=_cparams(("parallel", "arbitrary")),
        name="in_proj",
    )(x_bf, w_bf)


def pack_w_in(w_in):
    offs = np.cumsum((0,) + IN_SPLITS)
    z, xbc, dt, q, k, v, qi, ki, wi, gates = [w_in[:, offs[i]:offs[i + 1]] for i in range(len(IN_SPLITS))]
    d = w_in.shape[0]
    zeros = lambda n: jnp.zeros((d, n), w_in.dtype)
    small = jnp.concatenate([dt, wi, zeros(LANES - SSM_HEADS - IDX_HEADS)], axis=1)
    kib = jnp.concatenate([ki, zeros(LANES - IDX_DIM)], axis=1)
    packed = jnp.concatenate([z, xbc[:, :D_INNER], gates, xbc[:, D_INNER:], q, k, v, qi, small, kib,
                              zeros(PROJ_N - COL_KI - LANES)], axis=1)
    return packed.astype(BF16)


def _ssd_prompt_body(z_ref, xr_ref, bc_ref, sm_ref, cwx_ref, cwbc_ref, cbx_ref, cbbc_ref,
                     dtb_ref, alog_ref, dskip_ref, ng_ref,
                     y_ref, cs_ref, hs_ref,
                     xbuf, bcbuf, h_scr, cum_b, w_b, xc_scr, bcc_scr, y_scr, cb_scr, cum_t, dt_t):
    c = pl.program_id(1)
    nc = pl.num_programs(1)
    L = SSD_CHUNK
    halo = SUBLANES
    n_pairs = SSM_HEADS // 2
    pairs_per_group = n_pairs // SSM_GROUPS

    @pl.when(c == 0)
    def _():
        xbuf[0:halo, :] = jnp.zeros((halo, D_INNER), F32)
        bcbuf[0:halo, :] = jnp.zeros((halo, 2 * GN), F32)
        h_scr[...] = jnp.zeros_like(h_scr)

    xbuf[halo:halo + L, :] = xr_ref[...]
    bcbuf[halo:halo + L, :] = bc_ref[...]

    def conv(buf, w_ref, b_ref):
        full = buf[0:halo + L, :]
        acc = b_ref[...] + w_ref[CONV_W - 1:CONV_W, :] * full[halo:halo + L]
        for k in range(CONV_W - 1):
            shifted = pltpu.roll(full, CONV_W - 1 - k, axis=0)[halo:halo + L]
            acc = acc + w_ref[k:k + 1, :] * shifted
        return _silu(acc)

    xc_scr[...] = conv(xbuf, cwx_ref, cbx_ref)
    bcc_scr[...] = conv(bcbuf, cwbc_ref, cbbc_ref)
    xbuf[0:halo, :] = xbuf[L:L + halo, :]
    bcbuf[0:halo, :] = bcbuf[L:L + halo, :]

    dt = _softplus(sm_ref[...] + dtb_ref[...])
    a = -jnp.exp(alog_ref[...])
    row = lax.broadcasted_iota(I32, (L, L), 0)
    col = lax.broadcasted_iota(I32, (L, L), 1)
    tril = (row >= col).astype(F32)
    cum = jnp.dot(tril, dt * a, precision=lax.Precision.HIGHEST, preferred_element_type=F32)
    w = jnp.exp(cum[L - 1:L, :] - cum) * dt
    cum_t[...] = cum.T
    dt_t[...] = dt.T
    for h in range(SSM_HEADS):
        cum_b[h] = jnp.broadcast_to(cum[:, h:h + 1], (L, LANES))
        w_b[h] = jnp.broadcast_to(w[:, h:h + 1], (L, LANES))
    for g in range(SSM_GROUPS):
        bg = bcc_scr[:, g * SSM_STATE:(g + 1) * SSM_STATE].astype(BF16)
        cg = bcc_scr[:, GN + g * SSM_STATE:GN + (g + 1) * SSM_STATE].astype(BF16)
        cb_scr[g] = _dot_nt(cg, bg)

    causal = row >= col
    lane = lax.broadcasted_iota(I32, (L, LANES), 1)
    lo_half = lane < SSM_HEAD_DIM
    sub = lax.broadcasted_iota(I32, (2 * SSM_HEAD_DIM, SSM_STATE), 0)
    lo_rows = sub < SSM_HEAD_DIM

    def pair_body(i, carry):
        g = i // pairs_per_group
        h0 = 2 * i
        h1 = h0 + 1
        off = pl.multiple_of(i * LANES, LANES)
        boff = pl.multiple_of(g * SSM_STATE, SSM_STATE)
        xp = xc_scr[:, pl.ds(off, LANES)]
        bg = bcc_scr[:, pl.ds(boff, SSM_STATE)].astype(BF16)
        cg = bcc_scr[:, pl.ds(GN + boff, SSM_STATE)].astype(BF16)
        cb = cb_scr[g]
        c0 = cum_b[h0]
        c1 = cum_b[h1]

        def mix(ct, h):
            seg = jnp.exp(jnp.where(causal, ct - cum_t[pl.ds(h, 1), :], NEG_INF))
            return (cb * seg * dt_t[pl.ds(h, 1), :]).astype(BF16)

        zero = jnp.zeros_like(xp)
        yd = (_dot(mix(c0, h0), jnp.where(lo_half, xp, zero).astype(BF16))
              + _dot(mix(c1, h1), jnp.where(lo_half, zero, xp).astype(BF16)))
        hp = h_scr[i]
        yo = _dot_nt(cg, hp.astype(BF16)) * jnp.where(lo_half, jnp.exp(c0), jnp.exp(c1))
        y_scr[:, pl.ds(off, LANES)] = yd + yo + dskip_ref[:, pl.ds(off, LANES)] * xp
        xw = (xp * jnp.where(lo_half, w_b[h0], w_b[h1])).astype(BF16)
        st = _dot_tn(xw, bg)
        dec = jnp.where(lo_rows, jnp.exp(c0[L - 1:L, :]), jnp.exp(c1[L - 1:L, :]))
        h_scr[i] = dec * hp + st
        return carry

    lax.fori_loop(0, n_pairs, pair_body, 0, unroll=2)

    gw = D_INNER // SSM_GROUPS
    for g in range(SSM_GROUPS):
        sl = slice(g * gw, (g + 1) * gw)
        u = y_scr[:, sl] * _silu(z_ref[:, sl])
        ms = jnp.mean(u * u, axis=-1, keepdims=True)
        y_ref[:, sl] = (u * lax.rsqrt(ms + LN_EPS) * ng_ref[:, sl]).astype(y_ref.dtype)

    @pl.when(c == nc - 1)
    def _():
        cs_ref[0, :, 0:D_INNER] = xr_ref[L - (CONV_W - 1):L, :]
        cs_ref[0, :, D_INNER:CONV_DIM] = bc_ref[L - (CONV_W - 1):L, :]
        for i in range(n_pairs):
            hs_ref[0, i * LANES:(i + 1) * LANES, :] = h_scr[i]


def ssd_prompt(proj, batch, seq, conv_w, conv_b, dtb_pad, alog_pad, dskip_e, norm_g):
    L = SSD_CHUNK
    nc = seq // L
    assert seq % L == 0
    blk = lambda width, colblk: pl.BlockSpec((L, width), lambda b, c: (b * nc + c, colblk))
    full = lambda shape: pl.BlockSpec(shape, lambda b, c: tuple(0 for _ in shape))
    n_pairs = SSM_HEADS // 2
    return pl.pallas_call(
        _ssd_prompt_body,
        grid=(batch, nc),
        in_specs=[
            blk(D_INNER, COL_Z // D_INNER),
            blk(D_INNER, COL_X // D_INNER),
            blk(2 * GN, COL_BC // (2 * GN)),
            blk(LANES, COL_SM // LANES),
            full((CONV_W, D_INNER)), full((CONV_W, 2 * GN)), full((1, D_INNER)), full((1, 2 * GN)),
            full((1, LANES)), full((1, LANES)), full((1, D_INNER)), full((1, D_INNER)),
        ],
        out_specs=[
            pl.BlockSpec((L, D_INNER), lambda b, c: (b * nc + c, 0)),
            pl.BlockSpec((1, CONV_W - 1, CONV_DIM), lambda b, c: (b, 0, 0)),
            pl.BlockSpec((1, SSM_HEADS * SSM_HEAD_DIM, SSM_STATE), lambda b, c: (b, 0, 0)),
        ],
        out_shape=[
            jax.ShapeDtypeStruct((batch * seq, D_INNER), BF16),
            jax.ShapeDtypeStruct((batch, CONV_W - 1, CONV_DIM), F32),
            jax.ShapeDtypeStruct((batch, SSM_HEADS * SSM_HEAD_DIM, SSM_STATE), F32),
        ],
        scratch_shapes=[
            pltpu.VMEM((L + 2 * SUBLANES, D_INNER), F32),
            pltpu.VMEM((L + 2 * SUBLANES, 2 * GN), F32),
            pltpu.VMEM((n_pairs, 2 * SSM_HEAD_DIM, SSM_STATE), F32),
            pltpu.VMEM((SSM_HEADS, L, LANES), F32),
            pltpu.VMEM((SSM_HEADS, L, LANES), F32),
            pltpu.VMEM((L, D_INNER), F32),
            pltpu.VMEM((L, 2 * GN), F32),
            pltpu.VMEM((L, D_INNER), F32),
            pltpu.VMEM((SSM_GROUPS, L, L), F32),
            pltpu.VMEM((LANES, L), F32),
            pltpu.VMEM((LANES, L), F32),
        ],
        compiler_params=_cparams(("parallel", "arbitrary")),
        name="ssd_prompt",
    )(proj, proj, proj, proj, conv_w[:, :D_INNER], conv_w[:, D_INNER:], conv_b[:, :D_INNER], conv_b[:, D_INNER:],
      dtb_pad, alog_pad, dskip_e, norm_g)


def _ssd_sample_body(u_ref, z_ref, sm_ref, h0_ref, cw_ref, cb_ref, dtb_ref, alog_ref, dskip_ref, ng_ref, exp_ref,
                     y_ref, cs_ref, hn_ref, *, t_new):
    T = t_new
    n_pairs = SSM_HEADS // 2
    pairs_per_group = n_pairs // SSM_GROUPS
    acc = cb_ref[...] + cw_ref[0:1, :] * u_ref[0, 0:T, :]
    for k in range(1, CONV_W):
        acc = acc + cw_ref[k:k + 1, :] * u_ref[0, k:k + T, :]
    xbc = _silu(acc)
    cs_ref[0] = u_ref[0, T:T + CONV_W - 1, :]
    x = xbc[:, :D_INNER]

    dt = _softplus(sm_ref[0] + dtb_ref[...])
    da = dt * (-jnp.exp(alog_ref[...]))
    rows = [da[0:1, :]]
    for t in range(1, T):
        rows.append(rows[-1] + da[t:t + 1, :])
    cum = jnp.concatenate(rows, axis=0)
    both = _dot_exact_sel(jnp.concatenate([cum, dt], axis=0), exp_ref[...])
    cum_e = both[0:T, :]
    dt_e = both[T:2 * T, :]
    t_idx = lax.broadcasted_iota(I32, (T, GN), 0)

    gw = D_INNER // SSM_GROUPS
    y_parts = []
    for g in range(SSM_GROUPS):
        sl = slice(g * gw, (g + 1) * gw)
        bg = xbc[:, D_INNER + g * SSM_STATE:D_INNER + (g + 1) * SSM_STATE]
        cg = xbc[:, D_INNER + GN + g * SSM_STATE:D_INNER + GN + (g + 1) * SSM_STATE]
        cbg = _dot_nt(cg.astype(BF16), bg.astype(BF16))
        yg = dskip_ref[:, sl] * x[:, sl]
        for s in range(T):
            seg = jnp.exp(jnp.where(t_idx >= s, cum_e[:, sl] - cum_e[s:s + 1, sl], NEG_INF))
            coef = seg * dt_e[s:s + 1, sl] * jnp.broadcast_to(cbg[:, s:s + 1], (T, gw))
            yg = yg + coef * x[s:s + 1, sl]
        y_parts.append(yg)

    xw = x * dt_e * jnp.exp(cum_e[T - 1:T, :] - cum_e)
    ecum = jnp.exp(cum_e)
    sub = lax.broadcasted_iota(I32, (2 * SSM_HEAD_DIM, SSM_STATE), 0)
    lo_rows = sub < SSM_HEAD_DIM
    yo_parts = []
    for i in range(n_pairs):
        g = i // pairs_per_group
        sl = slice(i * LANES, (i + 1) * LANES)
        bg = xbc[:, D_INNER + g * SSM_STATE:D_INNER + (g + 1) * SSM_STATE].astype(BF16)
        cg = xbc[:, D_INNER + GN + g * SSM_STATE:D_INNER + GN + (g + 1) * SSM_STATE].astype(BF16)
        hp = h0_ref[0, i * LANES:(i + 1) * LANES, :]
        yo_parts.append(_dot_nt(cg, hp.astype(BF16)) * ecum[:, sl])
        st = _dot_tn(xw[:, sl].astype(BF16), bg)
        d0 = jnp.exp(cum[T - 1:T, 2 * i:2 * i + 1])
        d1 = jnp.exp(cum[T - 1:T, 2 * i + 1:2 * i + 2])
        dec = jnp.where(lo_rows, jnp.broadcast_to(d0, sub.shape), jnp.broadcast_to(d1, sub.shape))
        hn_ref[0, i * LANES:(i + 1) * LANES, :] = dec * hp + st

    for g in range(SSM_GROUPS):
        sl = slice(g * gw, (g + 1) * gw)
        yo = jnp.concatenate(yo_parts[g * pairs_per_group:(g + 1) * pairs_per_group], axis=1)
        u = (y_parts[g] + yo) * _silu(z_ref[0, :, sl])
        ms = jnp.mean(u * u, axis=-1, keepdims=True)
        y_ref[0, :, sl] = (u * lax.rsqrt(ms + LN_EPS) * ng_ref[:, sl]).astype(y_ref.dtype)


def ssd_sample(u_cat, proj3, h0, conv_w, conv_b, dtb_pad, alog_pad, dskip_e, norm_g, expand):
    nb, t_new, _ = proj3.shape
    full = lambda shape: pl.BlockSpec(shape, lambda b: tuple(0 for _ in shape))
    hp = SSM_HEADS * SSM_HEAD_DIM
    return pl.pallas_call(
        functools.partial(_ssd_sample_body, t_new=t_new),
        grid=(nb,),
        in_specs=[
            pl.BlockSpec((1, t_new + CONV_W - 1, CONV_DIM), lambda b: (b, 0, 0)),
            pl.BlockSpec((1, t_new, D_INNER), lambda b: (b, 0, COL_Z // D_INNER)),
            pl.BlockSpec((1, t_new, LANES), lambda b: (b, 0, COL_SM // LANES)),
            pl.BlockSpec((1, hp, SSM_STATE), lambda b: (b, 0, 0)),
            full((CONV_W, CONV_DIM)), full((1, CONV_DIM)), full((1, LANES)), full((1, LANES)),
            full((1, D_INNER)), full((1, D_INNER)), full((LANES, D_INNER)),
        ],
        out_specs=[
            pl.BlockSpec((1, t_new, D_INNER), lambda b: (b, 0, 0)),
            pl.BlockSpec((1, CONV_W - 1, CONV_DIM), lambda b: (b, 0, 0)),
            pl.BlockSpec((1, hp, SSM_STATE), lambda b: (b, 0, 0)),
        ],
        out_shape=[
            jax.ShapeDtypeStruct((nb, t_new, D_INNER), BF16),
            jax.ShapeDtypeStruct((nb, CONV_W - 1, CONV_DIM), F32),
            jax.ShapeDtypeStruct((nb, hp, SSM_STATE), F32),
        ],
        compiler_params=_cparams(("parallel",)),
        name="ssd_sample",
    )(u_cat, proj3, proj3, h0, conv_w, conv_b, dtb_pad, alog_pad, dskip_e, norm_g, expand)


def _sortable_key(score):
    score = jnp.where(score == 0.0, 0.0, score)
    bits = lax.bitcast_convert_type(score, I32)
    return bits ^ ((bits >> 31) & jnp.int32(0x7FFFFFFF))


def _t5_bucket_np(rel):
    n = np.maximum(rel, 0)
    max_exact = N_BUCKETS // 2
    nf = np.maximum(n, 1).astype(np.float32)
    large = max_exact + (np.log(nf / max_exact) / math.log(MAX_DISTANCE / max_exact)
                         * (N_BUCKETS - max_exact)).astype(np.int32)
    large = np.minimum(large, N_BUCKETS - 1)
    return np.where(n < max_exact, n, large).astype(np.int32)


def _kth_largest_cols(key_ref, n_rows, k):
    def count_ge(cand):
        return _col_sum(jnp.where(key_ref[0:n_rows, :] >= cand, 1.0, 0.0))

    r0 = jnp.where(count_ge(jnp.zeros((1, LANES), I32)) >= k, jnp.int32(0), jnp.int32(INT_MIN))
    r0 = jnp.broadcast_to(r0, (1, LANES))

    def body(it, r):
        cand = r + (jnp.int32(1) << (30 - it))
        return jnp.where(count_ge(cand) >= k, cand, r)

    return lax.fori_loop(0, 31, body, r0)


def _topk_mask_cols(key_ref, msk_ref, n_rows, k, valid_fn):
    thr = _kth_largest_cols(key_ref, n_rows, k)
    n_gt = _col_sum(jnp.where(key_ref[0:n_rows, :] > thr, 1.0, 0.0))
    need = k - n_gt
    run = jnp.zeros((1, LANES), F32)
    ri = lax.broadcasted_iota(I32, (LANES, LANES), 0)
    ci = lax.broadcasted_iota(I32, (LANES, LANES), 1)
    strict = (ri > ci).astype(BF16)
    for r0 in range(0, n_rows, LANES):
        kc = key_ref[r0:r0 + LANES, :]
        eq = kc == thr
        eqf = eq.astype(F32)
        before = _dot(strict, eqf.astype(BF16)) + run
        sel = ((kc > thr) | (eq & (before < need))) & valid_fn(r0, LANES)
        msk_ref[r0:r0 + LANES, :] = jnp.where(sel, 0.0, NEG_INF)
        run = run + jnp.sum(eqf, axis=0, keepdims=True)


def _kth_largest_rows(key_ref, k):
    rows = key_ref.shape[0]
    digits = 2 ** RADIX_BITS
    n_rounds = 32 // RADIX_BITS

    def body(rnd, r):
        shift = (32 - RADIX_BITS) - RADIX_BITS * rnd
        n_ok = jnp.zeros((rows, 1), I32)
        for i in range(1, digits):
            cand = r + (jnp.int32(i) << shift)
            cnt = jnp.sum((key_ref[...] >= cand).astype(I32), axis=1, keepdims=True)
            n_ok = n_ok + (cnt >= k).astype(I32)
        return r + (n_ok << shift)

    return lax.fori_loop(0, n_rounds, body, jnp.full((rows, 1), INT_MIN, I32))


def _topk_select_rows(key_ref, sel_ref, k, valid_fn):
    n_keys = key_ref.shape[1]
    thr = _kth_largest_rows(key_ref, k)
    n_gt = jnp.sum((key_ref[...] > thr).astype(I32), axis=1, keepdims=True)
    need = (k - n_gt).astype(F32)
    run = jnp.zeros((key_ref.shape[0], 1), F32)
    ri = lax.broadcasted_iota(I32, (LANES, LANES), 0)
    ci = lax.broadcasted_iota(I32, (LANES, LANES), 1)
    strict = (ri < ci).astype(BF16)
    for l0 in range(0, n_keys, LANES):
        kc = key_ref[:, l0:l0 + LANES]
        eq = kc == thr
        eqf = eq.astype(F32)
        before = _dot(eqf.astype(BF16), strict) + run
        sel = ((kc > thr) | (eq & (before < need))) & valid_fn(l0)
        sel_ref[:, l0:l0 + LANES] = sel.astype(F32).astype(BF16)
        run = run + jnp.sum(eqf, axis=1, keepdims=True)


def _attn_prompt_body(tab_ref, q_ref, k_ref, v_ref, qi_ref, ki_ref, sm_ref, bucket_ref,
                      o_ref, key_scr, msk_scr, lg_scr, ot_scr, bias_scr, *, nk, topk, j0):
    b = pl.program_id(0)
    j = j0 + pl.program_id(1)
    QB = LANES

    @pl.when((b == 0) & (pl.program_id(1) == 0))
    def _():
        for d in range(2):
            bk = bucket_ref[d]
            for h in range(N_HEADS):
                acc = jnp.zeros(bk.shape, F32)
                for bb in range(N_BUCKETS):
                    acc = jnp.where(bk == bb, tab_ref[bb, h], acc)
                bias_scr[h, d] = acc - tab_ref[N_BUCKETS - 1, h]

    ki = ki_ref[0:nk, 0:IDX_DIM]
    kih, kil = _hi_lo(ki)
    wt = sm_ref[...].T
    score = jnp.zeros((nk, QB), F32)
    for h in range(IDX_HEADS):
        qh, ql = _hi_lo(qi_ref[:, h * IDX_DIM:(h + 1) * IDX_DIM])
        s = ((_dot_nt(kih, ql) + _dot_nt(kil, qh)) + _dot_nt(kih, qh)) * (IDX_DIM ** -0.5)
        score = score + jnp.maximum(s, 0.0) * wt[SM_WI + h:SM_WI + h + 1, :]
    score = score * (IDX_HEADS ** -0.5)
    kpos = lax.broadcasted_iota(I32, (nk, QB), 0)
    qpos = j * QB + lax.broadcasted_iota(I32, (nk, QB), 1)
    score = jnp.where(kpos <= qpos, score, NEG_INF)
    key_scr[...] = _sortable_key(score)

    def valid(r0, rows):
        kp = r0 + lax.broadcasted_iota(I32, (rows, QB), 0)
        qp = j * QB + lax.broadcasted_iota(I32, (rows, QB), 1)
        return kp <= qp

    _topk_mask_cols(key_scr, msk_scr, nk, topk, valid)

    near = min(2 * QB, nk)
    start = pl.multiple_of(jnp.clip((j - 1) * QB, 0, nk - near), QB)
    msk_near = msk_scr[pl.ds(start, near), :]
    rows = lax.broadcasted_iota(I32, (nk, QB), 0)
    lg_scr[...] = jnp.where((rows >= start) & (rows < start + near), NEG_INF, msk_scr[...])
    first_is_diag = j == 0
    for g in range(N_KV_HEADS):
        cols = slice(g * HEAD_DIM, (g + 1) * HEAD_DIM)
        kg = k_ref[0:nk, cols].astype(BF16)
        vg = v_ref[0:nk, cols].astype(BF16)
        kn = k_ref[pl.ds(start, near), cols].astype(BF16)
        vn = v_ref[pl.ds(start, near), cols].astype(BF16)
        for r in range(Q_PER_KV):
            h = g * Q_PER_KV + r
            qh = (q_ref[:, h * HEAD_DIM:(h + 1) * HEAD_DIM] * (HEAD_DIM ** -0.5)).astype(BF16)
            bias = jnp.where(first_is_diag, bias_scr[h, 0], bias_scr[h, 1])
            if near > QB:
                bias = jnp.concatenate([bias, bias_scr[h, 0]], axis=0)
            lg_n = (_dot_nt(kn, qh) + bias) + msk_near
            lg_m = _dot_nt(kg, qh) + lg_scr[...]
            m = jnp.maximum(_col_max(lg_m), _col_max(lg_n))
            p_m = jnp.exp(lg_m - m)
            p_n = jnp.exp(lg_n - m)
            l = _col_sum(p_m) + _col_sum(p_n)
            pv = _dot_tn(vg, p_m.astype(BF16)) + _dot_tn(vn, p_n.astype(BF16))
            ot_scr[h * HEAD_DIM:(h + 1) * HEAD_DIM, :] = pv * (1.0 / l)
    o_ref[0] = ot_scr[...].T.astype(o_ref.dtype)


def attn_prompt(proj, bias_table, batch, seq):
    QB = LANES
    nq = seq // QB
    assert seq % QB == 0
    topk = min(TOPK_MAX, seq // 4)
    ts = np.arange(QB)[:, None]
    tq = np.arange(QB)[None, :]
    bucket = jnp.asarray(np.stack([_t5_bucket_np(tq - ts), _t5_bucket_np(QB + tq - ts)]))
    n_groups = math.gcd(nq, PROMPT_KEY_GROUPS)
    per_group = nq // n_groups
    kvw = N_KV_HEADS * HEAD_DIM
    outs = []
    for grp in range(n_groups):
        j0 = grp * per_group
        nk = (j0 + per_group) * QB
        rowblk = lambda width, colblk: pl.BlockSpec((QB, width), lambda b, j: (b * nq + j0 + j, colblk))
        seqblk = lambda width, colblk: pl.BlockSpec((seq, width), lambda b, j: (b, colblk))
        outs.append(pl.pallas_call(
            functools.partial(_attn_prompt_body, nk=nk, topk=topk, j0=j0),
            grid=(batch, per_group),
            in_specs=[
                pl.BlockSpec(memory_space=pltpu.SMEM),
                rowblk(N_HEADS * HEAD_DIM, COL_Q // (N_HEADS * HEAD_DIM)),
                seqblk(kvw, COL_K // kvw),
                seqblk(kvw, COL_V // kvw),
                rowblk(IDX_HEADS * IDX_DIM, COL_QI // (IDX_HEADS * IDX_DIM)),
                seqblk(LANES, COL_KI // LANES),
                rowblk(LANES, COL_SM // LANES),
                pl.BlockSpec((2, QB, QB), lambda b, j: (0, 0, 0)),
            ],
            out_specs=pl.BlockSpec((1, QB, N_HEADS * HEAD_DIM), lambda b, j: (b, j, 0)),
            out_shape=jax.ShapeDtypeStruct((batch, per_group * QB, N_HEADS * HEAD_DIM), BF16),
            scratch_shapes=[
                pltpu.VMEM((nk, QB), I32),
                pltpu.VMEM((nk, QB), F32),
                pltpu.VMEM((nk, QB), F32),
                pltpu.VMEM((N_HEADS * HEAD_DIM, QB), F32),
                pltpu.VMEM((N_HEADS, 2, QB, QB), F32),
            ],
            compiler_params=_cparams(("arbitrary", "arbitrary")),
            name=f"attn_prompt_g{grp}",
        )(bias_table, proj, proj, proj, proj, proj, proj, bucket))
    return jnp.concatenate(outs, axis=1).reshape(batch * seq, N_HEADS * HEAD_DIM)


def _attn_sample_body(pt_ref, q_ref, qi_ref, w_ref, kn_ref, vn_ref, kin_ref, tabt_ref, bucket_ref, gsum_ref, xpand_ref,
                      *rest, n_seq, n_pages, page, t_new, topk):
    n_pg = n_seq * n_pages
    ck = rest[0:n_pg]
    cv = rest[n_pg:2 * n_pg]
    cki = rest[2 * n_pg:3 * n_pg]
    o_ref, sc_scr, key_scr, sel_scr, lg_scr, p_scr, bias_scr, ki_st, k_st, v_st = rest[3 * n_pg:]
    past = n_pages * page
    n_tiles = n_pages + 1
    rows_q = t_new * N_HEADS
    rows_g = t_new * Q_PER_KV

    @pl.when(pl.program_id(0) == 0)
    def _():
        bk = bucket_ref[...]
        acc = jnp.zeros(bk.shape, F32)
        for bb in range(N_BUCKETS):
            acc = jnp.where(bk == bb, tabt_ref[:, bb:bb + 1], acc)
        bias_scr[...] = acc - tabt_ref[:, N_BUCKETS - 1:N_BUCKETS]

    for s in range(n_seq):
        for c in range(n_pages):
            sl = slice(c * page, (c + 1) * page)
            ki_st[s, :, sl] = cki[s * n_pages + c][0]
            k_st[s, :, :, sl] = ck[s * n_pages + c][0].astype(BF16)
            v_st[s, :, :, sl] = cv[s * n_pages + c][0].astype(BF16)

    for s in range(n_seq):
        qh, ql = _hi_lo(qi_ref[s])
        wcol = w_ref[s]
        gsum = gsum_ref[s]

        def idx_scores(kt_h, kt_l, dot):
            sc = (dot(qh, kt_l) + dot(ql, kt_h)) + dot(qh, kt_h)
            sc = jnp.maximum(sc * (IDX_DIM ** -0.5), 0.0) * wcol
            return _sel_dot_exact(gsum, sc) * (IDX_HEADS ** -0.5)

        main = idx_scores(*_hi_lo(ki_st[s]), _dot)
        tail = idx_scores(*_hi_lo(kin_ref[s]), _dot_nt)
        sc_scr[:, 0:past] = main if s == 0 else sc_scr[:, 0:past] + main
        sc_scr[:, past:] = tail if s == 0 else sc_scr[:, past:] + tail

    n_keys = n_tiles * page
    n_rows = n_seq * t_new

    def valid(l0):
        kp = l0 + lax.broadcasted_iota(I32, (n_rows, LANES), 1)
        row = lax.broadcasted_iota(I32, (n_rows, LANES), 0)
        tok = row
        for s in range(1, n_seq):
            tok = jnp.where(row >= s * t_new, row - s * t_new, tok)
        return kp <= past + tok

    for c in range(n_tiles):
        sl = slice(c * page, (c + 1) * page)
        key_scr[:, sl] = _sortable_key(jnp.where(valid(c * page), sc_scr[:, sl], NEG_INF))
    _topk_select_rows(key_scr, sel_scr, topk, valid)

    for s in range(n_seq):
        for g in range(N_KV_HEADS):
            rs = slice(g * rows_g, (g + 1) * rows_g)
            cols = slice(g * HEAD_DIM, (g + 1) * HEAD_DIM)
            qg = (q_ref[s, rs, :] * (HEAD_DIM ** -0.5)).astype(BF16)
            lg_scr[rs, 0:past] = _dot(qg, k_st[s, g])
            lg_scr[rs, past:] = _dot_nt(qg, kn_ref[s, :, cols].astype(BF16))
        near = slice(past - page, n_keys)
        lg_scr[:, near] = lg_scr[:, near] + bias_scr[...]
        picked = _dot(xpand_ref[s], sel_scr[...]) > 0.5
        lg = jnp.where(picked, lg_scr[...], NEG_INF)
        m = jnp.max(lg, axis=1, keepdims=True)
        p = jnp.exp(lg - m)
        p_scr[...] = (p * (1.0 / jnp.sum(p, axis=1, keepdims=True))).astype(BF16)
        for g in range(N_KV_HEADS):
            rs = slice(g * rows_g, (g + 1) * rows_g)
            cols = slice(g * HEAD_DIM, (g + 1) * HEAD_DIM)
            acc = _dot_nt(p_scr[rs, 0:past], v_st[s, g]) + _dot(p_scr[rs, past:], vn_ref[s, :, cols].astype(BF16))
            o_ref[s, rs, :] = acc.astype(o_ref.dtype)


def attn_sample(proj_s, cache_k, cache_v, cache_kidx, page_table, bias_table, t_new):
    nb, n_pages = page_table.shape
    n_pool, page = cache_k.shape[0], cache_k.shape[1]
    past = n_pages * page
    total = past + t_new
    topk = min(TOPK_MAX, total // 4)
    kvw = N_KV_HEADS * HEAD_DIM
    n_seq = SAMPLE_SEQS_PER_STEP if nb % SAMPLE_SEQS_PER_STEP == 0 else 1
    rows_q = t_new * N_HEADS
    rows_i = t_new * IDX_HEADS
    n_rows = n_seq * t_new
    n_keys = past + page
    assert page == LANES and page >= MAX_DISTANCE and t_new <= page
    ck_t = jnp.transpose(cache_k, (0, 2, 3, 1))
    cv_t = jnp.transpose(cache_v, (0, 2, 3, 1))
    cki_t = jnp.transpose(cache_kidx, (0, 2, 1))
    seg = lambda col, width: proj_s[:, col:col + width]
    q3 = seg(COL_Q, N_HEADS * HEAD_DIM).reshape(nb, t_new, N_KV_HEADS, Q_PER_KV, HEAD_DIM)
    q3 = q3.transpose(0, 2, 1, 3, 4).reshape(nb, rows_q, HEAD_DIM)
    qi3 = seg(COL_QI, IDX_HEADS * IDX_DIM).reshape(nb, rows_i, IDX_DIM)
    w3 = seg(COL_SM + SM_WI, IDX_HEADS).reshape(nb, rows_i, 1)
    pad_rows = lambda a: jnp.pad(a.reshape(nb, t_new, -1), ((0, 0), (0, page - t_new), (0, 0)))
    kn3, vn3, kin3 = pad_rows(seg(COL_K, kvw)), pad_rows(seg(COL_V, kvw)), pad_rows(seg(COL_KI, IDX_DIM))
    row = np.arange(rows_q)
    row_tok = (row // Q_PER_KV) % t_new
    row_head = (row // (t_new * Q_PER_KV)) * Q_PER_KV + row % Q_PER_KV
    tab_t = jnp.pad(bias_table.T[row_head], ((0, 0), (0, LANES - N_BUCKETS)))
    kpos = np.concatenate([past - page + np.arange(page), past + np.arange(page)])[None, :]
    bucket = jnp.asarray(_t5_bucket_np(past + row_tok[:, None] - kpos))
    gsum = np.zeros((n_seq, n_rows, rows_i), np.float32)
    xpand = np.zeros((n_seq, rows_q, n_rows), np.float32)
    for s in range(n_seq):
        gsum[s, s * t_new + np.arange(rows_i) // IDX_HEADS, np.arange(rows_i)] = 1.0
        xpand[s, row, s * t_new + row_tok] = 1.0
    per_step = lambda shape: pl.BlockSpec((n_seq,) + shape, lambda b, pt: (b,) + tuple(0 for _ in shape))
    const = lambda shape: pl.BlockSpec(shape, lambda b, pt: tuple(0 for _ in shape))

    def page_specs(shape):
        return [pl.BlockSpec((1,) + shape, lambda b, pt, s=s, c=c: (pt[b * n_seq + s, c],) + tuple(0 for _ in shape))
                for s in range(n_seq) for c in range(n_pages)]

    grid_spec = pltpu.PrefetchScalarGridSpec(
        num_scalar_prefetch=1,
        grid=(nb // n_seq,),
        in_specs=[per_step((rows_q, HEAD_DIM)), per_step((rows_i, IDX_DIM)), per_step((rows_i, 1)),
                  per_step((page, kvw)), per_step((page, kvw)), per_step((page, IDX_DIM)),
                  const((rows_q, LANES)), const((rows_q, 2 * page)),
                  const((n_seq, n_rows, rows_i)), const((n_seq, rows_q, n_rows))]
                 + page_specs((N_KV_HEADS, HEAD_DIM, page)) + page_specs((N_KV_HEADS, HEAD_DIM, page))
                 + page_specs((IDX_DIM, page)),
        out_specs=pl.BlockSpec((n_seq, rows_q, HEAD_DIM), lambda b, pt: (b, 0, 0)),
        scratch_shapes=[
            pltpu.VMEM((n_rows, n_keys), F32),
            pltpu.VMEM((n_rows, n_keys), I32),
            pltpu.VMEM((n_rows, n_keys), BF16),
            pltpu.VMEM((rows_q, n_keys), F32),
            pltpu.VMEM((rows_q, n_keys), BF16),
            pltpu.VMEM((rows_q, 2 * page), F32),
            pltpu.VMEM((n_seq, IDX_DIM, past), F32),
            pltpu.VMEM((n_seq, N_KV_HEADS, HEAD_DIM, past), BF16),
            pltpu.VMEM((n_seq, N_KV_HEADS, HEAD_DIM, past), BF16),
        ],
    )
    n_pg = n_seq * n_pages
    out = pl.pallas_call(
        functools.partial(_attn_sample_body, n_seq=n_seq, n_pages=n_pages, page=page, t_new=t_new, topk=topk),
        grid_spec=grid_spec,
        out_shape=jax.ShapeDtypeStruct((nb, rows_q, HEAD_DIM), BF16),
        compiler_params=_cparams(("arbitrary",)),
        name="attn_sample",
    )(page_table, q3, qi3, w3, kn3, vn3, kin3, tab_t, bucket, jnp.asarray(gsum, BF16), jnp.asarray(xpand, BF16),
      *([ck_t] * n_pg), *([cv_t] * n_pg), *([cki_t] * n_pg))
    out = out.reshape(nb, N_KV_HEADS, t_new, Q_PER_KV, HEAD_DIM).transpose(0, 2, 1, 3, 4)
    return out.reshape(nb * t_new, N_HEADS * HEAD_DIM)


def _merge_body(ys_ref, ya_ref, g_ref, h_ref, wbs_ref, wba_ref, wo_ref, lg_ref, lb_ref, o_ref):
    a = _dot(ys_ref[...], wbs_ref[...])
    b = _dot(ya_ref[...], wba_ref[...])
    gates = jax.nn.sigmoid(g_ref[...])
    merged = gates[:, :D_MODEL] * a + gates[:, D_MODEL:] * b
    r = ALPHA * h_ref[...] + _dot(merged.astype(BF16), wo_ref[...])
    o_ref[...] = _layer_norm(r, lg_ref[...], lb_ref[...])


def merge_out(y_ssm, y_att, proj, h1, wbs, wba, wo, g, b, *, tm=512):
    m = h1.shape[0]
    tm = min(tm, m)
    assert m % tm == 0
    row = lambda width, colblk=0: pl.BlockSpec((tm, width), lambda i: (i, colblk))
    full = lambda shape: pl.BlockSpec(shape, lambda i: tuple(0 for _ in shape))
    return pl.pallas_call(
        _merge_body,
        grid=(m // tm,),
        in_specs=[row(D_INNER), row(N_HEADS * HEAD_DIM), row(2 * D_MODEL, COL_G // (2 * D_MODEL)), row(D_MODEL),
                  full(wbs.shape), full(wba.shape), full(wo.shape), full((1, D_MODEL)), full((1, D_MODEL))],
        out_specs=row(D_MODEL),
        out_shape=jax.ShapeDtypeStruct((m, D_MODEL), F32),
        compiler_params=_cparams(("parallel",)),
        name="merge_out",
    )(y_ssm, y_att, proj, h1, wbs, wba, wo, g, b)


def kernel(x_prompt, x_sample, cache_k, cache_v, cache_kidx, state_conv, state_ssm, page_table, bias_table, w_in, conv_w, conv_b, dt_bias, a_log, d_skip, ssm_norm_g, w_branch_ssm, w_branch_attn, w_out, ffn1_wi, ffn1_wo, ffn2_wi, ffn2_wo, ln1_g, ln1_b, ln2_g, ln2_b, ln3_g, ln3_b):
    assert w_in.shape[0] == DEPTH
    batch, seq, d = x_prompt.shape
    nb, t_new, _ = x_sample.shape
    xs = (x_prompt.reshape(batch * seq, d), x_sample.reshape(nb * t_new, d))
    outs_p, outs_s = [], []
    for i in range(DEPTH):
        w_pack = pack_w_in(w_in[i])
        f1i, f1o = ffn1_wi[i].astype(BF16), ffn1_wo[i].astype(BF16)
        f2i, f2o = ffn2_wi[i].astype(BF16), ffn2_wo[i].astype(BF16)
        wbs, wba, wo = w_branch_ssm[i].astype(BF16), w_branch_attn[i].astype(BF16), w_out[i].astype(BF16)
        row = lambda v: v.reshape(1, -1)
        pad_heads = lambda v: jnp.concatenate([v, jnp.zeros((LANES - SSM_HEADS,), v.dtype)]).reshape(1, LANES)
        dtb_pad, alog_pad = pad_heads(dt_bias[i]), pad_heads(a_log[i])
        dskip_e = jnp.repeat(d_skip[i], SSM_HEAD_DIM).reshape(1, D_INNER)
        ng = row(ssm_norm_g[i])
        cw, cb = conv_w[i], row(conv_b[i])

        h1, proj = [], []
        for x in xs:
            hf, hb = ffn_ln(x, f1i, f1o, row(ln1_g[i]), row(ln1_b[i]), emit_bf16=True)
            h1.append(hf)
            proj.append(matmul(hb, w_pack, tm=2048, tn=PROJ_TN))
        proj_p, proj_s = proj

        ys_p, conv_p, ssm_p = ssd_prompt(proj_p, batch, seq, cw, cb, dtb_pad, alog_pad, dskip_e, ng)
        ya_p = attn_prompt(proj_p, bias_table, batch, seq)

        raw_s = jnp.concatenate([proj_s[:, COL_X:COL_X + D_INNER], proj_s[:, COL_BC:COL_BC + 2 * GN]], axis=1)
        u_cat = jnp.concatenate([state_conv[i], raw_s.reshape(nb, t_new, CONV_DIM)], axis=1)
        expand = np.zeros((LANES, D_INNER), np.float32)
        expand[np.arange(D_INNER) // SSM_HEAD_DIM, np.arange(D_INNER)] = 1.0
        ys_s, conv_s, ssm_s = ssd_sample(u_cat, proj_s.reshape(nb, t_new, PROJ_N),
                                         state_ssm[i].reshape(nb, SSM_HEADS * SSM_HEAD_DIM, SSM_STATE),
                                         cw, cb, dtb_pad, alog_pad, dskip_e, ng, jnp.asarray(expand, BF16))
        ya_s = attn_sample(proj_s, cache_k[i], cache_v[i], cache_kidx[i], page_table, bias_table, t_new)

        new_xs = []
        for hf, pj, ys, ya in ((h1[0], proj_p, ys_p, ya_p), (h1[1], proj_s, ys_s.reshape(nb * t_new, D_INNER), ya_s)):
            h2 = merge_out(ys, ya, pj, hf, wbs, wba, wo, row(ln2_g[i]), row(ln2_b[i]))
            new_xs.append(ffn_ln(h2, f2i, f2o, row(ln3_g[i]), row(ln3_b[i]), emit_bf16=False))
        xs = tuple(new_xs)

        kvw = N_KV_HEADS * HEAD_DIM
        shp = lambda nb_, l_: (nb_, l_, N_KV_HEADS, HEAD_DIM)
        outs_p.append((proj_p[:, COL_K:COL_K + kvw].reshape(shp(batch, seq)),
                       proj_p[:, COL_V:COL_V + kvw].reshape(shp(batch, seq)),
                       proj_p[:, COL_KI:COL_KI + IDX_DIM].reshape(batch, seq, IDX_DIM),
                       conv_p, ssm_p.reshape(batch, SSM_HEADS, SSM_HEAD_DIM, SSM_STATE)))
        outs_s.append((proj_s[:, COL_K:COL_K + kvw].reshape(shp(nb, t_new)),
                       proj_s[:, COL_V:COL_V + kvw].reshape(shp(nb, t_new)),
                       proj_s[:, COL_KI:COL_KI + IDX_DIM].reshape(nb, t_new, IDX_DIM),
                       conv_s, ssm_s.reshape(nb, SSM_HEADS, SSM_HEAD_DIM, SSM_STATE)))
    k_p, v_p, kidx_p, conv_pp, ssm_pp = [jnp.stack(a) for a in zip(*outs_p)]
    k_s, v_s, kidx_s, conv_ss, ssm_ss = [jnp.stack(a) for a in zip(*outs_s)]
    return (xs[0].reshape(batch, seq, d), xs[1].reshape(nb, t_new, d),
            k_p, v_p, kidx_p, conv_pp, ssm_pp, k_s, v_s, kidx_s, conv_ss, ssm_ss)
```

```python
import functools
import math

import numpy as np
import jax
import jax.numpy as jnp
from jax import lax
from jax.experimental import pallas as pl
from jax.experimental.pallas import tpu as pltpu

F32 = jnp.float32
BF16 = jnp.bfloat16
I32 = jnp.int32

D_MODEL = 1024
D_INNER = 2 * D_MODEL
SSM_HEAD_DIM = 64
SSM_HEADS = D_INNER // SSM_HEAD_DIM
SSM_GROUPS = 4
SSM_STATE = 128
CONV_W = 4
CONV_DIM = D_INNER + 2 * SSM_GROUPS * SSM_STATE
SSD_CHUNK = 128
N_HEADS = 16
HEAD_DIM = 64
N_KV_HEADS = 4
Q_PER_KV = N_HEADS // N_KV_HEADS
IDX_HEADS = 8
IDX_DIM = 64
TOPK_MAX = 256
N_BUCKETS = 32
MAX_DISTANCE = 128
D_FF = 2816
DEPTH = 1
ALPHA = (2 * DEPTH) ** 0.25
LN_EPS = 1e-5
IN_SPLITS = (D_INNER, CONV_DIM, SSM_HEADS, N_HEADS * HEAD_DIM, N_KV_HEADS * HEAD_DIM,
             N_KV_HEADS * HEAD_DIM, IDX_HEADS * IDX_DIM, IDX_DIM, IDX_HEADS, 2 * D_MODEL)

LANES = 128
SUBLANES = 8
VMEM_LIMIT_BYTES = 56 * 1024 * 1024

GN = SSM_GROUPS * SSM_STATE
COL_Z = 0
COL_X = COL_Z + D_INNER
COL_G = COL_X + D_INNER
COL_BC = COL_G + 2 * D_MODEL
COL_Q = COL_BC + 2 * GN
COL_K = COL_Q + N_HEADS * HEAD_DIM
COL_V = COL_K + N_KV_HEADS * HEAD_DIM
COL_QI = COL_V + N_KV_HEADS * HEAD_DIM
COL_SM = COL_QI + IDX_HEADS * IDX_DIM
COL_KI = COL_SM + LANES
PROJ_TN = 512
PROJ_N = -(-(COL_KI + LANES) // PROJ_TN) * PROJ_TN
SM_WI = SSM_HEADS

INT_MIN = -(2 ** 31)
LOG2E = math.log2(math.e)
NEG_INF = float("-inf")
PROMPT_KEY_GROUPS = 8
SAMPLE_SEQS_PER_STEP = 2
RADIX_BITS = 4


def _cparams(sem):
    return pltpu.CompilerParams(dimension_semantics=sem, vmem_limit_bytes=VMEM_LIMIT_BYTES)


def _layer_norm(r, g, b):
    mu = jnp.mean(r, axis=-1, keepdims=True)
    d = r - mu
    var = jnp.mean(d * d, axis=-1, keepdims=True)
    return d * lax.rsqrt(var + LN_EPS) * g + b


def _silu(x):
    return x * jax.nn.sigmoid(x)


def _softplus(x):
    return jnp.maximum(x, 0.0) + jnp.log1p(jnp.exp(-jnp.abs(x)))


def _dot(a, b):
    return jnp.dot(a, b, preferred_element_type=F32)


def _dot_nt(a, b):
    return lax.dot_general(a, b, (((1,), (1,)), ((), ())), preferred_element_type=F32)


def _dot_tn(a, b):
    return lax.dot_general(a, b, (((0,), (0,)), ((), ())), preferred_element_type=F32)


def _split3(v):
    hi = v.astype(BF16)
    r1 = v - hi.astype(F32)
    mid = r1.astype(BF16)
    lo = (r1 - mid.astype(F32)).astype(BF16)
    return hi, mid, lo


def _dot_exact_sel(v, sel_bf16):
    hi, mid, lo = _split3(v)
    return (_dot(hi, sel_bf16) + _dot(mid, sel_bf16)) + _dot(lo, sel_bf16)


def _sel_dot_exact(sel_bf16, v):
    hi, mid, lo = _split3(v)
    return (_dot(sel_bf16, hi) + _dot(sel_bf16, mid)) + _dot(sel_bf16, lo)


def _hi_lo(a):
    ah = a.astype(BF16)
    return ah, (a - ah.astype(F32)).astype(BF16)


def _dot_nt_x3(a, b):
    ah, al = _hi_lo(a)
    bh, bl = _hi_lo(b)
    return (_dot_nt(ah, bl) + _dot_nt(al, bh)) + _dot_nt(ah, bh)


def _dot_x3(a, b):
    ah, al = _hi_lo(a)
    bh, bl = _hi_lo(b)
    return (_dot(ah, bl) + _dot(al, bh)) + _dot(ah, bh)


def _col_reduce(x, op2, op):
    parts = []
    for r0 in range(0, x.shape[0], LANES):
        y = x[r0:r0 + LANES]
        n = y.shape[0]
        while n > SUBLANES and n % (2 * SUBLANES) == 0:
            n //= 2
            y = op2(y[:n], y[n:])
        parts.append(y)
    while len(parts) > 1:
        parts = [op2(parts[i], parts[i + 1]) if i + 1 < len(parts) else parts[i] for i in range(0, len(parts), 2)]
    return op(parts[0], axis=0, keepdims=True)


def _col_sum(x):
    return _col_reduce(x, jnp.add, jnp.sum)


def _col_max(x):
    return _col_reduce(x, jnp.maximum, jnp.max)


def _ffn_ln_body(x_ref, wg_ref, wu_ref, wo_ref, g_ref, b_ref, *rest, emit_bf16):
    if emit_bf16:
        o_ref, ob_ref, acc_ref, xb_ref = rest
    else:
        o_ref, acc_ref, xb_ref = rest
    j = pl.program_id(1)

    @pl.when(j == 0)
    def _():
        xb_ref[...] = x_ref[...].astype(BF16)
        acc_ref[...] = jnp.zeros_like(acc_ref)

    xb = xb_ref[...]
    gate = _dot(xb, wg_ref[...])
    up = _dot(xb, wu_ref[...])
    act = (_silu(gate) * up).astype(BF16)
    acc_ref[...] += _dot(act, wo_ref[...])

    @pl.when(j == pl.num_programs(1) - 1)
    def _():
        y = _layer_norm(ALPHA * x_ref[...] + 0.5 * acc_ref[...], g_ref[...], b_ref[...])
        o_ref[...] = y
        if emit_bf16:
            ob_ref[...] = y.astype(BF16)


def ffn_ln(x, wi_bf, wo_bf, g, b, *, emit_bf16, tm=512, tf=1408):
    m, d = x.shape
    dff = wo_bf.shape[0]
    tm = min(tm, m)
    nf = dff // tf
    assert m % tm == 0 and dff % tf == 0
    out_shape = [jax.ShapeDtypeStruct((m, d), F32)]
    out_specs = [pl.BlockSpec((tm, d), lambda i, j: (i, 0))]
    if emit_bf16:
        out_shape.append(jax.ShapeDtypeStruct((m, d), BF16))
        out_specs.append(pl.BlockSpec((tm, d), lambda i, j: (i, 0)))
    res = pl.pallas_call(
        functools.partial(_ffn_ln_body, emit_bf16=emit_bf16),
        grid=(m // tm, nf),
        in_specs=[
            pl.BlockSpec((tm, d), lambda i, j: (i, 0)),
            pl.BlockSpec((d, tf), lambda i, j: (0, j)),
            pl.BlockSpec((d, tf), lambda i, j: (0, j + nf)),
            pl.BlockSpec((tf, d), lambda i, j: (j, 0)),
            pl.BlockSpec((1, d), lambda i, j: (0, 0)),
            pl.BlockSpec((1, d), lambda i, j: (0, 0)),
        ],
        out_specs=out_specs,
        out_shape=out_shape,
        scratch_shapes=[pltpu.VMEM((tm, d), F32), pltpu.VMEM((tm, d), BF16)],
        compiler_params=_cparams(("parallel", "arbitrary")),
        name="ffn_ln",
    )(x, wi_bf, wi_bf, wo_bf, g, b)
    return res if emit_bf16 else res[0]


def _matmul_body(x_ref, w_ref, o_ref):
    o_ref[...] = _dot(x_ref[...], w_ref[...])


def matmul(x_bf, w_bf, *, tm, tn):
    m, k = x_bf.shape
    n = w_bf.shape[1]
    tm = min(tm, m)
    assert m % tm == 0 and n % tn == 0
    return pl.pallas_call(
        _matmul_body,
        grid=(m // tm, n // tn),
        in_specs=[pl.BlockSpec((tm, k), lambda i, j: (i, 0)),
                  pl.BlockSpec((k, tn), lambda i, j: (0, j))],
        out_specs=pl.BlockSpec((tm, tn), lambda i, j: (i, j)),
        out_shape=jax.ShapeDtypeStruct((m, n), F32),
        compiler_params=_cparams(("parallel", "arbitrary")),
        name="in_proj",
    )(x_bf, w_bf)


def pack_w_in(w_in):
    offs = np.cumsum((0,) + IN_SPLITS)
    z, xbc, dt, q, k, v, qi, ki, wi, gates = [w_in[:, offs[i]:offs[i + 1]] for i in range(len(IN_SPLITS))]
    d = w_in.shape[0]
    zeros = lambda n: jnp.zeros((d, n), w_in.dtype)
    small = jnp.concatenate([dt, wi, zeros(LANES - SSM_HEADS - IDX_HEADS)], axis=1)
    kib = jnp.concatenate([ki, ki], axis=1)
    packed = jnp.concatenate([z, xbc[:, :D_INNER], gates, xbc[:, D_INNER:], q, k, v, qi, small, kib,
                              zeros(PROJ_N - COL_KI - LANES)], axis=1)
    return packed.astype(BF16)


def _ssd_prompt_body(z_ref, xr_ref, bc_ref, sm_ref, cwx_ref, cwbc_ref, cbx_ref, cbbc_ref,
                     dtb_ref, alog_ref, dskip_ref, ng_ref,
                     y_ref, cs_ref, hs_ref,
                     xbuf, bcbuf, h_scr, cum_b, w_b, xc_scr, bcc_scr, y_scr, cb_scr, cum_t, dt_t):
    c = pl.program_id(1)
    nc = pl.num_programs(1)
    L = SSD_CHUNK
    halo = SUBLANES
    n_pairs = SSM_HEADS // 2
    pairs_per_group = n_pairs // SSM_GROUPS

    @pl.when(c == 0)
    def _():
        xbuf[0:halo, :] = jnp.zeros((halo, D_INNER), F32)
        bcbuf[0:halo, :] = jnp.zeros((halo, 2 * GN), F32)
        h_scr[...] = jnp.zeros_like(h_scr)

    xbuf[halo:halo + L, :] = xr_ref[...]
    bcbuf[halo:halo + L, :] = bc_ref[...]

    def conv(buf, w_ref, b_ref):
        full = buf[0:halo + L, :]
        acc = b_ref[...] + w_ref[CONV_W - 1:CONV_W, :] * full[halo:halo + L]
        for k in range(CONV_W - 1):
            shifted = pltpu.roll(full, CONV_W - 1 - k, axis=0)[halo:halo + L]
            acc = acc + w_ref[k:k + 1, :] * shifted
        return _silu(acc)

    xc_scr[...] = conv(xbuf, cwx_ref, cbx_ref)
    bcc_scr[...] = conv(bcbuf, cwbc_ref, cbbc_ref)
    xbuf[0:halo, :] = xbuf[L:L + halo, :]
    bcbuf[0:halo, :] = bcbuf[L:L + halo, :]

    dt = _softplus(sm_ref[...] + dtb_ref[...])
    a = -jnp.exp(alog_ref[...])
    row = lax.broadcasted_iota(I32, (L, L), 0)
    col = lax.broadcasted_iota(I32, (L, L), 1)
    tril = (row >= col).astype(F32)
    cum = jnp.dot(tril, dt * a, precision=lax.Precision.HIGHEST, preferred_element_type=F32)
    w = jnp.exp(cum[L - 1:L, :] - cum) * dt
    cum_t[...] = cum.T
    dt_t[...] = dt.T
    for h in range(SSM_HEADS):
        cum_b[h] = jnp.broadcast_to(cum[:, h:h + 1], (L, LANES))
        w_b[h] = jnp.broadcast_to(w[:, h:h + 1], (L, LANES))
    for g in range(SSM_GROUPS):
        bg = bcc_scr[:, g * SSM_STATE:(g + 1) * SSM_STATE].astype(BF16)
        cg = bcc_scr[:, GN + g * SSM_STATE:GN + (g + 1) * SSM_STATE].astype(BF16)
        cb_scr[g] = _dot_nt(cg, bg)

    causal = row >= col
    lane = lax.broadcasted_iota(I32, (L, LANES), 1)
    lo_half = lane < SSM_HEAD_DIM
    sub = lax.broadcasted_iota(I32, (2 * SSM_HEAD_DIM, SSM_STATE), 0)
    lo_rows = sub < SSM_HEAD_DIM

    def pair_body(i, carry):
        g = i // pairs_per_group
        h0 = 2 * i
        h1 = h0 + 1
        off = pl.multiple_of(i * LANES, LANES)
        boff = pl.multiple_of(g * SSM_STATE, SSM_STATE)
        xp = xc_scr[:, pl.ds(off, LANES)]
        bg = bcc_scr[:, pl.ds(boff, SSM_STATE)].astype(BF16)
        cg = bcc_scr[:, pl.ds(GN + boff, SSM_STATE)].astype(BF16)
        cb = cb_scr[g]
        c0 = cum_b[h0]
        c1 = cum_b[h1]

        def mix(ct, h):
            seg = jnp.exp(jnp.where(causal, ct - cum_t[pl.ds(h, 1), :], NEG_INF))
            return (cb * seg * dt_t[pl.ds(h, 1), :]).astype(BF16)

        zero = jnp.zeros_like(xp)
        yd = (_dot(mix(c0, h0), jnp.where(lo_half, xp, zero).astype(BF16))
              + _dot(mix(c1, h1), jnp.where(lo_half, zero, xp).astype(BF16)))
        hp = h_scr[i]
        yo = _dot_nt(cg, hp.astype(BF16)) * jnp.where(lo_half, jnp.exp(c0), jnp.exp(c1))
        y_scr[:, pl.ds(off, LANES)] = yd + yo + dskip_ref[:, pl.ds(off, LANES)] * xp
        xw = (xp * jnp.where(lo_half, w_b[h0], w_b[h1])).astype(BF16)
        st = _dot_tn(xw, bg)
        dec = jnp.where(lo_rows, jnp.exp(c0[L - 1:L, :]), jnp.exp(c1[L - 1:L, :]))
        h_scr[i] = dec * hp + st
        return carry

    lax.fori_loop(0, n_pairs, pair_body, 0, unroll=2)

    gw = D_INNER // SSM_GROUPS
    for g in range(SSM_GROUPS):
        sl = slice(g * gw, (g + 1) * gw)
        u = y_scr[:, sl] * _silu(z_ref[:, sl])
        ms = jnp.mean(u * u, axis=-1, keepdims=True)
        y_ref[:, sl] = (u * lax.rsqrt(ms + LN_EPS) * ng_ref[:, sl]).astype(y_ref.dtype)

    @pl.when(c == nc - 1)
    def _():
        cs_ref[0, :, 0:D_INNER] = xr_ref[L - (CONV_W - 1):L, :]
        cs_ref[0, :, D_INNER:CONV_DIM] = bc_ref[L - (CONV_W - 1):L, :]
        for i in range(n_pairs):
            hs_ref[0, i * LANES:(i + 1) * LANES, :] = h_scr[i]


def ssd_prompt(proj, batch, seq, conv_w, conv_b, dtb_pad, alog_pad, dskip_e, norm_g):
    L = SSD_CHUNK
    nc = seq // L
    assert seq % L == 0
    blk = lambda width, colblk: pl.BlockSpec((L, width), lambda b, c: (b * nc + c, colblk))
    full = lambda shape: pl.BlockSpec(shape, lambda b, c: tuple(0 for _ in shape))
    n_pairs = SSM_HEADS // 2
    return pl.pallas_call(
        _ssd_prompt_body,
        grid=(batch, nc),
        in_specs=[
            blk(D_INNER, COL_Z // D_INNER),
            blk(D_INNER, COL_X // D_INNER),
            blk(2 * GN, COL_BC // (2 * GN)),
            blk(LANES, COL_SM // LANES),
            full((CONV_W, D_INNER)), full((CONV_W, 2 * GN)), full((1, D_INNER)), full((1, 2 * GN)),
            full((1, LANES)), full((1, LANES)), full((1, D_INNER)), full((1, D_INNER)),
        ],
        out_specs=[
            pl.BlockSpec((L, D_INNER), lambda b, c: (b * nc + c, 0)),
            pl.BlockSpec((1, CONV_W - 1, CONV_DIM), lambda b, c: (b, 0, 0)),
            pl.BlockSpec((1, SSM_HEADS * SSM_HEAD_DIM, SSM_STATE), lambda b, c: (b, 0, 0)),
        ],
        out_shape=[
            jax.ShapeDtypeStruct((batch * seq, D_INNER), BF16),
            jax.ShapeDtypeStruct((batch, CONV_W - 1, CONV_DIM), F32),
            jax.ShapeDtypeStruct((batch, SSM_HEADS * SSM_HEAD_DIM, SSM_STATE), F32),
        ],
        scratch_shapes=[
            pltpu.VMEM((L + 2 * SUBLANES, D_INNER), F32),
            pltpu.VMEM((L + 2 * SUBLANES, 2 * GN), F32),
            pltpu.VMEM((n_pairs, 2 * SSM_HEAD_DIM, SSM_STATE), F32),
            pltpu.VMEM((SSM_HEADS, L, LANES), F32),
            pltpu.VMEM((SSM_HEADS, L, LANES), F32),
            pltpu.VMEM((L, D_INNER), F32),
            pltpu.VMEM((L, 2 * GN), F32),
            pltpu.VMEM((L, D_INNER), F32),
            pltpu.VMEM((SSM_GROUPS, L, L), F32),
            pltpu.VMEM((LANES, L), F32),
            pltpu.VMEM((LANES, L), F32),
        ],
        compiler_params=_cparams(("parallel", "arbitrary")),
        name="ssd_prompt",
    )(proj, proj, proj, proj, conv_w[:, :D_INNER], conv_w[:, D_INNER:], conv_b[:, :D_INNER], conv_b[:, D_INNER:],
      dtb_pad, alog_pad, dskip_e, norm_g)


def _ssd_sample_body(u_ref, z_ref, sm_ref, h0_ref, cw_ref, cb_ref, dtb_ref, alog_ref, dskip_ref, ng_ref, exp_ref,
                     y_ref, cs_ref, hn_ref, *, t_new):
    T = t_new
    n_pairs = SSM_HEADS // 2
    pairs_per_group = n_pairs // SSM_GROUPS
    acc = cb_ref[...] + cw_ref[0:1, :] * u_ref[0, 0:T, :]
    for k in range(1, CONV_W):
        acc = acc + cw_ref[k:k + 1, :] * u_ref[0, k:k + T, :]
    xbc = _silu(acc)
    cs_ref[0] = u_ref[0, T:T + CONV_W - 1, :]
    x = xbc[:, :D_INNER]

    dt = _softplus(sm_ref[0] + dtb_ref[...])
    da = dt * (-jnp.exp(alog_ref[...]))
    rows = [da[0:1, :]]
    for t in range(1, T):
        rows.append(rows[-1] + da[t:t + 1, :])
    cum = jnp.concatenate(rows, axis=0)
    both = _dot_exact_sel(jnp.concatenate([cum, dt], axis=0), exp_ref[...])
    cum_e = both[0:T, :]
    dt_e = both[T:2 * T, :]
    t_idx = lax.broadcasted_iota(I32, (T, GN), 0)

    gw = D_INNER // SSM_GROUPS
    y_parts = []
    for g in range(SSM_GROUPS):
        sl = slice(g * gw, (g + 1) * gw)
        bg = xbc[:, D_INNER + g * SSM_STATE:D_INNER + (g + 1) * SSM_STATE]
        cg = xbc[:, D_INNER + GN + g * SSM_STATE:D_INNER + GN + (g + 1) * SSM_STATE]
        cbg = _dot_nt(cg.astype(BF16), bg.astype(BF16))
        yg = dskip_ref[:, sl] * x[:, sl]
        for s in range(T):
            seg = jnp.exp(jnp.where(t_idx >= s, cum_e[:, sl] - cum_e[s:s + 1, sl], NEG_INF))
            coef = seg * dt_e[s:s + 1, sl] * jnp.broadcast_to(cbg[:, s:s + 1], (T, gw))
            yg = yg + coef * x[s:s + 1, sl]
        y_parts.append(yg)

    xw = x * dt_e * jnp.exp(cum_e[T - 1:T, :] - cum_e)
    ecum = jnp.exp(cum_e)
    sub = lax.broadcasted_iota(I32, (2 * SSM_HEAD_DIM, SSM_STATE), 0)
    lo_rows = sub < SSM_HEAD_DIM
    yo_parts = []
    for i in range(n_pairs):
        g = i // pairs_per_group
        sl = slice(i * LANES, (i + 1) * LANES)
        bg = xbc[:, D_INNER + g * SSM_STATE:D_INNER + (g + 1) * SSM_STATE].astype(BF16)
        cg = xbc[:, D_INNER + GN + g * SSM_STATE:D_INNER + GN + (g + 1) * SSM_STATE].astype(BF16)
        hp = h0_ref[0, i * LANES:(i + 1) * LANES, :]
        yo_parts.append(_dot_nt(cg, hp.astype(BF16)) * ecum[:, sl])
        st = _dot_tn(xw[:, sl].astype(BF16), bg)
        d0 = jnp.exp(cum[T - 1:T, 2 * i:2 * i + 1])
        d1 = jnp.exp(cum[T - 1:T, 2 * i + 1:2 * i + 2])
        dec = jnp.where(lo_rows, jnp.broadcast_to(d0, sub.shape), jnp.broadcast_to(d1, sub.shape))
        hn_ref[0, i * LANES:(i + 1) * LANES, :] = dec * hp + st

    for g in range(SSM_GROUPS):
        sl = slice(g * gw, (g + 1) * gw)
        yo = jnp.concatenate(yo_parts[g * pairs_per_group:(g + 1) * pairs_per_group], axis=1)
        u = (y_parts[g] + yo) * _silu(z_ref[0, :, sl])
        ms = jnp.mean(u * u, axis=-1, keepdims=True)
        y_ref[0, :, sl] = (u * lax.rsqrt(ms + LN_EPS) * ng_ref[:, sl]).astype(y_ref.dtype)


def ssd_sample(u_cat, proj3, h0, conv_w, conv_b, dtb_pad, alog_pad, dskip_e, norm_g, expand):
    nb, t_new, _ = proj3.shape
    full = lambda shape: pl.BlockSpec(shape, lambda b: tuple(0 for _ in shape))
    hp = SSM_HEADS * SSM_HEAD_DIM
    return pl.pallas_call(
        functools.partial(_ssd_sample_body, t_new=t_new),
        grid=(nb,),
        in_specs=[
            pl.BlockSpec((1, t_new + CONV_W - 1, CONV_DIM), lambda b: (b, 0, 0)),
            pl.BlockSpec((1, t_new, D_INNER), lambda b: (b, 0, COL_Z // D_INNER)),
            pl.BlockSpec((1, t_new, LANES), lambda b: (b, 0, COL_SM // LANES)),
            pl.BlockSpec((1, hp, SSM_STATE), lambda b: (b, 0, 0)),
            full((CONV_W, CONV_DIM)), full((1, CONV_DIM)), full((1, LANES)), full((1, LANES)),
            full((1, D_INNER)), full((1, D_INNER)), full((LANES, D_INNER)),
        ],
        out_specs=[
            pl.BlockSpec((1, t_new, D_INNER), lambda b: (b, 0, 0)),
            pl.BlockSpec((1, CONV_W - 1, CONV_DIM), lambda b: (b, 0, 0)),
            pl.BlockSpec((1, hp, SSM_STATE), lambda b: (b, 0, 0)),
        ],
        out_shape=[
            jax.ShapeDtypeStruct((nb, t_new, D_INNER), BF16),
            jax.ShapeDtypeStruct((nb, CONV_W - 1, CONV_DIM), F32),
            jax.ShapeDtypeStruct((nb, hp, SSM_STATE), F32),
        ],
        compiler_params=_cparams(("parallel",)),
        name="ssd_sample",
    )(u_cat, proj3, proj3, h0, conv_w, conv_b, dtb_pad, alog_pad, dskip_e, norm_g, expand)


def _sortable_key(score):
    score = jnp.where(score == 0.0, 0.0, score)
    bits = lax.bitcast_convert_type(score, I32)
    return bits ^ ((bits >> 31) & jnp.int32(0x7FFFFFFF))


def _t5_bucket_np(rel):
    n = np.maximum(rel, 0)
    max_exact = N_BUCKETS // 2
    nf = np.maximum(n, 1).astype(np.float32)
    large = max_exact + (np.log(nf / max_exact) / math.log(MAX_DISTANCE / max_exact)
                         * (N_BUCKETS - max_exact)).astype(np.int32)
    large = np.minimum(large, N_BUCKETS - 1)
    return np.where(n < max_exact, n, large).astype(np.int32)


def _kth_largest_cols(key_ref, n_rows, k):
    def count_ge(cand):
        return _col_sum(jnp.where(key_ref[0:n_rows, :] >= cand, 1.0, 0.0))

    r0 = jnp.where(count_ge(jnp.zeros((1, LANES), I32)) >= k, jnp.int32(0), jnp.int32(INT_MIN))
    r0 = jnp.broadcast_to(r0, (1, LANES))

    def body(it, r):
        cand = r + (jnp.int32(1) << (30 - it))
        return jnp.where(count_ge(cand) >= k, cand, r)

    return lax.fori_loop(0, 31, body, r0)


def _topk_mask_cols(key_ref, msk_ref, n_rows, k, valid_fn):
    thr = _kth_largest_cols(key_ref, n_rows, k)
    n_gt = _col_sum(jnp.where(key_ref[0:n_rows, :] > thr, 1.0, 0.0))
    need = k - n_gt
    run = jnp.zeros((1, LANES), F32)
    ri = lax.broadcasted_iota(I32, (LANES, LANES), 0)
    ci = lax.broadcasted_iota(I32, (LANES, LANES), 1)
    strict = (ri > ci).astype(BF16)
    for r0 in range(0, n_rows, LANES):
        kc = key_ref[r0:r0 + LANES, :]
        eq = kc == thr
        eqf = eq.astype(F32)
        before = _dot(strict, eqf.astype(BF16)) + run
        sel = ((kc > thr) | (eq & (before < need))) & valid_fn(r0, LANES)
        msk_ref[r0:r0 + LANES, :] = jnp.where(sel, 0.0, NEG_INF)
        run = run + jnp.sum(eqf, axis=0, keepdims=True)


def _kth_largest_rows(key_ref, k):
    rows = key_ref.shape[0]
    digits = 2 ** RADIX_BITS
    n_rounds = 32 // RADIX_BITS

    def body(rnd, r):
        shift = (32 - RADIX_BITS) - RADIX_BITS * rnd
        n_ok = jnp.zeros((rows, 1), I32)
        for i in range(1, digits):
            cand = r + (jnp.int32(i) << shift)
            cnt = jnp.sum((key_ref[...] >= cand).astype(I32), axis=1, keepdims=True)
            n_ok = n_ok + (cnt >= k).astype(I32)
        return r + (n_ok << shift)

    return lax.fori_loop(0, n_rounds, body, jnp.full((rows, 1), INT_MIN, I32))


def _topk_select_rows(key_ref, sel_ref, k, valid_fn):
    n_keys = key_ref.shape[1]
    thr = _kth_largest_rows(key_ref, k)
    n_gt = jnp.sum((key_ref[...] > thr).astype(I32), axis=1, keepdims=True)
    need = (k - n_gt).astype(F32)
    run = jnp.zeros((key_ref.shape[0], 1), F32)
    ri = lax.broadcasted_iota(I32, (LANES, LANES), 0)
    ci = lax.broadcasted_iota(I32, (LANES, LANES), 1)
    strict = (ri < ci).astype(BF16)
    for l0 in range(0, n_keys, LANES):
        kc = key_ref[:, l0:l0 + LANES]
        eq = kc == thr
        eqf = eq.astype(F32)
        before = _dot(eqf.astype(BF16), strict) + run
        sel = ((kc > thr) | (eq & (before < need))) & valid_fn(l0)
        sel_ref[:, l0:l0 + LANES] = sel.astype(F32).astype(BF16)
        run = run + jnp.sum(eqf, axis=1, keepdims=True)


def _attn_prompt_body(tab_ref, q_ref, k_ref, v_ref, qi_ref, ki_ref, sm_ref, bucket_ref,
                      o_ref, key_scr, msk_scr, lg_scr, ot_scr, bias_scr, ka_scr, *, nk, topk, j0):
    b = pl.program_id(0)
    j = j0 + pl.program_id(1)
    QB = LANES

    @pl.when((b == 0) & (pl.program_id(1) == 0))
    def _():
        for d in range(2):
            bk = bucket_ref[d]
            for h in range(N_HEADS):
                acc = jnp.zeros(bk.shape, F32)
                for bb in range(N_BUCKETS):
                    acc = jnp.where(bk == bb, tab_ref[bb, h], acc)
                bias_scr[h, d] = (acc - tab_ref[N_BUCKETS - 1, h]) * LOG2E

    @pl.when(pl.program_id(1) == 0)
    def _():
        x = ki_ref[0:nk, :]
        hi, lo = _hi_lo(x)
        first = lax.broadcasted_iota(I32, (nk, LANES), 1) < IDX_DIM
        ka_scr[:, 0:LANES] = jnp.where(first, hi, lo)
        ka_scr[:, LANES:2 * LANES] = jnp.where(first, hi, jnp.zeros_like(hi))

    wt = sm_ref[...].T
    ka = ka_scr[...]
    score = jnp.zeros((nk, QB), F32)
    pad = jnp.zeros((QB, 2 * LANES - 3 * IDX_DIM), BF16)
    for h2 in range(IDX_HEADS // 2):
        blocks = []
        for h in (2 * h2, 2 * h2 + 1):
            qh, ql = _hi_lo(qi_ref[:, h * IDX_DIM:(h + 1) * IDX_DIM] * (IDX_DIM ** -0.5))
            blocks.append(jnp.concatenate([ql, qh, qh, pad], axis=1))
        s = _dot_nt(ka, jnp.concatenate(blocks, axis=0))
        for i, h in enumerate((2 * h2, 2 * h2 + 1)):
            score = score + jnp.maximum(s[:, i * QB:(i + 1) * QB], 0.0) * wt[SM_WI + h:SM_WI + h + 1, :]
    score = score * (IDX_HEADS ** -0.5)
    kpos = lax.broadcasted_iota(I32, (nk, QB), 0)
    qpos = j * QB + lax.broadcasted_iota(I32, (nk, QB), 1)
    score = jnp.where(kpos <= qpos, score, NEG_INF)
    key_scr[...] = _sortable_key(score)

    def valid(r0, rows):
        kp = r0 + lax.broadcasted_iota(I32, (rows, QB), 0)
        qp = j * QB + lax.broadcasted_iota(I32, (rows, QB), 1)
        return kp <= qp

    _topk_mask_cols(key_scr, msk_scr, nk, topk, valid)

    near = min(2 * QB, nk)
    start = pl.multiple_of(jnp.clip((j - 1) * QB, 0, nk - near), QB)
    msk_near = msk_scr[pl.ds(start, near), :]
    rows = lax.broadcasted_iota(I32, (nk, QB), 0)
    lg_scr[...] = jnp.where((rows >= start) & (rows < start + near), NEG_INF, msk_scr[...])
    first_is_diag = j == 0
    for g in range(N_KV_HEADS):
        cols = slice(g * HEAD_DIM, (g + 1) * HEAD_DIM)
        kg = k_ref[0:nk, cols]
        vg = v_ref[0:nk, cols]
        kn = k_ref[pl.ds(start, near), cols]
        vn = v_ref[pl.ds(start, near), cols]
        for r in range(Q_PER_KV):
            h = g * Q_PER_KV + r
            qh = (q_ref[:, h * HEAD_DIM:(h + 1) * HEAD_DIM] * (HEAD_DIM ** -0.5 * LOG2E)).astype(BF16)
            bias = jnp.where(first_is_diag, bias_scr[h, 0], bias_scr[h, 1])
            if near > QB:
                bias = jnp.concatenate([bias, bias_scr[h, 0]], axis=0)
            lg_n = (_dot_nt(kn, qh) + bias) + msk_near
            lg_m = _dot_nt(kg, qh) + lg_scr[...]
            m = jnp.maximum(_col_max(lg_m), _col_max(lg_n))
            p_m = jnp.exp2(lg_m - m)
            p_n = jnp.exp2(lg_n - m)
            l = _col_sum(p_m) + _col_sum(p_n)
            pv = _dot_tn(vg, p_m.astype(BF16)) + _dot_tn(vn, p_n.astype(BF16))
            ot_scr[h * HEAD_DIM:(h + 1) * HEAD_DIM, :] = pv * (1.0 / l)
    o_ref[0] = ot_scr[...].T.astype(o_ref.dtype)


def attn_prompt(proj, bias_table, batch, seq):
    QB = LANES
    nq = seq // QB
    assert seq % QB == 0
    topk = min(TOPK_MAX, seq // 4)
    ts = np.arange(QB)[:, None]
    tq = np.arange(QB)[None, :]
    bucket = jnp.asarray(np.stack([_t5_bucket_np(tq - ts), _t5_bucket_np(QB + tq - ts)]))
    n_groups = math.gcd(nq, PROMPT_KEY_GROUPS)
    per_group = nq // n_groups
    kvw = N_KV_HEADS * HEAD_DIM
    kv_bf = proj[:, COL_K:COL_K + 2 * kvw].astype(BF16)
    outs = []
    for grp in range(n_groups):
        j0 = grp * per_group
        nk = (j0 + per_group) * QB
        rowblk = lambda width, colblk: pl.BlockSpec((QB, width), lambda b, j: (b * nq + j0 + j, colblk))
        seqblk = lambda width, colblk: pl.BlockSpec((seq, width), lambda b, j: (b, colblk))
        outs.append(pl.pallas_call(
            functools.partial(_attn_prompt_body, nk=nk, topk=topk, j0=j0),
            grid=(batch, per_group),
            in_specs=[
                pl.BlockSpec(memory_space=pltpu.SMEM),
                rowblk(N_HEADS * HEAD_DIM, COL_Q // (N_HEADS * HEAD_DIM)),
                seqblk(kvw, 0),
                seqblk(kvw, 1),
                rowblk(IDX_HEADS * IDX_DIM, COL_QI // (IDX_HEADS * IDX_DIM)),
                seqblk(LANES, COL_KI // LANES),
                rowblk(LANES, COL_SM // LANES),
                pl.BlockSpec((2, QB, QB), lambda b, j: (0, 0, 0)),
            ],
            out_specs=pl.BlockSpec((1, QB, N_HEADS * HEAD_DIM), lambda b, j: (b, j, 0)),
            out_shape=jax.ShapeDtypeStruct((batch, per_group * QB, N_HEADS * HEAD_DIM), BF16),
            scratch_shapes=[
                pltpu.VMEM((nk, QB), I32),
                pltpu.VMEM((nk, QB), F32),
                pltpu.VMEM((nk, QB), F32),
                pltpu.VMEM((N_HEADS * HEAD_DIM, QB), F32),
                pltpu.VMEM((N_HEADS, 2, QB, QB), F32),
                pltpu.VMEM((nk, 2 * LANES), BF16),
            ],
            compiler_params=_cparams(("arbitrary", "arbitrary")),
            name=f"attn_prompt_g{grp}",
        )(bias_table, proj, kv_bf, kv_bf, proj, proj, proj, bucket))
    return jnp.concatenate(outs, axis=1).reshape(batch * seq, N_HEADS * HEAD_DIM)


def _attn_sample_body(pt_ref, q_ref, qi_ref, w_ref, kn_ref, vn_ref, kin_ref, tabt_ref, bucket_ref, gsum_ref, xpand_ref,
                      *rest, n_seq, n_pages, page, t_new, topk):
    n_pg = n_seq * n_pages
    ck = rest[0:n_pg]
    cv = rest[n_pg:2 * n_pg]
    cki = rest[2 * n_pg:3 * n_pg]
    o_ref, sc_scr, key_scr, sel_scr, lg_scr, p_scr, bias_scr, ki_st, k_st, v_st = rest[3 * n_pg:]
    past = n_pages * page
    n_tiles = n_pages + 1
    rows_q = t_new * N_HEADS
    rows_g = t_new * Q_PER_KV

    @pl.when(pl.program_id(0) == 0)
    def _():
        bk = bucket_ref[...]
        acc = jnp.zeros(bk.shape, F32)
        for bb in range(N_BUCKETS):
            acc = jnp.where(bk == bb, tabt_ref[:, bb:bb + 1], acc)
        bias_scr[...] = acc - tabt_ref[:, N_BUCKETS - 1:N_BUCKETS]

    for s in range(n_seq):
        for c in range(n_pages):
            sl = slice(c * page, (c + 1) * page)
            ki_st[s, :, sl] = cki[s * n_pages + c][0]
            k_st[s, :, :, sl] = ck[s * n_pages + c][0].astype(BF16)
            v_st[s, :, :, sl] = cv[s * n_pages + c][0].astype(BF16)

    for s in range(n_seq):
        qh, ql = _hi_lo(qi_ref[s])
        wcol = w_ref[s]
        gsum = gsum_ref[s]

        def idx_scores(kt_h, kt_l, dot):
            sc = (dot(qh, kt_l) + dot(ql, kt_h)) + dot(qh, kt_h)
            sc = jnp.maximum(sc * (IDX_DIM ** -0.5), 0.0) * wcol
            return _sel_dot_exact(gsum, sc) * (IDX_HEADS ** -0.5)

        main = idx_scores(*_hi_lo(ki_st[s]), _dot)
        tail = idx_scores(*_hi_lo(kin_ref[s]), _dot_nt)
        sc_scr[:, 0:past] = main if s == 0 else sc_scr[:, 0:past] + main
        sc_scr[:, past:] = tail if s == 0 else sc_scr[:, past:] + tail

    n_keys = n_tiles * page
    n_rows = n_seq * t_new

    def valid(l0):
        kp = l0 + lax.broadcasted_iota(I32, (n_rows, LANES), 1)
        row = lax.broadcasted_iota(I32, (n_rows, LANES), 0)
        tok = row
        for s in range(1, n_seq):
            tok = jnp.where(row >= s * t_new, row - s * t_new, tok)
        return kp <= past + tok

    for c in range(n_tiles):
        sl = slice(c * page, (c + 1) * page)
        key_scr[:, sl] = _sortable_key(jnp.where(valid(c * page), sc_scr[:, sl], NEG_INF))
    _topk_select_rows(key_scr, sel_scr, topk, valid)

    for s in range(n_seq):
        for g in range(N_KV_HEADS):
            rs = slice(g * rows_g, (g + 1) * rows_g)
            cols = slice(g * HEAD_DIM, (g + 1) * HEAD_DIM)
            qg = (q_ref[s, rs, :] * (HEAD_DIM ** -0.5)).astype(BF16)
            lg_scr[rs, 0:past] = _dot(qg, k_st[s, g])
            lg_scr[rs, past:] = _dot_nt(qg, kn_ref[s, :, cols].astype(BF16))
        near = slice(past - page, n_keys)
        lg_scr[:, near] = lg_scr[:, near] + bias_scr[...]
        picked = _dot(xpand_ref[s], sel_scr[...]) > 0.5
        lg = jnp.where(picked, lg_scr[...], NEG_INF)
        m = jnp.max(lg, axis=1, keepdims=True)
        p = jnp.exp(lg - m)
        p_scr[...] = (p * (1.0 / jnp.sum(p, axis=1, keepdims=True))).astype(BF16)
        for g in range(N_KV_HEADS):
            rs = slice(g * rows_g, (g + 1) * rows_g)
            cols = slice(g * HEAD_DIM, (g + 1) * HEAD_DIM)
            acc = _dot_nt(p_scr[rs, 0:past], v_st[s, g]) + _dot(p_scr[rs, past:], vn_ref[s, :, cols].astype(BF16))
            o_ref[s, rs, :] = acc.astype(o_ref.dtype)


def attn_sample(proj_s, cache_k, cache_v, cache_kidx, page_table, bias_table, t_new):
    nb, n_pages = page_table.shape
    n_pool, page = cache_k.shape[0], cache_k.shape[1]
    past = n_pages * page
    total = past + t_new
    topk = min(TOPK_MAX, total // 4)
    kvw = N_KV_HEADS * HEAD_DIM
    n_seq = SAMPLE_SEQS_PER_STEP if nb % SAMPLE_SEQS_PER_STEP == 0 else 1
    rows_q = t_new * N_HEADS
    rows_i = t_new * IDX_HEADS
    n_rows = n_seq * t_new
    n_keys = past + page
    assert page == LANES and page >= MAX_DISTANCE and t_new <= page
    ck_t = jnp.transpose(cache_k, (0, 2, 3, 1))
    cv_t = jnp.transpose(cache_v, (0, 2, 3, 1))
    cki_t = jnp.transpose(cache_kidx, (0, 2, 1))
    seg = lambda col, width: proj_s[:, col:col + width]
    q3 = seg(COL_Q, N_HEADS * HEAD_DIM).reshape(nb, t_new, N_KV_HEADS, Q_PER_KV, HEAD_DIM)
    q3 = q3.transpose(0, 2, 1, 3, 4).reshape(nb, rows_q, HEAD_DIM)
    qi3 = seg(COL_QI, IDX_HEADS * IDX_DIM).reshape(nb, rows_i, IDX_DIM)
    w3 = seg(COL_SM + SM_WI, IDX_HEADS).reshape(nb, rows_i, 1)
    pad_rows = lambda a: jnp.pad(a.reshape(nb, t_new, -1), ((0, 0), (0, page - t_new), (0, 0)))
    kn3, vn3, kin3 = pad_rows(seg(COL_K, kvw)), pad_rows(seg(COL_V, kvw)), pad_rows(seg(COL_KI, IDX_DIM))
    row = np.arange(rows_q)
    row_tok = (row // Q_PER_KV) % t_new
    row_head = (row // (t_new * Q_PER_KV)) * Q_PER_KV + row % Q_PER_KV
    tab_t = jnp.pad(bias_table.T[row_head], ((0, 0), (0, LANES - N_BUCKETS)))
    kpos = np.concatenate([past - page + np.arange(page), past + np.arange(page)])[None, :]
    bucket = jnp.asarray(_t5_bucket_np(past + row_tok[:, None] - kpos))
    gsum = np.zeros((n_seq, n_rows, rows_i), np.float32)
    xpand = np.zeros((n_seq, rows_q, n_rows), np.float32)
    for s in range(n_seq):
        gsum[s, s * t_new + np.arange(rows_i) // IDX_HEADS, np.arange(rows_i)] = 1.0
        xpand[s, row, s * t_new + row_tok] = 1.0
    per_step = lambda shape: pl.BlockSpec((n_seq,) + shape, lambda b, pt: (b,) + tuple(0 for _ in shape))
    const = lambda shape: pl.BlockSpec(shape, lambda b, pt: tuple(0 for _ in shape))

    def page_specs(shape):
        return [pl.BlockSpec((1,) + shape, lambda b, pt, s=s, c=c: (pt[b * n_seq + s, c],) + tuple(0 for _ in shape))
                for s in range(n_seq) for c in range(n_pages)]

    grid_spec = pltpu.PrefetchScalarGridSpec(
        num_scalar_prefetch=1,
        grid=(nb // n_seq,),
        in_specs=[per_step((rows_q, HEAD_DIM)), per_step((rows_i, IDX_DIM)), per_step((rows_i, 1)),
                  per_step((page, kvw)), per_step((page, kvw)), per_step((page, IDX_DIM)),
                  const((rows_q, LANES)), const((rows_q, 2 * page)),
                  const((n_seq, n_rows, rows_i)), const((n_seq, rows_q, n_rows))]
                 + page_specs((N_KV_HEADS, HEAD_DIM, page)) + page_specs((N_KV_HEADS, HEAD_DIM, page))
                 + page_specs((IDX_DIM, page)),
        out_specs=pl.BlockSpec((n_seq, rows_q, HEAD_DIM), lambda b, pt: (b, 0, 0)),
        scratch_shapes=[
            pltpu.VMEM((n_rows, n_keys), F32),
            pltpu.VMEM((n_rows, n_keys), I32),
            pltpu.VMEM((n_rows, n_keys), BF16),
            pltpu.VMEM((rows_q, n_keys), F32),
            pltpu.VMEM((rows_q, n_keys), BF16),
            pltpu.VMEM((rows_q, 2 * page), F32),
            pltpu.VMEM((n_seq, IDX_DIM, past), F32),
            pltpu.VMEM((n_seq, N_KV_HEADS, HEAD_DIM, past), BF16),
            pltpu.VMEM((n_seq, N_KV_HEADS, HEAD_DIM, past), BF16),
        ],
    )
    n_pg = n_seq * n_pages
    out = pl.pallas_call(
        functools.partial(_attn_sample_body, n_seq=n_seq, n_pages=n_pages, page=page, t_new=t_new, topk=topk),
        grid_spec=grid_spec,
        out_shape=jax.ShapeDtypeStruct((nb, rows_q, HEAD_DIM), BF16),
        compiler_params=_cparams(("arbitrary",)),
        name="attn_sample",
    )(page_table, q3, qi3, w3, kn3, vn3, kin3, tab_t, bucket, jnp.asarray(gsum, BF16), jnp.asarray(xpand, BF16),
      *([ck_t] * n_pg), *([cv_t] * n_pg), *([cki_t] * n_pg))
    out = out.reshape(nb, N_KV_HEADS, t_new, Q_PER_KV, HEAD_DIM).transpose(0, 2, 1, 3, 4)
    return out.reshape(nb * t_new, N_HEADS * HEAD_DIM)


def _merge_body(ys_ref, ya_ref, g_ref, h_ref, wbs_ref, wba_ref, wo_ref, lg_ref, lb_ref, o_ref):
    a = _dot(ys_ref[...], wbs_ref[...])
    b = _dot(ya_ref[...], wba_ref[...])
    gates = jax.nn.sigmoid(g_ref[...])
    merged = gates[:, :D_MODEL] * a + gates[:, D_MODEL:] * b
    r = ALPHA * h_ref[...] + _dot(merged.astype(BF16), wo_ref[...])
    o_ref[...] = _layer_norm(r, lg_ref[...], lb_ref[...])


def merge_out(y_ssm, y_att, proj, h1, wbs, wba, wo, g, b, *, tm=512):
    m = h1.shape[0]
    tm = min(tm, m)
    assert m % tm == 0
    row = lambda width, colblk=0: pl.BlockSpec((tm, width), lambda i: (i, colblk))
    full = lambda shape: pl.BlockSpec(shape, lambda i: tuple(0 for _ in shape))
    return pl.pallas_call(
        _merge_body,
        grid=(m // tm,),
        in_specs=[row(D_INNER), row(N_HEADS * HEAD_DIM), row(2 * D_MODEL, COL_G // (2 * D_MODEL)), row(D_MODEL),
                  full(wbs.shape), full(wba.shape), full(wo.shape), full((1, D_MODEL)), full((1, D_MODEL))],
        out_specs=row(D_MODEL),
        out_shape=jax.ShapeDtypeStruct((m, D_MODEL), F32),
        compiler_params=_cparams(("parallel",)),
        name="merge_out",
    )(y_ssm, y_att, proj, h1, wbs, wba, wo, g, b)


def kernel(x_prompt, x_sample, cache_k, cache_v, cache_kidx, state_conv, state_ssm, page_table, bias_table, w_in, conv_w, conv_b, dt_bias, a_log, d_skip, ssm_norm_g, w_branch_ssm, w_branch_attn, w_out, ffn1_wi, ffn1_wo, ffn2_wi, ffn2_wo, ln1_g, ln1_b, ln2_g, ln2_b, ln3_g, ln3_b):
    assert w_in.shape[0] == DEPTH
    batch, seq, d = x_prompt.shape
    nb, t_new, _ = x_sample.shape
    xs = (x_prompt.reshape(batch * seq, d), x_sample.reshape(nb * t_new, d))
    outs_p, outs_s = [], []
    for i in range(DEPTH):
        w_pack = pack_w_in(w_in[i])
        f1i, f1o = ffn1_wi[i].astype(BF16), ffn1_wo[i].astype(BF16)
        f2i, f2o = ffn2_wi[i].astype(BF16), ffn2_wo[i].astype(BF16)
        wbs, wba, wo = w_branch_ssm[i].astype(BF16), w_branch_attn[i].astype(BF16), w_out[i].astype(BF16)
        row = lambda v: v.reshape(1, -1)
        pad_heads = lambda v: jnp.concatenate([v, jnp.zeros((LANES - SSM_HEADS,), v.dtype)]).reshape(1, LANES)
        dtb_pad, alog_pad = pad_heads(dt_bias[i]), pad_heads(a_log[i])
        dskip_e = jnp.repeat(d_skip[i], SSM_HEAD_DIM).reshape(1, D_INNER)
        ng = row(ssm_norm_g[i])
        cw, cb = conv_w[i], row(conv_b[i])

        h1, proj = [], []
        for x in xs:
            hf, hb = ffn_ln(x, f1i, f1o, row(ln1_g[i]), row(ln1_b[i]), emit_bf16=True)
            h1.append(hf)
            proj.append(matmul(hb, w_pack, tm=2048, tn=PROJ_TN))
        proj_p, proj_s = proj

        ys_p, conv_p, ssm_p = ssd_prompt(proj_p, batch, seq, cw, cb, dtb_pad, alog_pad, dskip_e, ng)
        ya_p = attn_prompt(proj_p, bias_table, batch, seq)

        raw_s = jnp.concatenate([proj_s[:, COL_X:COL_X + D_INNER], proj_s[:, COL_BC:COL_BC + 2 * GN]], axis=1)
        u_cat = jnp.concatenate([state_conv[i], raw_s.reshape(nb, t_new, CONV_DIM)], axis=1)
        expand = np.zeros((LANES, D_INNER), np.float32)
        expand[np.arange(D_INNER) // SSM_HEAD_DIM, np.arange(D_INNER)] = 1.0
        ys_s, conv_s, ssm_s = ssd_sample(u_cat, proj_s.reshape(nb, t_new, PROJ_N),
                                         state_ssm[i].reshape(nb, SSM_HEADS * SSM_HEAD_DIM, SSM_STATE),
                                         cw, cb, dtb_pad, alog_pad, dskip_e, ng, jnp.asarray(expand, BF16))
        ya_s = attn_sample(proj_s, cache_k[i], cache_v[i], cache_kidx[i], page_table, bias_table, t_new)

        new_xs = []
        for hf, pj, ys, ya in ((h1[0], proj_p, ys_p, ya_p), (h1[1], proj_s, ys_s.reshape(nb * t_new, D_INNER), ya_s)):
            h2 = merge_out(ys, ya, pj, hf, wbs, wba, wo, row(ln2_g[i]), row(ln2_b[i]))
            new_xs.append(ffn_ln(h2, f2i, f2o, row(ln3_g[i]), row(ln3_b[i]), emit_bf16=False))
        xs = tuple(new_xs)

        kvw = N_KV_HEADS * HEAD_DIM
        shp = lambda nb_, l_: (nb_, l_, N_KV_HEADS, HEAD_DIM)
        outs_p.append((proj_p[:, COL_K:COL_K + kvw].reshape(shp(batch, seq)),
                       proj_p[:, COL_V:COL_V + kvw].reshape(shp(batch, seq)),
                       proj_p[:, COL_KI:COL_KI + IDX_DIM].reshape(batch, seq, IDX_DIM),
                       conv_p, ssm_p.reshape(batch, SSM_HEADS, SSM_HEAD_DIM, SSM_STATE)))
        outs_s.append((proj_s[:, COL_K:COL_K + kvw].reshape(shp(nb, t_new)),
                       proj_s[:, COL_V:COL_V + kvw].reshape(shp(nb, t_new)),
                       proj_s[:, COL_KI:COL_KI + IDX_DIM].reshape(nb, t_new, IDX_DIM),
                       conv_s, ssm_s.reshape(nb, SSM_HEADS, SSM_HEAD_DIM, SSM_STATE)))
    k_p, v_p, kidx_p, conv_pp, ssm_pp = [jnp.stack(a) for a in zip(*outs_p)]
    k_s, v_s, kidx_s, conv_ss, ssm_ss = [jnp.stack(a) for a in zip(*outs_s)]
    return (xs[0].reshape(batch, seq, d), xs[1].reshape(nb, t_new, d),
            k_p, v_p, kidx_p, conv_pp, ssm_pp, k_s, v_s, kidx_s, conv_ss, ssm_ss)
```

```python
import functools
import math

import numpy as np
import jax
import jax.numpy as jnp
from jax import lax
from jax.experimental import pallas as pl
from jax.experimental.pallas import tpu as pltpu

F32 = jnp.float32
BF16 = jnp.bfloat16
I32 = jnp.int32

D_MODEL = 1024
D_INNER = 2 * D_MODEL
SSM_HEAD_DIM = 64
SSM_HEADS = D_INNER // SSM_HEAD_DIM
SSM_GROUPS = 4
SSM_STATE = 128
CONV_W = 4
CONV_DIM = D_INNER + 2 * SSM_GROUPS * SSM_STATE
SSD_CHUNK = 128
N_HEADS = 16
HEAD_DIM = 64
N_KV_HEADS = 4
Q_PER_KV = N_HEADS // N_KV_HEADS
IDX_HEADS = 8
IDX_DIM = 64
TOPK_MAX = 256
N_BUCKETS = 32
MAX_DISTANCE = 128
D_FF = 2816
DEPTH = 1
ALPHA = (2 * DEPTH) ** 0.25
LN_EPS = 1e-5
IN_SPLITS = (D_INNER, CONV_DIM, SSM_HEADS, N_HEADS * HEAD_DIM, N_KV_HEADS * HEAD_DIM,
             N_KV_HEADS * HEAD_DIM, IDX_HEADS * IDX_DIM, IDX_DIM, IDX_HEADS, 2 * D_MODEL)

LANES = 128
SUBLANES = 8
VMEM_LIMIT_BYTES = 56 * 1024 * 1024

GN = SSM_GROUPS * SSM_STATE
COL_Z = 0
COL_X = COL_Z + D_INNER
COL_G = COL_X + D_INNER
COL_BC = COL_G + 2 * D_MODEL
COL_Q = COL_BC + 2 * GN
COL_K = COL_Q + N_HEADS * HEAD_DIM
COL_V = COL_K + N_KV_HEADS * HEAD_DIM
COL_QI = COL_V + N_KV_HEADS * HEAD_DIM
COL_SM = COL_QI + IDX_HEADS * IDX_DIM
COL_KI = COL_SM + LANES
PROJ_TN = 512
PROJ_N = -(-(COL_KI + LANES) // PROJ_TN) * PROJ_TN
SM_WI = SSM_HEADS

INT_MIN = -(2 ** 31)
LOG2E = math.log2(math.e)
NEG_INF = float("-inf")
PROMPT_KEY_GROUPS = 8
SAMPLE_SEQS_PER_STEP = 2
RADIX_BITS = 4


def _cparams(sem):
    return pltpu.CompilerParams(dimension_semantics=sem, vmem_limit_bytes=VMEM_LIMIT_BYTES)


def _layer_norm(r, g, b):
    mu = jnp.mean(r, axis=-1, keepdims=True)
    d = r - mu
    var = jnp.mean(d * d, axis=-1, keepdims=True)
    return d * lax.rsqrt(var + LN_EPS) * g + b


def _silu(x):
    return x * jax.nn.sigmoid(x)


def _softplus(x):
    return jnp.maximum(x, 0.0) + jnp.log1p(jnp.exp(-jnp.abs(x)))


def _dot(a, b):
    return jnp.dot(a, b, preferred_element_type=F32)


def _dot_nt(a, b):
    return lax.dot_general(a, b, (((1,), (1,)), ((), ())), preferred_element_type=F32)


def _dot_tn(a, b):
    return lax.dot_general(a, b, (((0,), (0,)), ((), ())), preferred_element_type=F32)


def _split3(v):
    hi = v.astype(BF16)
    r1 = v - hi.astype(F32)
    mid = r1.astype(BF16)
    lo = (r1 - mid.astype(F32)).astype(BF16)
    return hi, mid, lo


def _dot_exact_sel(v, sel_bf16):
    hi, mid, lo = _split3(v)
    return (_dot(hi, sel_bf16) + _dot(mid, sel_bf16)) + _dot(lo, sel_bf16)


def _sel_dot_exact(sel_bf16, v):
    hi, mid, lo = _split3(v)
    return (_dot(sel_bf16, hi) + _dot(sel_bf16, mid)) + _dot(sel_bf16, lo)


def _hi_lo(a):
    ah = a.astype(BF16)
    return ah, (a - ah.astype(F32)).astype(BF16)


def _dot_nt_x3(a, b):
    ah, al = _hi_lo(a)
    bh, bl = _hi_lo(b)
    return (_dot_nt(ah, bl) + _dot_nt(al, bh)) + _dot_nt(ah, bh)


def _dot_x3(a, b):
    ah, al = _hi_lo(a)
    bh, bl = _hi_lo(b)
    return (_dot(ah, bl) + _dot(al, bh)) + _dot(ah, bh)


def _col_reduce(x, op2, op):
    parts = []
    for r0 in range(0, x.shape[0], LANES):
        y = x[r0:r0 + LANES]
        n = y.shape[0]
        while n > SUBLANES and n % (2 * SUBLANES) == 0:
            n //= 2
            y = op2(y[:n], y[n:])
        parts.append(y)
    while len(parts) > 1:
        parts = [op2(parts[i], parts[i + 1]) if i + 1 < len(parts) else parts[i] for i in range(0, len(parts), 2)]
    return op(parts[0], axis=0, keepdims=True)


def _col_sum(x):
    return _col_reduce(x, jnp.add, jnp.sum)


def _col_max(x):
    return _col_reduce(x, jnp.maximum, jnp.max)


def _ffn_ln_body(x_ref, wg_ref, wu_ref, wo_ref, g_ref, b_ref, *rest, emit_bf16):
    if emit_bf16:
        o_ref, ob_ref, acc_ref, xb_ref = rest
    else:
        o_ref, acc_ref, xb_ref = rest
    j = pl.program_id(1)

    @pl.when(j == 0)
    def _():
        xb_ref[...] = x_ref[...].astype(BF16)
        acc_ref[...] = jnp.zeros_like(acc_ref)

    xb = xb_ref[...]
    gate = _dot(xb, wg_ref[...])
    up = _dot(xb, wu_ref[...])
    act = (_silu(gate) * up).astype(BF16)
    acc_ref[...] += _dot(act, wo_ref[...])

    @pl.when(j == pl.num_programs(1) - 1)
    def _():
        y = _layer_norm(ALPHA * x_ref[...] + 0.5 * acc_ref[...], g_ref[...], b_ref[...])
        o_ref[...] = y
        if emit_bf16:
            ob_ref[...] = y.astype(BF16)


def ffn_ln(x, wi_bf, wo_bf, g, b, *, emit_bf16, tm=512, tf=1408):
    m, d = x.shape
    dff = wo_bf.shape[0]
    tm = min(tm, m)
    nf = dff // tf
    assert m % tm == 0 and dff % tf == 0
    out_shape = [jax.ShapeDtypeStruct((m, d), F32)]
    out_specs = [pl.BlockSpec((tm, d), lambda i, j: (i, 0))]
    if emit_bf16:
        out_shape.append(jax.ShapeDtypeStruct((m, d), BF16))
        out_specs.append(pl.BlockSpec((tm, d), lambda i, j: (i, 0)))
    res = pl.pallas_call(
        functools.partial(_ffn_ln_body, emit_bf16=emit_bf16),
        grid=(m // tm, nf),
        in_specs=[
            pl.BlockSpec((tm, d), lambda i, j: (i, 0)),
            pl.BlockSpec((d, tf), lambda i, j: (0, j)),
            pl.BlockSpec((d, tf), lambda i, j: (0, j + nf)),
            pl.BlockSpec((tf, d), lambda i, j: (j, 0)),
            pl.BlockSpec((1, d), lambda i, j: (0, 0)),
            pl.BlockSpec((1, d), lambda i, j: (0, 0)),
        ],
        out_specs=out_specs,
        out_shape=out_shape,
        scratch_shapes=[pltpu.VMEM((tm, d), F32), pltpu.VMEM((tm, d), BF16)],
        compiler_params=_cparams(("parallel", "arbitrary")),
        name="ffn_ln",
    )(x, wi_bf, wi_bf, wo_bf, g, b)
    return res if emit_bf16 else res[0]


def _matmul_body(x_ref, w_ref, o_ref):
    o_ref[...] = _dot(x_ref[...], w_ref[...])


def matmul(x_bf, w_bf, *, tm, tn):
    m, k = x_bf.shape
    n = w_bf.shape[1]
    tm = min(tm, m)
    assert m % tm == 0 and n % tn == 0
    return pl.pallas_call(
        _matmul_body,
        grid=(m // tm, n // tn),
        in_specs=[pl.BlockSpec((tm, k), lambda i, j: (i, 0)),
                  pl.BlockSpec((k, tn), lambda i, j: (0, j))],
        out_specs=pl.BlockSpec((tm, tn), lambda i, j: (i, j)),
        out_shape=jax.ShapeDtypeStruct((m, n), F32),
        compiler_params=_cparams(("parallel", "arbitrary")),
        name="in_proj",
    )(x_bf, w_bf)


def pack_w_in(w_in):
    offs = np.cumsum((0,) + IN_SPLITS)
    z, xbc, dt, q, k, v, qi, ki, wi, gates = [w_in[:, offs[i]:offs[i + 1]] for i in range(len(IN_SPLITS))]
    d = w_in.shape[0]
    zeros = lambda n: jnp.zeros((d, n), w_in.dtype)
    small = jnp.concatenate([dt, wi, zeros(LANES - SSM_HEADS - IDX_HEADS)], axis=1)
    kib = jnp.concatenate([ki, ki], axis=1)
    packed = jnp.concatenate([z, xbc[:, :D_INNER], gates, xbc[:, D_INNER:], q, k, v, qi, small, kib,
                              zeros(PROJ_N - COL_KI - LANES)], axis=1)
    return packed.astype(BF16)


def _ssd_prompt_body(z_ref, xr_ref, bc_ref, sm_ref, cwx_ref, cwbc_ref, cbx_ref, cbbc_ref,
                     dtb_ref, alog_ref, dskip_ref, ng_ref,
                     y_ref, cs_ref, hs_ref,
                     xbuf, bcbuf, h_scr, cum_b, w_b, xc_scr, bcc_scr, y_scr, cb_scr, cum_t, dt_t):
    c = pl.program_id(1)
    nc = pl.num_programs(1)
    L = SSD_CHUNK
    halo = SUBLANES
    n_pairs = SSM_HEADS // 2
    pairs_per_group = n_pairs // SSM_GROUPS

    @pl.when(c == 0)
    def _():
        xbuf[0:halo, :] = jnp.zeros((halo, D_INNER), F32)
        bcbuf[0:halo, :] = jnp.zeros((halo, 2 * GN), F32)
        h_scr[...] = jnp.zeros_like(h_scr)

    xbuf[halo:halo + L, :] = xr_ref[...]
    bcbuf[halo:halo + L, :] = bc_ref[...]

    def conv(buf, w_ref, b_ref):
        full = buf[0:halo + L, :]
        acc = b_ref[...] + w_ref[CONV_W - 1:CONV_W, :] * full[halo:halo + L]
        for k in range(CONV_W - 1):
            shifted = pltpu.roll(full, CONV_W - 1 - k, axis=0)[halo:halo + L]
            acc = acc + w_ref[k:k + 1, :] * shifted
        return _silu(acc)

    xc_scr[...] = conv(xbuf, cwx_ref, cbx_ref)
    bcc_scr[...] = conv(bcbuf, cwbc_ref, cbbc_ref)
    xbuf[0:halo, :] = xbuf[L:L + halo, :]
    bcbuf[0:halo, :] = bcbuf[L:L + halo, :]

    dt = _softplus(sm_ref[...] + dtb_ref[...])
    a = -jnp.exp(alog_ref[...])
    row = lax.broadcasted_iota(I32, (L, L), 0)
    col = lax.broadcasted_iota(I32, (L, L), 1)
    tril = (row >= col).astype(F32)
    cum = jnp.dot(tril, dt * a, precision=lax.Precision.HIGHEST, preferred_element_type=F32)
    w = jnp.exp(cum[L - 1:L, :] - cum) * dt
    cum_t[...] = cum.T
    dt_t[...] = dt.T
    for h in range(SSM_HEADS):
        cum_b[h] = jnp.broadcast_to(cum[:, h:h + 1], (L, LANES))
        w_b[h] = jnp.broadcast_to(w[:, h:h + 1], (L, LANES))
    for g in range(SSM_GROUPS):
        bg = bcc_scr[:, g * SSM_STATE:(g + 1) * SSM_STATE].astype(BF16)
        cg = bcc_scr[:, GN + g * SSM_STATE:GN + (g + 1) * SSM_STATE].astype(BF16)
        cb_scr[g] = _dot_nt(cg, bg)

    causal = row >= col
    lane = lax.broadcasted_iota(I32, (L, LANES), 1)
    lo_half = lane < SSM_HEAD_DIM
    sub = lax.broadcasted_iota(I32, (2 * SSM_HEAD_DIM, SSM_STATE), 0)
    lo_rows = sub < SSM_HEAD_DIM

    def pair_body(i, carry):
        g = i // pairs_per_group
        h0 = 2 * i
        h1 = h0 + 1
        off = pl.multiple_of(i * LANES, LANES)
        boff = pl.multiple_of(g * SSM_STATE, SSM_STATE)
        xp = xc_scr[:, pl.ds(off, LANES)]
        bg = bcc_scr[:, pl.ds(boff, SSM_STATE)].astype(BF16)
        cg = bcc_scr[:, pl.ds(GN + boff, SSM_STATE)].astype(BF16)
        cb = cb_scr[g]
        c0 = cum_b[h0]
        c1 = cum_b[h1]

        def mix(ct, h):
            seg = jnp.exp(jnp.where(causal, ct - cum_t[pl.ds(h, 1), :], NEG_INF))
            return (cb * seg * dt_t[pl.ds(h, 1), :]).astype(BF16)

        zero = jnp.zeros_like(xp)
        yd = (_dot(mix(c0, h0), jnp.where(lo_half, xp, zero).astype(BF16))
              + _dot(mix(c1, h1), jnp.where(lo_half, zero, xp).astype(BF16)))
        hp = h_scr[i]
        yo = _dot_nt(cg, hp.astype(BF16)) * jnp.where(lo_half, jnp.exp(c0), jnp.exp(c1))
        y_scr[:, pl.ds(off, LANES)] = yd + yo + dskip_ref[:, pl.ds(off, LANES)] * xp
        xw = (xp * jnp.where(lo_half, w_b[h0], w_b[h1])).astype(BF16)
        st = _dot_tn(xw, bg)
        dec = jnp.where(lo_rows, jnp.exp(c0[L - 1:L, :]), jnp.exp(c1[L - 1:L, :]))
        h_scr[i] = dec * hp + st
        return carry

    lax.fori_loop(0, n_pairs, pair_body, 0, unroll=4)

    gw = D_INNER // SSM_GROUPS
    for g in range(SSM_GROUPS):
        sl = slice(g * gw, (g + 1) * gw)
        u = y_scr[:, sl] * _silu(z_ref[:, sl])
        ms = jnp.mean(u * u, axis=-1, keepdims=True)
        y_ref[:, sl] = (u * lax.rsqrt(ms + LN_EPS) * ng_ref[:, sl]).astype(y_ref.dtype)

    @pl.when(c == nc - 1)
    def _():
        cs_ref[0, :, 0:D_INNER] = xr_ref[L - (CONV_W - 1):L, :]
        cs_ref[0, :, D_INNER:CONV_DIM] = bc_ref[L - (CONV_W - 1):L, :]
        for i in range(n_pairs):
            hs_ref[0, i * LANES:(i + 1) * LANES, :] = h_scr[i]


def ssd_prompt(proj, batch, seq, conv_w, conv_b, dtb_pad, alog_pad, dskip_e, norm_g):
    L = SSD_CHUNK
    nc = seq // L
    assert seq % L == 0
    blk = lambda width, colblk: pl.BlockSpec((L, width), lambda b, c: (b * nc + c, colblk))
    full = lambda shape: pl.BlockSpec(shape, lambda b, c: tuple(0 for _ in shape))
    n_pairs = SSM_HEADS // 2
    return pl.pallas_call(
        _ssd_prompt_body,
        grid=(batch, nc),
        in_specs=[
            blk(D_INNER, COL_Z // D_INNER),
            blk(D_INNER, COL_X // D_INNER),
            blk(2 * GN, COL_BC // (2 * GN)),
            blk(LANES, COL_SM // LANES),
            full((CONV_W, D_INNER)), full((CONV_W, 2 * GN)), full((1, D_INNER)), full((1, 2 * GN)),
            full((1, LANES)), full((1, LANES)), full((1, D_INNER)), full((1, D_INNER)),
        ],
        out_specs=[
            pl.BlockSpec((L, D_INNER), lambda b, c: (b * nc + c, 0)),
            pl.BlockSpec((1, CONV_W - 1, CONV_DIM), lambda b, c: (b, 0, 0)),
            pl.BlockSpec((1, SSM_HEADS * SSM_HEAD_DIM, SSM_STATE), lambda b, c: (b, 0, 0)),
        ],
        out_shape=[
            jax.ShapeDtypeStruct((batch * seq, D_INNER), BF16),
            jax.ShapeDtypeStruct((batch, CONV_W - 1, CONV_DIM), F32),
            jax.ShapeDtypeStruct((batch, SSM_HEADS * SSM_HEAD_DIM, SSM_STATE), F32),
        ],
        scratch_shapes=[
            pltpu.VMEM((L + 2 * SUBLANES, D_INNER), F32),
            pltpu.VMEM((L + 2 * SUBLANES, 2 * GN), F32),
            pltpu.VMEM((n_pairs, 2 * SSM_HEAD_DIM, SSM_STATE), F32),
            pltpu.VMEM((SSM_HEADS, L, LANES), F32),
            pltpu.VMEM((SSM_HEADS, L, LANES), F32),
            pltpu.VMEM((L, D_INNER), F32),
            pltpu.VMEM((L, 2 * GN), F32),
            pltpu.VMEM((L, D_INNER), F32),
            pltpu.VMEM((SSM_GROUPS, L, L), F32),
            pltpu.VMEM((LANES, L), F32),
            pltpu.VMEM((LANES, L), F32),
        ],
        compiler_params=_cparams(("parallel", "arbitrary")),
        name="ssd_prompt",
    )(proj, proj, proj, proj, conv_w[:, :D_INNER], conv_w[:, D_INNER:], conv_b[:, :D_INNER], conv_b[:, D_INNER:],
      dtb_pad, alog_pad, dskip_e, norm_g)


def _ssd_sample_body(u_ref, z_ref, sm_ref, h0_ref, cw_ref, cb_ref, dtb_ref, alog_ref, dskip_ref, ng_ref, exp_ref,
                     y_ref, cs_ref, hn_ref, *, t_new):
    T = t_new
    n_pairs = SSM_HEADS // 2
    pairs_per_group = n_pairs // SSM_GROUPS
    acc = cb_ref[...] + cw_ref[0:1, :] * u_ref[0, 0:T, :]
    for k in range(1, CONV_W):
        acc = acc + cw_ref[k:k + 1, :] * u_ref[0, k:k + T, :]
    xbc = _silu(acc)
    cs_ref[0] = u_ref[0, T:T + CONV_W - 1, :]
    x = xbc[:, :D_INNER]

    dt = _softplus(sm_ref[0] + dtb_ref[...])
    da = dt * (-jnp.exp(alog_ref[...]))
    rows = [da[0:1, :]]
    for t in range(1, T):
        rows.append(rows[-1] + da[t:t + 1, :])
    cum = jnp.concatenate(rows, axis=0)
    both = _dot_exact_sel(jnp.concatenate([cum, dt], axis=0), exp_ref[...])
    cum_e = both[0:T, :]
    dt_e = both[T:2 * T, :]
    t_idx = lax.broadcasted_iota(I32, (T, GN), 0)

    gw = D_INNER // SSM_GROUPS
    y_parts = []
    for g in range(SSM_GROUPS):
        sl = slice(g * gw, (g + 1) * gw)
        bg = xbc[:, D_INNER + g * SSM_STATE:D_INNER + (g + 1) * SSM_STATE]
        cg = xbc[:, D_INNER + GN + g * SSM_STATE:D_INNER + GN + (g + 1) * SSM_STATE]
        cbg = _dot_nt(cg.astype(BF16), bg.astype(BF16))
        yg = dskip_ref[:, sl] * x[:, sl]
        for s in range(T):
            seg = jnp.exp(jnp.where(t_idx >= s, cum_e[:, sl] - cum_e[s:s + 1, sl], NEG_INF))
            coef = seg * dt_e[s:s + 1, sl] * jnp.broadcast_to(cbg[:, s:s + 1], (T, gw))
            yg = yg + coef * x[s:s + 1, sl]
        y_parts.append(yg)

    xw = x * dt_e * jnp.exp(cum_e[T - 1:T, :] - cum_e)
    ecum = jnp.exp(cum_e)
    sub = lax.broadcasted_iota(I32, (2 * SSM_HEAD_DIM, SSM_STATE), 0)
    lo_rows = sub < SSM_HEAD_DIM
    yo_parts = []
    for i in range(n_pairs):
        g = i // pairs_per_group
        sl = slice(i * LANES, (i + 1) * LANES)
        bg = xbc[:, D_INNER + g * SSM_STATE:D_INNER + (g + 1) * SSM_STATE].astype(BF16)
        cg = xbc[:, D_INNER + GN + g * SSM_STATE:D_INNER + GN + (g + 1) * SSM_STATE].astype(BF16)
        hp = h0_ref[0, i * LANES:(i + 1) * LANES, :]
        yo_parts.append(_dot_nt(cg, hp.astype(BF16)) * ecum[:, sl])
        st = _dot_tn(xw[:, sl].astype(BF16), bg)
        d0 = jnp.exp(cum[T - 1:T, 2 * i:2 * i + 1])
        d1 = jnp.exp(cum[T - 1:T, 2 * i + 1:2 * i + 2])
        dec = jnp.where(lo_rows, jnp.broadcast_to(d0, sub.shape), jnp.broadcast_to(d1, sub.shape))
        hn_ref[0, i * LANES:(i + 1) * LANES, :] = dec * hp + st

    for g in range(SSM_GROUPS):
        sl = slice(g * gw, (g + 1) * gw)
        yo = jnp.concatenate(yo_parts[g * pairs_per_group:(g + 1) * pairs_per_group], axis=1)
        u = (y_parts[g] + yo) * _silu(z_ref[0, :, sl])
        ms = jnp.mean(u * u, axis=-1, keepdims=True)
        y_ref[0, :, sl] = (u * lax.rsqrt(ms + LN_EPS) * ng_ref[:, sl]).astype(y_ref.dtype)


def ssd_sample(u_cat, proj3, h0, conv_w, conv_b, dtb_pad, alog_pad, dskip_e, norm_g, expand):
    nb, t_new, _ = proj3.shape
    full = lambda shape: pl.BlockSpec(shape, lambda b: tuple(0 for _ in shape))
    hp = SSM_HEADS * SSM_HEAD_DIM
    return pl.pallas_call(
        functools.partial(_ssd_sample_body, t_new=t_new),
        grid=(nb,),
        in_specs=[
            pl.BlockSpec((1, t_new + CONV_W - 1, CONV_DIM), lambda b: (b, 0, 0)),
            pl.BlockSpec((1, t_new, D_INNER), lambda b: (b, 0, COL_Z // D_INNER)),
            pl.BlockSpec((1, t_new, LANES), lambda b: (b, 0, COL_SM // LANES)),
            pl.BlockSpec((1, hp, SSM_STATE), lambda b: (b, 0, 0)),
            full((CONV_W, CONV_DIM)), full((1, CONV_DIM)), full((1, LANES)), full((1, LANES)),
            full((1, D_INNER)), full((1, D_INNER)), full((LANES, D_INNER)),
        ],
        out_specs=[
            pl.BlockSpec((1, t_new, D_INNER), lambda b: (b, 0, 0)),
            pl.BlockSpec((1, CONV_W - 1, CONV_DIM), lambda b: (b, 0, 0)),
            pl.BlockSpec((1, hp, SSM_STATE), lambda b: (b, 0, 0)),
        ],
        out_shape=[
            jax.ShapeDtypeStruct((nb, t_new, D_INNER), BF16),
            jax.ShapeDtypeStruct((nb, CONV_W - 1, CONV_DIM), F32),
            jax.ShapeDtypeStruct((nb, hp, SSM_STATE), F32),
        ],
        compiler_params=_cparams(("parallel",)),
        name="ssd_sample",
    )(u_cat, proj3, proj3, h0, conv_w, conv_b, dtb_pad, alog_pad, dskip_e, norm_g, expand)


def _sortable_key(score):
    score = jnp.where(score == 0.0, 0.0, score)
    bits = lax.bitcast_convert_type(score, I32)
    return bits ^ ((bits >> 31) & jnp.int32(0x7FFFFFFF))


def _t5_bucket_np(rel):
    n = np.maximum(rel, 0)
    max_exact = N_BUCKETS // 2
    nf = np.maximum(n, 1).astype(np.float32)
    large = max_exact + (np.log(nf / max_exact) / math.log(MAX_DISTANCE / max_exact)
                         * (N_BUCKETS - max_exact)).astype(np.int32)
    large = np.minimum(large, N_BUCKETS - 1)
    return np.where(n < max_exact, n, large).astype(np.int32)


def _count_cols_i16(ref, n_rows, pred):
    packed_rows = 2 * SUBLANES
    parts = []
    for r0 in range(0, n_rows, LANES):
        x = jnp.where(pred(ref[r0:r0 + LANES, :]), jnp.int16(1), jnp.int16(0))
        n = x.shape[0]
        while n > packed_rows:
            n //= 2
            x = x[:n] + x[n:]
        parts.append(x)
    while len(parts) > 1:
        parts = [parts[i] + parts[i + 1] if i + 1 < len(parts) else parts[i] for i in range(0, len(parts), 2)]
    return jnp.sum(parts[0].astype(I32), axis=0, keepdims=True)


def _kth_largest_i16(ref, n_rows, k):
    i16_min = -(2 ** 15)

    def try_cand(r, cand):
        cnt = _count_cols_i16(ref, n_rows, lambda v: v >= cand.astype(jnp.int16))
        return jnp.where(cnt >= k, cand, r)

    r0 = try_cand(jnp.full((1, LANES), i16_min, I32), jnp.zeros((1, LANES), I32))
    return lax.fori_loop(0, 15, lambda it, r: try_cand(r, r + (jnp.int32(1) << (14 - it))), r0)


def _kth_largest_cols(key_ref, hi_ref, lo_ref, n_rows, k):
    i16_min = -(2 ** 15)
    for r0 in range(0, n_rows, LANES):
        key = key_ref[r0:r0 + LANES, :]
        hi_ref[r0:r0 + LANES, :] = (key >> 16).astype(jnp.int16)
        lo_ref[r0:r0 + LANES, :] = ((key & 0xFFFF) + i16_min).astype(jnp.int16)
    t_hi = _kth_largest_i16(hi_ref, n_rows, k)
    t_hi16 = t_hi.astype(jnp.int16)
    k_lo = k - _count_cols_i16(hi_ref, n_rows, lambda v: v > t_hi16)
    for r0 in range(0, n_rows, LANES):
        sl = slice(r0, r0 + LANES)
        lo_ref[sl, :] = jnp.where(hi_ref[sl, :] == t_hi16, lo_ref[sl, :], jnp.int16(i16_min))
    t_lo = _kth_largest_i16(lo_ref, n_rows, k_lo)
    return (t_hi << 16) | ((t_lo - i16_min) & 0xFFFF)


def _topk_mask_cols(key_ref, hi_ref, lo_ref, msk_ref, n_rows, k, valid_fn):
    thr = _kth_largest_cols(key_ref, hi_ref, lo_ref, n_rows, k)
    n_gt = _col_sum(jnp.where(key_ref[0:n_rows, :] > thr, 1.0, 0.0))
    need = k - n_gt
    run = jnp.zeros((1, LANES), F32)
    ri = lax.broadcasted_iota(I32, (LANES, LANES), 0)
    ci = lax.broadcasted_iota(I32, (LANES, LANES), 1)
    strict = (ri > ci).astype(BF16)
    for r0 in range(0, n_rows, LANES):
        kc = key_ref[r0:r0 + LANES, :]
        eq = kc == thr
        eqf = eq.astype(F32)
        before = _dot(strict, eqf.astype(BF16)) + run
        sel = ((kc > thr) | (eq & (before < need))) & valid_fn(r0, LANES)
        msk_ref[r0:r0 + LANES, :] = jnp.where(sel, 0.0, NEG_INF)
        run = run + jnp.sum(eqf, axis=0, keepdims=True)


def _kth_largest_rows(key_ref, k):
    rows = key_ref.shape[0]
    digits = 2 ** RADIX_BITS
    n_rounds = 32 // RADIX_BITS

    def body(rnd, r):
        shift = (32 - RADIX_BITS) - RADIX_BITS * rnd
        n_ok = jnp.zeros((rows, 1), I32)
        for i in range(1, digits):
            cand = r + (jnp.int32(i) << shift)
            cnt = jnp.sum((key_ref[...] >= cand).astype(I32), axis=1, keepdims=True)
            n_ok = n_ok + (cnt >= k).astype(I32)
        return r + (n_ok << shift)

    return lax.fori_loop(0, n_rounds, body, jnp.full((rows, 1), INT_MIN, I32))


def _topk_select_rows(key_ref, sel_ref, k, valid_fn):
    n_keys = key_ref.shape[1]
    thr = _kth_largest_rows(key_ref, k)
    n_gt = jnp.sum((key_ref[...] > thr).astype(I32), axis=1, keepdims=True)
    need = (k - n_gt).astype(F32)
    run = jnp.zeros((key_ref.shape[0], 1), F32)
    ri = lax.broadcasted_iota(I32, (LANES, LANES), 0)
    ci = lax.broadcasted_iota(I32, (LANES, LANES), 1)
    strict = (ri < ci).astype(BF16)
    for l0 in range(0, n_keys, LANES):
        kc = key_ref[:, l0:l0 + LANES]
        eq = kc == thr
        eqf = eq.astype(F32)
        before = _dot(eqf.astype(BF16), strict) + run
        sel = ((kc > thr) | (eq & (before < need))) & valid_fn(l0)
        sel_ref[:, l0:l0 + LANES] = sel.astype(F32).astype(BF16)
        run = run + jnp.sum(eqf, axis=1, keepdims=True)


def _attn_prompt_body(tab_ref, q_ref, k_ref, v_ref, qi_ref, ki_ref, sm_ref, bucket_ref, *rest, nk, topk, j0):
    o_ref, key_scr, hi_scr, lo_scr, msk_scr, lg_scr, ot_scr, bias_scr, ka_scr = rest[-9:]
    b = pl.program_id(0)
    j = j0 + pl.program_id(1)
    QB = LANES

    @pl.when((b == 0) & (pl.program_id(1) == 0))
    def _():
        for d in range(2):
            bk = bucket_ref[d]
            for h in range(N_HEADS):
                acc = jnp.zeros(bk.shape, F32)
                for bb in range(N_BUCKETS):
                    acc = jnp.where(bk == bb, tab_ref[bb, h], acc)
                bias_scr[h, d] = (acc - tab_ref[N_BUCKETS - 1, h]) * LOG2E

    @pl.when(pl.program_id(1) == 0)
    def _():
        x = ki_ref[0:nk, :]
        hi, lo = _hi_lo(x)
        first = lax.broadcasted_iota(I32, (nk, LANES), 1) < IDX_DIM
        ka_scr[:, 0:LANES] = jnp.where(first, hi, lo)
        ka_scr[:, LANES:2 * LANES] = jnp.where(first, hi, jnp.zeros_like(hi))

    wt = sm_ref[...].T
    ka = ka_scr[...]
    score = jnp.zeros((nk, QB), F32)
    pad = jnp.zeros((QB, 2 * LANES - 3 * IDX_DIM), BF16)
    for h2 in range(IDX_HEADS // 2):
        blocks = []
        for h in (2 * h2, 2 * h2 + 1):
            qh, ql = _hi_lo(qi_ref[:, h * IDX_DIM:(h + 1) * IDX_DIM] * (IDX_DIM ** -0.5))
            blocks.append(jnp.concatenate([ql, qh, qh, pad], axis=1))
        s = _dot_nt(ka, jnp.concatenate(blocks, axis=0))
        for i, h in enumerate((2 * h2, 2 * h2 + 1)):
            score = score + jnp.maximum(s[:, i * QB:(i + 1) * QB], 0.0) * wt[SM_WI + h:SM_WI + h + 1, :]
    score = score * (IDX_HEADS ** -0.5)
    kpos = lax.broadcasted_iota(I32, (nk, QB), 0)
    qpos = j * QB + lax.broadcasted_iota(I32, (nk, QB), 1)
    score = jnp.where(kpos <= qpos, score, NEG_INF)
    key_scr[...] = _sortable_key(score)

    def valid(r0, rows):
        kp = r0 + lax.broadcasted_iota(I32, (rows, QB), 0)
        qp = j * QB + lax.broadcasted_iota(I32, (rows, QB), 1)
        return kp <= qp

    _topk_mask_cols(key_scr, hi_scr, lo_scr, msk_scr, nk, topk, valid)

    near = min(2 * QB, nk)
    start = pl.multiple_of(jnp.clip((j - 1) * QB, 0, nk - near), QB)
    msk_near = msk_scr[pl.ds(start, near), :]
    rows = lax.broadcasted_iota(I32, (nk, QB), 0)
    lg_scr[...] = jnp.where((rows >= start) & (rows < start + near), NEG_INF, msk_scr[...])
    first_is_diag = j == 0
    for g in range(N_KV_HEADS):
        kcols = slice(g * HEAD_DIM, (g + 1) * HEAD_DIM)
        vcols = slice(g * LANES, (g + 1) * LANES)
        heads = range(g * Q_PER_KV, (g + 1) * Q_PER_KV)
        qg = jnp.concatenate([(q_ref[:, h * HEAD_DIM:(h + 1) * HEAD_DIM] * (HEAD_DIM ** -0.5 * LOG2E)).astype(BF16)
                              for h in heads], axis=0)
        lg_main = _dot_nt(k_ref[0:nk, kcols], qg)
        lg_near = _dot_nt(k_ref[pl.ds(start, near), kcols], qg)
        p_main, p_near = [], []
        for r, h in enumerate(heads):
            sl = slice(r * QB, (r + 1) * QB)
            bias = jnp.where(first_is_diag, bias_scr[h, 0], bias_scr[h, 1])
            if near > QB:
                bias = jnp.concatenate([bias, bias_scr[h, 0]], axis=0)
            lg_n = (lg_near[:, sl] + bias) + msk_near
            lg_m = lg_main[:, sl] + lg_scr[...]
            m = jnp.maximum(_col_max(lg_m), _col_max(lg_n))
            p_main.append(jnp.exp2(lg_m - m).astype(BF16))
            p_near.append(jnp.exp2(lg_n - m).astype(BF16))
        pv = (_dot_tn(v_ref[0:nk, vcols], jnp.concatenate(p_main, axis=1))
              + _dot_tn(v_ref[pl.ds(start, near), vcols], jnp.concatenate(p_near, axis=1)))
        for r, h in enumerate(heads):
            sl = slice(r * QB, (r + 1) * QB)
            ot_scr[h * HEAD_DIM:(h + 1) * HEAD_DIM, :] = pv[0:HEAD_DIM, sl] * (1.0 / pv[HEAD_DIM:HEAD_DIM + 1, sl])
    o_ref[...] = ot_scr[...].T.astype(o_ref.dtype)


def attn_prompt(proj, bias_table, batch, seq):
    QB = LANES
    nq = seq // QB
    assert seq % QB == 0
    topk = min(TOPK_MAX, seq // 4)
    ts = np.arange(QB)[:, None]
    tq = np.arange(QB)[None, :]
    bucket = jnp.asarray(np.stack([_t5_bucket_np(tq - ts), _t5_bucket_np(QB + tq - ts)]))
    n_groups = math.gcd(nq, PROMPT_KEY_GROUPS)
    per_group = nq // n_groups
    kvw = N_KV_HEADS * HEAD_DIM
    k_bf = proj[:, COL_K:COL_K + kvw].astype(BF16)
    v4 = proj[:, COL_V:COL_V + kvw].astype(BF16).reshape(batch * seq, N_KV_HEADS, HEAD_DIM)
    v_aug = jnp.concatenate([v4, jnp.ones(v4.shape[:2] + (1,), BF16),
                             jnp.zeros(v4.shape[:2] + (LANES - HEAD_DIM - 1,), BF16)], axis=-1)
    v_aug = v_aug.reshape(batch * seq, N_KV_HEADS * LANES)
    out = None
    for grp in range(n_groups):
        j0 = grp * per_group
        nk = (j0 + per_group) * QB
        rowblk = lambda width, colblk: pl.BlockSpec((QB, width), lambda b, j: (b * nq + j0 + j, colblk))
        seqblk = lambda width, colblk: pl.BlockSpec((seq, width), lambda b, j: (b, colblk))
        prev = [] if out is None else [out]
        out = pl.pallas_call(
            functools.partial(_attn_prompt_body, nk=nk, topk=topk, j0=j0),
            grid=(batch, per_group),
            in_specs=[
                pl.BlockSpec(memory_space=pltpu.SMEM),
                rowblk(N_HEADS * HEAD_DIM, COL_Q // (N_HEADS * HEAD_DIM)),
                seqblk(kvw, 0),
                seqblk(N_KV_HEADS * LANES, 0),
                rowblk(IDX_HEADS * IDX_DIM, COL_QI // (IDX_HEADS * IDX_DIM)),
                seqblk(LANES, COL_KI // LANES),
                rowblk(LANES, COL_SM // LANES),
                pl.BlockSpec((2, QB, QB), lambda b, j: (0, 0, 0)),
            ] + [pl.BlockSpec(memory_space=pl.ANY) for _ in prev],
            out_specs=pl.BlockSpec((QB, N_HEADS * HEAD_DIM), lambda b, j: (b * nq + j0 + j, 0)),
            out_shape=jax.ShapeDtypeStruct((batch * seq, N_HEADS * HEAD_DIM), BF16),
            input_output_aliases={8: 0} if prev else {},
            scratch_shapes=[
                pltpu.VMEM((nk, QB), I32),
                pltpu.VMEM((nk, QB), jnp.int16),
                pltpu.VMEM((nk, QB), jnp.int16),
                pltpu.VMEM((nk, QB), F32),
                pltpu.VMEM((nk, QB), F32),
                pltpu.VMEM((N_HEADS * HEAD_DIM, QB), F32),
                pltpu.VMEM((N_HEADS, 2, QB, QB), F32),
                pltpu.VMEM((nk, 2 * LANES), BF16),
            ],
            compiler_params=_cparams(("arbitrary", "arbitrary")),
            name=f"attn_prompt_g{grp}",
        )(bias_table, proj, k_bf, v_aug, proj, proj, proj, bucket, *prev)
    return out


def _attn_sample_body(pt_ref, q_ref, qi_ref, w_ref, kn_ref, vn_ref, kin_ref, tabt_ref, bucket_ref, gsum_ref, xpand_ref,
                      *rest, n_seq, n_pages, page, t_new, topk):
    n_pg = n_seq * n_pages
    ck = rest[0:n_pg]
    cv = rest[n_pg:2 * n_pg]
    cki = rest[2 * n_pg:3 * n_pg]
    o_ref, sc_scr, key_scr, sel_scr, lg_scr, p_scr, bias_scr, ki_st, k_st, v_st = rest[3 * n_pg:]
    past = n_pages * page
    n_tiles = n_pages + 1
    rows_q = t_new * N_HEADS
    rows_g = t_new * Q_PER_KV

    @pl.when(pl.program_id(0) == 0)
    def _():
        bk = bucket_ref[...]
        acc = jnp.zeros(bk.shape, F32)
        for bb in range(N_BUCKETS):
            acc = jnp.where(bk == bb, tabt_ref[:, bb:bb + 1], acc)
        bias_scr[...] = acc - tabt_ref[:, N_BUCKETS - 1:N_BUCKETS]

    for s in range(n_seq):
        for c in range(n_pages):
            sl = slice(c * page, (c + 1) * page)
            ki_st[s, :, sl] = cki[s * n_pages + c][0]
            k_st[s, :, :, sl] = ck[s * n_pages + c][0].astype(BF16)
            v_st[s, :, :, sl] = cv[s * n_pages + c][0].astype(BF16)

    for s in range(n_seq):
        qh, ql = _hi_lo(qi_ref[s])
        wcol = w_ref[s]
        gsum = gsum_ref[s]

        def idx_scores(kt_h, kt_l, dot):
            sc = (dot(qh, kt_l) + dot(ql, kt_h)) + dot(qh, kt_h)
            sc = jnp.maximum(sc * (IDX_DIM ** -0.5), 0.0) * wcol
            return _sel_dot_exact(gsum, sc) * (IDX_HEADS ** -0.5)

        main = idx_scores(*_hi_lo(ki_st[s]), _dot)
        tail = idx_scores(*_hi_lo(kin_ref[s]), _dot_nt)
        sc_scr[:, 0:past] = main if s == 0 else sc_scr[:, 0:past] + main
        sc_scr[:, past:] = tail if s == 0 else sc_scr[:, past:] + tail

    n_keys = n_tiles * page
    n_rows = n_seq * t_new

    def valid(l0):
        kp = l0 + lax.broadcasted_iota(I32, (n_rows, LANES), 1)
        row = lax.broadcasted_iota(I32, (n_rows, LANES), 0)
        tok = row
        for s in range(1, n_seq):
            tok = jnp.where(row >= s * t_new, row - s * t_new, tok)
        return kp <= past + tok

    for c in range(n_tiles):
        sl = slice(c * page, (c + 1) * page)
        key_scr[:, sl] = _sortable_key(jnp.where(valid(c * page), sc_scr[:, sl], NEG_INF))
    _topk_select_rows(key_scr, sel_scr, topk, valid)

    for s in range(n_seq):
        for g in range(N_KV_HEADS):
            rs = slice(g * rows_g, (g + 1) * rows_g)
            cols = slice(g * HEAD_DIM, (g + 1) * HEAD_DIM)
            qg = (q_ref[s, rs, :] * (HEAD_DIM ** -0.5)).astype(BF16)
            lg_scr[rs, 0:past] = _dot(qg, k_st[s, g])
            lg_scr[rs, past:] = _dot_nt(qg, kn_ref[s, :, cols].astype(BF16))
        near = slice(past - page, n_keys)
        lg_scr[:, near] = lg_scr[:, near] + bias_scr[...]
        picked = _dot(xpand_ref[s], sel_scr[...]) > 0.5
        lg = jnp.where(picked, lg_scr[...], NEG_INF)
        m = jnp.max(lg, axis=1, keepdims=True)
        p = jnp.exp(lg - m)
        p_scr[...] = (p * (1.0 / jnp.sum(p, axis=1, keepdims=True))).astype(BF16)
        for g in range(N_KV_HEADS):
            rs = slice(g * rows_g, (g + 1) * rows_g)
            cols = slice(g * HEAD_DIM, (g + 1) * HEAD_DIM)
            acc = _dot_nt(p_scr[rs, 0:past], v_st[s, g]) + _dot(p_scr[rs, past:], vn_ref[s, :, cols].astype(BF16))
            o_ref[s, rs, :] = acc.astype(o_ref.dtype)


def attn_sample(proj_s, cache_k, cache_v, cache_kidx, page_table, bias_table, t_new):
    nb, n_pages = page_table.shape
    n_pool, page = cache_k.shape[0], cache_k.shape[1]
    past = n_pages * page
    total = past + t_new
    topk = min(TOPK_MAX, total // 4)
    kvw = N_KV_HEADS * HEAD_DIM
    n_seq = SAMPLE_SEQS_PER_STEP if nb % SAMPLE_SEQS_PER_STEP == 0 else 1
    rows_q = t_new * N_HEADS
    rows_i = t_new * IDX_HEADS
    n_rows = n_seq * t_new
    n_keys = past + page
    assert page == LANES and page >= MAX_DISTANCE and t_new <= page
    ck_t = jnp.transpose(cache_k, (0, 2, 3, 1))
    cv_t = jnp.transpose(cache_v, (0, 2, 3, 1))
    cki_t = jnp.transpose(cache_kidx, (0, 2, 1))
    seg = lambda col, width: proj_s[:, col:col + width]
    q3 = seg(COL_Q, N_HEADS * HEAD_DIM).reshape(nb, t_new, N_KV_HEADS, Q_PER_KV, HEAD_DIM)
    q3 = q3.transpose(0, 2, 1, 3, 4).reshape(nb, rows_q, HEAD_DIM)
    qi3 = seg(COL_QI, IDX_HEADS * IDX_DIM).reshape(nb, rows_i, IDX_DIM)
    w3 = seg(COL_SM + SM_WI, IDX_HEADS).reshape(nb, rows_i, 1)
    pad_rows = lambda a: jnp.pad(a.reshape(nb, t_new, -1), ((0, 0), (0, page - t_new), (0, 0)))
    kn3, vn3, kin3 = pad_rows(seg(COL_K, kvw)), pad_rows(seg(COL_V, kvw)), pad_rows(seg(COL_KI, IDX_DIM))
    row = np.arange(rows_q)
    row_tok = (row // Q_PER_KV) % t_new
    row_head = (row // (t_new * Q_PER_KV)) * Q_PER_KV + row % Q_PER_KV
    tab_t = jnp.pad(bias_table.T[row_head], ((0, 0), (0, LANES - N_BUCKETS)))
    kpos = np.concatenate([past - page + np.arange(page), past + np.arange(page)])[None, :]
    bucket = jnp.asarray(_t5_bucket_np(past + row_tok[:, None] - kpos))
    gsum = np.zeros((n_seq, n_rows, rows_i), np.float32)
    xpand = np.zeros((n_seq, rows_q, n_rows), np.float32)
    for s in range(n_seq):
        gsum[s, s * t_new + np.arange(rows_i) // IDX_HEADS, np.arange(rows_i)] = 1.0
        xpand[s, row, s * t_new + row_tok] = 1.0
    per_step = lambda shape: pl.BlockSpec((n_seq,) + shape, lambda b, pt: (b,) + tuple(0 for _ in shape))
    const = lambda shape: pl.BlockSpec(shape, lambda b, pt: tuple(0 for _ in shape))

    def page_specs(shape):
        return [pl.BlockSpec((1,) + shape, lambda b, pt, s=s, c=c: (pt[b * n_seq + s, c],) + tuple(0 for _ in shape))
                for s in range(n_seq) for c in range(n_pages)]

    grid_spec = pltpu.PrefetchScalarGridSpec(
        num_scalar_prefetch=1,
        grid=(nb // n_seq,),
        in_specs=[per_step((rows_q, HEAD_DIM)), per_step((rows_i, IDX_DIM)), per_step((rows_i, 1)),
                  per_step((page, kvw)), per_step((page, kvw)), per_step((page, IDX_DIM)),
                  const((rows_q, LANES)), const((rows_q, 2 * page)),
                  const((n_seq, n_rows, rows_i)), const((n_seq, rows_q, n_rows))]
                 + page_specs((N_KV_HEADS, HEAD_DIM, page)) + page_specs((N_KV_HEADS, HEAD_DIM, page))
                 + page_specs((IDX_DIM, page)),
        out_specs=pl.BlockSpec((n_seq, rows_q, HEAD_DIM), lambda b, pt: (b, 0, 0)),
        scratch_shapes=[
            pltpu.VMEM((n_rows, n_keys), F32),
            pltpu.VMEM((n_rows, n_keys), I32),
            pltpu.VMEM((n_rows, n_keys), BF16),
            pltpu.VMEM((rows_q, n_keys), F32),
            pltpu.VMEM((rows_q, n_keys), BF16),
            pltpu.VMEM((rows_q, 2 * page), F32),
            pltpu.VMEM((n_seq, IDX_DIM, past), F32),
            pltpu.VMEM((n_seq, N_KV_HEADS, HEAD_DIM, past), BF16),
            pltpu.VMEM((n_seq, N_KV_HEADS, HEAD_DIM, past), BF16),
        ],
    )
    n_pg = n_seq * n_pages
    out = pl.pallas_call(
        functools.partial(_attn_sample_body, n_seq=n_seq, n_pages=n_pages, page=page, t_new=t_new, topk=topk),
        grid_spec=grid_spec,
        out_shape=jax.ShapeDtypeStruct((nb, rows_q, HEAD_DIM), BF16),
        compiler_params=_cparams(("arbitrary",)),
        name="attn_sample",
    )(page_table, q3, qi3, w3, kn3, vn3, kin3, tab_t, bucket, jnp.asarray(gsum, BF16), jnp.asarray(xpand, BF16),
      *([ck_t] * n_pg), *([cv_t] * n_pg), *([cki_t] * n_pg))
    out = out.reshape(nb, N_KV_HEADS, t_new, Q_PER_KV, HEAD_DIM).transpose(0, 2, 1, 3, 4)
    return out.reshape(nb * t_new, N_HEADS * HEAD_DIM)


def _merge_body(ys_ref, ya_ref, g_ref, h_ref, wbs_ref, wba_ref, wo_ref, lg_ref, lb_ref, o_ref):
    a = _dot(ys_ref[...], wbs_ref[...])
    b = _dot(ya_ref[...], wba_ref[...])
    gates = jax.nn.sigmoid(g_ref[...])
    merged = gates[:, :D_MODEL] * a + gates[:, D_MODEL:] * b
    r = ALPHA * h_ref[...] + _dot(merged.astype(BF16), wo_ref[...])
    o_ref[...] = _layer_norm(r, lg_ref[...], lb_ref[...])


def merge_out(y_ssm, y_att, proj, h1, wbs, wba, wo, g, b, *, tm=512):
    m = h1.shape[0]
    tm = min(tm, m)
    assert m % tm == 0
    row = lambda width, colblk=0: pl.BlockSpec((tm, width), lambda i: (i, colblk))
    full = lambda shape: pl.BlockSpec(shape, lambda i: tuple(0 for _ in shape))
    return pl.pallas_call(
        _merge_body,
        grid=(m // tm,),
        in_specs=[row(D_INNER), row(N_HEADS * HEAD_DIM), row(2 * D_MODEL, COL_G // (2 * D_MODEL)), row(D_MODEL),
                  full(wbs.shape), full(wba.shape), full(wo.shape), full((1, D_MODEL)), full((1, D_MODEL))],
        out_specs=row(D_MODEL),
        out_shape=jax.ShapeDtypeStruct((m, D_MODEL), F32),
        compiler_params=_cparams(("parallel",)),
        name="merge_out",
    )(y_ssm, y_att, proj, h1, wbs, wba, wo, g, b)


def kernel(x_prompt, x_sample, cache_k, cache_v, cache_kidx, state_conv, state_ssm, page_table, bias_table, w_in, conv_w, conv_b, dt_bias, a_log, d_skip, ssm_norm_g, w_branch_ssm, w_branch_attn, w_out, ffn1_wi, ffn1_wo, ffn2_wi, ffn2_wo, ln1_g, ln1_b, ln2_g, ln2_b, ln3_g, ln3_b):
    assert w_in.shape[0] == DEPTH
    batch, seq, d = x_prompt.shape
    nb, t_new, _ = x_sample.shape
    xs = (x_prompt.reshape(batch * seq, d), x_sample.reshape(nb * t_new, d))
    outs_p, outs_s = [], []
    for i in range(DEPTH):
        w_pack = pack_w_in(w_in[i])
        f1i, f1o = ffn1_wi[i].astype(BF16), ffn1_wo[i].astype(BF16)
        f2i, f2o = ffn2_wi[i].astype(BF16), ffn2_wo[i].astype(BF16)
        wbs, wba, wo = w_branch_ssm[i].astype(BF16), w_branch_attn[i].astype(BF16), w_out[i].astype(BF16)
        row = lambda v: v.reshape(1, -1)
        pad_heads = lambda v: jnp.concatenate([v, jnp.zeros((LANES - SSM_HEADS,), v.dtype)]).reshape(1, LANES)
        dtb_pad, alog_pad = pad_heads(dt_bias[i]), pad_heads(a_log[i])
        dskip_e = jnp.repeat(d_skip[i], SSM_HEAD_DIM).reshape(1, D_INNER)
        ng = row(ssm_norm_g[i])
        cw, cb = conv_w[i], row(conv_b[i])

        h1, proj = [], []
        for x in xs:
            hf, hb = ffn_ln(x, f1i, f1o, row(ln1_g[i]), row(ln1_b[i]), emit_bf16=True)
            h1.append(hf)
            proj.append(matmul(hb, w_pack, tm=2048, tn=PROJ_TN))
        proj_p, proj_s = proj

        ys_p, conv_p, ssm_p = ssd_prompt(proj_p, batch, seq, cw, cb, dtb_pad, alog_pad, dskip_e, ng)
        ya_p = attn_prompt(proj_p, bias_table, batch, seq)

        raw_s = jnp.concatenate([proj_s[:, COL_X:COL_X + D_INNER], proj_s[:, COL_BC:COL_BC + 2 * GN]], axis=1)
        u_cat = jnp.concatenate([state_conv[i], raw_s.reshape(nb, t_new, CONV_DIM)], axis=1)
        expand = np.zeros((LANES, D_INNER), np.float32)
        expand[np.arange(D_INNER) // SSM_HEAD_DIM, np.arange(D_INNER)] = 1.0
        ys_s, conv_s, ssm_s = ssd_sample(u_cat, proj_s.reshape(nb, t_new, PROJ_N),
                                         state_ssm[i].reshape(nb, SSM_HEADS * SSM_HEAD_DIM, SSM_STATE),
                                         cw, cb, dtb_pad, alog_pad, dskip_e, ng, jnp.asarray(expand, BF16))
        ya_s = attn_sample(proj_s, cache_k[i], cache_v[i], cache_kidx[i], page_table, bias_table, t_new)

        new_xs = []
        for hf, pj, ys, ya in ((h1[0], proj_p, ys_p, ya_p), (h1[1], proj_s, ys_s.reshape(nb * t_new, D_INNER), ya_s)):
            h2 = merge_out(ys, ya, pj, hf, wbs, wba, wo, row(ln2_g[i]), row(ln2_b[i]))
            new_xs.append(ffn_ln(h2, f2i, f2o, row(ln3_g[i]), row(ln3_b[i]), emit_bf16=False))
        xs = tuple(new_xs)

        kvw = N_KV_HEADS * HEAD_DIM
        shp = lambda nb_, l_: (nb_, l_, N_KV_HEADS, HEAD_DIM)
        outs_p.append((proj_p[:, COL_K:COL_K + kvw].reshape(shp(batch, seq)),
                       proj_p[:, COL_V:COL_V + kvw].reshape(shp(batch, seq)),
                       proj_p[:, COL_KI:COL_KI + IDX_DIM].reshape(batch, seq, IDX_DIM),
                       conv_p, ssm_p.reshape(batch, SSM_HEADS, SSM_HEAD_DIM, SSM_STATE)))
        outs_s.append((proj_s[:, COL_K:COL_K + kvw].reshape(shp(nb, t_new)),
                       proj_s[:, COL_V:COL_V + kvw].reshape(shp(nb, t_new)),
                       proj_s[:, COL_KI:COL_KI + IDX_DIM].reshape(nb, t_new, IDX_DIM),
                       conv_s, ssm_s.reshape(nb, SSM_HEADS, SSM_HEAD_DIM, SSM_STATE)))
    k_p, v_p, kidx_p, conv_pp, ssm_pp = [jnp.stack(a) for a in zip(*outs_p)]
    k_s, v_s, kidx_s, conv_ss, ssm_ss = [jnp.stack(a) for a in zip(*outs_s)]
    return (xs[0].reshape(batch, seq, d), xs[1].reshape(nb, t_new, d),
            k_p, v_p, kidx_p, conv_pp, ssm_pp, k_s, v_s, kidx_s, conv_ss, ssm_ss)
```

```python
import functools
import math

import numpy as np
import jax
import jax.numpy as jnp
from jax import lax
from jax.experimental import pallas as pl
from jax.experimental.pallas import tpu as pltpu

F32 = jnp.float32
BF16 = jnp.bfloat16
I32 = jnp.int32

D_MODEL = 1024
D_INNER = 2 * D_MODEL
SSM_HEAD_DIM = 64
SSM_HEADS = D_INNER // SSM_HEAD_DIM
SSM_GROUPS = 4
SSM_STATE = 128
CONV_W = 4
CONV_DIM = D_INNER + 2 * SSM_GROUPS * SSM_STATE
SSD_CHUNK = 128
N_HEADS = 16
HEAD_DIM = 64
N_KV_HEADS = 4
Q_PER_KV = N_HEADS // N_KV_HEADS
IDX_HEADS = 8
IDX_DIM = 64
TOPK_MAX = 256
N_BUCKETS = 32
MAX_DISTANCE = 128
D_FF = 2816
DEPTH = 1
ALPHA = (2 * DEPTH) ** 0.25
LN_EPS = 1e-5
IN_SPLITS = (D_INNER, CONV_DIM, SSM_HEADS, N_HEADS * HEAD_DIM, N_KV_HEADS * HEAD_DIM,
             N_KV_HEADS * HEAD_DIM, IDX_HEADS * IDX_DIM, IDX_DIM, IDX_HEADS, 2 * D_MODEL)

LANES = 128
SUBLANES = 8
VMEM_LIMIT_BYTES = 56 * 1024 * 1024

GN = SSM_GROUPS * SSM_STATE
COL_Z = 0
COL_X = COL_Z + D_INNER
COL_G = COL_X + D_INNER
COL_BC = COL_G + 2 * D_MODEL
COL_Q = COL_BC + 2 * GN
COL_K = COL_Q + N_HEADS * HEAD_DIM
COL_V = COL_K + N_KV_HEADS * HEAD_DIM
COL_QI = COL_V + N_KV_HEADS * HEAD_DIM
COL_SM = COL_QI + IDX_HEADS * IDX_DIM
COL_KI = COL_SM + LANES
PROJ_TN = 512
PROJ_N = -(-(COL_KI + LANES) // PROJ_TN) * PROJ_TN
SM_WI = SSM_HEADS

INT_MIN = -(2 ** 31)
LOG2E = math.log2(math.e)
NEG_INF = float("-inf")
PROMPT_KEY_GROUPS = 8
SAMPLE_SEQS_PER_STEP = 4
RADIX_BITS = 4


def _cparams(sem):
    return pltpu.CompilerParams(dimension_semantics=sem, vmem_limit_bytes=VMEM_LIMIT_BYTES)


def _layer_norm(r, g, b):
    mu = jnp.mean(r, axis=-1, keepdims=True)
    d = r - mu
    var = jnp.mean(d * d, axis=-1, keepdims=True)
    return d * lax.rsqrt(var + LN_EPS) * g + b


def _silu(x):
    h = 0.5 * x
    return h + h * jnp.tanh(h)


def _softplus(x):
    return jnp.maximum(x, 0.0) + jnp.log1p(jnp.exp(-jnp.abs(x)))


def _dot(a, b):
    return jnp.dot(a, b, preferred_element_type=F32)


def _dot_nt(a, b):
    return lax.dot_general(a, b, (((1,), (1,)), ((), ())), preferred_element_type=F32)


def _dot_tn(a, b):
    return lax.dot_general(a, b, (((0,), (0,)), ((), ())), preferred_element_type=F32)


def _split3(v):
    hi = v.astype(BF16)
    r1 = v - hi.astype(F32)
    mid = r1.astype(BF16)
    lo = (r1 - mid.astype(F32)).astype(BF16)
    return hi, mid, lo


def _dot_exact_sel(v, sel_bf16):
    hi, mid, lo = _split3(v)
    return (_dot(hi, sel_bf16) + _dot(mid, sel_bf16)) + _dot(lo, sel_bf16)


def _sel_dot_exact(sel_bf16, v):
    hi, mid, lo = _split3(v)
    return (_dot(sel_bf16, hi) + _dot(sel_bf16, mid)) + _dot(sel_bf16, lo)


def _hi_lo(a):
    ah = a.astype(BF16)
    return ah, (a - ah.astype(F32)).astype(BF16)


def _dot_nt_x3(a, b):
    ah, al = _hi_lo(a)
    bh, bl = _hi_lo(b)
    return (_dot_nt(ah, bl) + _dot_nt(al, bh)) + _dot_nt(ah, bh)


def _dot_x3(a, b):
    ah, al = _hi_lo(a)
    bh, bl = _hi_lo(b)
    return (_dot(ah, bl) + _dot(al, bh)) + _dot(ah, bh)


def _col_reduce(x, op2, op):
    parts = []
    for r0 in range(0, x.shape[0], LANES):
        y = x[r0:r0 + LANES]
        n = y.shape[0]
        while n > SUBLANES and n % (2 * SUBLANES) == 0:
            n //= 2
            y = op2(y[:n], y[n:])
        parts.append(y)
    while len(parts) > 1:
        parts = [op2(parts[i], parts[i + 1]) if i + 1 < len(parts) else parts[i] for i in range(0, len(parts), 2)]
    return op(parts[0], axis=0, keepdims=True)


def _col_sum(x):
    return _col_reduce(x, jnp.add, jnp.sum)


def _col_max(x):
    return _col_reduce(x, jnp.maximum, jnp.max)


def _ffn_ln_body(x_ref, wi_ref, wo_ref, g_ref, b_ref, o_ref, *maybe_ob_ref, tf):
    x = x_ref[...]
    xb = x.astype(BF16)
    dff = wo_ref.shape[0]
    acc = None
    for c0 in range(0, dff, tf):
        gate = _dot(xb, wi_ref[:, c0:c0 + tf])
        up = _dot(xb, wi_ref[:, dff + c0:dff + c0 + tf])
        part = _dot((_silu(gate) * up).astype(BF16), wo_ref[c0:c0 + tf, :])
        acc = part if acc is None else acc + part
    y = _layer_norm(ALPHA * x + 0.5 * acc, g_ref[...], b_ref[...])
    o_ref[...] = y
    for ob_ref in maybe_ob_ref:
        ob_ref[...] = y.astype(BF16)


def ffn_ln(x, wi_bf, wo_bf, g, b, *, emit_bf16, tm=512, tf=1408):
    m, d = x.shape
    dff = wo_bf.shape[0]
    tm = min(tm, m)
    assert m % tm == 0 and dff % tf == 0
    row = pl.BlockSpec((tm, d), lambda i: (i, 0))
    resident = lambda shape: pl.BlockSpec(shape, lambda i: (0, 0), pipeline_mode=pl.Buffered(1))
    n_out = 2 if emit_bf16 else 1
    res = pl.pallas_call(
        functools.partial(_ffn_ln_body, tf=tf),
        grid=(m // tm,),
        in_specs=[row, resident(wi_bf.shape), resident(wo_bf.shape), resident((1, d)), resident((1, d))],
        out_specs=[row] * n_out,
        out_shape=[jax.ShapeDtypeStruct((m, d), F32), jax.ShapeDtypeStruct((m, d), BF16)][:n_out],
        compiler_params=_cparams(("parallel",)),
        name="ffn_ln",
    )(x, wi_bf, wo_bf, g, b)
    return res if emit_bf16 else res[0]


def _matmul_body(x_ref, w_ref, o_ref):
    o_ref[...] = _dot(x_ref[...], w_ref[...])


def matmul(x_bf, w_bf, *, tm, tn):
    m, k = x_bf.shape
    n = w_bf.shape[1]
    tm = min(tm, m)
    assert m % tm == 0 and n % tn == 0
    return pl.pallas_call(
        _matmul_body,
        grid=(m // tm, n // tn),
        in_specs=[pl.BlockSpec((tm, k), lambda i, j: (i, 0)),
                  pl.BlockSpec((k, tn), lambda i, j: (0, j))],
        out_specs=pl.BlockSpec((tm, tn), lambda i, j: (i, j)),
        out_shape=jax.ShapeDtypeStruct((m, n), F32),
        compiler_params=_cparams(("parallel", "arbitrary")),
        name="in_proj",
    )(x_bf, w_bf)


def pack_w_in(w_in):
    offs = np.cumsum((0,) + IN_SPLITS)
    z, xbc, dt, q, k, v, qi, ki, wi, gates = [w_in[:, offs[i]:offs[i + 1]] for i in range(len(IN_SPLITS))]
    d = w_in.shape[0]
    zeros = lambda n: jnp.zeros((d, n), w_in.dtype)
    small = jnp.concatenate([dt, wi, zeros(LANES - SSM_HEADS - IDX_HEADS)], axis=1)
    kib = jnp.concatenate([ki, ki], axis=1)
    packed = jnp.concatenate([z, xbc[:, :D_INNER], gates, xbc[:, D_INNER:], q, k, v, qi, small, kib,
                              zeros(PROJ_N - COL_KI - LANES)], axis=1)
    return packed.astype(BF16)


def _ssd_prompt_body(z_ref, xr_ref, bc_ref, sm_ref, cwx_ref, cwbc_ref, cbx_ref, cbbc_ref,
                     dtb_ref, alog_ref, dskip_ref, ng_ref,
                     y_ref, cs_ref, hs_ref,
                     xbuf, bcbuf, h_scr, cum_b, w_b, xc_scr, bcc_scr, y_scr, cb_scr, cum_t, dt_t):
    c = pl.program_id(1)
    nc = pl.num_programs(1)
    L = SSD_CHUNK
    halo = SUBLANES
    n_pairs = SSM_HEADS // 2
    pairs_per_group = n_pairs // SSM_GROUPS

    @pl.when(c == 0)
    def _():
        xbuf[0:halo, :] = jnp.zeros((halo, D_INNER), F32)
        bcbuf[0:halo, :] = jnp.zeros((halo, 2 * GN), F32)
        h_scr[...] = jnp.zeros_like(h_scr)

    xbuf[halo:halo + L, :] = xr_ref[...]
    bcbuf[halo:halo + L, :] = bc_ref[...]

    def conv(buf, w_ref, b_ref):
        full = buf[0:halo + L, :]
        acc = b_ref[...] + w_ref[CONV_W - 1:CONV_W, :] * full[halo:halo + L]
        for k in range(CONV_W - 1):
            shifted = pltpu.roll(full, CONV_W - 1 - k, axis=0)[halo:halo + L]
            acc = acc + w_ref[k:k + 1, :] * shifted
        return _silu(acc)

    xc_scr[...] = conv(xbuf, cwx_ref, cbx_ref)
    bcc_scr[...] = conv(bcbuf, cwbc_ref, cbbc_ref)
    xbuf[0:halo, :] = xbuf[L:L + halo, :]
    bcbuf[0:halo, :] = bcbuf[L:L + halo, :]

    dt = _softplus(sm_ref[...] + dtb_ref[...])
    a = -jnp.exp(alog_ref[...])
    row = lax.broadcasted_iota(I32, (L, L), 0)
    col = lax.broadcasted_iota(I32, (L, L), 1)
    tril = (row >= col).astype(F32)
    cum = jnp.dot(tril, dt * a, precision=lax.Precision.HIGHEST, preferred_element_type=F32)
    w = jnp.exp(cum[L - 1:L, :] - cum) * dt
    cum_t[...] = cum.T
    dt_t[...] = dt.T
    for h in range(SSM_HEADS):
        cum_b[h] = jnp.broadcast_to(cum[:, h:h + 1], (L, LANES))
        w_b[h] = jnp.broadcast_to(w[:, h:h + 1], (L, LANES))
    for g in range(SSM_GROUPS):
        bg = bcc_scr[:, g * SSM_STATE:(g + 1) * SSM_STATE].astype(BF16)
        cg = bcc_scr[:, GN + g * SSM_STATE:GN + (g + 1) * SSM_STATE].astype(BF16)
        cb_scr[g] = _dot_nt(cg, bg)

    causal = row >= col
    lane = lax.broadcasted_iota(I32, (L, LANES), 1)
    lo_half = lane < SSM_HEAD_DIM
    sub = lax.broadcasted_iota(I32, (2 * SSM_HEAD_DIM, SSM_STATE), 0)
    lo_rows = sub < SSM_HEAD_DIM

    def pair_body(i, carry):
        g = i // pairs_per_group
        h0 = 2 * i
        h1 = h0 + 1
        off = pl.multiple_of(i * LANES, LANES)
        boff = pl.multiple_of(g * SSM_STATE, SSM_STATE)
        xp = xc_scr[:, pl.ds(off, LANES)]
        bg = bcc_scr[:, pl.ds(boff, SSM_STATE)].astype(BF16)
        cg = bcc_scr[:, pl.ds(GN + boff, SSM_STATE)].astype(BF16)
        cb = cb_scr[g]
        c0 = cum_b[h0]
        c1 = cum_b[h1]

        def mix(ct, h):
            seg = jnp.exp(jnp.where(causal, ct - cum_t[pl.ds(h, 1), :], NEG_INF))
            return (cb * seg * dt_t[pl.ds(h, 1), :]).astype(BF16)

        zero = jnp.zeros_like(xp)
        yd = (_dot(mix(c0, h0), jnp.where(lo_half, xp, zero).astype(BF16))
              + _dot(mix(c1, h1), jnp.where(lo_half, zero, xp).astype(BF16)))
        hp = h_scr[i]
        yo = _dot_nt(cg, hp.astype(BF16)) * jnp.where(lo_half, jnp.exp(c0), jnp.exp(c1))
        y_scr[:, pl.ds(off, LANES)] = yd + yo + dskip_ref[:, pl.ds(off, LANES)] * xp
        xw = (xp * jnp.where(lo_half, w_b[h0], w_b[h1])).astype(BF16)
        st = _dot_tn(xw, bg)
        dec = jnp.where(lo_rows, jnp.exp(c0[L - 1:L, :]), jnp.exp(c1[L - 1:L, :]))
        h_scr[i] = dec * hp + st
        return carry

    lax.fori_loop(0, n_pairs, pair_body, 0, unroll=4)

    gw = D_INNER // SSM_GROUPS
    for g in range(SSM_GROUPS):
        sl = slice(g * gw, (g + 1) * gw)
        u = y_scr[:, sl] * _silu(z_ref[:, sl])
        ms = jnp.mean(u * u, axis=-1, keepdims=True)
        y_ref[:, sl] = (u * lax.rsqrt(ms + LN_EPS) * ng_ref[:, sl]).astype(y_ref.dtype)

    @pl.when(c == nc - 1)
    def _():
        cs_ref[0, :, 0:D_INNER] = xr_ref[L - (CONV_W - 1):L, :]
        cs_ref[0, :, D_INNER:CONV_DIM] = bc_ref[L - (CONV_W - 1):L, :]
        for i in range(n_pairs):
            hs_ref[0, i * LANES:(i + 1) * LANES, :] = h_scr[i]


def ssd_prompt(proj, batch, seq, conv_w, conv_b, dtb_pad, alog_pad, dskip_e, norm_g):
    L = SSD_CHUNK
    nc = seq // L
    assert seq % L == 0
    blk = lambda width, colblk: pl.BlockSpec((L, width), lambda b, c: (b * nc + c, colblk))
    full = lambda shape: pl.BlockSpec(shape, lambda b, c: tuple(0 for _ in shape))
    n_pairs = SSM_HEADS // 2
    return pl.pallas_call(
        _ssd_prompt_body,
        grid=(batch, nc),
        in_specs=[
            blk(D_INNER, COL_Z // D_INNER),
            blk(D_INNER, COL_X // D_INNER),
            blk(2 * GN, COL_BC // (2 * GN)),
            blk(LANES, COL_SM // LANES),
            full((CONV_W, D_INNER)), full((CONV_W, 2 * GN)), full((1, D_INNER)), full((1, 2 * GN)),
            full((1, LANES)), full((1, LANES)), full((1, D_INNER)), full((1, D_INNER)),
        ],
        out_specs=[
            pl.BlockSpec((L, D_INNER), lambda b, c: (b * nc + c, 0)),
            pl.BlockSpec((1, CONV_W - 1, CONV_DIM), lambda b, c: (b, 0, 0)),
            pl.BlockSpec((1, SSM_HEADS * SSM_HEAD_DIM, SSM_STATE), lambda b, c: (b, 0, 0)),
        ],
        out_shape=[
            jax.ShapeDtypeStruct((batch * seq, D_INNER), BF16),
            jax.ShapeDtypeStruct((batch, CONV_W - 1, CONV_DIM), F32),
            jax.ShapeDtypeStruct((batch, SSM_HEADS * SSM_HEAD_DIM, SSM_STATE), F32),
        ],
        scratch_shapes=[
            pltpu.VMEM((L + 2 * SUBLANES, D_INNER), F32),
            pltpu.VMEM((L + 2 * SUBLANES, 2 * GN), F32),
            pltpu.VMEM((n_pairs, 2 * SSM_HEAD_DIM, SSM_STATE), F32),
            pltpu.VMEM((SSM_HEADS, L, LANES), F32),
            pltpu.VMEM((SSM_HEADS, L, LANES), F32),
            pltpu.VMEM((L, D_INNER), F32),
            pltpu.VMEM((L, 2 * GN), F32),
            pltpu.VMEM((L, D_INNER), F32),
            pltpu.VMEM((SSM_GROUPS, L, L), F32),
            pltpu.VMEM((LANES, L), F32),
            pltpu.VMEM((LANES, L), F32),
        ],
        compiler_params=_cparams(("parallel", "arbitrary")),
        name="ssd_prompt",
    )(proj, proj, proj, proj, conv_w[:, :D_INNER], conv_w[:, D_INNER:], conv_b[:, :D_INNER], conv_b[:, D_INNER:],
      dtb_pad, alog_pad, dskip_e, norm_g)


def _ssd_sample_body(u_ref, z_ref, sm_ref, h0_ref, cw_ref, cb_ref, dtb_ref, alog_ref, dskip_ref, ng_ref, exp_ref,
                     y_ref, cs_ref, hn_ref, *, t_new):
    T = t_new
    n_pairs = SSM_HEADS // 2
    pairs_per_group = n_pairs // SSM_GROUPS
    acc = cb_ref[...] + cw_ref[0:1, :] * u_ref[0, 0:T, :]
    for k in range(1, CONV_W):
        acc = acc + cw_ref[k:k + 1, :] * u_ref[0, k:k + T, :]
    xbc = _silu(acc)
    cs_ref[0] = u_ref[0, T:T + CONV_W - 1, :]
    x = xbc[:, :D_INNER]

    dt = _softplus(sm_ref[0] + dtb_ref[...])
    da = dt * (-jnp.exp(alog_ref[...]))
    rows = [da[0:1, :]]
    for t in range(1, T):
        rows.append(rows[-1] + da[t:t + 1, :])
    cum = jnp.concatenate(rows, axis=0)
    both = _dot_exact_sel(jnp.concatenate([cum, dt], axis=0), exp_ref[...])
    cum_e = both[0:T, :]
    dt_e = both[T:2 * T, :]
    t_idx = lax.broadcasted_iota(I32, (T, GN), 0)

    gw = D_INNER // SSM_GROUPS
    y_parts = []
    for g in range(SSM_GROUPS):
        sl = slice(g * gw, (g + 1) * gw)
        bg = xbc[:, D_INNER + g * SSM_STATE:D_INNER + (g + 1) * SSM_STATE]
        cg = xbc[:, D_INNER + GN + g * SSM_STATE:D_INNER + GN + (g + 1) * SSM_STATE]
        cbg = _dot_nt(cg.astype(BF16), bg.astype(BF16))
        yg = dskip_ref[:, sl] * x[:, sl]
        for s in range(T):
            seg = jnp.exp(jnp.where(t_idx >= s, cum_e[:, sl] - cum_e[s:s + 1, sl], NEG_INF))
            coef = seg * dt_e[s:s + 1, sl] * jnp.broadcast_to(cbg[:, s:s + 1], (T, gw))
            yg = yg + coef * x[s:s + 1, sl]
        y_parts.append(yg)

    xw = x * dt_e * jnp.exp(cum_e[T - 1:T, :] - cum_e)
    ecum = jnp.exp(cum_e)
    sub = lax.broadcasted_iota(I32, (2 * SSM_HEAD_DIM, SSM_STATE), 0)
    lo_rows = sub < SSM_HEAD_DIM
    yo_parts = []
    for i in range(n_pairs):
        g = i // pairs_per_group
        sl = slice(i * LANES, (i + 1) * LANES)
        bg = xbc[:, D_INNER + g * SSM_STATE:D_INNER + (g + 1) * SSM_STATE].astype(BF16)
        cg = xbc[:, D_INNER + GN + g * SSM_STATE:D_INNER + GN + (g + 1) * SSM_STATE].astype(BF16)
        hp = h0_ref[0, i * LANES:(i + 1) * LANES, :]
        yo_parts.append(_dot_nt(cg, hp.astype(BF16)) * ecum[:, sl])
        st = _dot_tn(xw[:, sl].astype(BF16), bg)
        d0 = jnp.exp(cum[T - 1:T, 2 * i:2 * i + 1])
        d1 = jnp.exp(cum[T - 1:T, 2 * i + 1:2 * i + 2])
        dec = jnp.where(lo_rows, jnp.broadcast_to(d0, sub.shape), jnp.broadcast_to(d1, sub.shape))
        hn_ref[0, i * LANES:(i + 1) * LANES, :] = dec * hp + st

    for g in range(SSM_GROUPS):
        sl = slice(g * gw, (g + 1) * gw)
        yo = jnp.concatenate(yo_parts[g * pairs_per_group:(g + 1) * pairs_per_group], axis=1)
        u = (y_parts[g] + yo) * _silu(z_ref[0, :, sl])
        ms = jnp.mean(u * u, axis=-1, keepdims=True)
        y_ref[0, :, sl] = (u * lax.rsqrt(ms + LN_EPS) * ng_ref[:, sl]).astype(y_ref.dtype)


def ssd_sample(u_cat, proj3, h0, conv_w, conv_b, dtb_pad, alog_pad, dskip_e, norm_g, expand):
    nb, t_new, _ = proj3.shape
    full = lambda shape: pl.BlockSpec(shape, lambda b: tuple(0 for _ in shape))
    hp = SSM_HEADS * SSM_HEAD_DIM
    return pl.pallas_call(
        functools.partial(_ssd_sample_body, t_new=t_new),
        grid=(nb,),
        in_specs=[
            pl.BlockSpec((1, t_new + CONV_W - 1, CONV_DIM), lambda b: (b, 0, 0)),
            pl.BlockSpec((1, t_new, D_INNER), lambda b: (b, 0, COL_Z // D_INNER)),
            pl.BlockSpec((1, t_new, LANES), lambda b: (b, 0, COL_SM // LANES)),
            pl.BlockSpec((1, hp, SSM_STATE), lambda b: (b, 0, 0)),
            full((CONV_W, CONV_DIM)), full((1, CONV_DIM)), full((1, LANES)), full((1, LANES)),
            full((1, D_INNER)), full((1, D_INNER)), full((LANES, D_INNER)),
        ],
        out_specs=[
            pl.BlockSpec((1, t_new, D_INNER), lambda b: (b, 0, 0)),
            pl.BlockSpec((1, CONV_W - 1, CONV_DIM), lambda b: (b, 0, 0)),
            pl.BlockSpec((1, hp, SSM_STATE), lambda b: (b, 0, 0)),
        ],
        out_shape=[
            jax.ShapeDtypeStruct((nb, t_new, D_INNER), BF16),
            jax.ShapeDtypeStruct((nb, CONV_W - 1, CONV_DIM), F32),
            jax.ShapeDtypeStruct((nb, hp, SSM_STATE), F32),
        ],
        compiler_params=_cparams(("parallel",)),
        name="ssd_sample",
    )(u_cat, proj3, proj3, h0, conv_w, conv_b, dtb_pad, alog_pad, dskip_e, norm_g, expand)


def _sortable_key(score):
    score = jnp.where(score == 0.0, 0.0, score)
    bits = lax.bitcast_convert_type(score, I32)
    return bits ^ ((bits >> 31) & jnp.int32(0x7FFFFFFF))


def _t5_bucket_np(rel):
    n = np.maximum(rel, 0)
    max_exact = N_BUCKETS // 2
    nf = np.maximum(n, 1).astype(np.float32)
    large = max_exact + (np.log(nf / max_exact) / math.log(MAX_DISTANCE / max_exact)
                         * (N_BUCKETS - max_exact)).astype(np.int32)
    large = np.minimum(large, N_BUCKETS - 1)
    return np.where(n < max_exact, n, large).astype(np.int32)


def _count_cols_i16(ref, n_rows, pred):
    packed_rows = 2 * SUBLANES
    parts = []
    for r0 in range(0, n_rows, LANES):
        x = jnp.where(pred(ref[r0:r0 + LANES, :]), jnp.int16(1), jnp.int16(0))
        n = x.shape[0]
        while n > packed_rows:
            n //= 2
            x = x[:n] + x[n:]
        parts.append(x)
    while len(parts) > 1:
        parts = [parts[i] + parts[i + 1] if i + 1 < len(parts) else parts[i] for i in range(0, len(parts), 2)]
    return jnp.sum(parts[0].astype(I32), axis=0, keepdims=True)


def _kth_largest_i16(ref, n_rows, k):
    i16_min = -(2 ** 15)

    def try_cand(r, cand):
        cnt = _count_cols_i16(ref, n_rows, lambda v: v >= cand.astype(jnp.int16))
        return jnp.where(cnt >= k, cand, r)

    r0 = try_cand(jnp.full((1, LANES), i16_min, I32), jnp.zeros((1, LANES), I32))
    return lax.fori_loop(0, 15, lambda it, r: try_cand(r, r + (jnp.int32(1) << (14 - it))), r0)


def _kth_largest_cols(key_ref, hi_ref, lo_ref, n_rows, k):
    i16_min = -(2 ** 15)
    for r0 in range(0, n_rows, LANES):
        key = key_ref[r0:r0 + LANES, :]
        hi_ref[r0:r0 + LANES, :] = (key >> 16).astype(jnp.int16)
        lo_ref[r0:r0 + LANES, :] = ((key & 0xFFFF) + i16_min).astype(jnp.int16)
    t_hi = _kth_largest_i16(hi_ref, n_rows, k)
    t_hi16 = t_hi.astype(jnp.int16)
    k_lo = k - _count_cols_i16(hi_ref, n_rows, lambda v: v > t_hi16)
    for r0 in range(0, n_rows, LANES):
        sl = slice(r0, r0 + LANES)
        lo_ref[sl, :] = jnp.where(hi_ref[sl, :] == t_hi16, lo_ref[sl, :], jnp.int16(i16_min))
    t_lo = _kth_largest_i16(lo_ref, n_rows, k_lo)
    return (t_hi << 16) | ((t_lo - i16_min) & 0xFFFF)


def _topk_mask_cols(key_ref, hi_ref, lo_ref, msk_ref, n_rows, k, valid_fn):
    thr = _kth_largest_cols(key_ref, hi_ref, lo_ref, n_rows, k)
    n_gt = _col_sum(jnp.where(key_ref[0:n_rows, :] > thr, 1.0, 0.0))
    need = k - n_gt
    run = jnp.zeros((1, LANES), F32)
    ri = lax.broadcasted_iota(I32, (LANES, LANES), 0)
    ci = lax.broadcasted_iota(I32, (LANES, LANES), 1)
    strict = (ri > ci).astype(BF16)
    for r0 in range(0, n_rows, LANES):
        kc = key_ref[r0:r0 + LANES, :]
        eq = kc == thr
        eqf = eq.astype(F32)
        before = _dot(strict, eqf.astype(BF16)) + run
        sel = ((kc > thr) | (eq & (before < need))) & valid_fn(r0, LANES)
        msk_ref[r0:r0 + LANES, :] = jnp.where(sel, 0.0, NEG_INF)
        run = run + jnp.sum(eqf, axis=0, keepdims=True)


def _kth_largest_rows(key_ref, k):
    rows = key_ref.shape[0]
    digits = 2 ** RADIX_BITS
    n_rounds = 32 // RADIX_BITS

    def body(rnd, r):
        shift = (32 - RADIX_BITS) - RADIX_BITS * rnd
        n_ok = jnp.zeros((rows, 1), I32)
        for i in range(1, digits):
            cand = r + (jnp.int32(i) << shift)
            cnt = jnp.sum((key_ref[...] >= cand).astype(I32), axis=1, keepdims=True)
            n_ok = n_ok + (cnt >= k).astype(I32)
        return r + (n_ok << shift)

    return lax.fori_loop(0, n_rounds, body, jnp.full((rows, 1), INT_MIN, I32))


def _topk_select_rows(key_ref, sel_ref, k, valid_fn):
    n_keys = key_ref.shape[1]
    thr = _kth_largest_rows(key_ref, k)
    n_gt = jnp.sum((key_ref[...] > thr).astype(I32), axis=1, keepdims=True)
    need = (k - n_gt).astype(F32)
    run = jnp.zeros((key_ref.shape[0], 1), F32)
    ri = lax.broadcasted_iota(I32, (LANES, LANES), 0)
    ci = lax.broadcasted_iota(I32, (LANES, LANES), 1)
    strict = (ri < ci).astype(BF16)
    for l0 in range(0, n_keys, LANES):
        kc = key_ref[:, l0:l0 + LANES]
        eq = kc == thr
        eqf = eq.astype(F32)
        before = _dot(eqf.astype(BF16), strict) + run
        sel = ((kc > thr) | (eq & (before < need))) & valid_fn(l0)
        sel_ref[:, l0:l0 + LANES] = sel.astype(F32).astype(BF16)
        run = run + jnp.sum(eqf, axis=1, keepdims=True)


def _attn_prompt_body(tab_ref, q_ref, k_ref, v_ref, qi_ref, ki_ref, sm_ref, bucket_ref, *rest, nk, topk, j0):
    o_ref, key_scr, hi_scr, lo_scr, msk_scr, lg_scr, ot_scr, bias_scr, ka_scr = rest[-9:]
    b = pl.program_id(0)
    j = j0 + pl.program_id(1)
    QB = LANES

    @pl.when((b == 0) & (pl.program_id(1) == 0))
    def _():
        for d in range(2):
            bk = bucket_ref[d]
            for h in range(N_HEADS):
                acc = jnp.zeros(bk.shape, F32)
                for bb in range(N_BUCKETS):
                    acc = jnp.where(bk == bb, tab_ref[bb, h], acc)
                bias_scr[h, d] = (acc - tab_ref[N_BUCKETS - 1, h]) * LOG2E

    @pl.when(pl.program_id(1) == 0)
    def _():
        x = ki_ref[0:nk, :]
        hi, lo = _hi_lo(x)
        first = lax.broadcasted_iota(I32, (nk, LANES), 1) < IDX_DIM
        ka_scr[:, 0:LANES] = jnp.where(first, hi, lo)
        ka_scr[:, LANES:2 * LANES] = jnp.where(first, hi, jnp.zeros_like(hi))

    wt = sm_ref[...].T
    ka = ka_scr[...]
    score = jnp.zeros((nk, QB), F32)
    pad = jnp.zeros((QB, 2 * LANES - 3 * IDX_DIM), BF16)
    for h2 in range(IDX_HEADS // 2):
        blocks = []
        for h in (2 * h2, 2 * h2 + 1):
            qh, ql = _hi_lo(qi_ref[:, h * IDX_DIM:(h + 1) * IDX_DIM] * (IDX_DIM ** -0.5))
            blocks.append(jnp.concatenate([ql, qh, qh, pad], axis=1))
        s = _dot_nt(ka, jnp.concatenate(blocks, axis=0))
        for i, h in enumerate((2 * h2, 2 * h2 + 1)):
            score = score + jnp.maximum(s[:, i * QB:(i + 1) * QB], 0.0) * wt[SM_WI + h:SM_WI + h + 1, :]
    score = score * (IDX_HEADS ** -0.5)
    kpos = lax.broadcasted_iota(I32, (nk, QB), 0)
    qpos = j * QB + lax.broadcasted_iota(I32, (nk, QB), 1)
    score = jnp.where(kpos <= qpos, score, NEG_INF)
    key_scr[...] = _sortable_key(score)

    def valid(r0, rows):
        kp = r0 + lax.broadcasted_iota(I32, (rows, QB), 0)
        qp = j * QB + lax.broadcasted_iota(I32, (rows, QB), 1)
        return kp <= qp

    _topk_mask_cols(key_scr, hi_scr, lo_scr, msk_scr, nk, topk, valid)

    near = min(2 * QB, nk)
    start = pl.multiple_of(jnp.clip((j - 1) * QB, 0, nk - near), QB)
    msk_near = msk_scr[pl.ds(start, near), :]
    rows = lax.broadcasted_iota(I32, (nk, QB), 0)
    lg_scr[...] = jnp.where((rows >= start) & (rows < start + near), NEG_INF, msk_scr[...])
    first_is_diag = j == 0
    for g in range(N_KV_HEADS):
        kcols = slice(g * HEAD_DIM, (g + 1) * HEAD_DIM)
        vcols = slice(g * LANES, (g + 1) * LANES)
        heads = range(g * Q_PER_KV, (g + 1) * Q_PER_KV)
        qg = jnp.concatenate([(q_ref[:, h * HEAD_DIM:(h + 1) * HEAD_DIM] * (HEAD_DIM ** -0.5 * LOG2E)).astype(BF16)
                              for h in heads], axis=0)
        lg_main = _dot_nt(k_ref[0:nk, kcols], qg)
        lg_near = _dot_nt(k_ref[pl.ds(start, near), kcols], qg)
        p_main, p_near = [], []
        for r, h in enumerate(heads):
            sl = slice(r * QB, (r + 1) * QB)
            bias = jnp.where(first_is_diag, bias_scr[h, 0], bias_scr[h, 1])
            if near > QB:
                bias = jnp.concatenate([bias, bias_scr[h, 0]], axis=0)
            lg_n = (lg_near[:, sl] + bias) + msk_near
            lg_m = lg_main[:, sl] + lg_scr[...]
            m = jnp.maximum(_col_max(lg_m), _col_max(lg_n))
            p_main.append(jnp.exp2(lg_m - m).astype(BF16))
            p_near.append(jnp.exp2(lg_n - m).astype(BF16))
        pv = (_dot_tn(v_ref[0:nk, vcols], jnp.concatenate(p_main, axis=1))
              + _dot_tn(v_ref[pl.ds(start, near), vcols], jnp.concatenate(p_near, axis=1)))
        for r, h in enumerate(heads):
            sl = slice(r * QB, (r + 1) * QB)
            ot_scr[h * HEAD_DIM:(h + 1) * HEAD_DIM, :] = pv[0:HEAD_DIM, sl] * (1.0 / pv[HEAD_DIM:HEAD_DIM + 1, sl])
    o_ref[...] = ot_scr[...].T.astype(o_ref.dtype)


def attn_prompt(proj, bias_table, batch, seq):
    QB = LANES
    nq = seq // QB
    assert seq % QB == 0
    topk = min(TOPK_MAX, seq // 4)
    ts = np.arange(QB)[:, None]
    tq = np.arange(QB)[None, :]
    bucket = jnp.asarray(np.stack([_t5_bucket_np(tq - ts), _t5_bucket_np(QB + tq - ts)]))
    n_groups = math.gcd(nq, PROMPT_KEY_GROUPS)
    per_group = nq // n_groups
    kvw = N_KV_HEADS * HEAD_DIM
    k_bf = proj[:, COL_K:COL_K + kvw].astype(BF16)
    v4 = proj[:, COL_V:COL_V + kvw].astype(BF16).reshape(batch * seq, N_KV_HEADS, HEAD_DIM)
    v_aug = jnp.concatenate([v4, jnp.ones(v4.shape[:2] + (1,), BF16),
                             jnp.zeros(v4.shape[:2] + (LANES - HEAD_DIM - 1,), BF16)], axis=-1)
    v_aug = v_aug.reshape(batch * seq, N_KV_HEADS * LANES)
    out = None
    for grp in range(n_groups):
        j0 = grp * per_group
        nk = (j0 + per_group) * QB
        rowblk = lambda width, colblk: pl.BlockSpec((QB, width), lambda b, j: (b * nq + j0 + j, colblk))
        seqblk = lambda width, colblk: pl.BlockSpec((seq, width), lambda b, j: (b, colblk))
        prev = [] if out is None else [out]
        out = pl.pallas_call(
            functools.partial(_attn_prompt_body, nk=nk, topk=topk, j0=j0),
            grid=(batch, per_group),
            in_specs=[
                pl.BlockSpec(memory_space=pltpu.SMEM),
                rowblk(N_HEADS * HEAD_DIM, COL_Q // (N_HEADS * HEAD_DIM)),
                seqblk(kvw, 0),
                seqblk(N_KV_HEADS * LANES, 0),
                rowblk(IDX_HEADS * IDX_DIM, COL_QI // (IDX_HEADS * IDX_DIM)),
                seqblk(LANES, COL_KI // LANES),
                rowblk(LANES, COL_SM // LANES),
                pl.BlockSpec((2, QB, QB), lambda b, j: (0, 0, 0)),
            ] + [pl.BlockSpec(memory_space=pl.ANY) for _ in prev],
            out_specs=pl.BlockSpec((QB, N_HEADS * HEAD_DIM), lambda b, j: (b * nq + j0 + j, 0)),
            out_shape=jax.ShapeDtypeStruct((batch * seq, N_HEADS * HEAD_DIM), BF16),
            input_output_aliases={8: 0} if prev else {},
            scratch_shapes=[
                pltpu.VMEM((nk, QB), I32),
                pltpu.VMEM((nk, QB), jnp.int16),
                pltpu.VMEM((nk, QB), jnp.int16),
                pltpu.VMEM((nk, QB), F32),
                pltpu.VMEM((nk, QB), F32),
                pltpu.VMEM((N_HEADS * HEAD_DIM, QB), F32),
                pltpu.VMEM((N_HEADS, 2, QB, QB), F32),
                pltpu.VMEM((nk, 2 * LANES), BF16),
            ],
            compiler_params=_cparams(("arbitrary", "arbitrary")),
            name=f"attn_prompt_g{grp}",
        )(bias_table, proj, k_bf, v_aug, proj, proj, proj, bucket, *prev)
    return out


def _attn_sample_body(pt_ref, q_ref, qi_ref, w_ref, kn_ref, vn_ref, kin_ref, tabt_ref, bucket_ref, gsum_ref, xpand_ref,
                      *rest, n_seq, n_pages, page, t_new, topk):
    n_pg = n_seq * n_pages
    ck = rest[0:n_pg]
    cv = rest[n_pg:2 * n_pg]
    cki = rest[2 * n_pg:3 * n_pg]
    o_ref, sc_scr, key_scr, sel_scr, lg_scr, p_scr, bias_scr, ki_st, k_st, v_st = rest[3 * n_pg:]
    past = n_pages * page
    n_tiles = n_pages + 1
    rows_q = t_new * N_HEADS
    rows_g = t_new * Q_PER_KV

    @pl.when(pl.program_id(0) == 0)
    def _():
        bk = bucket_ref[...]
        acc = jnp.zeros(bk.shape, F32)
        for bb in range(N_BUCKETS):
            acc = jnp.where(bk == bb, tabt_ref[:, bb:bb + 1], acc)
        bias_scr[...] = acc - tabt_ref[:, N_BUCKETS - 1:N_BUCKETS]

    for s in range(n_seq):
        for c in range(n_pages):
            sl = slice(c * page, (c + 1) * page)
            ki_st[s, :, sl] = cki[s * n_pages + c][0]
            k_st[s, :, :, sl] = ck[s * n_pages + c][0].astype(BF16)
            v_st[s, :, :, sl] = cv[s * n_pages + c][0].astype(BF16)

    for s in range(n_seq):
        qh, ql = _hi_lo(qi_ref[s])
        wcol = w_ref[s]
        gsum = gsum_ref[s]

        def idx_scores(kt_h, kt_l, dot):
            sc = (dot(qh, kt_l) + dot(ql, kt_h)) + dot(qh, kt_h)
            sc = jnp.maximum(sc * (IDX_DIM ** -0.5), 0.0) * wcol
            return _sel_dot_exact(gsum, sc) * (IDX_HEADS ** -0.5)

        main = idx_scores(*_hi_lo(ki_st[s]), _dot)
        tail = idx_scores(*_hi_lo(kin_ref[s]), _dot_nt)
        sc_scr[:, 0:past] = main if s == 0 else sc_scr[:, 0:past] + main
        sc_scr[:, past:] = tail if s == 0 else sc_scr[:, past:] + tail

    n_keys = n_tiles * page
    n_rows = n_seq * t_new

    def valid(l0):
        kp = l0 + lax.broadcasted_iota(I32, (n_rows, LANES), 1)
        row = lax.broadcasted_iota(I32, (n_rows, LANES), 0)
        tok = row
        for s in range(1, n_seq):
            tok = jnp.where(row >= s * t_new, row - s * t_new, tok)
        return kp <= past + tok

    for c in range(n_tiles):
        sl = slice(c * page, (c + 1) * page)
        key_scr[:, sl] = _sortable_key(jnp.where(valid(c * page), sc_scr[:, sl], NEG_INF))
    _topk_select_rows(key_scr, sel_scr, topk, valid)

    for s in range(n_seq):
        for g in range(N_KV_HEADS):
            rs = slice(g * rows_g, (g + 1) * rows_g)
            cols = slice(g * HEAD_DIM, (g + 1) * HEAD_DIM)
            qg = (q_ref[s, rs, :] * (HEAD_DIM ** -0.5)).astype(BF16)
            lg_scr[rs, 0:past] = _dot(qg, k_st[s, g])
            lg_scr[rs, past:] = _dot_nt(qg, kn_ref[s, :, cols].astype(BF16))
        near = slice(past - page, n_keys)
        lg_scr[:, near] = lg_scr[:, near] + bias_scr[...]
        picked = _dot(xpand_ref[s], sel_scr[...]) > 0.5
        lg = jnp.where(picked, lg_scr[...], NEG_INF)
        m = jnp.max(lg, axis=1, keepdims=True)
        p = jnp.exp(lg - m)
        p_scr[...] = (p * (1.0 / jnp.sum(p, axis=1, keepdims=True))).astype(BF16)
        for g in range(N_KV_HEADS):
            rs = slice(g * rows_g, (g + 1) * rows_g)
            cols = slice(g * HEAD_DIM, (g + 1) * HEAD_DIM)
            acc = _dot_nt(p_scr[rs, 0:past], v_st[s, g]) + _dot(p_scr[rs, past:], vn_ref[s, :, cols].astype(BF16))
            o_ref[s, rs, :] = acc.astype(o_ref.dtype)


def attn_sample(proj_s, cache_k, cache_v, cache_kidx, page_table, bias_table, t_new):
    nb, n_pages = page_table.shape
    n_pool, page = cache_k.shape[0], cache_k.shape[1]
    past = n_pages * page
    total = past + t_new
    topk = min(TOPK_MAX, total // 4)
    kvw = N_KV_HEADS * HEAD_DIM
    n_seq = SAMPLE_SEQS_PER_STEP if nb % SAMPLE_SEQS_PER_STEP == 0 else 1
    rows_q = t_new * N_HEADS
    rows_i = t_new * IDX_HEADS
    n_rows = n_seq * t_new
    n_keys = past + page
    assert page == LANES and page >= MAX_DISTANCE and t_new <= page
    ck_t = jnp.transpose(cache_k, (0, 2, 3, 1))
    cv_t = jnp.transpose(cache_v, (0, 2, 3, 1))
    cki_t = jnp.transpose(cache_kidx, (0, 2, 1))
    seg = lambda col, width: proj_s[:, col:col + width]
    q3 = seg(COL_Q, N_HEADS * HEAD_DIM).reshape(nb, t_new, N_KV_HEADS, Q_PER_KV, HEAD_DIM)
    q3 = q3.transpose(0, 2, 1, 3, 4).reshape(nb, rows_q, HEAD_DIM)
    qi3 = seg(COL_QI, IDX_HEADS * IDX_DIM).reshape(nb, rows_i, IDX_DIM)
    w3 = seg(COL_SM + SM_WI, IDX_HEADS).reshape(nb, rows_i, 1)
    pad_rows = lambda a: jnp.pad(a.reshape(nb, t_new, -1), ((0, 0), (0, page - t_new), (0, 0)))
    kn3, vn3, kin3 = pad_rows(seg(COL_K, kvw)), pad_rows(seg(COL_V, kvw)), pad_rows(seg(COL_KI, IDX_DIM))
    row = np.arange(rows_q)
    row_tok = (row // Q_PER_KV) % t_new
    row_head = (row // (t_new * Q_PER_KV)) * Q_PER_KV + row % Q_PER_KV
    tab_t = jnp.pad(bias_table.T[row_head], ((0, 0), (0, LANES - N_BUCKETS)))
    kpos = np.concatenate([past - page + np.arange(page), past + np.arange(page)])[None, :]
    bucket = jnp.asarray(_t5_bucket_np(past + row_tok[:, None] - kpos))
    gsum = np.zeros((n_seq, n_rows, rows_i), np.float32)
    xpand = np.zeros((n_seq, rows_q, n_rows), np.float32)
    for s in range(n_seq):
        gsum[s, s * t_new + np.arange(rows_i) // IDX_HEADS, np.arange(rows_i)] = 1.0
        xpand[s, row, s * t_new + row_tok] = 1.0
    per_step = lambda shape: pl.BlockSpec((n_seq,) + shape, lambda b, pt: (b,) + tuple(0 for _ in shape))
    const = lambda shape: pl.BlockSpec(shape, lambda b, pt: tuple(0 for _ in shape))

    def page_specs(shape):
        return [pl.BlockSpec((1,) + shape, lambda b, pt, s=s, c=c: (pt[b * n_seq + s, c],) + tuple(0 for _ in shape))
                for s in range(n_seq) for c in range(n_pages)]

    grid_spec = pltpu.PrefetchScalarGridSpec(
        num_scalar_prefetch=1,
        grid=(nb // n_seq,),
        in_specs=[per_step((rows_q, HEAD_DIM)), per_step((rows_i, IDX_DIM)), per_step((rows_i, 1)),
                  per_step((page, kvw)), per_step((page, kvw)), per_step((page, IDX_DIM)),
                  const((rows_q, LANES)), const((rows_q, 2 * page)),
                  const((n_seq, n_rows, rows_i)), const((n_seq, rows_q, n_rows))]
                 + page_specs((N_KV_HEADS, HEAD_DIM, page)) + page_specs((N_KV_HEADS, HEAD_DIM, page))
                 + page_specs((IDX_DIM, page)),
        out_specs=pl.BlockSpec((n_seq, rows_q, HEAD_DIM), lambda b, pt: (b, 0, 0)),
        scratch_shapes=[
            pltpu.VMEM((n_rows, n_keys), F32),
            pltpu.VMEM((n_rows, n_keys), I32),
            pltpu.VMEM((n_rows, n_keys), BF16),
            pltpu.VMEM((rows_q, n_keys), F32),
            pltpu.VMEM((rows_q, n_keys), BF16),
            pltpu.VMEM((rows_q, 2 * page), F32),
            pltpu.VMEM((n_seq, IDX_DIM, past), F32),
            pltpu.VMEM((n_seq, N_KV_HEADS, HEAD_DIM, past), BF16),
            pltpu.VMEM((n_seq, N_KV_HEADS, HEAD_DIM, past), BF16),
        ],
    )
    n_pg = n_seq * n_pages
    out = pl.pallas_call(
        functools.partial(_attn_sample_body, n_seq=n_seq, n_pages=n_pages, page=page, t_new=t_new, topk=topk),
        grid_spec=grid_spec,
        out_shape=jax.ShapeDtypeStruct((nb, rows_q, HEAD_DIM), BF16),
        compiler_params=_cparams(("arbitrary",)),
        name="attn_sample",
    )(page_table, q3, qi3, w3, kn3, vn3, kin3, tab_t, bucket, jnp.asarray(gsum, BF16), jnp.asarray(xpand, BF16),
      *([ck_t] * n_pg), *([cv_t] * n_pg), *([cki_t] * n_pg))
    out = out.reshape(nb, N_KV_HEADS, t_new, Q_PER_KV, HEAD_DIM).transpose(0, 2, 1, 3, 4)
    return out.reshape(nb * t_new, N_HEADS * HEAD_DIM)


def _merge_body(ys_ref, ya_ref, g_ref, h_ref, wbs_ref, wba_ref, wo_ref, lg_ref, lb_ref, o_ref):
    a = _dot(ys_ref[...], wbs_ref[...])
    b = _dot(ya_ref[...], wba_ref[...])
    gates = jax.nn.sigmoid(g_ref[...])
    merged = gates[:, :D_MODEL] * a + gates[:, D_MODEL:] * b
    r = ALPHA * h_ref[...] + _dot(merged.astype(BF16), wo_ref[...])
    o_ref[...] = _layer_norm(r, lg_ref[...], lb_ref[...])


def merge_out(y_ssm, y_att, proj, h1, wbs, wba, wo, g, b, *, tm=512):
    m = h1.shape[0]
    tm = min(tm, m)
    assert m % tm == 0
    row = lambda width, colblk=0: pl.BlockSpec((tm, width), lambda i: (i, colblk))
    full = lambda shape: pl.BlockSpec(shape, lambda i: tuple(0 for _ in shape))
    return pl.pallas_call(
        _merge_body,
        grid=(m // tm,),
        in_specs=[row(D_INNER), row(N_HEADS * HEAD_DIM), row(2 * D_MODEL, COL_G // (2 * D_MODEL)), row(D_MODEL),
                  full(wbs.shape), full(wba.shape), full(wo.shape), full((1, D_MODEL)), full((1, D_MODEL))],
        out_specs=row(D_MODEL),
        out_shape=jax.ShapeDtypeStruct((m, D_MODEL), F32),
        compiler_params=_cparams(("parallel",)),
        name="merge_out",
    )(y_ssm, y_att, proj, h1, wbs, wba, wo, g, b)


def kernel(x_prompt, x_sample, cache_k, cache_v, cache_kidx, state_conv, state_ssm, page_table, bias_table, w_in, conv_w, conv_b, dt_bias, a_log, d_skip, ssm_norm_g, w_branch_ssm, w_branch_attn, w_out, ffn1_wi, ffn1_wo, ffn2_wi, ffn2_wo, ln1_g, ln1_b, ln2_g, ln2_b, ln3_g, ln3_b):
    assert w_in.shape[0] == DEPTH
    batch, seq, d = x_prompt.shape
    nb, t_new, _ = x_sample.shape
    xs = (x_prompt.reshape(batch * seq, d), x_sample.reshape(nb * t_new, d))
    outs_p, outs_s = [], []
    for i in range(DEPTH):
        w_pack = pack_w_in(w_in[i])
        f1i, f1o = ffn1_wi[i].astype(BF16), ffn1_wo[i].astype(BF16)
        f2i, f2o = ffn2_wi[i].astype(BF16), ffn2_wo[i].astype(BF16)
        wbs, wba, wo = w_branch_ssm[i].astype(BF16), w_branch_attn[i].astype(BF16), w_out[i].astype(BF16)
        row = lambda v: v.reshape(1, -1)
        pad_heads = lambda v: jnp.concatenate([v, jnp.zeros((LANES - SSM_HEADS,), v.dtype)]).reshape(1, LANES)
        dtb_pad, alog_pad = pad_heads(dt_bias[i]), pad_heads(a_log[i])
        dskip_e = jnp.repeat(d_skip[i], SSM_HEAD_DIM).reshape(1, D_INNER)
        ng = row(ssm_norm_g[i])
        cw, cb = conv_w[i], row(conv_b[i])

        h1, proj = [], []
        for x in xs:
            hf, hb = ffn_ln(x, f1i, f1o, row(ln1_g[i]), row(ln1_b[i]), emit_bf16=True)
            h1.append(hf)
            proj.append(matmul(hb, w_pack, tm=2048, tn=PROJ_TN))
        proj_p, proj_s = proj

        ys_p, conv_p, ssm_p = ssd_prompt(proj_p, batch, seq, cw, cb, dtb_pad, alog_pad, dskip_e, ng)
        ya_p = attn_prompt(proj_p, bias_table, batch, seq)

        raw_s = jnp.concatenate([proj_s[:, COL_X:COL_X + D_INNER], proj_s[:, COL_BC:COL_BC + 2 * GN]], axis=1)
        u_cat = jnp.concatenate([state_conv[i], raw_s.reshape(nb, t_new, CONV_DIM)], axis=1)
        expand = np.zeros((LANES, D_INNER), np.float32)
        expand[np.arange(D_INNER) // SSM_HEAD_DIM, np.arange(D_INNER)] = 1.0
        ys_s, conv_s, ssm_s = ssd_sample(u_cat, proj_s.reshape(nb, t_new, PROJ_N),
                                         state_ssm[i].reshape(nb, SSM_HEADS * SSM_HEAD_DIM, SSM_STATE),
                                         cw, cb, dtb_pad, alog_pad, dskip_e, ng, jnp.asarray(expand, BF16))
        ya_s = attn_sample(proj_s, cache_k[i], cache_v[i], cache_kidx[i], page_table, bias_table, t_new)

        new_xs = []
        for hf, pj, ys, ya in ((h1[0], proj_p, ys_p, ya_p), (h1[1], proj_s, ys_s.reshape(nb * t_new, D_INNER), ya_s)):
            h2 = merge_out(ys, ya, pj, hf, wbs, wba, wo, row(ln2_g[i]), row(ln2_b[i]))
            new_xs.append(ffn_ln(h2, f2i, f2o, row(ln3_g[i]), row(ln3_b[i]), emit_bf16=False))
        xs = tuple(new_xs)

        kvw = N_KV_HEADS * HEAD_DIM
        shp = lambda nb_, l_: (nb_, l_, N_KV_HEADS, HEAD_DIM)
        outs_p.append((proj_p[:, COL_K:COL_K + kvw].reshape(shp(batch, seq)),
                       proj_p[:, COL_V:COL_V + kvw].reshape(shp(batch, seq)),
                       proj_p[:, COL_KI:COL_KI + IDX_DIM].reshape(batch, seq, IDX_DIM),
                       conv_p, ssm_p.reshape(batch, SSM_HEADS, SSM_HEAD_DIM, SSM_STATE)))
        outs_s.append((proj_s[:, COL_K:COL_K + kvw].reshape(shp(nb, t_new)),
                       proj_s[:, COL_V:COL_V + kvw].reshape(shp(nb, t_new)),
                       proj_s[:, COL_KI:COL_KI + IDX_DIM].reshape(nb, t_new, IDX_DIM),
                       conv_s, ssm_s.reshape(nb, SSM_HEADS, SSM_HEAD_DIM, SSM_STATE)))
    k_p, v_p, kidx_p, conv_pp, ssm_pp = [jnp.stack(a) for a in zip(*outs_p)]
    k_s, v_s, kidx_s, conv_ss, ssm_ss = [jnp.stack(a) for a in zip(*outs_s)]
    return (xs[0].reshape(batch, seq, d), xs[1].reshape(nb, t_new, d),
            k_p, v_p, kidx_p, conv_pp, ssm_pp, k_s, v_s, kidx_s, conv_ss, ssm_ss)
```

```python
import functools
import math

import numpy as np
import jax
import jax.numpy as jnp
from jax import lax
from jax.experimental import pallas as pl
from jax.experimental.pallas import tpu as pltpu

F32 = jnp.float32
BF16 = jnp.bfloat16
I32 = jnp.int32

D_MODEL = 1024
D_INNER = 2 * D_MODEL
SSM_HEAD_DIM = 64
SSM_HEADS = D_INNER // SSM_HEAD_DIM
SSM_GROUPS = 4
SSM_STATE = 128
CONV_W = 4
CONV_DIM = D_INNER + 2 * SSM_GROUPS * SSM_STATE
SSD_CHUNK = 128
N_HEADS = 16
HEAD_DIM = 64
N_KV_HEADS = 4
Q_PER_KV = N_HEADS // N_KV_HEADS
IDX_HEADS = 8
IDX_DIM = 64
TOPK_MAX = 256
N_BUCKETS = 32
MAX_DISTANCE = 128
D_FF = 2816
DEPTH = 1
ALPHA = (2 * DEPTH) ** 0.25
LN_EPS = 1e-5
IN_SPLITS = (D_INNER, CONV_DIM, SSM_HEADS, N_HEADS * HEAD_DIM, N_KV_HEADS * HEAD_DIM,
             N_KV_HEADS * HEAD_DIM, IDX_HEADS * IDX_DIM, IDX_DIM, IDX_HEADS, 2 * D_MODEL)

LANES = 128
SUBLANES = 8
VMEM_LIMIT_BYTES = 56 * 1024 * 1024

GN = SSM_GROUPS * SSM_STATE
COL_Z = 0
COL_X = COL_Z + D_INNER
COL_G = COL_X + D_INNER
COL_BC = COL_G + 2 * D_MODEL
COL_Q = COL_BC + 2 * GN
COL_K = COL_Q + N_HEADS * HEAD_DIM
COL_V = COL_K + N_KV_HEADS * HEAD_DIM
COL_QI = COL_V + N_KV_HEADS * HEAD_DIM
COL_SM = COL_QI + IDX_HEADS * IDX_DIM
COL_KI = COL_SM + LANES
PROJ_TN = 512
PROJ_N = -(-(COL_KI + LANES) // PROJ_TN) * PROJ_TN
SM_WI = SSM_HEADS

INT_MIN = -(2 ** 31)
LOG2E = math.log2(math.e)
NEG_INF = float("-inf")
PROMPT_KEY_GROUPS = 8
SAMPLE_SEQS_PER_STEP = 4
RADIX_BITS = 4


def _cparams(sem):
    return pltpu.CompilerParams(dimension_semantics=sem, vmem_limit_bytes=VMEM_LIMIT_BYTES)


def _layer_norm(r, g, b):
    mu = jnp.mean(r, axis=-1, keepdims=True)
    d = r - mu
    var = jnp.mean(d * d, axis=-1, keepdims=True)
    return d * lax.rsqrt(var + LN_EPS) * g + b


def _silu(x):
    h = 0.5 * x
    return h + h * jnp.tanh(h)


def _softplus(x):
    return jnp.maximum(x, 0.0) + jnp.log1p(jnp.exp(-jnp.abs(x)))


def _dot(a, b):
    return jnp.dot(a, b, preferred_element_type=F32)


def _dot_nt(a, b):
    return lax.dot_general(a, b, (((1,), (1,)), ((), ())), preferred_element_type=F32)


def _dot_tn(a, b):
    return lax.dot_general(a, b, (((0,), (0,)), ((), ())), preferred_element_type=F32)


def _split3(v):
    hi = v.astype(BF16)
    r1 = v - hi.astype(F32)
    mid = r1.astype(BF16)
    lo = (r1 - mid.astype(F32)).astype(BF16)
    return hi, mid, lo


def _dot_exact_sel(v, sel_bf16):
    hi, mid, lo = _split3(v)
    return (_dot(hi, sel_bf16) + _dot(mid, sel_bf16)) + _dot(lo, sel_bf16)


def _sel_dot_exact(sel_bf16, v):
    hi, mid, lo = _split3(v)
    return (_dot(sel_bf16, hi) + _dot(sel_bf16, mid)) + _dot(sel_bf16, lo)


def _hi_lo(a):
    ah = a.astype(BF16)
    return ah, (a - ah.astype(F32)).astype(BF16)


def _dot_nt_x3(a, b):
    ah, al = _hi_lo(a)
    bh, bl = _hi_lo(b)
    return (_dot_nt(ah, bl) + _dot_nt(al, bh)) + _dot_nt(ah, bh)


def _dot_x3(a, b):
    ah, al = _hi_lo(a)
    bh, bl = _hi_lo(b)
    return (_dot(ah, bl) + _dot(al, bh)) + _dot(ah, bh)


def _col_reduce(x, op2, op):
    parts = []
    for r0 in range(0, x.shape[0], LANES):
        y = x[r0:r0 + LANES]
        n = y.shape[0]
        while n > SUBLANES and n % (2 * SUBLANES) == 0:
            n //= 2
            y = op2(y[:n], y[n:])
        parts.append(y)
    while len(parts) > 1:
        parts = [op2(parts[i], parts[i + 1]) if i + 1 < len(parts) else parts[i] for i in range(0, len(parts), 2)]
    return op(parts[0], axis=0, keepdims=True)


def _col_sum(x):
    return _col_reduce(x, jnp.add, jnp.sum)


def _col_max(x):
    return _col_reduce(x, jnp.maximum, jnp.max)


def _ffn_ln_body(x_ref, wi_ref, wo_ref, g_ref, b_ref, o_ref, *maybe_ob_ref, tf):
    x = x_ref[...]
    xb = x.astype(BF16)
    dff = wo_ref.shape[0]
    acc = None
    for c0 in range(0, dff, tf):
        gate = _dot(xb, wi_ref[:, c0:c0 + tf])
        up = _dot(xb, wi_ref[:, dff + c0:dff + c0 + tf])
        part = _dot((_silu(gate) * up).astype(BF16), wo_ref[c0:c0 + tf, :])
        acc = part if acc is None else acc + part
    y = _layer_norm(ALPHA * x + 0.5 * acc, g_ref[...], b_ref[...])
    o_ref[...] = y
    for ob_ref in maybe_ob_ref:
        ob_ref[...] = y.astype(BF16)


def ffn_ln(x, wi_bf, wo_bf, g, b, *, emit_bf16, tm=512, tf=1408):
    m, d = x.shape
    dff = wo_bf.shape[0]
    tm = min(tm, m)
    assert m % tm == 0 and dff % tf == 0
    row = pl.BlockSpec((tm, d), lambda i: (i, 0))
    resident = lambda shape: pl.BlockSpec(shape, lambda i: (0, 0), pipeline_mode=pl.Buffered(1))
    n_out = 2 if emit_bf16 else 1
    res = pl.pallas_call(
        functools.partial(_ffn_ln_body, tf=tf),
        grid=(m // tm,),
        in_specs=[row, resident(wi_bf.shape), resident(wo_bf.shape), resident((1, d)), resident((1, d))],
        out_specs=[row] * n_out,
        out_shape=[jax.ShapeDtypeStruct((m, d), F32), jax.ShapeDtypeStruct((m, d), BF16)][:n_out],
        compiler_params=_cparams(("parallel",)),
        name="ffn_ln",
    )(x, wi_bf, wo_bf, g, b)
    return res if emit_bf16 else res[0]


def _matmul_body(x_ref, w_ref, o_ref):
    o_ref[...] = _dot(x_ref[...], w_ref[...])


def matmul(x_bf, w_bf, *, tm, tn):
    m, k = x_bf.shape
    n = w_bf.shape[1]
    tm = min(tm, m)
    assert m % tm == 0 and n % tn == 0
    return pl.pallas_call(
        _matmul_body,
        grid=(m // tm, n // tn),
        in_specs=[pl.BlockSpec((tm, k), lambda i, j: (i, 0)),
                  pl.BlockSpec((k, tn), lambda i, j: (0, j))],
        out_specs=pl.BlockSpec((tm, tn), lambda i, j: (i, j)),
        out_shape=jax.ShapeDtypeStruct((m, n), F32),
        compiler_params=_cparams(("parallel", "arbitrary")),
        name="in_proj",
    )(x_bf, w_bf)


def pack_w_in(w_in):
    offs = np.cumsum((0,) + IN_SPLITS)
    z, xbc, dt, q, k, v, qi, ki, wi, gates = [w_in[:, offs[i]:offs[i + 1]] for i in range(len(IN_SPLITS))]
    d = w_in.shape[0]
    zeros = lambda n: jnp.zeros((d, n), w_in.dtype)
    small = jnp.concatenate([dt, wi, zeros(LANES - SSM_HEADS - IDX_HEADS)], axis=1)
    kib = jnp.concatenate([ki, ki], axis=1)
    packed = jnp.concatenate([z, xbc[:, :D_INNER], gates, xbc[:, D_INNER:], q, k, v, qi, small, kib,
                              zeros(PROJ_N - COL_KI - LANES)], axis=1)
    return packed.astype(BF16)


def _ssd_prompt_body(z_ref, xr_ref, bc_ref, sm_ref, cwx_ref, cwbc_ref, cbx_ref, cbbc_ref,
                     dtb_ref, alog_ref, dskip_ref, ng_ref,
                     y_ref, cs_ref, hs_ref,
                     xbuf, bcbuf, h_scr, xc_scr, bcc_scr, y_scr):
    c = pl.program_id(1)
    nc = pl.num_programs(1)
    L = SSD_CHUNK
    halo = SUBLANES
    n_pairs = SSM_HEADS // 2
    pairs_per_group = n_pairs // SSM_GROUPS

    @pl.when(c == 0)
    def _():
        xbuf[0:halo, :] = jnp.zeros((halo, D_INNER), F32)
        bcbuf[0:halo, :] = jnp.zeros((halo, 2 * GN), F32)
        h_scr[...] = jnp.zeros_like(h_scr)

    xbuf[halo:halo + L, :] = xr_ref[...]
    bcbuf[halo:halo + L, :] = bc_ref[...]

    def conv(buf, w_ref, b_ref):
        full = buf[0:halo + L, :]
        acc = b_ref[...] + w_ref[CONV_W - 1:CONV_W, :] * full[halo:halo + L]
        for k in range(CONV_W - 1):
            shifted = pltpu.roll(full, CONV_W - 1 - k, axis=0)[halo:halo + L]
            acc = acc + w_ref[k:k + 1, :] * shifted
        return _silu(acc)

    xc_scr[...] = conv(xbuf, cwx_ref, cbx_ref)
    bcc_scr[...] = conv(bcbuf, cwbc_ref, cbbc_ref)
    xbuf[0:halo, :] = xbuf[L:L + halo, :]
    bcbuf[0:halo, :] = bcbuf[L:L + halo, :]

    dt = _softplus(sm_ref[...] + dtb_ref[...])
    a = -jnp.exp(alog_ref[...])
    row = lax.broadcasted_iota(I32, (L, L), 0)
    col = lax.broadcasted_iota(I32, (L, L), 1)
    tril = (row >= col).astype(F32)
    cum = jnp.dot(tril, dt * a, precision=lax.Precision.HIGHEST, preferred_element_type=F32)
    w = jnp.exp(cum[L - 1:L, :] - cum) * dt
    cum_t = cum.T
    dt_t = dt.T
    causal = row >= col
    lane = lax.broadcasted_iota(I32, (L, LANES), 1)
    lo_half = lane < SSM_HEAD_DIM
    sub = lax.broadcasted_iota(I32, (2 * SSM_HEAD_DIM, SSM_STATE), 0)
    lo_rows = sub < SSM_HEAD_DIM

    for i in range(n_pairs):
        g = i // pairs_per_group
        h0, h1 = 2 * i, 2 * i + 1
        sl = slice(i * LANES, (i + 1) * LANES)
        if i % pairs_per_group == 0:
            bg = bcc_scr[:, g * SSM_STATE:(g + 1) * SSM_STATE].astype(BF16)
            cg = bcc_scr[:, GN + g * SSM_STATE:GN + (g + 1) * SSM_STATE].astype(BF16)
            cb = _dot_nt(cg, bg)
        xp = xc_scr[:, sl]
        c0 = jnp.broadcast_to(cum[:, h0:h0 + 1], (L, LANES))
        c1 = jnp.broadcast_to(cum[:, h1:h1 + 1], (L, LANES))
        w0 = jnp.broadcast_to(w[:, h0:h0 + 1], (L, LANES))
        w1 = jnp.broadcast_to(w[:, h1:h1 + 1], (L, LANES))

        def mix(ct, h):
            seg = jnp.exp(jnp.where(causal, ct - cum_t[h:h + 1, :], NEG_INF))
            return (cb * seg * dt_t[h:h + 1, :]).astype(BF16)

        zero = jnp.zeros_like(xp)
        yd = (_dot(mix(c0, h0), jnp.where(lo_half, xp, zero).astype(BF16))
              + _dot(mix(c1, h1), jnp.where(lo_half, zero, xp).astype(BF16)))
        hp = h_scr[i]
        yo = _dot_nt(cg, hp.astype(BF16)) * jnp.where(lo_half, jnp.exp(c0), jnp.exp(c1))
        y_scr[:, sl] = yd + yo + dskip_ref[:, sl] * xp
        xw = (xp * jnp.where(lo_half, w0, w1)).astype(BF16)
        st = _dot_tn(xw, bg)
        dec = jnp.where(lo_rows, jnp.exp(c0[L - 1:L, :]), jnp.exp(c1[L - 1:L, :]))
        h_scr[i] = dec * hp + st

    gw = D_INNER // SSM_GROUPS
    for g in range(SSM_GROUPS):
        sl = slice(g * gw, (g + 1) * gw)
        u = y_scr[:, sl] * _silu(z_ref[:, sl])
        ms = jnp.mean(u * u, axis=-1, keepdims=True)
        y_ref[:, sl] = (u * lax.rsqrt(ms + LN_EPS) * ng_ref[:, sl]).astype(y_ref.dtype)

    @pl.when(c == nc - 1)
    def _():
        cs_ref[0, :, 0:D_INNER] = xr_ref[L - (CONV_W - 1):L, :]
        cs_ref[0, :, D_INNER:CONV_DIM] = bc_ref[L - (CONV_W - 1):L, :]
        for i in range(n_pairs):
            hs_ref[0, i * LANES:(i + 1) * LANES, :] = h_scr[i]


def ssd_prompt(proj, batch, seq, conv_w, conv_b, dtb_pad, alog_pad, dskip_e, norm_g):
    L = SSD_CHUNK
    nc = seq // L
    assert seq % L == 0
    blk = lambda width, colblk: pl.BlockSpec((L, width), lambda b, c: (b * nc + c, colblk))
    full = lambda shape: pl.BlockSpec(shape, lambda b, c: tuple(0 for _ in shape))
    n_pairs = SSM_HEADS // 2
    return pl.pallas_call(
        _ssd_prompt_body,
        grid=(batch, nc),
        in_specs=[
            blk(D_INNER, COL_Z // D_INNER),
            blk(D_INNER, COL_X // D_INNER),
            blk(2 * GN, COL_BC // (2 * GN)),
            blk(LANES, COL_SM // LANES),
            full((CONV_W, D_INNER)), full((CONV_W, 2 * GN)), full((1, D_INNER)), full((1, 2 * GN)),
            full((1, LANES)), full((1, LANES)), full((1, D_INNER)), full((1, D_INNER)),
        ],
        out_specs=[
            pl.BlockSpec((L, D_INNER), lambda b, c: (b * nc + c, 0)),
            pl.BlockSpec((1, CONV_W - 1, CONV_DIM), lambda b, c: (b, 0, 0)),
            pl.BlockSpec((1, SSM_HEADS * SSM_HEAD_DIM, SSM_STATE), lambda b, c: (b, 0, 0)),
        ],
        out_shape=[
            jax.ShapeDtypeStruct((batch * seq, D_INNER), BF16),
            jax.ShapeDtypeStruct((batch, CONV_W - 1, CONV_DIM), F32),
            jax.ShapeDtypeStruct((batch, SSM_HEADS * SSM_HEAD_DIM, SSM_STATE), F32),
        ],
        scratch_shapes=[
            pltpu.VMEM((L + 2 * SUBLANES, D_INNER), F32),
            pltpu.VMEM((L + 2 * SUBLANES, 2 * GN), F32),
            pltpu.VMEM((n_pairs, 2 * SSM_HEAD_DIM, SSM_STATE), F32),
            pltpu.VMEM((L, D_INNER), F32),
            pltpu.VMEM((L, 2 * GN), F32),
            pltpu.VMEM((L, D_INNER), F32),
        ],
        compiler_params=_cparams(("parallel", "arbitrary")),
        name="ssd_prompt",
    )(proj, proj, proj, proj, conv_w[:, :D_INNER], conv_w[:, D_INNER:], conv_b[:, :D_INNER], conv_b[:, D_INNER:],
      dtb_pad, alog_pad, dskip_e, norm_g)


def _ssd_sample_body(u_ref, z_ref, sm_ref, h0_ref, cw_ref, cb_ref, dtb_ref, alog_ref, dskip_ref, ng_ref, exp_ref,
                     y_ref, cs_ref, hn_ref, *, t_new):
    T = t_new
    n_pairs = SSM_HEADS // 2
    pairs_per_group = n_pairs // SSM_GROUPS
    acc = cb_ref[...] + cw_ref[0:1, :] * u_ref[0, 0:T, :]
    for k in range(1, CONV_W):
        acc = acc + cw_ref[k:k + 1, :] * u_ref[0, k:k + T, :]
    xbc = _silu(acc)
    cs_ref[0] = u_ref[0, T:T + CONV_W - 1, :]
    x = xbc[:, :D_INNER]

    dt = _softplus(sm_ref[0] + dtb_ref[...])
    da = dt * (-jnp.exp(alog_ref[...]))
    rows = [da[0:1, :]]
    for t in range(1, T):
        rows.append(rows[-1] + da[t:t + 1, :])
    cum = jnp.concatenate(rows, axis=0)
    both = _dot_exact_sel(jnp.concatenate([cum, dt], axis=0), exp_ref[...])
    cum_e = both[0:T, :]
    dt_e = both[T:2 * T, :]
    t_idx = lax.broadcasted_iota(I32, (T, GN), 0)

    gw = D_INNER // SSM_GROUPS
    y_parts = []
    for g in range(SSM_GROUPS):
        sl = slice(g * gw, (g + 1) * gw)
        bg = xbc[:, D_INNER + g * SSM_STATE:D_INNER + (g + 1) * SSM_STATE]
        cg = xbc[:, D_INNER + GN + g * SSM_STATE:D_INNER + GN + (g + 1) * SSM_STATE]
        cbg = _dot_nt(cg.astype(BF16), bg.astype(BF16))
        yg = dskip_ref[:, sl] * x[:, sl]
        for s in range(T):
            seg = jnp.exp(jnp.where(t_idx >= s, cum_e[:, sl] - cum_e[s:s + 1, sl], NEG_INF))
            coef = seg * dt_e[s:s + 1, sl] * jnp.broadcast_to(cbg[:, s:s + 1], (T, gw))
            yg = yg + coef * x[s:s + 1, sl]
        y_parts.append(yg)

    xw = x * dt_e * jnp.exp(cum_e[T - 1:T, :] - cum_e)
    ecum = jnp.exp(cum_e)
    sub = lax.broadcasted_iota(I32, (2 * SSM_HEAD_DIM, SSM_STATE), 0)
    lo_rows = sub < SSM_HEAD_DIM
    yo_parts = []
    for i in range(n_pairs):
        g = i // pairs_per_group
        sl = slice(i * LANES, (i + 1) * LANES)
        bg = xbc[:, D_INNER + g * SSM_STATE:D_INNER + (g + 1) * SSM_STATE].astype(BF16)
        cg = xbc[:, D_INNER + GN + g * SSM_STATE:D_INNER + GN + (g + 1) * SSM_STATE].astype(BF16)
        hp = h0_ref[0, i * LANES:(i + 1) * LANES, :]
        yo_parts.append(_dot_nt(cg, hp.astype(BF16)) * ecum[:, sl])
        st = _dot_tn(xw[:, sl].astype(BF16), bg)
        d0 = jnp.exp(cum[T - 1:T, 2 * i:2 * i + 1])
        d1 = jnp.exp(cum[T - 1:T, 2 * i + 1:2 * i + 2])
        dec = jnp.where(lo_rows, jnp.broadcast_to(d0, sub.shape), jnp.broadcast_to(d1, sub.shape))
        hn_ref[0, i * LANES:(i + 1) * LANES, :] = dec * hp + st

    for g in range(SSM_GROUPS):
        sl = slice(g * gw, (g + 1) * gw)
        yo = jnp.concatenate(yo_parts[g * pairs_per_group:(g + 1) * pairs_per_group], axis=1)
        u = (y_parts[g] + yo) * _silu(z_ref[0, :, sl])
        ms = jnp.mean(u * u, axis=-1, keepdims=True)
        y_ref[0, :, sl] = (u * lax.rsqrt(ms + LN_EPS) * ng_ref[:, sl]).astype(y_ref.dtype)


def ssd_sample(u_cat, proj3, h0, conv_w, conv_b, dtb_pad, alog_pad, dskip_e, norm_g, expand):
    nb, t_new, _ = proj3.shape
    full = lambda shape: pl.BlockSpec(shape, lambda b: tuple(0 for _ in shape))
    hp = SSM_HEADS * SSM_HEAD_DIM
    return pl.pallas_call(
        functools.partial(_ssd_sample_body, t_new=t_new),
        grid=(nb,),
        in_specs=[
            pl.BlockSpec((1, t_new + CONV_W - 1, CONV_DIM), lambda b: (b, 0, 0)),
            pl.BlockSpec((1, t_new, D_INNER), lambda b: (b, 0, COL_Z // D_INNER)),
            pl.BlockSpec((1, t_new, LANES), lambda b: (b, 0, COL_SM // LANES)),
            pl.BlockSpec((1, hp, SSM_STATE), lambda b: (b, 0, 0)),
            full((CONV_W, CONV_DIM)), full((1, CONV_DIM)), full((1, LANES)), full((1, LANES)),
            full((1, D_INNER)), full((1, D_INNER)), full((LANES, D_INNER)),
        ],
        out_specs=[
            pl.BlockSpec((1, t_new, D_INNER), lambda b: (b, 0, 0)),
            pl.BlockSpec((1, CONV_W - 1, CONV_DIM), lambda b: (b, 0, 0)),
            pl.BlockSpec((1, hp, SSM_STATE), lambda b: (b, 0, 0)),
        ],
        out_shape=[
            jax.ShapeDtypeStruct((nb, t_new, D_INNER), BF16),
            jax.ShapeDtypeStruct((nb, CONV_W - 1, CONV_DIM), F32),
            jax.ShapeDtypeStruct((nb, hp, SSM_STATE), F32),
        ],
        compiler_params=_cparams(("parallel",)),
        name="ssd_sample",
    )(u_cat, proj3, proj3, h0, conv_w, conv_b, dtb_pad, alog_pad, dskip_e, norm_g, expand)


def _sortable_key(score):
    score = jnp.where(score == 0.0, 0.0, score)
    bits = lax.bitcast_convert_type(score, I32)
    return bits ^ ((bits >> 31) & jnp.int32(0x7FFFFFFF))


def _t5_bucket_np(rel):
    n = np.maximum(rel, 0)
    max_exact = N_BUCKETS // 2
    nf = np.maximum(n, 1).astype(np.float32)
    large = max_exact + (np.log(nf / max_exact) / math.log(MAX_DISTANCE / max_exact)
                         * (N_BUCKETS - max_exact)).astype(np.int32)
    large = np.minimum(large, N_BUCKETS - 1)
    return np.where(n < max_exact, n, large).astype(np.int32)


def _count_cols_i16(ref, n_rows, pred):
    packed_rows = 2 * SUBLANES
    parts = []
    for r0 in range(0, n_rows, LANES):
        x = jnp.where(pred(ref[r0:r0 + LANES, :]), jnp.int16(1), jnp.int16(0))
        n = x.shape[0]
        while n > packed_rows:
            n //= 2
            x = x[:n] + x[n:]
        parts.append(x)
    while len(parts) > 1:
        parts = [parts[i] + parts[i + 1] if i + 1 < len(parts) else parts[i] for i in range(0, len(parts), 2)]
    return jnp.sum(parts[0].astype(I32), axis=0, keepdims=True)


def _kth_largest_i16(ref, n_rows, k):
    i16_min = -(2 ** 15)

    def try_cand(r, cand):
        cnt = _count_cols_i16(ref, n_rows, lambda v: v >= cand.astype(jnp.int16))
        return jnp.where(cnt >= k, cand, r)

    r0 = try_cand(jnp.full((1, LANES), i16_min, I32), jnp.zeros((1, LANES), I32))
    return lax.fori_loop(0, 15, lambda it, r: try_cand(r, r + (jnp.int32(1) << (14 - it))), r0)


def _kth_largest_cols(key_ref, hi_ref, lo_ref, n_rows, k):
    i16_min = -(2 ** 15)
    for r0 in range(0, n_rows, LANES):
        key = key_ref[r0:r0 + LANES, :]
        hi_ref[r0:r0 + LANES, :] = (key >> 16).astype(jnp.int16)
        lo_ref[r0:r0 + LANES, :] = ((key & 0xFFFF) + i16_min).astype(jnp.int16)
    t_hi = _kth_largest_i16(hi_ref, n_rows, k)
    t_hi16 = t_hi.astype(jnp.int16)
    k_lo = k - _count_cols_i16(hi_ref, n_rows, lambda v: v > t_hi16)
    for r0 in range(0, n_rows, LANES):
        sl = slice(r0, r0 + LANES)
        lo_ref[sl, :] = jnp.where(hi_ref[sl, :] == t_hi16, lo_ref[sl, :], jnp.int16(i16_min))
    t_lo = _kth_largest_i16(lo_ref, n_rows, k_lo)
    return (t_hi << 16) | ((t_lo - i16_min) & 0xFFFF)


def _topk_mask_cols(key_ref, hi_ref, lo_ref, msk_ref, n_rows, k, valid_fn):
    thr = _kth_largest_cols(key_ref, hi_ref, lo_ref, n_rows, k)
    n_gt = _col_sum(jnp.where(key_ref[0:n_rows, :] > thr, 1.0, 0.0))
    need = k - n_gt
    run = jnp.zeros((1, LANES), F32)
    ri = lax.broadcasted_iota(I32, (LANES, LANES), 0)
    ci = lax.broadcasted_iota(I32, (LANES, LANES), 1)
    strict = (ri > ci).astype(BF16)
    for r0 in range(0, n_rows, LANES):
        kc = key_ref[r0:r0 + LANES, :]
        eq = kc == thr
        eqf = eq.astype(F32)
        before = _dot(strict, eqf.astype(BF16)) + run
        sel = ((kc > thr) | (eq & (before < need))) & valid_fn(r0, LANES)
        msk_ref[r0:r0 + LANES, :] = jnp.where(sel, 0.0, NEG_INF)
        run = run + jnp.sum(eqf, axis=0, keepdims=True)


def _kth_largest_rows(key_ref, k):
    rows = key_ref.shape[0]
    digits = 2 ** RADIX_BITS
    n_rounds = 32 // RADIX_BITS

    def body(rnd, r):
        shift = (32 - RADIX_BITS) - RADIX_BITS * rnd
        n_ok = jnp.zeros((rows, 1), I32)
        for i in range(1, digits):
            cand = r + (jnp.int32(i) << shift)
            cnt = jnp.sum((key_ref[...] >= cand).astype(I32), axis=1, keepdims=True)
            n_ok = n_ok + (cnt >= k).astype(I32)
        return r + (n_ok << shift)

    return lax.fori_loop(0, n_rounds, body, jnp.full((rows, 1), INT_MIN, I32))


def _topk_select_rows(key_ref, sel_ref, k, valid_fn):
    n_keys = key_ref.shape[1]
    thr = _kth_largest_rows(key_ref, k)
    n_gt = jnp.sum((key_ref[...] > thr).astype(I32), axis=1, keepdims=True)
    need = (k - n_gt).astype(F32)
    run = jnp.zeros((key_ref.shape[0], 1), F32)
    ri = lax.broadcasted_iota(I32, (LANES, LANES), 0)
    ci = lax.broadcasted_iota(I32, (LANES, LANES), 1)
    strict = (ri < ci).astype(BF16)
    for l0 in range(0, n_keys, LANES):
        kc = key_ref[:, l0:l0 + LANES]
        eq = kc == thr
        eqf = eq.astype(F32)
        before = _dot(eqf.astype(BF16), strict) + run
        sel = ((kc > thr) | (eq & (before < need))) & valid_fn(l0)
        sel_ref[:, l0:l0 + LANES] = sel.astype(F32).astype(BF16)
        run = run + jnp.sum(eqf, axis=1, keepdims=True)


def _attn_prompt_body(tab_ref, q_ref, k_ref, v_ref, qi_ref, ki_ref, sm_ref, bucket_ref, *rest, nk, topk, j0,
                      emit_transposed):
    key_scr, hi_scr, lo_scr, msk_scr, lg_scr, ot_scr, bias_scr, ka_scr = rest[-8:]
    if emit_transposed:
        k32_ref, v32_ref, o_ref, kt_ref, vt_ref, kit_ref = rest[-14:-8]

        @pl.when(pl.program_id(1) == 0)
        def _():
            kt_ref[0] = k32_ref[...].T
            vt_ref[0] = v32_ref[...].T
            kit_ref[0] = ki_ref[...].T[0:IDX_DIM, :]
    else:
        o_ref = rest[-9]
    b = pl.program_id(0)
    j = j0 + pl.program_id(1)
    QB = LANES

    @pl.when((b == 0) & (pl.program_id(1) == 0))
    def _():
        for d in range(2):
            bk = bucket_ref[d]
            for h in range(N_HEADS):
                acc = jnp.zeros(bk.shape, F32)
                for bb in range(N_BUCKETS):
                    acc = jnp.where(bk == bb, tab_ref[bb, h], acc)
                bias_scr[h, d] = (acc - tab_ref[N_BUCKETS - 1, h]) * LOG2E

    @pl.when(pl.program_id(1) == 0)
    def _():
        x = ki_ref[0:nk, :]
        hi, lo = _hi_lo(x)
        first = lax.broadcasted_iota(I32, (nk, LANES), 1) < IDX_DIM
        ka_scr[:, 0:LANES] = jnp.where(first, hi, lo)
        ka_scr[:, LANES:2 * LANES] = jnp.where(first, hi, jnp.zeros_like(hi))

    wt = sm_ref[...].T
    ka = ka_scr[...]
    score = jnp.zeros((nk, QB), F32)
    pad = jnp.zeros((QB, 2 * LANES - 3 * IDX_DIM), BF16)
    for h2 in range(IDX_HEADS // 2):
        blocks = []
        for h in (2 * h2, 2 * h2 + 1):
            qh, ql = _hi_lo(qi_ref[:, h * IDX_DIM:(h + 1) * IDX_DIM] * (IDX_DIM ** -0.5))
            blocks.append(jnp.concatenate([ql, qh, qh, pad], axis=1))
        s = _dot_nt(ka, jnp.concatenate(blocks, axis=0))
        for i, h in enumerate((2 * h2, 2 * h2 + 1)):
            score = score + jnp.maximum(s[:, i * QB:(i + 1) * QB], 0.0) * wt[SM_WI + h:SM_WI + h + 1, :]
    score = score * (IDX_HEADS ** -0.5)
    kpos = lax.broadcasted_iota(I32, (nk, QB), 0)
    qpos = j * QB + lax.broadcasted_iota(I32, (nk, QB), 1)
    score = jnp.where(kpos <= qpos, score, NEG_INF)
    key_scr[...] = _sortable_key(score)

    def valid(r0, rows):
        kp = r0 + lax.broadcasted_iota(I32, (rows, QB), 0)
        qp = j * QB + lax.broadcasted_iota(I32, (rows, QB), 1)
        return kp <= qp

    _topk_mask_cols(key_scr, hi_scr, lo_scr, msk_scr, nk, topk, valid)

    near = min(2 * QB, nk)
    start = pl.multiple_of(jnp.clip((j - 1) * QB, 0, nk - near), QB)
    msk_near = msk_scr[pl.ds(start, near), :]
    rows = lax.broadcasted_iota(I32, (nk, QB), 0)
    lg_scr[...] = jnp.where((rows >= start) & (rows < start + near), NEG_INF, msk_scr[...])
    first_is_diag = j == 0
    for g in range(N_KV_HEADS):
        kcols = slice(g * HEAD_DIM, (g + 1) * HEAD_DIM)
        vcols = slice(g * LANES, (g + 1) * LANES)
        heads = range(g * Q_PER_KV, (g + 1) * Q_PER_KV)
        qg = jnp.concatenate([(q_ref[:, h * HEAD_DIM:(h + 1) * HEAD_DIM] * (HEAD_DIM ** -0.5 * LOG2E)).astype(BF16)
                              for h in heads], axis=0)
        lg_main = _dot_nt(k_ref[0:nk, kcols], qg)
        lg_near = _dot_nt(k_ref[pl.ds(start, near), kcols], qg)
        p_main, p_near = [], []
        for r, h in enumerate(heads):
            sl = slice(r * QB, (r + 1) * QB)
            bias = jnp.where(first_is_diag, bias_scr[h, 0], bias_scr[h, 1])
            if near > QB:
                bias = jnp.concatenate([bias, bias_scr[h, 0]], axis=0)
            lg_n = (lg_near[:, sl] + bias) + msk_near
            lg_m = lg_main[:, sl] + lg_scr[...]
            m = jnp.maximum(_col_max(lg_m), _col_max(lg_n))
            p_main.append(jnp.exp2(lg_m - m).astype(BF16))
            p_near.append(jnp.exp2(lg_n - m).astype(BF16))
        pv = (_dot_tn(v_ref[0:nk, vcols], jnp.concatenate(p_main, axis=1))
              + _dot_tn(v_ref[pl.ds(start, near), vcols], jnp.concatenate(p_near, axis=1)))
        for r, h in enumerate(heads):
            sl = slice(r * QB, (r + 1) * QB)
            ot_scr[h * HEAD_DIM:(h + 1) * HEAD_DIM, :] = pv[0:HEAD_DIM, sl] * (1.0 / pv[HEAD_DIM:HEAD_DIM + 1, sl])
    o_ref[...] = ot_scr[...].T.astype(o_ref.dtype)


def attn_prompt(proj, bias_table, batch, seq):
    QB = LANES
    nq = seq // QB
    assert seq % QB == 0
    topk = min(TOPK_MAX, seq // 4)
    ts = np.arange(QB)[:, None]
    tq = np.arange(QB)[None, :]
    bucket = jnp.asarray(np.stack([_t5_bucket_np(tq - ts), _t5_bucket_np(QB + tq - ts)]))
    n_groups = math.gcd(nq, PROMPT_KEY_GROUPS)
    per_group = nq // n_groups
    kvw = N_KV_HEADS * HEAD_DIM
    k_bf = proj[:, COL_K:COL_K + kvw].astype(BF16)
    v4 = proj[:, COL_V:COL_V + kvw].astype(BF16).reshape(batch * seq, N_KV_HEADS, HEAD_DIM)
    v_aug = jnp.concatenate([v4, jnp.ones(v4.shape[:2] + (1,), BF16),
                             jnp.zeros(v4.shape[:2] + (LANES - HEAD_DIM - 1,), BF16)], axis=-1)
    v_aug = v_aug.reshape(batch * seq, N_KV_HEADS * LANES)
    out = None
    for grp in range(n_groups):
        j0 = grp * per_group
        nk = (j0 + per_group) * QB
        rowblk = lambda width, colblk: pl.BlockSpec((QB, width), lambda b, j: (b * nq + j0 + j, colblk))
        seqblk = lambda width, colblk: pl.BlockSpec((seq, width), lambda b, j: (b, colblk))
        prev = [] if out is None else [out]
        last = grp == n_groups - 1
        extra_in = [seqblk(kvw, COL_K // kvw), seqblk(kvw, COL_V // kvw)] if last else []
        tblk = lambda rows: pl.BlockSpec((1, rows, seq), lambda b, j: (b, 0, 0))
        tshape = lambda rows: jax.ShapeDtypeStruct((batch, rows, seq), F32)
        res = pl.pallas_call(
            functools.partial(_attn_prompt_body, nk=nk, topk=topk, j0=j0, emit_transposed=last),
            grid=(batch, per_group),
            in_specs=[
                pl.BlockSpec(memory_space=pltpu.SMEM),
                rowblk(N_HEADS * HEAD_DIM, COL_Q // (N_HEADS * HEAD_DIM)),
                seqblk(kvw, 0),
                seqblk(N_KV_HEADS * LANES, 0),
                rowblk(IDX_HEADS * IDX_DIM, COL_QI // (IDX_HEADS * IDX_DIM)),
                seqblk(LANES, COL_KI // LANES),
                rowblk(LANES, COL_SM // LANES),
                pl.BlockSpec((2, QB, QB), lambda b, j: (0, 0, 0)),
            ] + [pl.BlockSpec(memory_space=pl.ANY) for _ in prev] + extra_in,
            out_specs=[pl.BlockSpec((QB, N_HEADS * HEAD_DIM), lambda b, j: (b * nq + j0 + j, 0))]
                      + ([tblk(kvw), tblk(kvw), tblk(IDX_DIM)] if last else []),
            out_shape=[jax.ShapeDtypeStruct((batch * seq, N_HEADS * HEAD_DIM), BF16)]
                      + ([tshape(kvw), tshape(kvw), tshape(IDX_DIM)] if last else []),
            input_output_aliases={8: 0} if prev else {},
            scratch_shapes=[
                pltpu.VMEM((nk, QB), I32),
                pltpu.VMEM((nk, QB), jnp.int16),
                pltpu.VMEM((nk, QB), jnp.int16),
                pltpu.VMEM((nk, QB), F32),
                pltpu.VMEM((nk, QB), F32),
                pltpu.VMEM((N_HEADS * HEAD_DIM, QB), F32),
                pltpu.VMEM((N_HEADS, 2, QB, QB), F32),
                pltpu.VMEM((nk, 2 * LANES), BF16),
            ],
            compiler_params=_cparams(("arbitrary", "arbitrary")),
            name=f"attn_prompt_g{grp}",
        )(bias_table, proj, k_bf, v_aug, proj, proj, proj, bucket, *prev, *([proj, proj] if last else []))
        out = res[0]
    assert nk == seq
    return out, res[1:]


def _attn_sample_body(pt_ref, q_ref, qi_ref, w_ref, kn_ref, vn_ref, kin_ref, tabt_ref, bucket_ref, gsum_ref, xpand_ref,
                      *rest, n_seq, n_pages, page, t_new, topk):
    n_pg = n_seq * n_pages
    ck = rest[0:n_pg]
    cv = rest[n_pg:2 * n_pg]
    cki = rest[2 * n_pg:3 * n_pg]
    o_ref, sc_scr, key_scr, sel_scr, lg_scr, p_scr, bias_scr, ki_st, k_st, v_st = rest[3 * n_pg:]
    past = n_pages * page
    n_tiles = n_pages + 1
    rows_q = t_new * N_HEADS
    rows_g = t_new * Q_PER_KV

    @pl.when(pl.program_id(0) == 0)
    def _():
        bk = bucket_ref[...]
        acc = jnp.zeros(bk.shape, F32)
        for bb in range(N_BUCKETS):
            acc = jnp.where(bk == bb, tabt_ref[:, bb:bb + 1], acc)
        bias_scr[...] = acc - tabt_ref[:, N_BUCKETS - 1:N_BUCKETS]

    for s in range(n_seq):
        for c in range(n_pages):
            sl = slice(c * page, (c + 1) * page)
            ki_st[s, :, sl] = cki[s * n_pages + c][0]
            k_st[s, :, :, sl] = ck[s * n_pages + c][0].astype(BF16)
            v_st[s, :, :, sl] = cv[s * n_pages + c][0].astype(BF16)

    for s in range(n_seq):
        qh, ql = _hi_lo(qi_ref[s])
        wcol = w_ref[s]
        gsum = gsum_ref[s]

        def idx_scores(kt_h, kt_l, dot):
            sc = (dot(qh, kt_l) + dot(ql, kt_h)) + dot(qh, kt_h)
            sc = jnp.maximum(sc * (IDX_DIM ** -0.5), 0.0) * wcol
            return _sel_dot_exact(gsum, sc) * (IDX_HEADS ** -0.5)

        main = idx_scores(*_hi_lo(ki_st[s]), _dot)
        tail = idx_scores(*_hi_lo(kin_ref[s]), _dot_nt)
        sc_scr[:, 0:past] = main if s == 0 else sc_scr[:, 0:past] + main
        sc_scr[:, past:] = tail if s == 0 else sc_scr[:, past:] + tail

    n_keys = n_tiles * page
    n_rows = n_seq * t_new

    def valid(l0):
        kp = l0 + lax.broadcasted_iota(I32, (n_rows, LANES), 1)
        row = lax.broadcasted_iota(I32, (n_rows, LANES), 0)
        tok = row
        for s in range(1, n_seq):
            tok = jnp.where(row >= s * t_new, row - s * t_new, tok)
        return kp <= past + tok

    for c in range(n_tiles):
        sl = slice(c * page, (c + 1) * page)
        key_scr[:, sl] = _sortable_key(jnp.where(valid(c * page), sc_scr[:, sl], NEG_INF))
    _topk_select_rows(key_scr, sel_scr, topk, valid)

    for s in range(n_seq):
        for g in range(N_KV_HEADS):
            rs = slice(g * rows_g, (g + 1) * rows_g)
            cols = slice(g * HEAD_DIM, (g + 1) * HEAD_DIM)
            qg = (q_ref[s, rs, :] * (HEAD_DIM ** -0.5)).astype(BF16)
            lg_scr[rs, 0:past] = _dot(qg, k_st[s, g])
            lg_scr[rs, past:] = _dot_nt(qg, kn_ref[s, :, cols].astype(BF16))
        near = slice(past - page, n_keys)
        lg_scr[:, near] = lg_scr[:, near] + bias_scr[...]
        picked = _dot(xpand_ref[s], sel_scr[...]) > 0.5
        lg = jnp.where(picked, lg_scr[...], NEG_INF)
        m = jnp.max(lg, axis=1, keepdims=True)
        p = jnp.exp(lg - m)
        p_scr[...] = (p * (1.0 / jnp.sum(p, axis=1, keepdims=True))).astype(BF16)
        for g in range(N_KV_HEADS):
            rs = slice(g * rows_g, (g + 1) * rows_g)
            cols = slice(g * HEAD_DIM, (g + 1) * HEAD_DIM)
            acc = _dot_nt(p_scr[rs, 0:past], v_st[s, g]) + _dot(p_scr[rs, past:], vn_ref[s, :, cols].astype(BF16))
            o_ref[s, rs, :] = acc.astype(o_ref.dtype)


def attn_sample(proj_s, cache_k, cache_v, cache_kidx, page_table, bias_table, t_new):
    nb, n_pages = page_table.shape
    n_pool, page = cache_k.shape[0], cache_k.shape[1]
    past = n_pages * page
    total = past + t_new
    topk = min(TOPK_MAX, total // 4)
    kvw = N_KV_HEADS * HEAD_DIM
    n_seq = SAMPLE_SEQS_PER_STEP if nb % SAMPLE_SEQS_PER_STEP == 0 else 1
    rows_q = t_new * N_HEADS
    rows_i = t_new * IDX_HEADS
    n_rows = n_seq * t_new
    n_keys = past + page
    assert page == LANES and page >= MAX_DISTANCE and t_new <= page
    ck_t = jnp.transpose(cache_k, (0, 2, 3, 1))
    cv_t = jnp.transpose(cache_v, (0, 2, 3, 1))
    cki_t = jnp.transpose(cache_kidx, (0, 2, 1))
    seg = lambda col, width: proj_s[:, col:col + width]
    q3 = seg(COL_Q, N_HEADS * HEAD_DIM).reshape(nb, t_new, N_KV_HEADS, Q_PER_KV, HEAD_DIM)
    q3 = q3.transpose(0, 2, 1, 3, 4).reshape(nb, rows_q, HEAD_DIM)
    qi3 = seg(COL_QI, IDX_HEADS * IDX_DIM).reshape(nb, rows_i, IDX_DIM)
    w3 = seg(COL_SM + SM_WI, IDX_HEADS).reshape(nb, rows_i, 1)
    pad_rows = lambda a: jnp.pad(a.reshape(nb, t_new, -1), ((0, 0), (0, page - t_new), (0, 0)))
    kn3, vn3, kin3 = pad_rows(seg(COL_K, kvw)), pad_rows(seg(COL_V, kvw)), pad_rows(seg(COL_KI, IDX_DIM))
    row = np.arange(rows_q)
    row_tok = (row // Q_PER_KV) % t_new
    row_head = (row // (t_new * Q_PER_KV)) * Q_PER_KV + row % Q_PER_KV
    tab_t = jnp.pad(bias_table.T[row_head], ((0, 0), (0, LANES - N_BUCKETS)))
    kpos = np.concatenate([past - page + np.arange(page), past + np.arange(page)])[None, :]
    bucket = jnp.asarray(_t5_bucket_np(past + row_tok[:, None] - kpos))
    gsum = np.zeros((n_seq, n_rows, rows_i), np.float32)
    xpand = np.zeros((n_seq, rows_q, n_rows), np.float32)
    for s in range(n_seq):
        gsum[s, s * t_new + np.arange(rows_i) // IDX_HEADS, np.arange(rows_i)] = 1.0
        xpand[s, row, s * t_new + row_tok] = 1.0
    per_step = lambda shape: pl.BlockSpec((n_seq,) + shape, lambda b, pt: (b,) + tuple(0 for _ in shape))
    const = lambda shape: pl.BlockSpec(shape, lambda b, pt: tuple(0 for _ in shape))

    def page_specs(shape):
        return [pl.BlockSpec((1,) + shape, lambda b, pt, s=s, c=c: (pt[b * n_seq + s, c],) + tuple(0 for _ in shape))
                for s in range(n_seq) for c in range(n_pages)]

    grid_spec = pltpu.PrefetchScalarGridSpec(
        num_scalar_prefetch=1,
        grid=(nb // n_seq,),
        in_specs=[per_step((rows_q, HEAD_DIM)), per_step((rows_i, IDX_DIM)), per_step((rows_i, 1)),
                  per_step((page, kvw)), per_step((page, kvw)), per_step((page, IDX_DIM)),
                  const((rows_q, LANES)), const((rows_q, 2 * page)),
                  const((n_seq, n_rows, rows_i)), const((n_seq, rows_q, n_rows))]
                 + page_specs((N_KV_HEADS, HEAD_DIM, page)) + page_specs((N_KV_HEADS, HEAD_DIM, page))
                 + page_specs((IDX_DIM, page)),
        out_specs=pl.BlockSpec((n_seq, rows_q, HEAD_DIM), lambda b, pt: (b, 0, 0)),
        scratch_shapes=[
            pltpu.VMEM((n_rows, n_keys), F32),
            pltpu.VMEM((n_rows, n_keys), I32),
            pltpu.VMEM((n_rows, n_keys), BF16),
            pltpu.VMEM((rows_q, n_keys), F32),
            pltpu.VMEM((rows_q, n_keys), BF16),
            pltpu.VMEM((rows_q, 2 * page), F32),
            pltpu.VMEM((n_seq, IDX_DIM, past), F32),
            pltpu.VMEM((n_seq, N_KV_HEADS, HEAD_DIM, past), BF16),
            pltpu.VMEM((n_seq, N_KV_HEADS, HEAD_DIM, past), BF16),
        ],
    )
    n_pg = n_seq * n_pages
    out = pl.pallas_call(
        functools.partial(_attn_sample_body, n_seq=n_seq, n_pages=n_pages, page=page, t_new=t_new, topk=topk),
        grid_spec=grid_spec,
        out_shape=jax.ShapeDtypeStruct((nb, rows_q, HEAD_DIM), BF16),
        compiler_params=_cparams(("arbitrary",)),
        name="attn_sample",
    )(page_table, q3, qi3, w3, kn3, vn3, kin3, tab_t, bucket, jnp.asarray(gsum, BF16), jnp.asarray(xpand, BF16),
      *([ck_t] * n_pg), *([cv_t] * n_pg), *([cki_t] * n_pg))
    out = out.reshape(nb, N_KV_HEADS, t_new, Q_PER_KV, HEAD_DIM).transpose(0, 2, 1, 3, 4)
    return out.reshape(nb * t_new, N_HEADS * HEAD_DIM)


def _merge_body(ys_ref, ya_ref, g_ref, h_ref, wbs_ref, wba_ref, wo_ref, lg_ref, lb_ref, o_ref):
    a = _dot(ys_ref[...], wbs_ref[...])
    b = _dot(ya_ref[...], wba_ref[...])
    gates = jax.nn.sigmoid(g_ref[...])
    merged = gates[:, :D_MODEL] * a + gates[:, D_MODEL:] * b
    r = ALPHA * h_ref[...] + _dot(merged.astype(BF16), wo_ref[...])
    o_ref[...] = _layer_norm(r, lg_ref[...], lb_ref[...])


def merge_out(y_ssm, y_att, proj, h1, wbs, wba, wo, g, b, *, tm=512):
    m = h1.shape[0]
    tm = min(tm, m)
    assert m % tm == 0
    row = lambda width, colblk=0: pl.BlockSpec((tm, width), lambda i: (i, colblk))
    full = lambda shape: pl.BlockSpec(shape, lambda i: tuple(0 for _ in shape))
    return pl.pallas_call(
        _merge_body,
        grid=(m // tm,),
        in_specs=[row(D_INNER), row(N_HEADS * HEAD_DIM), row(2 * D_MODEL, COL_G // (2 * D_MODEL)), row(D_MODEL),
                  full(wbs.shape), full(wba.shape), full(wo.shape), full((1, D_MODEL)), full((1, D_MODEL))],
        out_specs=row(D_MODEL),
        out_shape=jax.ShapeDtypeStruct((m, D_MODEL), F32),
        compiler_params=_cparams(("parallel",)),
        name="merge_out",
    )(y_ssm, y_att, proj, h1, wbs, wba, wo, g, b)


def kernel(x_prompt, x_sample, cache_k, cache_v, cache_kidx, state_conv, state_ssm, page_table, bias_table, w_in, conv_w, conv_b, dt_bias, a_log, d_skip, ssm_norm_g, w_branch_ssm, w_branch_attn, w_out, ffn1_wi, ffn1_wo, ffn2_wi, ffn2_wo, ln1_g, ln1_b, ln2_g, ln2_b, ln3_g, ln3_b):
    assert w_in.shape[0] == DEPTH
    batch, seq, d = x_prompt.shape
    nb, t_new, _ = x_sample.shape
    xs = (x_prompt.reshape(batch * seq, d), x_sample.reshape(nb * t_new, d))
    outs_p, outs_s = [], []
    for i in range(DEPTH):
        w_pack = pack_w_in(w_in[i])
        f1i, f1o = ffn1_wi[i].astype(BF16), ffn1_wo[i].astype(BF16)
        f2i, f2o = ffn2_wi[i].astype(BF16), ffn2_wo[i].astype(BF16)
        wbs, wba, wo = w_branch_ssm[i].astype(BF16), w_branch_attn[i].astype(BF16), w_out[i].astype(BF16)
        row = lambda v: v.reshape(1, -1)
        pad_heads = lambda v: jnp.concatenate([v, jnp.zeros((LANES - SSM_HEADS,), v.dtype)]).reshape(1, LANES)
        dtb_pad, alog_pad = pad_heads(dt_bias[i]), pad_heads(a_log[i])
        dskip_e = jnp.repeat(d_skip[i], SSM_HEAD_DIM).reshape(1, D_INNER)
        ng = row(ssm_norm_g[i])
        cw, cb = conv_w[i], row(conv_b[i])

        h1, proj = [], []
        for x in xs:
            hf, hb = ffn_ln(x, f1i, f1o, row(ln1_g[i]), row(ln1_b[i]), emit_bf16=True)
            h1.append(hf)
            proj.append(matmul(hb, w_pack, tm=2048, tn=PROJ_TN))
        proj_p, proj_s = proj

        ys_p, conv_p, ssm_p = ssd_prompt(proj_p, batch, seq, cw, cb, dtb_pad, alog_pad, dskip_e, ng)
        ya_p, (kt_p, vt_p, kit_p) = attn_prompt(proj_p, bias_table, batch, seq)

        raw_s = jnp.concatenate([proj_s[:, COL_X:COL_X + D_INNER], proj_s[:, COL_BC:COL_BC + 2 * GN]], axis=1)
        u_cat = jnp.concatenate([state_conv[i], raw_s.reshape(nb, t_new, CONV_DIM)], axis=1)
        expand = np.zeros((LANES, D_INNER), np.float32)
        expand[np.arange(D_INNER) // SSM_HEAD_DIM, np.arange(D_INNER)] = 1.0
        ys_s, conv_s, ssm_s = ssd_sample(u_cat, proj_s.reshape(nb, t_new, PROJ_N),
                                         state_ssm[i].reshape(nb, SSM_HEADS * SSM_HEAD_DIM, SSM_STATE),
                                         cw, cb, dtb_pad, alog_pad, dskip_e, ng, jnp.asarray(expand, BF16))
        ya_s = attn_sample(proj_s, cache_k[i], cache_v[i], cache_kidx[i], page_table, bias_table, t_new)

        new_xs = []
        for hf, pj, ys, ya in ((h1[0], proj_p, ys_p, ya_p), (h1[1], proj_s, ys_s.reshape(nb * t_new, D_INNER), ya_s)):
            h2 = merge_out(ys, ya, pj, hf, wbs, wba, wo, row(ln2_g[i]), row(ln2_b[i]))
            new_xs.append(ffn_ln(h2, f2i, f2o, row(ln3_g[i]), row(ln3_b[i]), emit_bf16=False))
        xs = tuple(new_xs)

        kvw = N_KV_HEADS * HEAD_DIM
        shp = lambda nb_, l_: (nb_, l_, N_KV_HEADS, HEAD_DIM)
        head_major = lambda a: a.reshape(batch, N_KV_HEADS, HEAD_DIM, seq).transpose(0, 3, 1, 2)
        outs_p.append((head_major(kt_p), head_major(vt_p), kit_p.transpose(0, 2, 1),
                       conv_p, ssm_p.reshape(batch, SSM_HEADS, SSM_HEAD_DIM, SSM_STATE)))
        outs_s.append((proj_s[:, COL_K:COL_K + kvw].reshape(shp(nb, t_new)),
                       proj_s[:, COL_V:COL_V + kvw].reshape(shp(nb, t_new)),
                       proj_s[:, COL_KI:COL_KI + IDX_DIM].reshape(nb, t_new, IDX_DIM),
                       conv_s, ssm_s.reshape(nb, SSM_HEADS, SSM_HEAD_DIM, SSM_STATE)))
    k_p, v_p, kidx_p, conv_pp, ssm_pp = [jnp.stack(a) for a in zip(*outs_p)]
    k_s, v_s, kidx_s, conv_ss, ssm_ss = [jnp.stack(a) for a in zip(*outs_s)]
    return (xs[0].reshape(batch, seq, d), xs[1].reshape(nb, t_new, d),
            k_p, v_p, kidx_p, conv_pp, ssm_pp, k_s, v_s, kidx_s, conv_ss, ssm_ss)
```

```python
import functools
import math

import numpy as np
import jax
import jax.numpy as jnp
from jax import lax
from jax.experimental import pallas as pl
from jax.experimental.pallas import tpu as pltpu

F32 = jnp.float32
BF16 = jnp.bfloat16
I32 = jnp.int32

D_MODEL = 1024
D_INNER = 2 * D_MODEL
SSM_HEAD_DIM = 64
SSM_HEADS = D_INNER // SSM_HEAD_DIM
SSM_GROUPS = 4
SSM_STATE = 128
CONV_W = 4
CONV_DIM = D_INNER + 2 * SSM_GROUPS * SSM_STATE
SSD_CHUNK = 128
N_HEADS = 16
HEAD_DIM = 64
N_KV_HEADS = 4
Q_PER_KV = N_HEADS // N_KV_HEADS
IDX_HEADS = 8
IDX_DIM = 64
TOPK_MAX = 256
N_BUCKETS = 32
MAX_DISTANCE = 128
D_FF = 2816
DEPTH = 1
ALPHA = (2 * DEPTH) ** 0.25
LN_EPS = 1e-5
IN_SPLITS = (D_INNER, CONV_DIM, SSM_HEADS, N_HEADS * HEAD_DIM, N_KV_HEADS * HEAD_DIM,
             N_KV_HEADS * HEAD_DIM, IDX_HEADS * IDX_DIM, IDX_DIM, IDX_HEADS, 2 * D_MODEL)

LANES = 128
SUBLANES = 8
VMEM_LIMIT_BYTES = 56 * 1024 * 1024

GN = SSM_GROUPS * SSM_STATE
COL_Z = 0
COL_X = COL_Z + D_INNER
COL_G = COL_X + D_INNER
COL_BC = COL_G + 2 * D_MODEL
COL_Q = COL_BC + 2 * GN
COL_K = COL_Q + N_HEADS * HEAD_DIM
COL_V = COL_K + N_KV_HEADS * HEAD_DIM
COL_QI = COL_V + N_KV_HEADS * HEAD_DIM
COL_SM = COL_QI + IDX_HEADS * IDX_DIM
COL_KI = COL_SM + LANES
PROJ_TN = 512
PROJ_N = -(-(COL_KI + LANES) // PROJ_TN) * PROJ_TN
SM_WI = SSM_HEADS

INT_MIN = -(2 ** 31)
LOG2E = math.log2(math.e)
NEG_INF = float("-inf")
PROMPT_KEY_GROUPS = 8
SAMPLE_SEQS_PER_STEP = 4
SSD_SAMPLE_SEQS_PER_STEP = 2
RADIX_BITS = 4


def _cparams(sem):
    return pltpu.CompilerParams(dimension_semantics=sem, vmem_limit_bytes=VMEM_LIMIT_BYTES)


def _layer_norm(r, g, b):
    mu = jnp.mean(r, axis=-1, keepdims=True)
    d = r - mu
    var = jnp.mean(d * d, axis=-1, keepdims=True)
    return d * lax.rsqrt(var + LN_EPS) * g + b


def _silu(x):
    h = 0.5 * x
    return h + h * jnp.tanh(h)


def _softplus(x):
    return jnp.maximum(x, 0.0) + jnp.log1p(jnp.exp(-jnp.abs(x)))


def _dot(a, b):
    return jnp.dot(a, b, preferred_element_type=F32)


def _dot_nt(a, b):
    return lax.dot_general(a, b, (((1,), (1,)), ((), ())), preferred_element_type=F32)


def _dot_tn(a, b):
    return lax.dot_general(a, b, (((0,), (0,)), ((), ())), preferred_element_type=F32)


def _split3(v):
    hi = v.astype(BF16)
    r1 = v - hi.astype(F32)
    mid = r1.astype(BF16)
    lo = (r1 - mid.astype(F32)).astype(BF16)
    return hi, mid, lo


def _dot_exact_sel(v, sel_bf16):
    hi, mid, lo = _split3(v)
    return (_dot(hi, sel_bf16) + _dot(mid, sel_bf16)) + _dot(lo, sel_bf16)


def _sel_dot_exact(sel_bf16, v):
    hi, mid, lo = _split3(v)
    return (_dot(sel_bf16, hi) + _dot(sel_bf16, mid)) + _dot(sel_bf16, lo)


def _hi_lo(a):
    ah = a.astype(BF16)
    return ah, (a - ah.astype(F32)).astype(BF16)


def _dot_nt_x3(a, b):
    ah, al = _hi_lo(a)
    bh, bl = _hi_lo(b)
    return (_dot_nt(ah, bl) + _dot_nt(al, bh)) + _dot_nt(ah, bh)


def _dot_x3(a, b):
    ah, al = _hi_lo(a)
    bh, bl = _hi_lo(b)
    return (_dot(ah, bl) + _dot(al, bh)) + _dot(ah, bh)


def _col_reduce(x, op2, op):
    parts = []
    for r0 in range(0, x.shape[0], LANES):
        y = x[r0:r0 + LANES]
        n = y.shape[0]
        while n > SUBLANES and n % (2 * SUBLANES) == 0:
            n //= 2
            y = op2(y[:n], y[n:])
        parts.append(y)
    while len(parts) > 1:
        parts = [op2(parts[i], parts[i + 1]) if i + 1 < len(parts) else parts[i] for i in range(0, len(parts), 2)]
    return op(parts[0], axis=0, keepdims=True)


def _col_sum(x):
    return _col_reduce(x, jnp.add, jnp.sum)


def _col_max(x):
    return _col_reduce(x, jnp.maximum, jnp.max)


def _ffn_ln_body(x_ref, wi_ref, wo_ref, g_ref, b_ref, o_ref, *maybe_ob_ref, tf):
    x = x_ref[...]
    xb = x.astype(BF16)
    dff = wo_ref.shape[0]
    acc = None
    for c0 in range(0, dff, tf):
        gate = _dot(xb, wi_ref[:, c0:c0 + tf])
        up = _dot(xb, wi_ref[:, dff + c0:dff + c0 + tf])
        part = _dot((_silu(gate) * up).astype(BF16), wo_ref[c0:c0 + tf, :])
        acc = part if acc is None else acc + part
    y = _layer_norm(ALPHA * x + 0.5 * acc, g_ref[...], b_ref[...])
    o_ref[...] = y
    for ob_ref in maybe_ob_ref:
        ob_ref[...] = y.astype(BF16)


def ffn_ln(x, wi_bf, wo_bf, g, b, *, emit_bf16, tm=512, tf=1408):
    m, d = x.shape
    dff = wo_bf.shape[0]
    tm = min(tm, m)
    assert m % tm == 0 and dff % tf == 0
    row = pl.BlockSpec((tm, d), lambda i: (i, 0))
    resident = lambda shape: pl.BlockSpec(shape, lambda i: (0, 0), pipeline_mode=pl.Buffered(1))
    n_out = 2 if emit_bf16 else 1
    res = pl.pallas_call(
        functools.partial(_ffn_ln_body, tf=tf),
        grid=(m // tm,),
        in_specs=[row, resident(wi_bf.shape), resident(wo_bf.shape), resident((1, d)), resident((1, d))],
        out_specs=[row] * n_out,
        out_shape=[jax.ShapeDtypeStruct((m, d), F32), jax.ShapeDtypeStruct((m, d), BF16)][:n_out],
        compiler_params=_cparams(("parallel",)),
        name="ffn_ln",
    )(x, wi_bf, wo_bf, g, b)
    return res if emit_bf16 else res[0]


def _matmul_body(x_ref, w_ref, o_ref):
    o_ref[...] = _dot(x_ref[...], w_ref[...])


def matmul(x_bf, w_bf, *, tm, tn):
    m, k = x_bf.shape
    n = w_bf.shape[1]
    tm = min(tm, m)
    assert m % tm == 0 and n % tn == 0
    return pl.pallas_call(
        _matmul_body,
        grid=(m // tm, n // tn),
        in_specs=[pl.BlockSpec((tm, k), lambda i, j: (i, 0)),
                  pl.BlockSpec((k, tn), lambda i, j: (0, j))],
        out_specs=pl.BlockSpec((tm, tn), lambda i, j: (i, j)),
        out_shape=jax.ShapeDtypeStruct((m, n), F32),
        compiler_params=_cparams(("parallel", "arbitrary")),
        name="in_proj",
    )(x_bf, w_bf)


def pack_w_in(w_in):
    offs = np.cumsum((0,) + IN_SPLITS)
    z, xbc, dt, q, k, v, qi, ki, wi, gates = [w_in[:, offs[i]:offs[i + 1]] for i in range(len(IN_SPLITS))]
    d = w_in.shape[0]
    zeros = lambda n: jnp.zeros((d, n), w_in.dtype)
    small = jnp.concatenate([dt, wi, zeros(LANES - SSM_HEADS - IDX_HEADS)], axis=1)
    kib = jnp.concatenate([ki, ki], axis=1)
    packed = jnp.concatenate([z, xbc[:, :D_INNER], gates, xbc[:, D_INNER:], q, k, v, qi, small, kib,
                              zeros(PROJ_N - COL_KI - LANES)], axis=1)
    return packed.astype(BF16)


def _ssd_prompt_body(z_ref, xr_ref, bc_ref, sm_ref, cwx_ref, cwbc_ref, cbx_ref, cbbc_ref,
                     dtb_ref, alog_ref, dskip_ref, ng_ref,
                     y_ref, cs_ref, hs_ref,
                     xbuf, bcbuf, h_scr, xc_scr, bcc_scr, y_scr):
    c = pl.program_id(1)
    nc = pl.num_programs(1)
    L = SSD_CHUNK
    halo = SUBLANES
    n_pairs = SSM_HEADS // 2
    pairs_per_group = n_pairs // SSM_GROUPS

    @pl.when(c == 0)
    def _():
        xbuf[0:halo, :] = jnp.zeros((halo, D_INNER), F32)
        bcbuf[0:halo, :] = jnp.zeros((halo, 2 * GN), F32)
        h_scr[...] = jnp.zeros_like(h_scr)

    xbuf[halo:halo + L, :] = xr_ref[...]
    bcbuf[halo:halo + L, :] = bc_ref[...]

    def conv(buf, w_ref, b_ref):
        full = buf[0:halo + L, :]
        acc = b_ref[...] + w_ref[CONV_W - 1:CONV_W, :] * full[halo:halo + L]
        for k in range(CONV_W - 1):
            shifted = pltpu.roll(full, CONV_W - 1 - k, axis=0)[halo:halo + L]
            acc = acc + w_ref[k:k + 1, :] * shifted
        return _silu(acc)

    xc_scr[...] = conv(xbuf, cwx_ref, cbx_ref)
    bcc_scr[...] = conv(bcbuf, cwbc_ref, cbbc_ref)
    xbuf[0:halo, :] = xbuf[L:L + halo, :]
    bcbuf[0:halo, :] = bcbuf[L:L + halo, :]

    dt = _softplus(sm_ref[...] + dtb_ref[...])
    a = -jnp.exp(alog_ref[...])
    row = lax.broadcasted_iota(I32, (L, L), 0)
    col = lax.broadcasted_iota(I32, (L, L), 1)
    tril = (row >= col).astype(F32)
    cum = jnp.dot(tril, dt * a, precision=lax.Precision.HIGHEST, preferred_element_type=F32)
    w = jnp.exp(cum[L - 1:L, :] - cum) * dt
    cum_t = cum.T
    dt_t = dt.T
    causal = row >= col
    lane = lax.broadcasted_iota(I32, (L, LANES), 1)
    lo_half = lane < SSM_HEAD_DIM
    sub = lax.broadcasted_iota(I32, (2 * SSM_HEAD_DIM, SSM_STATE), 0)
    lo_rows = sub < SSM_HEAD_DIM

    for i in range(n_pairs):
        g = i // pairs_per_group
        h0, h1 = 2 * i, 2 * i + 1
        sl = slice(i * LANES, (i + 1) * LANES)
        if i % pairs_per_group == 0:
            bg = bcc_scr[:, g * SSM_STATE:(g + 1) * SSM_STATE].astype(BF16)
            cg = bcc_scr[:, GN + g * SSM_STATE:GN + (g + 1) * SSM_STATE].astype(BF16)
            cb = _dot_nt(cg, bg)
        xp = xc_scr[:, sl]
        c0 = jnp.broadcast_to(cum[:, h0:h0 + 1], (L, LANES))
        c1 = jnp.broadcast_to(cum[:, h1:h1 + 1], (L, LANES))
        w0 = jnp.broadcast_to(w[:, h0:h0 + 1], (L, LANES))
        w1 = jnp.broadcast_to(w[:, h1:h1 + 1], (L, LANES))

        def mix(ct, h):
            seg = jnp.exp(jnp.where(causal, ct - cum_t[h:h + 1, :], NEG_INF))
            return (cb * seg * dt_t[h:h + 1, :]).astype(BF16)

        zero = jnp.zeros_like(xp)
        yd = (_dot(mix(c0, h0), jnp.where(lo_half, xp, zero).astype(BF16))
              + _dot(mix(c1, h1), jnp.where(lo_half, zero, xp).astype(BF16)))
        hp = h_scr[i]
        yo = _dot_nt(cg, hp.astype(BF16)) * jnp.where(lo_half, jnp.exp(c0), jnp.exp(c1))
        y_scr[:, sl] = yd + yo + dskip_ref[:, sl] * xp
        xw = (xp * jnp.where(lo_half, w0, w1)).astype(BF16)
        st = _dot_tn(xw, bg)
        dec = jnp.where(lo_rows, jnp.exp(c0[L - 1:L, :]), jnp.exp(c1[L - 1:L, :]))
        h_scr[i] = dec * hp + st

    gw = D_INNER // SSM_GROUPS
    for g in range(SSM_GROUPS):
        sl = slice(g * gw, (g + 1) * gw)
        u = y_scr[:, sl] * _silu(z_ref[:, sl])
        ms = jnp.mean(u * u, axis=-1, keepdims=True)
        y_ref[:, sl] = (u * lax.rsqrt(ms + LN_EPS) * ng_ref[:, sl]).astype(y_ref.dtype)

    @pl.when(c == nc - 1)
    def _():
        cs_ref[0, :, 0:D_INNER] = xr_ref[L - (CONV_W - 1):L, :]
        cs_ref[0, :, D_INNER:CONV_DIM] = bc_ref[L - (CONV_W - 1):L, :]
        for i in range(n_pairs):
            hs_ref[0, i * LANES:(i + 1) * LANES, :] = h_scr[i]


def ssd_prompt(proj, batch, seq, conv_w, conv_b, dtb_pad, alog_pad, dskip_e, norm_g):
    L = SSD_CHUNK
    nc = seq // L
    assert seq % L == 0
    blk = lambda width, colblk: pl.BlockSpec((L, width), lambda b, c: (b * nc + c, colblk))
    full = lambda shape: pl.BlockSpec(shape, lambda b, c: tuple(0 for _ in shape))
    n_pairs = SSM_HEADS // 2
    return pl.pallas_call(
        _ssd_prompt_body,
        grid=(batch, nc),
        in_specs=[
            blk(D_INNER, COL_Z // D_INNER),
            blk(D_INNER, COL_X // D_INNER),
            blk(2 * GN, COL_BC // (2 * GN)),
            blk(LANES, COL_SM // LANES),
            full((CONV_W, D_INNER)), full((CONV_W, 2 * GN)), full((1, D_INNER)), full((1, 2 * GN)),
            full((1, LANES)), full((1, LANES)), full((1, D_INNER)), full((1, D_INNER)),
        ],
        out_specs=[
            pl.BlockSpec((L, D_INNER), lambda b, c: (b * nc + c, 0)),
            pl.BlockSpec((1, CONV_W - 1, CONV_DIM), lambda b, c: (b, 0, 0)),
            pl.BlockSpec((1, SSM_HEADS * SSM_HEAD_DIM, SSM_STATE), lambda b, c: (b, 0, 0)),
        ],
        out_shape=[
            jax.ShapeDtypeStruct((batch * seq, D_INNER), BF16),
            jax.ShapeDtypeStruct((batch, CONV_W - 1, CONV_DIM), F32),
            jax.ShapeDtypeStruct((batch, SSM_HEADS * SSM_HEAD_DIM, SSM_STATE), F32),
        ],
        scratch_shapes=[
            pltpu.VMEM((L + 2 * SUBLANES, D_INNER), F32),
            pltpu.VMEM((L + 2 * SUBLANES, 2 * GN), F32),
            pltpu.VMEM((n_pairs, 2 * SSM_HEAD_DIM, SSM_STATE), F32),
            pltpu.VMEM((L, D_INNER), F32),
            pltpu.VMEM((L, 2 * GN), F32),
            pltpu.VMEM((L, D_INNER), F32),
        ],
        compiler_params=_cparams(("parallel", "arbitrary")),
        name="ssd_prompt",
    )(proj, proj, proj, proj, conv_w[:, :D_INNER], conv_w[:, D_INNER:], conv_b[:, :D_INNER], conv_b[:, D_INNER:],
      dtb_pad, alog_pad, dskip_e, norm_g)


def _ssd_sample_body(u_ref, z_ref, sm_ref, h0_ref, cw_ref, cb_ref, dtb_ref, alog_ref, dskip_ref, ng_ref, exp_ref,
                     y_ref, cs_ref, hn_ref, *, t_new, n_seq):
    for s in range(n_seq):
        _ssd_sample_one(s, u_ref, z_ref, sm_ref, h0_ref, cw_ref, cb_ref, dtb_ref, alog_ref, dskip_ref, ng_ref, exp_ref,
                        y_ref, cs_ref, hn_ref, t_new)


def _ssd_sample_one(s, u_ref, z_ref, sm_ref, h0_ref, cw_ref, cb_ref, dtb_ref, alog_ref, dskip_ref, ng_ref, exp_ref,
                    y_ref, cs_ref, hn_ref, t_new):
    T = t_new
    n_pairs = SSM_HEADS // 2
    pairs_per_group = n_pairs // SSM_GROUPS
    acc = cb_ref[...] + cw_ref[0:1, :] * u_ref[s, 0:T, :]
    for k in range(1, CONV_W):
        acc = acc + cw_ref[k:k + 1, :] * u_ref[s, k:k + T, :]
    xbc = _silu(acc)
    cs_ref[s] = u_ref[s, T:T + CONV_W - 1, :]
    x = xbc[:, :D_INNER]

    dt = _softplus(sm_ref[s] + dtb_ref[...])
    da = dt * (-jnp.exp(alog_ref[...]))
    rows = [da[0:1, :]]
    for t in range(1, T):
        rows.append(rows[-1] + da[t:t + 1, :])
    cum = jnp.concatenate(rows, axis=0)
    both = _dot_exact_sel(jnp.concatenate([cum, dt], axis=0), exp_ref[...])
    cum_e = both[0:T, :]
    dt_e = both[T:2 * T, :]
    t_idx = lax.broadcasted_iota(I32, (T, GN), 0)

    gw = D_INNER // SSM_GROUPS
    y_parts = []
    for g in range(SSM_GROUPS):
        sl = slice(g * gw, (g + 1) * gw)
        bg = xbc[:, D_INNER + g * SSM_STATE:D_INNER + (g + 1) * SSM_STATE]
        cg = xbc[:, D_INNER + GN + g * SSM_STATE:D_INNER + GN + (g + 1) * SSM_STATE]
        cbg = _dot_nt(cg.astype(BF16), bg.astype(BF16))
        yg = dskip_ref[:, sl] * x[:, sl]
        for s_tok in range(T):
            seg = jnp.exp(jnp.where(t_idx >= s_tok, cum_e[:, sl] - cum_e[s_tok:s_tok + 1, sl], NEG_INF))
            coef = seg * dt_e[s_tok:s_tok + 1, sl] * jnp.broadcast_to(cbg[:, s_tok:s_tok + 1], (T, gw))
            yg = yg + coef * x[s_tok:s_tok + 1, sl]
        y_parts.append(yg)

    xw = x * dt_e * jnp.exp(cum_e[T - 1:T, :] - cum_e)
    ecum = jnp.exp(cum_e)
    sub = lax.broadcasted_iota(I32, (2 * SSM_HEAD_DIM, SSM_STATE), 0)
    lo_rows = sub < SSM_HEAD_DIM
    yo_parts = []
    for i in range(n_pairs):
        g = i // pairs_per_group
        sl = slice(i * LANES, (i + 1) * LANES)
        bg = xbc[:, D_INNER + g * SSM_STATE:D_INNER + (g + 1) * SSM_STATE].astype(BF16)
        cg = xbc[:, D_INNER + GN + g * SSM_STATE:D_INNER + GN + (g + 1) * SSM_STATE].astype(BF16)
        hp = h0_ref[s, i * LANES:(i + 1) * LANES, :]
        yo_parts.append(_dot_nt(cg, hp.astype(BF16)) * ecum[:, sl])
        st = _dot_tn(xw[:, sl].astype(BF16), bg)
        d0 = jnp.exp(cum[T - 1:T, 2 * i:2 * i + 1])
        d1 = jnp.exp(cum[T - 1:T, 2 * i + 1:2 * i + 2])
        dec = jnp.where(lo_rows, jnp.broadcast_to(d0, sub.shape), jnp.broadcast_to(d1, sub.shape))
        hn_ref[s, i * LANES:(i + 1) * LANES, :] = dec * hp + st

    for g in range(SSM_GROUPS):
        sl = slice(g * gw, (g + 1) * gw)
        yo = jnp.concatenate(yo_parts[g * pairs_per_group:(g + 1) * pairs_per_group], axis=1)
        u = (y_parts[g] + yo) * _silu(z_ref[s, :, sl])
        ms = jnp.mean(u * u, axis=-1, keepdims=True)
        y_ref[s, :, sl] = (u * lax.rsqrt(ms + LN_EPS) * ng_ref[:, sl]).astype(y_ref.dtype)


def ssd_sample(u_cat, proj3, h0, conv_w, conv_b, dtb_pad, alog_pad, dskip_e, norm_g, expand):
    nb, t_new, _ = proj3.shape
    full = lambda shape: pl.BlockSpec(shape, lambda b: tuple(0 for _ in shape))
    hp = SSM_HEADS * SSM_HEAD_DIM
    n_seq = SSD_SAMPLE_SEQS_PER_STEP if nb % SSD_SAMPLE_SEQS_PER_STEP == 0 else 1
    return pl.pallas_call(
        functools.partial(_ssd_sample_body, t_new=t_new, n_seq=n_seq),
        grid=(nb // n_seq,),
        in_specs=[
            pl.BlockSpec((n_seq, t_new + CONV_W - 1, CONV_DIM), lambda b: (b, 0, 0)),
            pl.BlockSpec((n_seq, t_new, D_INNER), lambda b: (b, 0, COL_Z // D_INNER)),
            pl.BlockSpec((n_seq, t_new, LANES), lambda b: (b, 0, COL_SM // LANES)),
            pl.BlockSpec((n_seq, hp, SSM_STATE), lambda b: (b, 0, 0)),
            full((CONV_W, CONV_DIM)), full((1, CONV_DIM)), full((1, LANES)), full((1, LANES)),
            full((1, D_INNER)), full((1, D_INNER)), full((LANES, D_INNER)),
        ],
        out_specs=[
            pl.BlockSpec((n_seq, t_new, D_INNER), lambda b: (b, 0, 0)),
            pl.BlockSpec((n_seq, CONV_W - 1, CONV_DIM), lambda b: (b, 0, 0)),
            pl.BlockSpec((n_seq, hp, SSM_STATE), lambda b: (b, 0, 0)),
        ],
        out_shape=[
            jax.ShapeDtypeStruct((nb, t_new, D_INNER), BF16),
            jax.ShapeDtypeStruct((nb, CONV_W - 1, CONV_DIM), F32),
            jax.ShapeDtypeStruct((nb, hp, SSM_STATE), F32),
        ],
        compiler_params=_cparams(("parallel",)),
        name="ssd_sample",
    )(u_cat, proj3, proj3, h0, conv_w, conv_b, dtb_pad, alog_pad, dskip_e, norm_g, expand)


def _sortable_key(score):
    score = jnp.where(score == 0.0, 0.0, score)
    bits = lax.bitcast_convert_type(score, I32)
    return bits ^ ((bits >> 31) & jnp.int32(0x7FFFFFFF))


def _t5_bucket_np(rel):
    n = np.maximum(rel, 0)
    max_exact = N_BUCKETS // 2
    nf = np.maximum(n, 1).astype(np.float32)
    large = max_exact + (np.log(nf / max_exact) / math.log(MAX_DISTANCE / max_exact)
                         * (N_BUCKETS - max_exact)).astype(np.int32)
    large = np.minimum(large, N_BUCKETS - 1)
    return np.where(n < max_exact, n, large).astype(np.int32)


def _count_cols_i16(ref, n_rows, pred):
    packed_rows = 2 * SUBLANES
    parts = []
    for r0 in range(0, n_rows, LANES):
        x = jnp.where(pred(ref[r0:r0 + LANES, :]), jnp.int16(1), jnp.int16(0))
        n = x.shape[0]
        while n > packed_rows:
            n //= 2
            x = x[:n] + x[n:]
        parts.append(x)
    while len(parts) > 1:
        parts = [parts[i] + parts[i + 1] if i + 1 < len(parts) else parts[i] for i in range(0, len(parts), 2)]
    return jnp.sum(parts[0].astype(I32), axis=0, keepdims=True)


def _kth_largest_i16(ref, n_rows, k):
    i16_min = -(2 ** 15)

    def try_cand(r, cand):
        cnt = _count_cols_i16(ref, n_rows, lambda v: v >= cand.astype(jnp.int16))
        return jnp.where(cnt >= k, cand, r)

    r0 = try_cand(jnp.full((1, LANES), i16_min, I32), jnp.zeros((1, LANES), I32))
    return lax.fori_loop(0, 15, lambda it, r: try_cand(r, r + (jnp.int32(1) << (14 - it))), r0)


def _kth_largest_cols(key_ref, hi_ref, lo_ref, n_rows, k):
    i16_min = -(2 ** 15)
    for r0 in range(0, n_rows, LANES):
        key = key_ref[r0:r0 + LANES, :]
        hi_ref[r0:r0 + LANES, :] = (key >> 16).astype(jnp.int16)
        lo_ref[r0:r0 + LANES, :] = ((key & 0xFFFF) + i16_min).astype(jnp.int16)
    t_hi = _kth_largest_i16(hi_ref, n_rows, k)
    t_hi16 = t_hi.astype(jnp.int16)
    k_lo = k - _count_cols_i16(hi_ref, n_rows, lambda v: v > t_hi16)
    for r0 in range(0, n_rows, LANES):
        sl = slice(r0, r0 + LANES)
        lo_ref[sl, :] = jnp.where(hi_ref[sl, :] == t_hi16, lo_ref[sl, :], jnp.int16(i16_min))
    t_lo = _kth_largest_i16(lo_ref, n_rows, k_lo)
    return (t_hi << 16) | ((t_lo - i16_min) & 0xFFFF)


def _topk_mask_cols(key_ref, hi_ref, lo_ref, msk_ref, n_rows, k, valid_fn):
    thr = _kth_largest_cols(key_ref, hi_ref, lo_ref, n_rows, k)
    n_gt = _col_sum(jnp.where(key_ref[0:n_rows, :] > thr, 1.0, 0.0))
    need = k - n_gt
    run = jnp.zeros((1, LANES), F32)
    ri = lax.broadcasted_iota(I32, (LANES, LANES), 0)
    ci = lax.broadcasted_iota(I32, (LANES, LANES), 1)
    strict = (ri > ci).astype(BF16)
    for r0 in range(0, n_rows, LANES):
        kc = key_ref[r0:r0 + LANES, :]
        eq = kc == thr
        eqf = eq.astype(F32)
        before = _dot(strict, eqf.astype(BF16)) + run
        sel = ((kc > thr) | (eq & (before < need))) & valid_fn(r0, LANES)
        msk_ref[r0:r0 + LANES, :] = jnp.where(sel, 0.0, NEG_INF)
        run = run + jnp.sum(eqf, axis=0, keepdims=True)


def _kth_largest_rows(key_ref, k):
    rows = key_ref.shape[0]
    digits = 2 ** RADIX_BITS
    n_rounds = 32 // RADIX_BITS

    def body(rnd, r):
        shift = (32 - RADIX_BITS) - RADIX_BITS * rnd
        n_ok = jnp.zeros((rows, 1), I32)
        for i in range(1, digits):
            cand = r + (jnp.int32(i) << shift)
            cnt = jnp.sum((key_ref[...] >= cand).astype(I32), axis=1, keepdims=True)
            n_ok = n_ok + (cnt >= k).astype(I32)
        return r + (n_ok << shift)

    return lax.fori_loop(0, n_rounds, body, jnp.full((rows, 1), INT_MIN, I32))


def _topk_select_rows(key_ref, sel_ref, k, valid_fn):
    n_keys = key_ref.shape[1]
    thr = _kth_largest_rows(key_ref, k)
    n_gt = jnp.sum((key_ref[...] > thr).astype(I32), axis=1, keepdims=True)
    need = (k - n_gt).astype(F32)
    run = jnp.zeros((key_ref.shape[0], 1), F32)
    ri = lax.broadcasted_iota(I32, (LANES, LANES), 0)
    ci = lax.broadcasted_iota(I32, (LANES, LANES), 1)
    strict = (ri < ci).astype(BF16)
    for l0 in range(0, n_keys, LANES):
        kc = key_ref[:, l0:l0 + LANES]
        eq = kc == thr
        eqf = eq.astype(F32)
        before = _dot(eqf.astype(BF16), strict) + run
        sel = ((kc > thr) | (eq & (before < need))) & valid_fn(l0)
        sel_ref[:, l0:l0 + LANES] = sel.astype(F32).astype(BF16)
        run = run + jnp.sum(eqf, axis=1, keepdims=True)


def _attn_prompt_body(tab_ref, q_ref, k_ref, v_ref, qi_ref, ki_ref, sm_ref, bucket_ref, *rest, nk, topk, j0,
                      emit_transposed):
    key_scr, hi_scr, lo_scr, msk_scr, lg_scr, ot_scr, bias_scr, ka_scr = rest[-8:]
    if emit_transposed:
        k32_ref, v32_ref, o_ref, kt_ref, vt_ref, kit_ref = rest[-14:-8]

        @pl.when(pl.program_id(1) == 0)
        def _():
            kt_ref[0] = k32_ref[...].T
            vt_ref[0] = v32_ref[...].T
            kit_ref[0] = ki_ref[...].T[0:IDX_DIM, :]
    else:
        o_ref = rest[-9]
    b = pl.program_id(0)
    j = j0 + pl.program_id(1)
    QB = LANES

    @pl.when((b == 0) & (pl.program_id(1) == 0))
    def _():
        for d in range(2):
            bk = bucket_ref[d]
            for h in range(N_HEADS):
                acc = jnp.zeros(bk.shape, F32)
                for bb in range(N_BUCKETS):
                    acc = jnp.where(bk == bb, tab_ref[bb, h], acc)
                bias_scr[h, d] = (acc - tab_ref[N_BUCKETS - 1, h]) * LOG2E

    @pl.when(pl.program_id(1) == 0)
    def _():
        x = ki_ref[0:nk, :]
        hi, lo = _hi_lo(x)
        first = lax.broadcasted_iota(I32, (nk, LANES), 1) < IDX_DIM
        ka_scr[:, 0:LANES] = jnp.where(first, hi, lo)
        ka_scr[:, LANES:2 * LANES] = jnp.where(first, hi, jnp.zeros_like(hi))

    wt = sm_ref[...].T
    ka = ka_scr[...]
    score = jnp.zeros((nk, QB), F32)
    pad = jnp.zeros((QB, 2 * LANES - 3 * IDX_DIM), BF16)
    for h2 in range(IDX_HEADS // 2):
        blocks = []
        for h in (2 * h2, 2 * h2 + 1):
            qh, ql = _hi_lo(qi_ref[:, h * IDX_DIM:(h + 1) * IDX_DIM] * (IDX_DIM ** -0.5))
            blocks.append(jnp.concatenate([ql, qh, qh, pad], axis=1))
        s = _dot_nt(ka, jnp.concatenate(blocks, axis=0))
        for i, h in enumerate((2 * h2, 2 * h2 + 1)):
            score = score + jnp.maximum(s[:, i * QB:(i + 1) * QB], 0.0) * wt[SM_WI + h:SM_WI + h + 1, :]
    score = score * (IDX_HEADS ** -0.5)
    kpos = lax.broadcasted_iota(I32, (nk, QB), 0)
    qpos = j * QB + lax.broadcasted_iota(I32, (nk, QB), 1)
    score = jnp.where(kpos <= qpos, score, NEG_INF)
    key_scr[...] = _sortable_key(score)

    def valid(r0, rows):
        kp = r0 + lax.broadcasted_iota(I32, (rows, QB), 0)
        qp = j * QB + lax.broadcasted_iota(I32, (rows, QB), 1)
        return kp <= qp

    _topk_mask_cols(key_scr, hi_scr, lo_scr, msk_scr, nk, topk, valid)

    near = min(2 * QB, nk)
    start = pl.multiple_of(jnp.clip((j - 1) * QB, 0, nk - near), QB)
    msk_near = msk_scr[pl.ds(start, near), :]
    rows = lax.broadcasted_iota(I32, (nk, QB), 0)
    lg_scr[...] = jnp.where((rows >= start) & (rows < start + near), NEG_INF, msk_scr[...])
    first_is_diag = j == 0
    for g in range(N_KV_HEADS):
        kcols = slice(g * HEAD_DIM, (g + 1) * HEAD_DIM)
        vcols = slice(g * LANES, (g + 1) * LANES)
        heads = range(g * Q_PER_KV, (g + 1) * Q_PER_KV)
        qg = jnp.concatenate([(q_ref[:, h * HEAD_DIM:(h + 1) * HEAD_DIM] * (HEAD_DIM ** -0.5 * LOG2E)).astype(BF16)
                              for h in heads], axis=0)
        lg_main = _dot_nt(k_ref[0:nk, kcols], qg)
        lg_near = _dot_nt(k_ref[pl.ds(start, near), kcols], qg)
        p_main, p_near = [], []
        for r, h in enumerate(heads):
            sl = slice(r * QB, (r + 1) * QB)
            bias = jnp.where(first_is_diag, bias_scr[h, 0], bias_scr[h, 1])
            if near > QB:
                bias = jnp.concatenate([bias, bias_scr[h, 0]], axis=0)
            lg_n = (lg_near[:, sl] + bias) + msk_near
            lg_m = lg_main[:, sl] + lg_scr[...]
            m = jnp.maximum(_col_max(lg_m), _col_max(lg_n))
            p_main.append(jnp.exp2(lg_m - m).astype(BF16))
            p_near.append(jnp.exp2(lg_n - m).astype(BF16))
        pv = (_dot_tn(v_ref[0:nk, vcols], jnp.concatenate(p_main, axis=1))
              + _dot_tn(v_ref[pl.ds(start, near), vcols], jnp.concatenate(p_near, axis=1)))
        for r, h in enumerate(heads):
            sl = slice(r * QB, (r + 1) * QB)
            ot_scr[h * HEAD_DIM:(h + 1) * HEAD_DIM, :] = pv[0:HEAD_DIM, sl] * (1.0 / pv[HEAD_DIM:HEAD_DIM + 1, sl])
    o_ref[...] = ot_scr[...].T.astype(o_ref.dtype)


def attn_prompt(proj, bias_table, batch, seq):
    QB = LANES
    nq = seq // QB
    assert seq % QB == 0
    topk = min(TOPK_MAX, seq // 4)
    ts = np.arange(QB)[:, None]
    tq = np.arange(QB)[None, :]
    bucket = jnp.asarray(np.stack([_t5_bucket_np(tq - ts), _t5_bucket_np(QB + tq - ts)]))
    n_groups = math.gcd(nq, PROMPT_KEY_GROUPS)
    per_group = nq // n_groups
    kvw = N_KV_HEADS * HEAD_DIM
    kv32 = lax.optimization_barrier(proj[:, COL_K:COL_K + 2 * kvw])
    k_bf = kv32[:, :kvw].astype(BF16)
    v4 = kv32[:, kvw:].astype(BF16).reshape(batch * seq, N_KV_HEADS, HEAD_DIM)
    v_aug = jnp.concatenate([v4, jnp.ones(v4.shape[:2] + (1,), BF16),
                             jnp.zeros(v4.shape[:2] + (LANES - HEAD_DIM - 1,), BF16)], axis=-1)
    v_aug = v_aug.reshape(batch * seq, N_KV_HEADS * LANES)
    out = None
    for grp in range(n_groups):
        j0 = grp * per_group
        nk = (j0 + per_group) * QB
        rowblk = lambda width, colblk: pl.BlockSpec((QB, width), lambda b, j: (b * nq + j0 + j, colblk))
        seqblk = lambda width, colblk: pl.BlockSpec((seq, width), lambda b, j: (b, colblk))
        prev = [] if out is None else [out]
        last = grp == n_groups - 1
        extra_in = [seqblk(kvw, COL_K // kvw), seqblk(kvw, COL_V // kvw)] if last else []
        tblk = lambda rows: pl.BlockSpec((1, rows, seq), lambda b, j: (b, 0, 0))
        tshape = lambda rows: jax.ShapeDtypeStruct((batch, rows, seq), F32)
        res = pl.pallas_call(
            functools.partial(_attn_prompt_body, nk=nk, topk=topk, j0=j0, emit_transposed=last),
            grid=(batch, per_group),
            in_specs=[
                pl.BlockSpec(memory_space=pltpu.SMEM),
                rowblk(N_HEADS * HEAD_DIM, COL_Q // (N_HEADS * HEAD_DIM)),
                seqblk(kvw, 0),
                seqblk(N_KV_HEADS * LANES, 0),
                rowblk(IDX_HEADS * IDX_DIM, COL_QI // (IDX_HEADS * IDX_DIM)),
                seqblk(LANES, COL_KI // LANES),
                rowblk(LANES, COL_SM // LANES),
                pl.BlockSpec((2, QB, QB), lambda b, j: (0, 0, 0)),
            ] + [pl.BlockSpec(memory_space=pl.ANY) for _ in prev] + extra_in,
            out_specs=[pl.BlockSpec((QB, N_HEADS * HEAD_DIM), lambda b, j: (b * nq + j0 + j, 0))]
                      + ([tblk(kvw), tblk(kvw), tblk(IDX_DIM)] if last else []),
            out_shape=[jax.ShapeDtypeStruct((batch * seq, N_HEADS * HEAD_DIM), BF16)]
                      + ([tshape(kvw), tshape(kvw), tshape(IDX_DIM)] if last else []),
            input_output_aliases={8: 0} if prev else {},
            scratch_shapes=[
                pltpu.VMEM((nk, QB), I32),
                pltpu.VMEM((nk, QB), jnp.int16),
                pltpu.VMEM((nk, QB), jnp.int16),
                pltpu.VMEM((nk, QB), F32),
                pltpu.VMEM((nk, QB), F32),
                pltpu.VMEM((N_HEADS * HEAD_DIM, QB), F32),
                pltpu.VMEM((N_HEADS, 2, QB, QB), F32),
                pltpu.VMEM((nk, 2 * LANES), BF16),
            ],
            compiler_params=_cparams(("arbitrary", "arbitrary")),
            name=f"attn_prompt_g{grp}",
        )(bias_table, proj, k_bf, v_aug, proj, proj, proj, bucket, *prev, *([proj, proj] if last else []))
        out = res[0]
    assert nk == seq
    return out, res[1:]


def _attn_sample_body(pt_ref, q_ref, qi_ref, w_ref, kn_ref, vn_ref, kin_ref, tabt_ref, bucket_ref, gsum_ref, xpand_ref,
                      *rest, n_seq, n_pages, page, t_new, topk):
    n_pg = n_seq * n_pages
    ck = rest[0:n_pg]
    cv = rest[n_pg:2 * n_pg]
    cki = rest[2 * n_pg:3 * n_pg]
    o_ref, sc_scr, key_scr, sel_scr, lg_scr, p_scr, bias_scr, ki_st, k_st, v_st = rest[3 * n_pg:]
    past = n_pages * page
    n_tiles = n_pages + 1
    rows_q = t_new * N_HEADS
    rows_g = t_new * Q_PER_KV

    @pl.when(pl.program_id(0) == 0)
    def _():
        bk = bucket_ref[...]
        acc = jnp.zeros(bk.shape, F32)
        for bb in range(N_BUCKETS):
            acc = jnp.where(bk == bb, tabt_ref[:, bb:bb + 1], acc)
        bias_scr[...] = acc - tabt_ref[:, N_BUCKETS - 1:N_BUCKETS]

    for s in range(n_seq):
        for c in range(n_pages):
            sl = slice(c * page, (c + 1) * page)
            ki_st[s, :, sl] = cki[s * n_pages + c][0]
            k_st[s, :, :, sl] = ck[s * n_pages + c][0].astype(BF16)
            v_st[s, :, :, sl] = cv[s * n_pages + c][0].astype(BF16)

    for s in range(n_seq):
        qh, ql = _hi_lo(qi_ref[s])
        wcol = w_ref[s]
        gsum = gsum_ref[s]

        def idx_scores(kt_h, kt_l, dot):
            sc = (dot(qh, kt_l) + dot(ql, kt_h)) + dot(qh, kt_h)
            sc = jnp.maximum(sc * (IDX_DIM ** -0.5), 0.0) * wcol
            return _sel_dot_exact(gsum, sc) * (IDX_HEADS ** -0.5)

        main = idx_scores(*_hi_lo(ki_st[s]), _dot)
        tail = idx_scores(*_hi_lo(kin_ref[s]), _dot_nt)
        sc_scr[:, 0:past] = main if s == 0 else sc_scr[:, 0:past] + main
        sc_scr[:, past:] = tail if s == 0 else sc_scr[:, past:] + tail

    n_keys = n_tiles * page
    n_rows = n_seq * t_new

    def valid(l0):
        kp = l0 + lax.broadcasted_iota(I32, (n_rows, LANES), 1)
        row = lax.broadcasted_iota(I32, (n_rows, LANES), 0)
        tok = row
        for s in range(1, n_seq):
            tok = jnp.where(row >= s * t_new, row - s * t_new, tok)
        return kp <= past + tok

    for c in range(n_tiles):
        sl = slice(c * page, (c + 1) * page)
        key_scr[:, sl] = _sortable_key(jnp.where(valid(c * page), sc_scr[:, sl], NEG_INF))
    _topk_select_rows(key_scr, sel_scr, topk, valid)

    for s in range(n_seq):
        for g in range(N_KV_HEADS):
            rs = slice(g * rows_g, (g + 1) * rows_g)
            cols = slice(g * HEAD_DIM, (g + 1) * HEAD_DIM)
            qg = (q_ref[s, rs, :] * (HEAD_DIM ** -0.5)).astype(BF16)
            lg_scr[rs, 0:past] = _dot(qg, k_st[s, g])
            lg_scr[rs, past:] = _dot_nt(qg, kn_ref[s, :, cols].astype(BF16))
        near = slice(past - page, n_keys)
        lg_scr[:, near] = lg_scr[:, near] + bias_scr[...]
        picked = _dot(xpand_ref[s], sel_scr[...]) > 0.5
        lg = jnp.where(picked, lg_scr[...], NEG_INF)
        m = jnp.max(lg, axis=1, keepdims=True)
        p = jnp.exp(lg - m)
        p_scr[...] = (p * (1.0 / jnp.sum(p, axis=1, keepdims=True))).astype(BF16)
        for g in range(N_KV_HEADS):
            rs = slice(g * rows_g, (g + 1) * rows_g)
            cols = slice(g * HEAD_DIM, (g + 1) * HEAD_DIM)
            acc = _dot_nt(p_scr[rs, 0:past], v_st[s, g]) + _dot(p_scr[rs, past:], vn_ref[s, :, cols].astype(BF16))
            o_ref[s, rs, :] = acc.astype(o_ref.dtype)


def attn_sample(proj_s, cache_k, cache_v, cache_kidx, page_table, bias_table, t_new):
    nb, n_pages = page_table.shape
    n_pool, page = cache_k.shape[0], cache_k.shape[1]
    past = n_pages * page
    total = past + t_new
    topk = min(TOPK_MAX, total // 4)
    kvw = N_KV_HEADS * HEAD_DIM
    n_seq = SAMPLE_SEQS_PER_STEP if nb % SAMPLE_SEQS_PER_STEP == 0 else 1
    rows_q = t_new * N_HEADS
    rows_i = t_new * IDX_HEADS
    n_rows = n_seq * t_new
    n_keys = past + page
    assert page == LANES and page >= MAX_DISTANCE and t_new <= page
    ck_t = jnp.transpose(cache_k, (0, 2, 3, 1))
    cv_t = jnp.transpose(cache_v, (0, 2, 3, 1))
    cki_t = jnp.transpose(cache_kidx, (0, 2, 1))
    seg = lambda col, width: proj_s[:, col:col + width]
    q3 = seg(COL_Q, N_HEADS * HEAD_DIM).reshape(nb, t_new, N_KV_HEADS, Q_PER_KV, HEAD_DIM)
    q3 = q3.transpose(0, 2, 1, 3, 4).reshape(nb, rows_q, HEAD_DIM)
    qi3 = seg(COL_QI, IDX_HEADS * IDX_DIM).reshape(nb, rows_i, IDX_DIM)
    w3 = seg(COL_SM + SM_WI, IDX_HEADS).reshape(nb, rows_i, 1)
    pad_rows = lambda a: jnp.pad(a.reshape(nb, t_new, -1), ((0, 0), (0, page - t_new), (0, 0)))
    kn3, vn3, kin3 = pad_rows(seg(COL_K, kvw)), pad_rows(seg(COL_V, kvw)), pad_rows(seg(COL_KI, IDX_DIM))
    row = np.arange(rows_q)
    row_tok = (row // Q_PER_KV) % t_new
    row_head = (row // (t_new * Q_PER_KV)) * Q_PER_KV + row % Q_PER_KV
    tab_t = jnp.pad(bias_table.T[row_head], ((0, 0), (0, LANES - N_BUCKETS)))
    kpos = np.concatenate([past - page + np.arange(page), past + np.arange(page)])[None, :]
    bucket = jnp.asarray(_t5_bucket_np(past + row_tok[:, None] - kpos))
    gsum = np.zeros((n_seq, n_rows, rows_i), np.float32)
    xpand = np.zeros((n_seq, rows_q, n_rows), np.float32)
    for s in range(n_seq):
        gsum[s, s * t_new + np.arange(rows_i) // IDX_HEADS, np.arange(rows_i)] = 1.0
        xpand[s, row, s * t_new + row_tok] = 1.0
    per_step = lambda shape: pl.BlockSpec((n_seq,) + shape, lambda b, pt: (b,) + tuple(0 for _ in shape))
    const = lambda shape: pl.BlockSpec(shape, lambda b, pt: tuple(0 for _ in shape))

    def page_specs(shape):
        return [pl.BlockSpec((1,) + shape, lambda b, pt, s=s, c=c: (pt[b * n_seq + s, c],) + tuple(0 for _ in shape))
                for s in range(n_seq) for c in range(n_pages)]

    grid_spec = pltpu.PrefetchScalarGridSpec(
        num_scalar_prefetch=1,
        grid=(nb // n_seq,),
        in_specs=[per_step((rows_q, HEAD_DIM)), per_step((rows_i, IDX_DIM)), per_step((rows_i, 1)),
                  per_step((page, kvw)), per_step((page, kvw)), per_step((page, IDX_DIM)),
                  const((rows_q, LANES)), const((rows_q, 2 * page)),
                  const((n_seq, n_rows, rows_i)), const((n_seq, rows_q, n_rows))]
                 + page_specs((N_KV_HEADS, HEAD_DIM, page)) + page_specs((N_KV_HEADS, HEAD_DIM, page))
                 + page_specs((IDX_DIM, page)),
        out_specs=pl.BlockSpec((n_seq, rows_q, HEAD_DIM), lambda b, pt: (b, 0, 0)),
        scratch_shapes=[
            pltpu.VMEM((n_rows, n_keys), F32),
            pltpu.VMEM((n_rows, n_keys), I32),
            pltpu.VMEM((n_rows, n_keys), BF16),
            pltpu.VMEM((rows_q, n_keys), F32),
            pltpu.VMEM((rows_q, n_keys), BF16),
            pltpu.VMEM((rows_q, 2 * page), F32),
            pltpu.VMEM((n_seq, IDX_DIM, past), F32),
            pltpu.VMEM((n_seq, N_KV_HEADS, HEAD_DIM, past), BF16),
            pltpu.VMEM((n_seq, N_KV_HEADS, HEAD_DIM, past), BF16),
        ],
    )
    n_pg = n_seq * n_pages
    out = pl.pallas_call(
        functools.partial(_attn_sample_body, n_seq=n_seq, n_pages=n_pages, page=page, t_new=t_new, topk=topk),
        grid_spec=grid_spec,
        out_shape=jax.ShapeDtypeStruct((nb, rows_q, HEAD_DIM), BF16),
        compiler_params=_cparams(("arbitrary",)),
        name="attn_sample",
    )(page_table, q3, qi3, w3, kn3, vn3, kin3, tab_t, bucket, jnp.asarray(gsum, BF16), jnp.asarray(xpand, BF16),
      *([ck_t] * n_pg), *([cv_t] * n_pg), *([cki_t] * n_pg))
    out = out.reshape(nb, N_KV_HEADS, t_new, Q_PER_KV, HEAD_DIM).transpose(0, 2, 1, 3, 4)
    return out.reshape(nb * t_new, N_HEADS * HEAD_DIM)


def _merge_body(ys_ref, ya_ref, g_ref, h_ref, wbs_ref, wba_ref, wo_ref, lg_ref, lb_ref, o_ref):
    a = _dot(ys_ref[...], wbs_ref[...])
    b = _dot(ya_ref[...], wba_ref[...])
    gates = jax.nn.sigmoid(g_ref[...])
    merged = gates[:, :D_MODEL] * a + gates[:, D_MODEL:] * b
    r = ALPHA * h_ref[...] + _dot(merged.astype(BF16), wo_ref[...])
    o_ref[...] = _layer_norm(r, lg_ref[...], lb_ref[...])


def merge_out(y_ssm, y_att, proj, h1, wbs, wba, wo, g, b, *, tm=512):
    m = h1.shape[0]
    tm = min(tm, m)
    assert m % tm == 0
    row = lambda width, colblk=0: pl.BlockSpec((tm, width), lambda i: (i, colblk))
    full = lambda shape: pl.BlockSpec(shape, lambda i: tuple(0 for _ in shape))
    return pl.pallas_call(
        _merge_body,
        grid=(m // tm,),
        in_specs=[row(D_INNER), row(N_HEADS * HEAD_DIM), row(2 * D_MODEL, COL_G // (2 * D_MODEL)), row(D_MODEL),
                  full(wbs.shape), full(wba.shape), full(wo.shape), full((1, D_MODEL)), full((1, D_MODEL))],
        out_specs=row(D_MODEL),
        out_shape=jax.ShapeDtypeStruct((m, D_MODEL), F32),
        compiler_params=_cparams(("parallel",)),
        name="merge_out",
    )(y_ssm, y_att, proj, h1, wbs, wba, wo, g, b)


def kernel(x_prompt, x_sample, cache_k, cache_v, cache_kidx, state_conv, state_ssm, page_table, bias_table, w_in, conv_w, conv_b, dt_bias, a_log, d_skip, ssm_norm_g, w_branch_ssm, w_branch_attn, w_out, ffn1_wi, ffn1_wo, ffn2_wi, ffn2_wo, ln1_g, ln1_b, ln2_g, ln2_b, ln3_g, ln3_b):
    assert w_in.shape[0] == DEPTH
    batch, seq, d = x_prompt.shape
    nb, t_new, _ = x_sample.shape
    xs = (x_prompt.reshape(batch * seq, d), x_sample.reshape(nb * t_new, d))
    outs_p, outs_s = [], []
    for i in range(DEPTH):
        w_pack = pack_w_in(w_in[i])
        f1i, f1o = ffn1_wi[i].astype(BF16), ffn1_wo[i].astype(BF16)
        f2i, f2o = ffn2_wi[i].astype(BF16), ffn2_wo[i].astype(BF16)
        wbs, wba, wo = w_branch_ssm[i].astype(BF16), w_branch_attn[i].astype(BF16), w_out[i].astype(BF16)
        row = lambda v: v.reshape(1, -1)
        pad_heads = lambda v: jnp.concatenate([v, jnp.zeros((LANES - SSM_HEADS,), v.dtype)]).reshape(1, LANES)
        dtb_pad, alog_pad = pad_heads(dt_bias[i]), pad_heads(a_log[i])
        dskip_e = jnp.repeat(d_skip[i], SSM_HEAD_DIM).reshape(1, D_INNER)
        ng = row(ssm_norm_g[i])
        cw, cb = conv_w[i], row(conv_b[i])

        h1, proj = [], []
        for x in xs:
            hf, hb = ffn_ln(x, f1i, f1o, row(ln1_g[i]), row(ln1_b[i]), emit_bf16=True)
            h1.append(hf)
            proj.append(matmul(hb, w_pack, tm=2048, tn=PROJ_TN))
        proj_p, proj_s = proj

        ys_p, conv_p, ssm_p = ssd_prompt(proj_p, batch, seq, cw, cb, dtb_pad, alog_pad, dskip_e, ng)
        ya_p, (kt_p, vt_p, kit_p) = attn_prompt(proj_p, bias_table, batch, seq)

        raw_s = jnp.concatenate([proj_s[:, COL_X:COL_X + D_INNER], proj_s[:, COL_BC:COL_BC + 2 * GN]], axis=1)
        u_cat = jnp.concatenate([state_conv[i], raw_s.reshape(nb, t_new, CONV_DIM)], axis=1)
        expand = np.zeros((LANES, D_INNER), np.float32)
        expand[np.arange(D_INNER) // SSM_HEAD_DIM, np.arange(D_INNER)] = 1.0
        ys_s, conv_s, ssm_s = ssd_sample(u_cat, proj_s.reshape(nb, t_new, PROJ_N),
                                         state_ssm[i].reshape(nb, SSM_HEADS * SSM_HEAD_DIM, SSM_STATE),
                                         cw, cb, dtb_pad, alog_pad, dskip_e, ng, jnp.asarray(expand, BF16))
        ya_s = attn_sample(proj_s, cache_k[i], cache_v[i], cache_kidx[i], page_table, bias_table, t_new)

        new_xs = []
        for hf, pj, ys, ya in ((h1[0], proj_p, ys_p, ya_p), (h1[1], proj_s, ys_s.reshape(nb * t_new, D_INNER), ya_s)):
            h2 = merge_out(ys, ya, pj, hf, wbs, wba, wo, row(ln2_g[i]), row(ln2_b[i]))
            new_xs.append(ffn_ln(h2, f2i, f2o, row(ln3_g[i]), row(ln3_b[i]), emit_bf16=False))
        xs = tuple(new_xs)

        kvw = N_KV_HEADS * HEAD_DIM
        shp = lambda nb_, l_: (nb_, l_, N_KV_HEADS, HEAD_DIM)
        head_major = lambda a: a.reshape(batch, N_KV_HEADS, HEAD_DIM, seq).transpose(0, 3, 1, 2)
        outs_p.append((head_major(kt_p), head_major(vt_p), kit_p.transpose(0, 2, 1),
                       conv_p, ssm_p.reshape(batch, SSM_HEADS, SSM_HEAD_DIM, SSM_STATE)))
        outs_s.append((proj_s[:, COL_K:COL_K + kvw].reshape(shp(nb, t_new)),
                       proj_s[:, COL_V:COL_V + kvw].reshape(shp(nb, t_new)),
                       proj_s[:, COL_KI:COL_KI + IDX_DIM].reshape(nb, t_new, IDX_DIM),
                       conv_s, ssm_s.reshape(nb, SSM_HEADS, SSM_HEAD_DIM, SSM_STATE)))
    k_p, v_p, kidx_p, conv_pp, ssm_pp = [jnp.stack(a) for a in zip(*outs_p)]
    k_s, v_s, kidx_s, conv_ss, ssm_ss = [jnp.stack(a) for a in zip(*outs_s)]
    return (xs[0].reshape(batch, seq, d), xs[1].reshape(nb, t_new, d),
            k_p, v_p, kidx_p, conv_pp, ssm_pp, k_s, v_s, kidx_s, conv_ss, ssm_ss)
```

```python
import functools
import math

import numpy as np
import jax
import jax.numpy as jnp
from jax import lax
from jax.experimental import pallas as pl
from jax.experimental.pallas import tpu as pltpu

F32 = jnp.float32
BF16 = jnp.bfloat16
I32 = jnp.int32

D_MODEL = 1024
D_INNER = 2 * D_MODEL
SSM_HEAD_DIM = 64
SSM_HEADS = D_INNER // SSM_HEAD_DIM
SSM_GROUPS = 4
SSM_STATE = 128
CONV_W = 4
CONV_DIM = D_INNER + 2 * SSM_GROUPS * SSM_STATE
SSD_CHUNK = 128
N_HEADS = 16
HEAD_DIM = 64
N_KV_HEADS = 4
Q_PER_KV = N_HEADS // N_KV_HEADS
IDX_HEADS = 8
IDX_DIM = 64
TOPK_MAX = 256
N_BUCKETS = 32
MAX_DISTANCE = 128
D_FF = 2816
DEPTH = 1
ALPHA = (2 * DEPTH) ** 0.25
LN_EPS = 1e-5
IN_SPLITS = (D_INNER, CONV_DIM, SSM_HEADS, N_HEADS * HEAD_DIM, N_KV_HEADS * HEAD_DIM,
             N_KV_HEADS * HEAD_DIM, IDX_HEADS * IDX_DIM, IDX_DIM, IDX_HEADS, 2 * D_MODEL)

LANES = 128
SUBLANES = 8
VMEM_LIMIT_BYTES = 56 * 1024 * 1024

GN = SSM_GROUPS * SSM_STATE
COL_Z = 0
COL_X = COL_Z + D_INNER
COL_G = COL_X + D_INNER
COL_BC = COL_G + 2 * D_MODEL
COL_Q = COL_BC + 2 * GN
COL_K = COL_Q + N_HEADS * HEAD_DIM
COL_V = COL_K + N_KV_HEADS * HEAD_DIM
COL_QI = COL_V + N_KV_HEADS * HEAD_DIM
COL_SM = COL_QI + IDX_HEADS * IDX_DIM
COL_KI = COL_SM + LANES
PROJ_TN = 512
PROJ_N = -(-(COL_KI + LANES) // PROJ_TN) * PROJ_TN
SM_WI = SSM_HEADS

INT_MIN = -(2 ** 31)
LOG2E = math.log2(math.e)
NEG_INF = float("-inf")
PROMPT_KEY_GROUPS = 8
SAMPLE_SEQS_PER_STEP = 4
SSD_SAMPLE_SEQS_PER_STEP = 4
RADIX_BITS = 4


def _cparams(sem):
    return pltpu.CompilerParams(dimension_semantics=sem, vmem_limit_bytes=VMEM_LIMIT_BYTES)


def _layer_norm(r, g, b):
    mu = jnp.mean(r, axis=-1, keepdims=True)
    d = r - mu
    var = jnp.mean(d * d, axis=-1, keepdims=True)
    return d * lax.rsqrt(var + LN_EPS) * g + b


def _silu(x):
    h = 0.5 * x
    return h + h * jnp.tanh(h)


def _softplus(x):
    return jnp.maximum(x, 0.0) + jnp.log1p(jnp.exp(-jnp.abs(x)))


def _dot(a, b):
    return jnp.dot(a, b, preferred_element_type=F32)


def _dot_nt(a, b):
    return lax.dot_general(a, b, (((1,), (1,)), ((), ())), preferred_element_type=F32)


def _dot_tn(a, b):
    return lax.dot_general(a, b, (((0,), (0,)), ((), ())), preferred_element_type=F32)


def _split3(v):
    hi = v.astype(BF16)
    r1 = v - hi.astype(F32)
    mid = r1.astype(BF16)
    lo = (r1 - mid.astype(F32)).astype(BF16)
    return hi, mid, lo


def _dot_exact_sel(v, sel_bf16):
    hi, mid, lo = _split3(v)
    return (_dot(hi, sel_bf16) + _dot(mid, sel_bf16)) + _dot(lo, sel_bf16)


def _sel_dot_exact(sel_bf16, v):
    hi, mid, lo = _split3(v)
    return (_dot(sel_bf16, hi) + _dot(sel_bf16, mid)) + _dot(sel_bf16, lo)


def _hi_lo(a):
    ah = a.astype(BF16)
    return ah, (a - ah.astype(F32)).astype(BF16)


def _dot_nt_x3(a, b):
    ah, al = _hi_lo(a)
    bh, bl = _hi_lo(b)
    return (_dot_nt(ah, bl) + _dot_nt(al, bh)) + _dot_nt(ah, bh)


def _dot_x3(a, b):
    ah, al = _hi_lo(a)
    bh, bl = _hi_lo(b)
    return (_dot(ah, bl) + _dot(al, bh)) + _dot(ah, bh)


def _col_reduce(x, op2, op):
    parts = []
    for r0 in range(0, x.shape[0], LANES):
        y = x[r0:r0 + LANES]
        n = y.shape[0]
        while n > SUBLANES and n % (2 * SUBLANES) == 0:
            n //= 2
            y = op2(y[:n], y[n:])
        parts.append(y)
    while len(parts) > 1:
        parts = [op2(parts[i], parts[i + 1]) if i + 1 < len(parts) else parts[i] for i in range(0, len(parts), 2)]
    return op(parts[0], axis=0, keepdims=True)


def _col_sum(x):
    return _col_reduce(x, jnp.add, jnp.sum)


def _col_max(x):
    return _col_reduce(x, jnp.maximum, jnp.max)


def _ffn_ln_body(x_ref, wi_ref, wo_ref, g_ref, b_ref, o_ref, *maybe_ob_ref, tf):
    x = x_ref[...]
    xb = x.astype(BF16)
    dff = wo_ref.shape[0]
    acc = None
    for c0 in range(0, dff, tf):
        gate = _dot(xb, wi_ref[:, c0:c0 + tf])
        up = _dot(xb, wi_ref[:, dff + c0:dff + c0 + tf])
        part = _dot((_silu(gate) * up).astype(BF16), wo_ref[c0:c0 + tf, :])
        acc = part if acc is None else acc + part
    y = _layer_norm(ALPHA * x + 0.5 * acc, g_ref[...], b_ref[...])
    o_ref[...] = y
    for ob_ref in maybe_ob_ref:
        ob_ref[...] = y.astype(BF16)


def ffn_ln(x, wi_bf, wo_bf, g, b, *, emit_bf16, tm=512, tf=1408):
    m, d = x.shape
    dff = wo_bf.shape[0]
    tm = min(tm, m)
    assert m % tm == 0 and dff % tf == 0
    row = pl.BlockSpec((tm, d), lambda i: (i, 0))
    resident = lambda shape: pl.BlockSpec(shape, lambda i: (0, 0), pipeline_mode=pl.Buffered(1))
    n_out = 2 if emit_bf16 else 1
    res = pl.pallas_call(
        functools.partial(_ffn_ln_body, tf=tf),
        grid=(m // tm,),
        in_specs=[row, resident(wi_bf.shape), resident(wo_bf.shape), resident((1, d)), resident((1, d))],
        out_specs=[row] * n_out,
        out_shape=[jax.ShapeDtypeStruct((m, d), F32), jax.ShapeDtypeStruct((m, d), BF16)][:n_out],
        compiler_params=_cparams(("parallel",)),
        name="ffn_ln",
    )(x, wi_bf, wo_bf, g, b)
    return res if emit_bf16 else res[0]


def _matmul_body(x_ref, w_ref, o_ref):
    o_ref[...] = _dot(x_ref[...], w_ref[...])


def matmul(x_bf, w_bf, *, tm, tn):
    m, k = x_bf.shape
    n = w_bf.shape[1]
    tm = min(tm, m)
    assert m % tm == 0 and n % tn == 0
    return pl.pallas_call(
        _matmul_body,
        grid=(m // tm, n // tn),
        in_specs=[pl.BlockSpec((tm, k), lambda i, j: (i, 0)),
                  pl.BlockSpec((k, tn), lambda i, j: (0, j))],
        out_specs=pl.BlockSpec((tm, tn), lambda i, j: (i, j)),
        out_shape=jax.ShapeDtypeStruct((m, n), F32),
        compiler_params=_cparams(("parallel", "arbitrary")),
        name="in_proj",
    )(x_bf, w_bf)


def pack_w_in(w_in):
    offs = np.cumsum((0,) + IN_SPLITS)
    z, xbc, dt, q, k, v, qi, ki, wi, gates = [w_in[:, offs[i]:offs[i + 1]] for i in range(len(IN_SPLITS))]
    d = w_in.shape[0]
    zeros = lambda n: jnp.zeros((d, n), w_in.dtype)
    small = jnp.concatenate([dt, wi, zeros(LANES - SSM_HEADS - IDX_HEADS)], axis=1)
    kib = jnp.concatenate([ki, ki], axis=1)
    packed = jnp.concatenate([z, xbc[:, :D_INNER], gates, xbc[:, D_INNER:], q, k, v, qi, small, kib,
                              zeros(PROJ_N - COL_KI - LANES)], axis=1)
    return packed.astype(BF16)


def _ssd_prompt_body(z_ref, xr_ref, bc_ref, sm_ref, cwx_ref, cwbc_ref, cbx_ref, cbbc_ref,
                     dtb_ref, alog_ref, dskip_ref, ng_ref,
                     y_ref, cs_ref, hs_ref,
                     xbuf, bcbuf, h_scr, xc_scr, bcc_scr, y_scr):
    c = pl.program_id(1)
    nc = pl.num_programs(1)
    L = SSD_CHUNK
    halo = SUBLANES
    n_pairs = SSM_HEADS // 2
    pairs_per_group = n_pairs // SSM_GROUPS

    @pl.when(c == 0)
    def _():
        xbuf[0:halo, :] = jnp.zeros((halo, D_INNER), F32)
        bcbuf[0:halo, :] = jnp.zeros((halo, 2 * GN), F32)
        h_scr[...] = jnp.zeros_like(h_scr)

    xbuf[halo:halo + L, :] = xr_ref[...]
    bcbuf[halo:halo + L, :] = bc_ref[...]

    def conv(buf, w_ref, b_ref):
        full = buf[0:halo + L, :]
        acc = b_ref[...] + w_ref[CONV_W - 1:CONV_W, :] * full[halo:halo + L]
        for k in range(CONV_W - 1):
            shifted = pltpu.roll(full, CONV_W - 1 - k, axis=0)[halo:halo + L]
            acc = acc + w_ref[k:k + 1, :] * shifted
        return _silu(acc)

    xc_scr[...] = conv(xbuf, cwx_ref, cbx_ref)
    bcc_scr[...] = conv(bcbuf, cwbc_ref, cbbc_ref)
    xbuf[0:halo, :] = xbuf[L:L + halo, :]
    bcbuf[0:halo, :] = bcbuf[L:L + halo, :]

    dt = _softplus(sm_ref[...] + dtb_ref[...])
    a = -jnp.exp(alog_ref[...])
    row = lax.broadcasted_iota(I32, (L, L), 0)
    col = lax.broadcasted_iota(I32, (L, L), 1)
    tril = (row >= col).astype(F32)
    cum = jnp.dot(tril, dt * a, precision=lax.Precision.HIGHEST, preferred_element_type=F32)
    w = jnp.exp(cum[L - 1:L, :] - cum) * dt
    cum_t = cum.T
    dt_t = dt.T
    causal = row >= col
    lane = lax.broadcasted_iota(I32, (L, LANES), 1)
    lo_half = lane < SSM_HEAD_DIM
    sub = lax.broadcasted_iota(I32, (2 * SSM_HEAD_DIM, SSM_STATE), 0)
    lo_rows = sub < SSM_HEAD_DIM

    for i in range(n_pairs):
        g = i // pairs_per_group
        h0, h1 = 2 * i, 2 * i + 1
        sl = slice(i * LANES, (i + 1) * LANES)
        if i % pairs_per_group == 0:
            bg = bcc_scr[:, g * SSM_STATE:(g + 1) * SSM_STATE].astype(BF16)
            cg = bcc_scr[:, GN + g * SSM_STATE:GN + (g + 1) * SSM_STATE].astype(BF16)
            cb = _dot_nt(cg, bg)
        xp = xc_scr[:, sl]
        c0 = jnp.broadcast_to(cum[:, h0:h0 + 1], (L, LANES))
        c1 = jnp.broadcast_to(cum[:, h1:h1 + 1], (L, LANES))
        w0 = jnp.broadcast_to(w[:, h0:h0 + 1], (L, LANES))
        w1 = jnp.broadcast_to(w[:, h1:h1 + 1], (L, LANES))

        def mix(ct, h):
            seg = jnp.exp(jnp.where(causal, ct - cum_t[h:h + 1, :], NEG_INF))
            return (cb * seg * dt_t[h:h + 1, :]).astype(BF16)

        zero = jnp.zeros_like(xp)
        yd = (_dot(mix(c0, h0), jnp.where(lo_half, xp, zero).astype(BF16))
              + _dot(mix(c1, h1), jnp.where(lo_half, zero, xp).astype(BF16)))
        hp = h_scr[i]
        yo = _dot_nt(cg, hp.astype(BF16)) * jnp.where(lo_half, jnp.exp(c0), jnp.exp(c1))
        y_scr[:, sl] = yd + yo + dskip_ref[:, sl] * xp
        xw = (xp * jnp.where(lo_half, w0, w1)).astype(BF16)
        st = _dot_tn(xw, bg)
        dec = jnp.where(lo_rows, jnp.exp(c0[L - 1:L, :]), jnp.exp(c1[L - 1:L, :]))
        h_scr[i] = dec * hp + st

    gw = D_INNER // SSM_GROUPS
    for g in range(SSM_GROUPS):
        sl = slice(g * gw, (g + 1) * gw)
        u = y_scr[:, sl] * _silu(z_ref[:, sl])
        ms = jnp.mean(u * u, axis=-1, keepdims=True)
        y_ref[:, sl] = (u * lax.rsqrt(ms + LN_EPS) * ng_ref[:, sl]).astype(y_ref.dtype)

    @pl.when(c == nc - 1)
    def _():
        cs_ref[0, :, 0:D_INNER] = xr_ref[L - (CONV_W - 1):L, :]
        cs_ref[0, :, D_INNER:CONV_DIM] = bc_ref[L - (CONV_W - 1):L, :]
        for i in range(n_pairs):
            hs_ref[0, i * LANES:(i + 1) * LANES, :] = h_scr[i]


def ssd_prompt(proj, batch, seq, conv_w, conv_b, dtb_pad, alog_pad, dskip_e, norm_g):
    L = SSD_CHUNK
    nc = seq // L
    assert seq % L == 0
    blk = lambda width, colblk: pl.BlockSpec((L, width), lambda b, c: (b * nc + c, colblk))
    full = lambda shape: pl.BlockSpec(shape, lambda b, c: tuple(0 for _ in shape))
    n_pairs = SSM_HEADS // 2
    return pl.pallas_call(
        _ssd_prompt_body,
        grid=(batch, nc),
        in_specs=[
            blk(D_INNER, COL_Z // D_INNER),
            blk(D_INNER, COL_X // D_INNER),
            blk(2 * GN, COL_BC // (2 * GN)),
            blk(LANES, COL_SM // LANES),
            full((CONV_W, D_INNER)), full((CONV_W, 2 * GN)), full((1, D_INNER)), full((1, 2 * GN)),
            full((1, LANES)), full((1, LANES)), full((1, D_INNER)), full((1, D_INNER)),
        ],
        out_specs=[
            pl.BlockSpec((L, D_INNER), lambda b, c: (b * nc + c, 0)),
            pl.BlockSpec((1, CONV_W - 1, CONV_DIM), lambda b, c: (b, 0, 0)),
            pl.BlockSpec((1, SSM_HEADS * SSM_HEAD_DIM, SSM_STATE), lambda b, c: (b, 0, 0)),
        ],
        out_shape=[
            jax.ShapeDtypeStruct((batch * seq, D_INNER), BF16),
            jax.ShapeDtypeStruct((batch, CONV_W - 1, CONV_DIM), F32),
            jax.ShapeDtypeStruct((batch, SSM_HEADS * SSM_HEAD_DIM, SSM_STATE), F32),
        ],
        scratch_shapes=[
            pltpu.VMEM((L + 2 * SUBLANES, D_INNER), F32),
            pltpu.VMEM((L + 2 * SUBLANES, 2 * GN), F32),
            pltpu.VMEM((n_pairs, 2 * SSM_HEAD_DIM, SSM_STATE), F32),
            pltpu.VMEM((L, D_INNER), F32),
            pltpu.VMEM((L, 2 * GN), F32),
            pltpu.VMEM((L, D_INNER), F32),
        ],
        compiler_params=_cparams(("parallel", "arbitrary")),
        name="ssd_prompt",
    )(proj, proj, proj, proj, conv_w[:, :D_INNER], conv_w[:, D_INNER:], conv_b[:, :D_INNER], conv_b[:, D_INNER:],
      dtb_pad, alog_pad, dskip_e, norm_g)


def _ssd_sample_body(u_ref, z_ref, sm_ref, h0_ref, cw_ref, cb_ref, dtb_ref, alog_ref, dskip_ref, ng_ref, exp_ref,
                     y_ref, cs_ref, hn_ref, *, t_new, n_seq):
    for s in range(n_seq):
        _ssd_sample_one(s, u_ref, z_ref, sm_ref, h0_ref, cw_ref, cb_ref, dtb_ref, alog_ref, dskip_ref, ng_ref, exp_ref,
                        y_ref, cs_ref, hn_ref, t_new)


def _ssd_sample_one(s, u_ref, z_ref, sm_ref, h0_ref, cw_ref, cb_ref, dtb_ref, alog_ref, dskip_ref, ng_ref, exp_ref,
                    y_ref, cs_ref, hn_ref, t_new):
    T = t_new
    n_pairs = SSM_HEADS // 2
    pairs_per_group = n_pairs // SSM_GROUPS
    acc = cb_ref[...] + cw_ref[0:1, :] * u_ref[s, 0:T, :]
    for k in range(1, CONV_W):
        acc = acc + cw_ref[k:k + 1, :] * u_ref[s, k:k + T, :]
    xbc = _silu(acc)
    cs_ref[s] = u_ref[s, T:T + CONV_W - 1, :]
    x = xbc[:, :D_INNER]

    dt = _softplus(sm_ref[s] + dtb_ref[...])
    da = dt * (-jnp.exp(alog_ref[...]))
    rows = [da[0:1, :]]
    for t in range(1, T):
        rows.append(rows[-1] + da[t:t + 1, :])
    cum = jnp.concatenate(rows, axis=0)
    both = _dot_exact_sel(jnp.concatenate([cum, dt], axis=0), exp_ref[...])
    cum_e = both[0:T, :]
    dt_e = both[T:2 * T, :]
    t_idx = lax.broadcasted_iota(I32, (T, GN), 0)

    gw = D_INNER // SSM_GROUPS
    y_parts = []
    for g in range(SSM_GROUPS):
        sl = slice(g * gw, (g + 1) * gw)
        bg = xbc[:, D_INNER + g * SSM_STATE:D_INNER + (g + 1) * SSM_STATE]
        cg = xbc[:, D_INNER + GN + g * SSM_STATE:D_INNER + GN + (g + 1) * SSM_STATE]
        cbg = _dot_nt(cg.astype(BF16), bg.astype(BF16))
        yg = dskip_ref[:, sl] * x[:, sl]
        for s_tok in range(T):
            seg = jnp.exp(jnp.where(t_idx >= s_tok, cum_e[:, sl] - cum_e[s_tok:s_tok + 1, sl], NEG_INF))
            coef = seg * dt_e[s_tok:s_tok + 1, sl] * jnp.broadcast_to(cbg[:, s_tok:s_tok + 1], (T, gw))
            yg = yg + coef * x[s_tok:s_tok + 1, sl]
        y_parts.append(yg)

    xw = x * dt_e * jnp.exp(cum_e[T - 1:T, :] - cum_e)
    ecum = jnp.exp(cum_e)
    sub = lax.broadcasted_iota(I32, (2 * SSM_HEAD_DIM, SSM_STATE), 0)
    lo_rows = sub < SSM_HEAD_DIM
    yo_parts = []
    for i in range(n_pairs):
        g = i // pairs_per_group
        sl = slice(i * LANES, (i + 1) * LANES)
        bg = xbc[:, D_INNER + g * SSM_STATE:D_INNER + (g + 1) * SSM_STATE].astype(BF16)
        cg = xbc[:, D_INNER + GN + g * SSM_STATE:D_INNER + GN + (g + 1) * SSM_STATE].astype(BF16)
        hp = h0_ref[s, i * LANES:(i + 1) * LANES, :]
        yo_parts.append(_dot_nt(cg, hp.astype(BF16)) * ecum[:, sl])
        st = _dot_tn(xw[:, sl].astype(BF16), bg)
        d0 = jnp.exp(cum[T - 1:T, 2 * i:2 * i + 1])
        d1 = jnp.exp(cum[T - 1:T, 2 * i + 1:2 * i + 2])
        dec = jnp.where(lo_rows, jnp.broadcast_to(d0, sub.shape), jnp.broadcast_to(d1, sub.shape))
        hn_ref[s, i * LANES:(i + 1) * LANES, :] = dec * hp + st

    for g in range(SSM_GROUPS):
        sl = slice(g * gw, (g + 1) * gw)
        yo = jnp.concatenate(yo_parts[g * pairs_per_group:(g + 1) * pairs_per_group], axis=1)
        u = (y_parts[g] + yo) * _silu(z_ref[s, :, sl])
        ms = jnp.mean(u * u, axis=-1, keepdims=True)
        y_ref[s, :, sl] = (u * lax.rsqrt(ms + LN_EPS) * ng_ref[:, sl]).astype(y_ref.dtype)


def ssd_sample(u_cat, proj3, h0, conv_w, conv_b, dtb_pad, alog_pad, dskip_e, norm_g, expand):
    nb, t_new, _ = proj3.shape
    full = lambda shape: pl.BlockSpec(shape, lambda b: tuple(0 for _ in shape))
    hp = SSM_HEADS * SSM_HEAD_DIM
    n_seq = SSD_SAMPLE_SEQS_PER_STEP if nb % SSD_SAMPLE_SEQS_PER_STEP == 0 else 1
    return pl.pallas_call(
        functools.partial(_ssd_sample_body, t_new=t_new, n_seq=n_seq),
        grid=(nb // n_seq,),
        in_specs=[
            pl.BlockSpec((n_seq, t_new + CONV_W - 1, CONV_DIM), lambda b: (b, 0, 0)),
            pl.BlockSpec((n_seq, t_new, D_INNER), lambda b: (b, 0, COL_Z // D_INNER)),
            pl.BlockSpec((n_seq, t_new, LANES), lambda b: (b, 0, COL_SM // LANES)),
            pl.BlockSpec((n_seq, hp, SSM_STATE), lambda b: (b, 0, 0)),
            full((CONV_W, CONV_DIM)), full((1, CONV_DIM)), full((1, LANES)), full((1, LANES)),
            full((1, D_INNER)), full((1, D_INNER)), full((LANES, D_INNER)),
        ],
        out_specs=[
            pl.BlockSpec((n_seq, t_new, D_INNER), lambda b: (b, 0, 0)),
            pl.BlockSpec((n_seq, CONV_W - 1, CONV_DIM), lambda b: (b, 0, 0)),
            pl.BlockSpec((n_seq, hp, SSM_STATE), lambda b: (b, 0, 0)),
        ],
        out_shape=[
            jax.ShapeDtypeStruct((nb, t_new, D_INNER), BF16),
            jax.ShapeDtypeStruct((nb, CONV_W - 1, CONV_DIM), F32),
            jax.ShapeDtypeStruct((nb, hp, SSM_STATE), F32),
        ],
        compiler_params=_cparams(("parallel",)),
        name="ssd_sample",
    )(u_cat, proj3, proj3, h0, conv_w, conv_b, dtb_pad, alog_pad, dskip_e, norm_g, expand)


def _sortable_key(score):
    score = jnp.where(score == 0.0, 0.0, score)
    bits = lax.bitcast_convert_type(score, I32)
    return bits ^ ((bits >> 31) & jnp.int32(0x7FFFFFFF))


def _t5_bucket_np(rel):
    n = np.maximum(rel, 0)
    max_exact = N_BUCKETS // 2
    nf = np.maximum(n, 1).astype(np.float32)
    large = max_exact + (np.log(nf / max_exact) / math.log(MAX_DISTANCE / max_exact)
                         * (N_BUCKETS - max_exact)).astype(np.int32)
    large = np.minimum(large, N_BUCKETS - 1)
    return np.where(n < max_exact, n, large).astype(np.int32)


def _count_cols_i16(ref, n_rows, pred):
    packed_rows = 2 * SUBLANES
    parts = []
    for r0 in range(0, n_rows, LANES):
        x = jnp.where(pred(ref[r0:r0 + LANES, :]), jnp.int16(1), jnp.int16(0))
        n = x.shape[0]
        while n > packed_rows:
            n //= 2
            x = x[:n] + x[n:]
        parts.append(x)
    while len(parts) > 1:
        parts = [parts[i] + parts[i + 1] if i + 1 < len(parts) else parts[i] for i in range(0, len(parts), 2)]
    return jnp.sum(parts[0].astype(I32), axis=0, keepdims=True)


def _kth_largest_i16(ref, n_rows, k):
    i16_min = -(2 ** 15)

    def try_cand(r, cand):
        cnt = _count_cols_i16(ref, n_rows, lambda v: v >= cand.astype(jnp.int16))
        return jnp.where(cnt >= k, cand, r)

    r0 = try_cand(jnp.full((1, LANES), i16_min, I32), jnp.zeros((1, LANES), I32))
    return lax.fori_loop(0, 15, lambda it, r: try_cand(r, r + (jnp.int32(1) << (14 - it))), r0)


def _kth_largest_cols(key_ref, hi_ref, lo_ref, n_rows, k):
    i16_min = -(2 ** 15)
    for r0 in range(0, n_rows, LANES):
        key = key_ref[r0:r0 + LANES, :]
        hi_ref[r0:r0 + LANES, :] = (key >> 16).astype(jnp.int16)
        lo_ref[r0:r0 + LANES, :] = ((key & 0xFFFF) + i16_min).astype(jnp.int16)
    t_hi = _kth_largest_i16(hi_ref, n_rows, k)
    t_hi16 = t_hi.astype(jnp.int16)
    k_lo = k - _count_cols_i16(hi_ref, n_rows, lambda v: v > t_hi16)
    for r0 in range(0, n_rows, LANES):
        sl = slice(r0, r0 + LANES)
        lo_ref[sl, :] = jnp.where(hi_ref[sl, :] == t_hi16, lo_ref[sl, :], jnp.int16(i16_min))
    t_lo = _kth_largest_i16(lo_ref, n_rows, k_lo)
    return (t_hi << 16) | ((t_lo - i16_min) & 0xFFFF)


def _topk_mask_cols(key_ref, hi_ref, lo_ref, msk_ref, n_rows, k, valid_fn):
    thr = _kth_largest_cols(key_ref, hi_ref, lo_ref, n_rows, k)
    n_gt = _col_sum(jnp.where(key_ref[0:n_rows, :] > thr, 1.0, 0.0))
    need = k - n_gt
    run = jnp.zeros((1, LANES), F32)
    ri = lax.broadcasted_iota(I32, (LANES, LANES), 0)
    ci = lax.broadcasted_iota(I32, (LANES, LANES), 1)
    strict = (ri > ci).astype(BF16)
    for r0 in range(0, n_rows, LANES):
        kc = key_ref[r0:r0 + LANES, :]
        eq = kc == thr
        eqf = eq.astype(F32)
        before = _dot(strict, eqf.astype(BF16)) + run
        sel = ((kc > thr) | (eq & (before < need))) & valid_fn(r0, LANES)
        msk_ref[r0:r0 + LANES, :] = jnp.where(sel, 0.0, NEG_INF)
        run = run + jnp.sum(eqf, axis=0, keepdims=True)


def _kth_largest_rows(key_ref, k):
    rows = key_ref.shape[0]
    digits = 2 ** RADIX_BITS
    n_rounds = 32 // RADIX_BITS

    def body(rnd, r):
        shift = (32 - RADIX_BITS) - RADIX_BITS * rnd
        n_ok = jnp.zeros((rows, 1), I32)
        for i in range(1, digits):
            cand = r + (jnp.int32(i) << shift)
            cnt = jnp.sum((key_ref[...] >= cand).astype(I32), axis=1, keepdims=True)
            n_ok = n_ok + (cnt >= k).astype(I32)
        return r + (n_ok << shift)

    return lax.fori_loop(0, n_rounds, body, jnp.full((rows, 1), INT_MIN, I32))


def _topk_select_rows(key_ref, sel_ref, k, valid_fn):
    n_keys = key_ref.shape[1]
    thr = _kth_largest_rows(key_ref, k)
    n_gt = jnp.sum((key_ref[...] > thr).astype(I32), axis=1, keepdims=True)
    need = (k - n_gt).astype(F32)
    run = jnp.zeros((key_ref.shape[0], 1), F32)
    ri = lax.broadcasted_iota(I32, (LANES, LANES), 0)
    ci = lax.broadcasted_iota(I32, (LANES, LANES), 1)
    strict = (ri < ci).astype(BF16)
    for l0 in range(0, n_keys, LANES):
        kc = key_ref[:, l0:l0 + LANES]
        eq = kc == thr
        eqf = eq.astype(F32)
        before = _dot(eqf.astype(BF16), strict) + run
        sel = ((kc > thr) | (eq & (before < need))) & valid_fn(l0)
        sel_ref[:, l0:l0 + LANES] = sel.astype(F32).astype(BF16)
        run = run + jnp.sum(eqf, axis=1, keepdims=True)


def _bias_tiles_body(tab_ref, bucket_ref, o_ref):
    for d in range(bucket_ref.shape[0]):
        bk = bucket_ref[d]
        for h in range(N_HEADS):
            acc = jnp.zeros(bk.shape, F32)
            for bb in range(N_BUCKETS):
                acc = jnp.where(bk == bb, tab_ref[bb, h], acc)
            o_ref[h, d] = (acc - tab_ref[N_BUCKETS - 1, h]) * LOG2E


def _attn_prompt_body(bias_ref, q_ref, k_ref, v_ref, qi_ref, ki_ref, sm_ref, *rest, nk, topk, j0, emit_transposed):
    key_scr, hi_scr, lo_scr, msk_scr, lg_scr, ot_scr, ka_scr = rest[-7:]
    if emit_transposed:
        k32_ref, v32_ref, o_ref, kt_ref, vt_ref, kit_ref = rest[-13:-7]

        @pl.when(pl.program_id(1) == 0)
        def _():
            kt_ref[0] = k32_ref[...].T
            vt_ref[0] = v32_ref[...].T
            kit_ref[0] = ki_ref[...].T[0:IDX_DIM, :]
    else:
        o_ref = rest[-8]
    j = j0 + pl.program_id(1)
    QB = LANES

    @pl.when(pl.program_id(1) == 0)
    def _():
        x = ki_ref[0:nk, :]
        hi, lo = _hi_lo(x)
        first = lax.broadcasted_iota(I32, (nk, LANES), 1) < IDX_DIM
        ka_scr[:, 0:LANES] = jnp.where(first, hi, lo)
        ka_scr[:, LANES:2 * LANES] = jnp.where(first, hi, jnp.zeros_like(hi))

    wt = sm_ref[...].T
    ka = ka_scr[...]
    score = jnp.zeros((nk, QB), F32)
    pad = jnp.zeros((QB, 2 * LANES - 3 * IDX_DIM), BF16)
    for h2 in range(IDX_HEADS // 2):
        blocks = []
        for h in (2 * h2, 2 * h2 + 1):
            qh, ql = _hi_lo(qi_ref[:, h * IDX_DIM:(h + 1) * IDX_DIM] * (IDX_DIM ** -0.5))
            blocks.append(jnp.concatenate([ql, qh, qh, pad], axis=1))
        s = _dot_nt(ka, jnp.concatenate(blocks, axis=0))
        for i, h in enumerate((2 * h2, 2 * h2 + 1)):
            score = score + jnp.maximum(s[:, i * QB:(i + 1) * QB], 0.0) * wt[SM_WI + h:SM_WI + h + 1, :]
    score = score * (IDX_HEADS ** -0.5)
    kpos = lax.broadcasted_iota(I32, (nk, QB), 0)
    qpos = j * QB + lax.broadcasted_iota(I32, (nk, QB), 1)
    score = jnp.where(kpos <= qpos, score, NEG_INF)
    key_scr[...] = _sortable_key(score)

    def valid(r0, rows):
        kp = r0 + lax.broadcasted_iota(I32, (rows, QB), 0)
        qp = j * QB + lax.broadcasted_iota(I32, (rows, QB), 1)
        return kp <= qp

    _topk_mask_cols(key_scr, hi_scr, lo_scr, msk_scr, nk, topk, valid)

    near = min(2 * QB, nk)
    start = pl.multiple_of(jnp.clip((j - 1) * QB, 0, nk - near), QB)
    msk_near = msk_scr[pl.ds(start, near), :]
    rows = lax.broadcasted_iota(I32, (nk, QB), 0)
    lg_scr[...] = jnp.where((rows >= start) & (rows < start + near), NEG_INF, msk_scr[...])
    first_is_diag = j == 0
    for g in range(N_KV_HEADS):
        kcols = slice(g * HEAD_DIM, (g + 1) * HEAD_DIM)
        vcols = slice(g * LANES, (g + 1) * LANES)
        heads = range(g * Q_PER_KV, (g + 1) * Q_PER_KV)
        qg = jnp.concatenate([(q_ref[:, h * HEAD_DIM:(h + 1) * HEAD_DIM] * (HEAD_DIM ** -0.5 * LOG2E)).astype(BF16)
                              for h in heads], axis=0)
        lg_main = _dot_nt(k_ref[0:nk, kcols], qg)
        lg_near = _dot_nt(k_ref[pl.ds(start, near), kcols], qg)
        p_main, p_near = [], []
        for r, h in enumerate(heads):
            sl = slice(r * QB, (r + 1) * QB)
            bias = jnp.where(first_is_diag, bias_ref[h, 0], bias_ref[h, 1])
            if near > QB:
                bias = jnp.concatenate([bias, bias_ref[h, 0]], axis=0)
            lg_n = (lg_near[:, sl] + bias) + msk_near
            lg_m = lg_main[:, sl] + lg_scr[...]
            m = jnp.maximum(_col_max(lg_m), _col_max(lg_n))
            p_main.append(jnp.exp2(lg_m - m).astype(BF16))
            p_near.append(jnp.exp2(lg_n - m).astype(BF16))
        pv = (_dot_tn(v_ref[0:nk, vcols], jnp.concatenate(p_main, axis=1))
              + _dot_tn(v_ref[pl.ds(start, near), vcols], jnp.concatenate(p_near, axis=1)))
        for r, h in enumerate(heads):
            sl = slice(r * QB, (r + 1) * QB)
            ot_scr[h * HEAD_DIM:(h + 1) * HEAD_DIM, :] = pv[0:HEAD_DIM, sl] * (1.0 / pv[HEAD_DIM:HEAD_DIM + 1, sl])
    o_ref[...] = ot_scr[...].T.astype(o_ref.dtype)


def attn_prompt(proj, bias_table, batch, seq):
    QB = LANES
    nq = seq // QB
    assert seq % QB == 0
    topk = min(TOPK_MAX, seq // 4)
    ts = np.arange(QB)[:, None]
    tq = np.arange(QB)[None, :]
    bucket = jnp.asarray(np.stack([_t5_bucket_np(tq - ts), _t5_bucket_np(QB + tq - ts)]))
    n_groups = math.gcd(nq, PROMPT_KEY_GROUPS)
    per_group = nq // n_groups
    kvw = N_KV_HEADS * HEAD_DIM
    kv32 = lax.optimization_barrier(proj[:, COL_K:COL_K + 2 * kvw])
    k_bf = kv32[:, :kvw].astype(BF16)
    v4 = kv32[:, kvw:].astype(BF16).reshape(batch * seq, N_KV_HEADS, HEAD_DIM)
    v_aug = jnp.concatenate([v4, jnp.ones(v4.shape[:2] + (1,), BF16),
                             jnp.zeros(v4.shape[:2] + (LANES - HEAD_DIM - 1,), BF16)], axis=-1)
    v_aug = v_aug.reshape(batch * seq, N_KV_HEADS * LANES)
    bias = pl.pallas_call(
        _bias_tiles_body,
        in_specs=[pl.BlockSpec(memory_space=pltpu.SMEM), pl.BlockSpec(memory_space=pltpu.VMEM)],
        out_specs=pl.BlockSpec(memory_space=pltpu.VMEM),
        out_shape=jax.ShapeDtypeStruct((N_HEADS, 2, QB, QB), F32),
        name="bias_tiles",
    )(bias_table, bucket)
    out = None
    for grp in range(n_groups):
        j0 = grp * per_group
        nk = (j0 + per_group) * QB
        rowblk = lambda width, colblk: pl.BlockSpec((QB, width), lambda b, j: (b * nq + j0 + j, colblk))
        seqblk = lambda width, colblk: pl.BlockSpec((seq, width), lambda b, j: (b, colblk))
        prev = [] if out is None else [out]
        last = grp == n_groups - 1
        extra_in = [seqblk(kvw, COL_K // kvw), seqblk(kvw, COL_V // kvw)] if last else []
        tblk = lambda rows: pl.BlockSpec((1, rows, seq), lambda b, j: (b, 0, 0))
        tshape = lambda rows: jax.ShapeDtypeStruct((batch, rows, seq), F32)
        res = pl.pallas_call(
            functools.partial(_attn_prompt_body, nk=nk, topk=topk, j0=j0, emit_transposed=last),
            grid=(batch, per_group),
            in_specs=[
                pl.BlockSpec((N_HEADS, 2, QB, QB), lambda b, j: (0, 0, 0, 0)),
                rowblk(N_HEADS * HEAD_DIM, COL_Q // (N_HEADS * HEAD_DIM)),
                seqblk(kvw, 0),
                seqblk(N_KV_HEADS * LANES, 0),
                rowblk(IDX_HEADS * IDX_DIM, COL_QI // (IDX_HEADS * IDX_DIM)),
                seqblk(LANES, COL_KI // LANES),
                rowblk(LANES, COL_SM // LANES),
            ] + [pl.BlockSpec(memory_space=pl.ANY) for _ in prev] + extra_in,
            out_specs=[pl.BlockSpec((QB, N_HEADS * HEAD_DIM), lambda b, j: (b * nq + j0 + j, 0))]
                      + ([tblk(kvw), tblk(kvw), tblk(IDX_DIM)] if last else []),
            out_shape=[jax.ShapeDtypeStruct((batch * seq, N_HEADS * HEAD_DIM), BF16)]
                      + ([tshape(kvw), tshape(kvw), tshape(IDX_DIM)] if last else []),
            input_output_aliases={7: 0} if prev else {},
            scratch_shapes=[
                pltpu.VMEM((nk, QB), I32),
                pltpu.VMEM((nk, QB), jnp.int16),
                pltpu.VMEM((nk, QB), jnp.int16),
                pltpu.VMEM((nk, QB), F32),
                pltpu.VMEM((nk, QB), F32),
                pltpu.VMEM((N_HEADS * HEAD_DIM, QB), F32),
                pltpu.VMEM((nk, 2 * LANES), BF16),
            ],
            compiler_params=_cparams(("arbitrary", "arbitrary")),
            name=f"attn_prompt_g{grp}",
        )(bias, proj, k_bf, v_aug, proj, proj, proj, *prev, *([proj, proj] if last else []))
        out = res[0]
    assert nk == seq
    return out, res[1:]


def _attn_sample_body(pt_ref, q_ref, qi_ref, w_ref, kn_ref, vn_ref, kin_ref, tabt_ref, bucket_ref, gsum_ref, xpand_ref,
                      *rest, n_seq, n_pages, page, t_new, topk):
    n_pg = n_seq * n_pages
    ck = rest[0:n_pg]
    cv = rest[n_pg:2 * n_pg]
    cki = rest[2 * n_pg:3 * n_pg]
    o_ref, sc_scr, key_scr, sel_scr, lg_scr, p_scr, bias_scr, ki_st, k_st, v_st = rest[3 * n_pg:]
    past = n_pages * page
    n_tiles = n_pages + 1
    rows_q = t_new * N_HEADS
    rows_g = t_new * Q_PER_KV

    @pl.when(pl.program_id(0) == 0)
    def _():
        bk = bucket_ref[...]
        acc = jnp.zeros(bk.shape, F32)
        for bb in range(N_BUCKETS):
            acc = jnp.where(bk == bb, tabt_ref[:, bb:bb + 1], acc)
        bias_scr[...] = acc - tabt_ref[:, N_BUCKETS - 1:N_BUCKETS]

    for s in range(n_seq):
        for c in range(n_pages):
            sl = slice(c * page, (c + 1) * page)
            ki_st[s, :, sl] = cki[s * n_pages + c][0]
            k_st[s, :, :, sl] = ck[s * n_pages + c][0].astype(BF16)
            v_st[s, :, :, sl] = cv[s * n_pages + c][0].astype(BF16)

    for s in range(n_seq):
        qh, ql = _hi_lo(qi_ref[s])
        wcol = w_ref[s]
        gsum = gsum_ref[s]

        def idx_scores(kt_h, kt_l, dot):
            sc = (dot(qh, kt_l) + dot(ql, kt_h)) + dot(qh, kt_h)
            sc = jnp.maximum(sc * (IDX_DIM ** -0.5), 0.0) * wcol
            return _sel_dot_exact(gsum, sc) * (IDX_HEADS ** -0.5)

        main = idx_scores(*_hi_lo(ki_st[s]), _dot)
        tail = idx_scores(*_hi_lo(kin_ref[s]), _dot_nt)
        sc_scr[:, 0:past] = main if s == 0 else sc_scr[:, 0:past] + main
        sc_scr[:, past:] = tail if s == 0 else sc_scr[:, past:] + tail

    n_keys = n_tiles * page
    n_rows = n_seq * t_new

    def valid(l0):
        kp = l0 + lax.broadcasted_iota(I32, (n_rows, LANES), 1)
        row = lax.broadcasted_iota(I32, (n_rows, LANES), 0)
        tok = row
        for s in range(1, n_seq):
            tok = jnp.where(row >= s * t_new, row - s * t_new, tok)
        return kp <= past + tok

    for c in range(n_tiles):
        sl = slice(c * page, (c + 1) * page)
        key_scr[:, sl] = _sortable_key(jnp.where(valid(c * page), sc_scr[:, sl], NEG_INF))
    _topk_select_rows(key_scr, sel_scr, topk, valid)

    for s in range(n_seq):
        for g in range(N_KV_HEADS):
            rs = slice(g * rows_g, (g + 1) * rows_g)
            cols = slice(g * HEAD_DIM, (g + 1) * HEAD_DIM)
            qg = (q_ref[s, rs, :] * (HEAD_DIM ** -0.5)).astype(BF16)
            lg_scr[rs, 0:past] = _dot(qg, k_st[s, g])
            lg_scr[rs, past:] = _dot_nt(qg, kn_ref[s, :, cols].astype(BF16))
        near = slice(past - page, n_keys)
        lg_scr[:, near] = lg_scr[:, near] + bias_scr[...]
        picked = _dot(xpand_ref[s], sel_scr[...]) > 0.5
        lg = jnp.where(picked, lg_scr[...], NEG_INF)
        m = jnp.max(lg, axis=1, keepdims=True)
        p = jnp.exp(lg - m)
        p_scr[...] = (p * (1.0 / jnp.sum(p, axis=1, keepdims=True))).astype(BF16)
        for g in range(N_KV_HEADS):
            rs = slice(g * rows_g, (g + 1) * rows_g)
            cols = slice(g * HEAD_DIM, (g + 1) * HEAD_DIM)
            acc = _dot_nt(p_scr[rs, 0:past], v_st[s, g]) + _dot(p_scr[rs, past:], vn_ref[s, :, cols].astype(BF16))
            o_ref[s, rs, :] = acc.astype(o_ref.dtype)


def attn_sample(proj_s, cache_k, cache_v, cache_kidx, page_table, bias_table, t_new):
    nb, n_pages = page_table.shape
    n_pool, page = cache_k.shape[0], cache_k.shape[1]
    past = n_pages * page
    total = past + t_new
    topk = min(TOPK_MAX, total // 4)
    kvw = N_KV_HEADS * HEAD_DIM
    n_seq = SAMPLE_SEQS_PER_STEP if nb % SAMPLE_SEQS_PER_STEP == 0 else 1
    rows_q = t_new * N_HEADS
    rows_i = t_new * IDX_HEADS
    n_rows = n_seq * t_new
    n_keys = past + page
    assert page == LANES and page >= MAX_DISTANCE and t_new <= page
    ck_t = jnp.transpose(cache_k, (0, 2, 3, 1))
    cv_t = jnp.transpose(cache_v, (0, 2, 3, 1))
    cki_t = jnp.transpose(cache_kidx, (0, 2, 1))
    seg = lambda col, width: proj_s[:, col:col + width]
    q3 = seg(COL_Q, N_HEADS * HEAD_DIM).reshape(nb, t_new, N_KV_HEADS, Q_PER_KV, HEAD_DIM)
    q3 = q3.transpose(0, 2, 1, 3, 4).reshape(nb, rows_q, HEAD_DIM)
    qi3 = seg(COL_QI, IDX_HEADS * IDX_DIM).reshape(nb, rows_i, IDX_DIM)
    w3 = seg(COL_SM + SM_WI, IDX_HEADS).reshape(nb, rows_i, 1)
    pad_rows = lambda a: jnp.pad(a.reshape(nb, t_new, -1), ((0, 0), (0, page - t_new), (0, 0)))
    kn3, vn3, kin3 = pad_rows(seg(COL_K, kvw)), pad_rows(seg(COL_V, kvw)), pad_rows(seg(COL_KI, IDX_DIM))
    row = np.arange(rows_q)
    row_tok = (row // Q_PER_KV) % t_new
    row_head = (row // (t_new * Q_PER_KV)) * Q_PER_KV + row % Q_PER_KV
    tab_t = jnp.pad(bias_table.T[row_head], ((0, 0), (0, LANES - N_BUCKETS)))
    kpos = np.concatenate([past - page + np.arange(page), past + np.arange(page)])[None, :]
    bucket = jnp.asarray(_t5_bucket_np(past + row_tok[:, None] - kpos))
    gsum = np.zeros((n_seq, n_rows, rows_i), np.float32)
    xpand = np.zeros((n_seq, rows_q, n_rows), np.float32)
    for s in range(n_seq):
        gsum[s, s * t_new + np.arange(rows_i) // IDX_HEADS, np.arange(rows_i)] = 1.0
        xpand[s, row, s * t_new + row_tok] = 1.0
    per_step = lambda shape: pl.BlockSpec((n_seq,) + shape, lambda b, pt: (b,) + tuple(0 for _ in shape))
    const = lambda shape: pl.BlockSpec(shape, lambda b, pt: tuple(0 for _ in shape))

    def page_specs(shape):
        return [pl.BlockSpec((1,) + shape, lambda b, pt, s=s, c=c: (pt[b * n_seq + s, c],) + tuple(0 for _ in shape))
                for s in range(n_seq) for c in range(n_pages)]

    grid_spec = pltpu.PrefetchScalarGridSpec(
        num_scalar_prefetch=1,
        grid=(nb // n_seq,),
        in_specs=[per_step((rows_q, HEAD_DIM)), per_step((rows_i, IDX_DIM)), per_step((rows_i, 1)),
                  per_step((page, kvw)), per_step((page, kvw)), per_step((page, IDX_DIM)),
                  const((rows_q, LANES)), const((rows_q, 2 * page)),
                  const((n_seq, n_rows, rows_i)), const((n_seq, rows_q, n_rows))]
                 + page_specs((N_KV_HEADS, HEAD_DIM, page)) + page_specs((N_KV_HEADS, HEAD_DIM, page))
                 + page_specs((IDX_DIM, page)),
        out_specs=pl.BlockSpec((n_seq, rows_q, HEAD_DIM), lambda b, pt: (b, 0, 0)),
        scratch_shapes=[
            pltpu.VMEM((n_rows, n_keys), F32),
            pltpu.VMEM((n_rows, n_keys), I32),
            pltpu.VMEM((n_rows, n_keys), BF16),
            pltpu.VMEM((rows_q, n_keys), F32),
            pltpu.VMEM((rows_q, n_keys), BF16),
            pltpu.VMEM((rows_q, 2 * page), F32),
            pltpu.VMEM((n_seq, IDX_DIM, past), F32),
            pltpu.VMEM((n_seq, N_KV_HEADS, HEAD_DIM, past), BF16),
            pltpu.VMEM((n_seq, N_KV_HEADS, HEAD_DIM, past), BF16),
        ],
    )
    n_pg = n_seq * n_pages
    out = pl.pallas_call(
        functools.partial(_attn_sample_body, n_seq=n_seq, n_pages=n_pages, page=page, t_new=t_new, topk=topk),
        grid_spec=grid_spec,
        out_shape=jax.ShapeDtypeStruct((nb, rows_q, HEAD_DIM), BF16),
        compiler_params=_cparams(("arbitrary",)),
        name="attn_sample",
    )(page_table, q3, qi3, w3, kn3, vn3, kin3, tab_t, bucket, jnp.asarray(gsum, BF16), jnp.asarray(xpand, BF16),
      *([ck_t] * n_pg), *([cv_t] * n_pg), *([cki_t] * n_pg))
    out = out.reshape(nb, N_KV_HEADS, t_new, Q_PER_KV, HEAD_DIM).transpose(0, 2, 1, 3, 4)
    return out.reshape(nb * t_new, N_HEADS * HEAD_DIM)


def _merge_body(ys_ref, ya_ref, g_ref, h_ref, wbs_ref, wba_ref, wo_ref, lg_ref, lb_ref, o_ref):
    a = _dot(ys_ref[...], wbs_ref[...])
    b = _dot(ya_ref[...], wba_ref[...])
    gates = jax.nn.sigmoid(g_ref[...])
    merged = gates[:, :D_MODEL] * a + gates[:, D_MODEL:] * b
    r = ALPHA * h_ref[...] + _dot(merged.astype(BF16), wo_ref[...])
    o_ref[...] = _layer_norm(r, lg_ref[...], lb_ref[...])


def merge_out(y_ssm, y_att, proj, h1, wbs, wba, wo, g, b, *, tm=512):
    m = h1.shape[0]
    tm = min(tm, m)
    assert m % tm == 0
    row = lambda width, colblk=0: pl.BlockSpec((tm, width), lambda i: (i, colblk))
    full = lambda shape: pl.BlockSpec(shape, lambda i: tuple(0 for _ in shape))
    return pl.pallas_call(
        _merge_body,
        grid=(m // tm,),
        in_specs=[row(D_INNER), row(N_HEADS * HEAD_DIM), row(2 * D_MODEL, COL_G // (2 * D_MODEL)), row(D_MODEL),
                  full(wbs.shape), full(wba.shape), full(wo.shape), full((1, D_MODEL)), full((1, D_MODEL))],
        out_specs=row(D_MODEL),
        out_shape=jax.ShapeDtypeStruct((m, D_MODEL), F32),
        compiler_params=_cparams(("parallel",)),
        name="merge_out",
    )(y_ssm, y_att, proj, h1, wbs, wba, wo, g, b)


def kernel(x_prompt, x_sample, cache_k, cache_v, cache_kidx, state_conv, state_ssm, page_table, bias_table, w_in, conv_w, conv_b, dt_bias, a_log, d_skip, ssm_norm_g, w_branch_ssm, w_branch_attn, w_out, ffn1_wi, ffn1_wo, ffn2_wi, ffn2_wo, ln1_g, ln1_b, ln2_g, ln2_b, ln3_g, ln3_b):
    assert w_in.shape[0] == DEPTH
    batch, seq, d = x_prompt.shape
    nb, t_new, _ = x_sample.shape
    xs = (x_prompt.reshape(batch * seq, d), x_sample.reshape(nb * t_new, d))
    outs_p, outs_s = [], []
    for i in range(DEPTH):
        w_pack = pack_w_in(w_in[i])
        f1i, f1o = ffn1_wi[i].astype(BF16), ffn1_wo[i].astype(BF16)
        f2i, f2o = ffn2_wi[i].astype(BF16), ffn2_wo[i].astype(BF16)
        wbs, wba, wo = w_branch_ssm[i].astype(BF16), w_branch_attn[i].astype(BF16), w_out[i].astype(BF16)
        row = lambda v: v.reshape(1, -1)
        pad_heads = lambda v: jnp.concatenate([v, jnp.zeros((LANES - SSM_HEADS,), v.dtype)]).reshape(1, LANES)
        dtb_pad, alog_pad = pad_heads(dt_bias[i]), pad_heads(a_log[i])
        dskip_e = jnp.repeat(d_skip[i], SSM_HEAD_DIM).reshape(1, D_INNER)
        ng = row(ssm_norm_g[i])
        cw, cb = conv_w[i], row(conv_b[i])

        h1, proj = [], []
        for x in xs:
            hf, hb = ffn_ln(x, f1i, f1o, row(ln1_g[i]), row(ln1_b[i]), emit_bf16=True)
            h1.append(hf)
            proj.append(matmul(hb, w_pack, tm=2048, tn=PROJ_TN))
        proj_p, proj_s = proj

        ys_p, conv_p, ssm_p = ssd_prompt(proj_p, batch, seq, cw, cb, dtb_pad, alog_pad, dskip_e, ng)
        ya_p, (kt_p, vt_p, kit_p) = attn_prompt(proj_p, bias_table, batch, seq)

        raw_s = jnp.concatenate([proj_s[:, COL_X:COL_X + D_INNER], proj_s[:, COL_BC:COL_BC + 2 * GN]], axis=1)
        u_cat = jnp.concatenate([state_conv[i], raw_s.reshape(nb, t_new, CONV_DIM)], axis=1)
        expand = np.zeros((LANES, D_INNER), np.float32)
        expand[np.arange(D_INNER) // SSM_HEAD_DIM, np.arange(D_INNER)] = 1.0
        ys_s, conv_s, ssm_s = ssd_sample(u_cat, proj_s.reshape(nb, t_new, PROJ_N),
                                         state_ssm[i].reshape(nb, SSM_HEADS * SSM_HEAD_DIM, SSM_STATE),
                                         cw, cb, dtb_pad, alog_pad, dskip_e, ng, jnp.asarray(expand, BF16))
        ya_s = attn_sample(proj_s, cache_k[i], cache_v[i], cache_kidx[i], page_table, bias_table, t_new)

        new_xs = []
        for hf, pj, ys, ya in ((h1[0], proj_p, ys_p, ya_p), (h1[1], proj_s, ys_s.reshape(nb * t_new, D_INNER), ya_s)):
            h2 = merge_out(ys, ya, pj, hf, wbs, wba, wo, row(ln2_g[i]), row(ln2_b[i]))
            new_xs.append(ffn_ln(h2, f2i, f2o, row(ln3_g[i]), row(ln3_b[i]), emit_bf16=False))
        xs = tuple(new_xs)

        kvw = N_KV_HEADS * HEAD_DIM
        shp = lambda nb_, l_: (nb_, l_, N_KV_HEADS, HEAD_DIM)
        head_major = lambda a: a.reshape(batch, N_KV_HEADS, HEAD_DIM, seq).transpose(0, 3, 1, 2)
        outs_p.append((head_major(kt_p), head_major(vt_p), kit_p.transpose(0, 2, 1),
                       conv_p, ssm_p.reshape(batch, SSM_HEADS, SSM_HEAD_DIM, SSM_STATE)))
        outs_s.append((proj_s[:, COL_K:COL_K + kvw].reshape(shp(nb, t_new)),
                       proj_s[:, COL_V:COL_V + kvw].reshape(shp(nb, t_new)),
                       proj_s[:, COL_KI:COL_KI + IDX_DIM].reshape(nb, t_new, IDX_DIM),
                       conv_s, ssm_s.reshape(nb, SSM_HEADS, SSM_HEAD_DIM, SSM_STATE)))
    k_p, v_p, kidx_p, conv_pp, ssm_pp = [jnp.stack(a) for a in zip(*outs_p)]
    k_s, v_s, kidx_s, conv_ss, ssm_ss = [jnp.stack(a) for a in zip(*outs_s)]
    return (xs[0].reshape(batch, seq, d), xs[1].reshape(nb, t_new, d),
            k_p, v_p, kidx_p, conv_pp, ssm_pp, k_s, v_s, kidx_s, conv_ss, ssm_ss)
```

```python
import functools
import math

import numpy as np
import jax
import jax.numpy as jnp
from jax import lax
from jax.experimental import pallas as pl
from jax.experimental.pallas import tpu as pltpu

F32 = jnp.float32
BF16 = jnp.bfloat16
I32 = jnp.int32

D_MODEL = 1024
D_INNER = 2 * D_MODEL
SSM_HEAD_DIM = 64
SSM_HEADS = D_INNER // SSM_HEAD_DIM
SSM_GROUPS = 4
SSM_STATE = 128
CONV_W = 4
CONV_DIM = D_INNER + 2 * SSM_GROUPS * SSM_STATE
SSD_CHUNK = 128
N_HEADS = 16
HEAD_DIM = 64
N_KV_HEADS = 4
Q_PER_KV = N_HEADS // N_KV_HEADS
IDX_HEADS = 8
IDX_DIM = 64
TOPK_MAX = 256
N_BUCKETS = 32
MAX_DISTANCE = 128
D_FF = 2816
DEPTH = 1
ALPHA = (2 * DEPTH) ** 0.25
LN_EPS = 1e-5
IN_SPLITS = (D_INNER, CONV_DIM, SSM_HEADS, N_HEADS * HEAD_DIM, N_KV_HEADS * HEAD_DIM,
             N_KV_HEADS * HEAD_DIM, IDX_HEADS * IDX_DIM, IDX_DIM, IDX_HEADS, 2 * D_MODEL)

LANES = 128
SUBLANES = 8
VMEM_LIMIT_BYTES = 56 * 1024 * 1024

GN = SSM_GROUPS * SSM_STATE
COL_Z = 0
COL_X = COL_Z + D_INNER
COL_G = COL_X + D_INNER
COL_BC = COL_G + 2 * D_MODEL
COL_Q = COL_BC + 2 * GN
COL_K = COL_Q + N_HEADS * HEAD_DIM
COL_V = COL_K + N_KV_HEADS * HEAD_DIM
COL_QI = COL_V + N_KV_HEADS * HEAD_DIM
COL_SM = COL_QI + IDX_HEADS * IDX_DIM
COL_KI = COL_SM + LANES
PROJ_TN = 512
PROJ_N = -(-(COL_KI + LANES) // PROJ_TN) * PROJ_TN
SM_WI = SSM_HEADS

INT_MIN = -(2 ** 31)
LOG2E = math.log2(math.e)
NEG_INF = float("-inf")
PROMPT_KEY_GROUPS = 8
SAMPLE_SEQS_PER_STEP = 4
SSD_SAMPLE_SEQS_PER_STEP = 4
RADIX_BITS = 4


def _cparams(sem):
    return pltpu.CompilerParams(dimension_semantics=sem, vmem_limit_bytes=VMEM_LIMIT_BYTES)


def _layer_norm(r, g, b):
    mu = jnp.mean(r, axis=-1, keepdims=True)
    d = r - mu
    var = jnp.mean(d * d, axis=-1, keepdims=True)
    return d * lax.rsqrt(var + LN_EPS) * g + b


def _silu(x):
    h = 0.5 * x
    return h + h * jnp.tanh(h)


def _softplus(x):
    return jnp.maximum(x, 0.0) + jnp.log1p(jnp.exp(-jnp.abs(x)))


def _dot(a, b):
    return jnp.dot(a, b, preferred_element_type=F32)


def _dot_nt(a, b):
    return lax.dot_general(a, b, (((1,), (1,)), ((), ())), preferred_element_type=F32)


def _dot_tn(a, b):
    return lax.dot_general(a, b, (((0,), (0,)), ((), ())), preferred_element_type=F32)


def _split3(v):
    hi = v.astype(BF16)
    r1 = v - hi.astype(F32)
    mid = r1.astype(BF16)
    lo = (r1 - mid.astype(F32)).astype(BF16)
    return hi, mid, lo


def _dot_exact_sel(v, sel_bf16):
    hi, mid, lo = _split3(v)
    return (_dot(hi, sel_bf16) + _dot(mid, sel_bf16)) + _dot(lo, sel_bf16)


def _sel_dot_exact(sel_bf16, v):
    hi, mid, lo = _split3(v)
    return (_dot(sel_bf16, hi) + _dot(sel_bf16, mid)) + _dot(sel_bf16, lo)


def _hi_lo(a):
    ah = a.astype(BF16)
    return ah, (a - ah.astype(F32)).astype(BF16)


def _dot_nt_x3(a, b):
    ah, al = _hi_lo(a)
    bh, bl = _hi_lo(b)
    return (_dot_nt(ah, bl) + _dot_nt(al, bh)) + _dot_nt(ah, bh)


def _dot_x3(a, b):
    ah, al = _hi_lo(a)
    bh, bl = _hi_lo(b)
    return (_dot(ah, bl) + _dot(al, bh)) + _dot(ah, bh)


def _col_reduce(x, op2, op):
    parts = []
    for r0 in range(0, x.shape[0], LANES):
        y = x[r0:r0 + LANES]
        n = y.shape[0]
        while n > SUBLANES and n % (2 * SUBLANES) == 0:
            n //= 2
            y = op2(y[:n], y[n:])
        parts.append(y)
    while len(parts) > 1:
        parts = [op2(parts[i], parts[i + 1]) if i + 1 < len(parts) else parts[i] for i in range(0, len(parts), 2)]
    return op(parts[0], axis=0, keepdims=True)


def _col_sum(x):
    return _col_reduce(x, jnp.add, jnp.sum)


def _col_max(x):
    return _col_reduce(x, jnp.maximum, jnp.max)


def _ffn_ln_body(x_ref, wi_ref, wo_ref, g_ref, b_ref, o_ref, *maybe_ob_ref, tf):
    x = x_ref[...]
    xb = x.astype(BF16)
    dff = wo_ref.shape[0]
    acc = None
    for c0 in range(0, dff, tf):
        gate = _dot(xb, wi_ref[:, c0:c0 + tf])
        up = _dot(xb, wi_ref[:, dff + c0:dff + c0 + tf])
        part = _dot((_silu(gate) * up).astype(BF16), wo_ref[c0:c0 + tf, :])
        acc = part if acc is None else acc + part
    y = _layer_norm(ALPHA * x + 0.5 * acc, g_ref[...], b_ref[...])
    o_ref[...] = y
    for ob_ref in maybe_ob_ref:
        ob_ref[...] = y.astype(BF16)


def ffn_ln(x, wi_bf, wo_bf, g, b, *, emit_bf16, tm=512, tf=1408):
    m, d = x.shape
    dff = wo_bf.shape[0]
    tm = min(tm, m)
    assert m % tm == 0 and dff % tf == 0
    row = pl.BlockSpec((tm, d), lambda i: (i, 0))
    resident = lambda shape: pl.BlockSpec(shape, lambda i: (0, 0), pipeline_mode=pl.Buffered(1))
    n_out = 2 if emit_bf16 else 1
    res = pl.pallas_call(
        functools.partial(_ffn_ln_body, tf=tf),
        grid=(m // tm,),
        in_specs=[row, resident(wi_bf.shape), resident(wo_bf.shape), resident((1, d)), resident((1, d))],
        out_specs=[row] * n_out,
        out_shape=[jax.ShapeDtypeStruct((m, d), F32), jax.ShapeDtypeStruct((m, d), BF16)][:n_out],
        compiler_params=_cparams(("parallel",)),
        name="ffn_ln",
    )(x, wi_bf, wo_bf, g, b)
    return res if emit_bf16 else res[0]


def _matmul_body(x_ref, w_ref, o_ref):
    o_ref[...] = _dot(x_ref[...], w_ref[...])


def matmul(x_bf, w_bf, *, tm, tn):
    m, k = x_bf.shape
    n = w_bf.shape[1]
    tm = min(tm, m)
    assert m % tm == 0 and n % tn == 0
    return pl.pallas_call(
        _matmul_body,
        grid=(m // tm, n // tn),
        in_specs=[pl.BlockSpec((tm, k), lambda i, j: (i, 0)),
                  pl.BlockSpec((k, tn), lambda i, j: (0, j))],
        out_specs=pl.BlockSpec((tm, tn), lambda i, j: (i, j)),
        out_shape=jax.ShapeDtypeStruct((m, n), F32),
        compiler_params=_cparams(("parallel", "arbitrary")),
        name="in_proj",
    )(x_bf, w_bf)


def pack_w_in(w_in):
    offs = np.cumsum((0,) + IN_SPLITS)
    z, xbc, dt, q, k, v, qi, ki, wi, gates = [w_in[:, offs[i]:offs[i + 1]] for i in range(len(IN_SPLITS))]
    d = w_in.shape[0]
    zeros = lambda n: jnp.zeros((d, n), w_in.dtype)
    small = jnp.concatenate([dt, wi, zeros(LANES - SSM_HEADS - IDX_HEADS)], axis=1)
    kib = jnp.concatenate([ki, ki], axis=1)
    packed = jnp.concatenate([z, xbc[:, :D_INNER], gates, xbc[:, D_INNER:], q, k, v, qi, small, kib,
                              zeros(PROJ_N - COL_KI - LANES)], axis=1)
    return packed.astype(BF16)


def _ssd_prompt_body(z_ref, xr_ref, bc_ref, sm_ref, cwx_ref, cwbc_ref, cbx_ref, cbbc_ref,
                     dtb_ref, alog_ref, dskip_ref, ng_ref,
                     y_ref, cs_ref, hs_ref,
                     xbuf, bcbuf, h_scr, xc_scr, bcc_scr, y_scr):
    c = pl.program_id(1)
    nc = pl.num_programs(1)
    L = SSD_CHUNK
    halo = SUBLANES
    n_pairs = SSM_HEADS // 2
    pairs_per_group = n_pairs // SSM_GROUPS

    @pl.when(c == 0)
    def _():
        xbuf[...] = jnp.zeros_like(xbuf)
        bcbuf[...] = jnp.zeros_like(bcbuf)
        h_scr[...] = jnp.zeros_like(h_scr)

    assert CONV_W == 4

    def conv(halo_ref, x_ref, w_ref, b_ref):
        full = jnp.concatenate([halo_ref[...], x_ref[...]], axis=0)
        prev = pltpu.roll(full, 1, axis=0)
        late = (w_ref[1:2, :] * full + w_ref[0:1, :] * prev)
        early = w_ref[3:4, :] * full[halo:] + w_ref[2:3, :] * prev[halo:]
        acc = (b_ref[...] + early) + pltpu.roll(late, 2, axis=0)[halo:]
        halo_ref[...] = x_ref[L - halo:L, :]
        return _silu(acc)

    xc_scr[...] = conv(xbuf, xr_ref, cwx_ref, cbx_ref)
    bcc_scr[...] = conv(bcbuf, bc_ref, cwbc_ref, cbbc_ref)

    dt = _softplus(sm_ref[...] + dtb_ref[...])
    a = -jnp.exp(alog_ref[...])
    row = lax.broadcasted_iota(I32, (L, L), 0)
    col = lax.broadcasted_iota(I32, (L, L), 1)
    tril = (row >= col).astype(F32)
    cum = jnp.dot(tril, dt * a, precision=lax.Precision.HIGHEST, preferred_element_type=F32)
    w = jnp.exp(cum[L - 1:L, :] - cum) * dt
    cum_t = cum.T
    dt_t = dt.T
    causal = row >= col
    lane = lax.broadcasted_iota(I32, (L, LANES), 1)
    lo_half = lane < SSM_HEAD_DIM
    sub = lax.broadcasted_iota(I32, (2 * SSM_HEAD_DIM, SSM_STATE), 0)
    lo_rows = sub < SSM_HEAD_DIM

    for i in range(n_pairs):
        g = i // pairs_per_group
        h0, h1 = 2 * i, 2 * i + 1
        sl = slice(i * LANES, (i + 1) * LANES)
        if i % pairs_per_group == 0:
            bg = bcc_scr[:, g * SSM_STATE:(g + 1) * SSM_STATE].astype(BF16)
            cg = bcc_scr[:, GN + g * SSM_STATE:GN + (g + 1) * SSM_STATE].astype(BF16)
            cb = _dot_nt(cg, bg)
        xp = xc_scr[:, sl]
        c0 = jnp.broadcast_to(cum[:, h0:h0 + 1], (L, LANES))
        c1 = jnp.broadcast_to(cum[:, h1:h1 + 1], (L, LANES))
        w0 = jnp.broadcast_to(w[:, h0:h0 + 1], (L, LANES))
        w1 = jnp.broadcast_to(w[:, h1:h1 + 1], (L, LANES))

        def mix(ct, h):
            seg = jnp.exp(jnp.where(causal, ct - cum_t[h:h + 1, :], NEG_INF))
            return (cb * seg * dt_t[h:h + 1, :]).astype(BF16)

        zero = jnp.zeros_like(xp)
        yd = (_dot(mix(c0, h0), jnp.where(lo_half, xp, zero).astype(BF16))
              + _dot(mix(c1, h1), jnp.where(lo_half, zero, xp).astype(BF16)))
        hp = h_scr[i]
        yo = _dot_nt(cg, hp.astype(BF16)) * jnp.where(lo_half, jnp.exp(c0), jnp.exp(c1))
        y_scr[:, sl] = yd + yo + dskip_ref[:, sl] * xp
        xw = (xp * jnp.where(lo_half, w0, w1)).astype(BF16)
        st = _dot_tn(xw, bg)
        dec = jnp.where(lo_rows, jnp.exp(c0[L - 1:L, :]), jnp.exp(c1[L - 1:L, :]))
        h_scr[i] = dec * hp + st

    gw = D_INNER // SSM_GROUPS
    for g in range(SSM_GROUPS):
        sl = slice(g * gw, (g + 1) * gw)
        u = y_scr[:, sl] * _silu(z_ref[:, sl])
        ms = jnp.mean(u * u, axis=-1, keepdims=True)
        y_ref[:, sl] = (u * lax.rsqrt(ms + LN_EPS) * ng_ref[:, sl]).astype(y_ref.dtype)

    @pl.when(c == nc - 1)
    def _():
        cs_ref[0, :, 0:D_INNER] = xr_ref[L - (CONV_W - 1):L, :]
        cs_ref[0, :, D_INNER:CONV_DIM] = bc_ref[L - (CONV_W - 1):L, :]
        for i in range(n_pairs):
            hs_ref[0, i * LANES:(i + 1) * LANES, :] = h_scr[i]


def ssd_prompt(proj, batch, seq, conv_w, conv_b, dtb_pad, alog_pad, dskip_e, norm_g):
    L = SSD_CHUNK
    nc = seq // L
    assert seq % L == 0
    blk = lambda width, colblk: pl.BlockSpec((L, width), lambda b, c: (b * nc + c, colblk))
    full = lambda shape: pl.BlockSpec(shape, lambda b, c: tuple(0 for _ in shape))
    n_pairs = SSM_HEADS // 2
    return pl.pallas_call(
        _ssd_prompt_body,
        grid=(batch, nc),
        in_specs=[
            blk(D_INNER, COL_Z // D_INNER),
            blk(D_INNER, COL_X // D_INNER),
            blk(2 * GN, COL_BC // (2 * GN)),
            blk(LANES, COL_SM // LANES),
            full((CONV_W, D_INNER)), full((CONV_W, 2 * GN)), full((1, D_INNER)), full((1, 2 * GN)),
            full((1, LANES)), full((1, LANES)), full((1, D_INNER)), full((1, D_INNER)),
        ],
        out_specs=[
            pl.BlockSpec((L, D_INNER), lambda b, c: (b * nc + c, 0)),
            pl.BlockSpec((1, CONV_W - 1, CONV_DIM), lambda b, c: (b, 0, 0)),
            pl.BlockSpec((1, SSM_HEADS * SSM_HEAD_DIM, SSM_STATE), lambda b, c: (b, 0, 0)),
        ],
        out_shape=[
            jax.ShapeDtypeStruct((batch * seq, D_INNER), BF16),
            jax.ShapeDtypeStruct((batch, CONV_W - 1, CONV_DIM), F32),
            jax.ShapeDtypeStruct((batch, SSM_HEADS * SSM_HEAD_DIM, SSM_STATE), F32),
        ],
        scratch_shapes=[
            pltpu.VMEM((SUBLANES, D_INNER), F32),
            pltpu.VMEM((SUBLANES, 2 * GN), F32),
            pltpu.VMEM((n_pairs, 2 * SSM_HEAD_DIM, SSM_STATE), F32),
            pltpu.VMEM((L, D_INNER), F32),
            pltpu.VMEM((L, 2 * GN), F32),
            pltpu.VMEM((L, D_INNER), F32),
        ],
        compiler_params=_cparams(("parallel", "arbitrary")),
        name="ssd_prompt",
    )(proj, proj, proj, proj, conv_w[:, :D_INNER], conv_w[:, D_INNER:], conv_b[:, :D_INNER], conv_b[:, D_INNER:],
      dtb_pad, alog_pad, dskip_e, norm_g)


def _ssd_sample_body(u_ref, z_ref, sm_ref, h0_ref, cw_ref, cb_ref, dtb_ref, alog_ref, dskip_ref, ng_ref, exp_ref,
                     y_ref, cs_ref, hn_ref, *, t_new, n_seq):
    for s in range(n_seq):
        _ssd_sample_one(s, u_ref, z_ref, sm_ref, h0_ref, cw_ref, cb_ref, dtb_ref, alog_ref, dskip_ref, ng_ref, exp_ref,
                        y_ref, cs_ref, hn_ref, t_new)


def _ssd_sample_one(s, u_ref, z_ref, sm_ref, h0_ref, cw_ref, cb_ref, dtb_ref, alog_ref, dskip_ref, ng_ref, exp_ref,
                    y_ref, cs_ref, hn_ref, t_new):
    T = t_new
    n_pairs = SSM_HEADS // 2
    pairs_per_group = n_pairs // SSM_GROUPS
    acc = cb_ref[...] + cw_ref[0:1, :] * u_ref[s, 0:T, :]
    for k in range(1, CONV_W):
        acc = acc + cw_ref[k:k + 1, :] * u_ref[s, k:k + T, :]
    xbc = _silu(acc)
    cs_ref[s] = u_ref[s, T:T + CONV_W - 1, :]
    x = xbc[:, :D_INNER]

    dt = _softplus(sm_ref[s] + dtb_ref[...])
    da = dt * (-jnp.exp(alog_ref[...]))
    rows = [da[0:1, :]]
    for t in range(1, T):
        rows.append(rows[-1] + da[t:t + 1, :])
    cum = jnp.concatenate(rows, axis=0)
    both = _dot_exact_sel(jnp.concatenate([cum, dt], axis=0), exp_ref[...])
    cum_e = both[0:T, :]
    dt_e = both[T:2 * T, :]
    t_idx = lax.broadcasted_iota(I32, (T, GN), 0)

    gw = D_INNER // SSM_GROUPS
    y_parts = []
    for g in range(SSM_GROUPS):
        sl = slice(g * gw, (g + 1) * gw)
        bg = xbc[:, D_INNER + g * SSM_STATE:D_INNER + (g + 1) * SSM_STATE]
        cg = xbc[:, D_INNER + GN + g * SSM_STATE:D_INNER + GN + (g + 1) * SSM_STATE]
        cbg = _dot_nt(cg.astype(BF16), bg.astype(BF16))
        yg = dskip_ref[:, sl] * x[:, sl]
        for s_tok in range(T):
            seg = jnp.exp(jnp.where(t_idx >= s_tok, cum_e[:, sl] - cum_e[s_tok:s_tok + 1, sl], NEG_INF))
            coef = seg * dt_e[s_tok:s_tok + 1, sl] * jnp.broadcast_to(cbg[:, s_tok:s_tok + 1], (T, gw))
            yg = yg + coef * x[s_tok:s_tok + 1, sl]
        y_parts.append(yg)

    xw = x * dt_e * jnp.exp(cum_e[T - 1:T, :] - cum_e)
    ecum = jnp.exp(cum_e)
    sub = lax.broadcasted_iota(I32, (2 * SSM_HEAD_DIM, SSM_STATE), 0)
    lo_rows = sub < SSM_HEAD_DIM
    yo_parts = []
    for i in range(n_pairs):
        g = i // pairs_per_group
        sl = slice(i * LANES, (i + 1) * LANES)
        bg = xbc[:, D_INNER + g * SSM_STATE:D_INNER + (g + 1) * SSM_STATE].astype(BF16)
        cg = xbc[:, D_INNER + GN + g * SSM_STATE:D_INNER + GN + (g + 1) * SSM_STATE].astype(BF16)
        hp = h0_ref[s, i * LANES:(i + 1) * LANES, :]
        yo_parts.append(_dot_nt(cg, hp.astype(BF16)) * ecum[:, sl])
        st = _dot_tn(xw[:, sl].astype(BF16), bg)
        d0 = jnp.exp(cum[T - 1:T, 2 * i:2 * i + 1])
        d1 = jnp.exp(cum[T - 1:T, 2 * i + 1:2 * i + 2])
        dec = jnp.where(lo_rows, jnp.broadcast_to(d0, sub.shape), jnp.broadcast_to(d1, sub.shape))
        hn_ref[s, i * LANES:(i + 1) * LANES, :] = dec * hp + st

    for g in range(SSM_GROUPS):
        sl = slice(g * gw, (g + 1) * gw)
        yo = jnp.concatenate(yo_parts[g * pairs_per_group:(g + 1) * pairs_per_group], axis=1)
        u = (y_parts[g] + yo) * _silu(z_ref[s, :, sl])
        ms = jnp.mean(u * u, axis=-1, keepdims=True)
        y_ref[s, :, sl] = (u * lax.rsqrt(ms + LN_EPS) * ng_ref[:, sl]).astype(y_ref.dtype)


def ssd_sample(u_cat, proj3, h0, conv_w, conv_b, dtb_pad, alog_pad, dskip_e, norm_g, expand):
    nb, t_new, _ = proj3.shape
    full = lambda shape: pl.BlockSpec(shape, lambda b: tuple(0 for _ in shape))
    hp = SSM_HEADS * SSM_HEAD_DIM
    n_seq = SSD_SAMPLE_SEQS_PER_STEP if nb % SSD_SAMPLE_SEQS_PER_STEP == 0 else 1
    return pl.pallas_call(
        functools.partial(_ssd_sample_body, t_new=t_new, n_seq=n_seq),
        grid=(nb // n_seq,),
        in_specs=[
            pl.BlockSpec((n_seq, t_new + CONV_W - 1, CONV_DIM), lambda b: (b, 0, 0)),
            pl.BlockSpec((n_seq, t_new, D_INNER), lambda b: (b, 0, COL_Z // D_INNER)),
            pl.BlockSpec((n_seq, t_new, LANES), lambda b: (b, 0, COL_SM // LANES)),
            pl.BlockSpec((n_seq, hp, SSM_STATE), lambda b: (b, 0, 0)),
            full((CONV_W, CONV_DIM)), full((1, CONV_DIM)), full((1, LANES)), full((1, LANES)),
            full((1, D_INNER)), full((1, D_INNER)), full((LANES, D_INNER)),
        ],
        out_specs=[
            pl.BlockSpec((n_seq, t_new, D_INNER), lambda b: (b, 0, 0)),
            pl.BlockSpec((n_seq, CONV_W - 1, CONV_DIM), lambda b: (b, 0, 0)),
            pl.BlockSpec((n_seq, hp, SSM_STATE), lambda b: (b, 0, 0)),
        ],
        out_shape=[
            jax.ShapeDtypeStruct((nb, t_new, D_INNER), BF16),
            jax.ShapeDtypeStruct((nb, CONV_W - 1, CONV_DIM), F32),
            jax.ShapeDtypeStruct((nb, hp, SSM_STATE), F32),
        ],
        compiler_params=_cparams(("parallel",)),
        name="ssd_sample",
    )(u_cat, proj3, proj3, h0, conv_w, conv_b, dtb_pad, alog_pad, dskip_e, norm_g, expand)


def _sortable_key(score):
    score = jnp.where(score == 0.0, 0.0, score)
    bits = lax.bitcast_convert_type(score, I32)
    return bits ^ ((bits >> 31) & jnp.int32(0x7FFFFFFF))


def _t5_bucket_np(rel):
    n = np.maximum(rel, 0)
    max_exact = N_BUCKETS // 2
    nf = np.maximum(n, 1).astype(np.float32)
    large = max_exact + (np.log(nf / max_exact) / math.log(MAX_DISTANCE / max_exact)
                         * (N_BUCKETS - max_exact)).astype(np.int32)
    large = np.minimum(large, N_BUCKETS - 1)
    return np.where(n < max_exact, n, large).astype(np.int32)


def _count_cols_i16(ref, n_rows, pred):
    packed_rows = 2 * SUBLANES
    parts = []
    for r0 in range(0, n_rows, LANES):
        x = jnp.where(pred(ref[r0:r0 + LANES, :]), jnp.int16(1), jnp.int16(0))
        n = x.shape[0]
        while n > packed_rows:
            n //= 2
            x = x[:n] + x[n:]
        parts.append(x)
    while len(parts) > 1:
        parts = [parts[i] + parts[i + 1] if i + 1 < len(parts) else parts[i] for i in range(0, len(parts), 2)]
    return jnp.sum(parts[0].astype(I32), axis=0, keepdims=True)


def _kth_largest_i16(ref, n_rows, k):
    i16_min = -(2 ** 15)

    def try_cand(r, cand):
        cnt = _count_cols_i16(ref, n_rows, lambda v: v >= cand.astype(jnp.int16))
        return jnp.where(cnt >= k, cand, r)

    r0 = try_cand(jnp.full((1, LANES), i16_min, I32), jnp.zeros((1, LANES), I32))
    return lax.fori_loop(0, 15, lambda it, r: try_cand(r, r + (jnp.int32(1) << (14 - it))), r0)


def _kth_largest_cols(key_ref, hi_ref, lo_ref, n_rows, k):
    i16_min = -(2 ** 15)
    for r0 in range(0, n_rows, LANES):
        key = key_ref[r0:r0 + LANES, :]
        hi_ref[r0:r0 + LANES, :] = (key >> 16).astype(jnp.int16)
        lo_ref[r0:r0 + LANES, :] = ((key & 0xFFFF) + i16_min).astype(jnp.int16)
    t_hi = _kth_largest_i16(hi_ref, n_rows, k)
    t_hi16 = t_hi.astype(jnp.int16)
    k_lo = k - _count_cols_i16(hi_ref, n_rows, lambda v: v > t_hi16)
    for r0 in range(0, n_rows, LANES):
        sl = slice(r0, r0 + LANES)
        lo_ref[sl, :] = jnp.where(hi_ref[sl, :] == t_hi16, lo_ref[sl, :], jnp.int16(i16_min))
    t_lo = _kth_largest_i16(lo_ref, n_rows, k_lo)
    return (t_hi << 16) | ((t_lo - i16_min) & 0xFFFF)


def _topk_mask_cols(key_ref, hi_ref, lo_ref, msk_ref, n_rows, k, valid_fn):
    thr = _kth_largest_cols(key_ref, hi_ref, lo_ref, n_rows, k)
    n_gt = _col_sum(jnp.where(key_ref[0:n_rows, :] > thr, 1.0, 0.0))
    need = k - n_gt
    run = jnp.zeros((1, LANES), F32)
    ri = lax.broadcasted_iota(I32, (LANES, LANES), 0)
    ci = lax.broadcasted_iota(I32, (LANES, LANES), 1)
    strict = (ri > ci).astype(BF16)
    for r0 in range(0, n_rows, LANES):
        kc = key_ref[r0:r0 + LANES, :]
        eq = kc == thr
        eqf = eq.astype(F32)
        before = _dot(strict, eqf.astype(BF16)) + run
        sel = ((kc > thr) | (eq & (before < need))) & valid_fn(r0, LANES)
        msk_ref[r0:r0 + LANES, :] = jnp.where(sel, 0.0, NEG_INF)
        run = run + jnp.sum(eqf, axis=0, keepdims=True)


def _kth_largest_rows(key_ref, k):
    rows = key_ref.shape[0]
    digits = 2 ** RADIX_BITS
    n_rounds = 32 // RADIX_BITS

    def body(rnd, r):
        shift = (32 - RADIX_BITS) - RADIX_BITS * rnd
        n_ok = jnp.zeros((rows, 1), I32)
        for i in range(1, digits):
            cand = r + (jnp.int32(i) << shift)
            cnt = jnp.sum((key_ref[...] >= cand).astype(I32), axis=1, keepdims=True)
            n_ok = n_ok + (cnt >= k).astype(I32)
        return r + (n_ok << shift)

    return lax.fori_loop(0, n_rounds, body, jnp.full((rows, 1), INT_MIN, I32))


def _topk_select_rows(key_ref, sel_ref, k, valid_fn):
    n_keys = key_ref.shape[1]
    thr = _kth_largest_rows(key_ref, k)
    n_gt = jnp.sum((key_ref[...] > thr).astype(I32), axis=1, keepdims=True)
    need = (k - n_gt).astype(F32)
    run = jnp.zeros((key_ref.shape[0], 1), F32)
    ri = lax.broadcasted_iota(I32, (LANES, LANES), 0)
    ci = lax.broadcasted_iota(I32, (LANES, LANES), 1)
    strict = (ri < ci).astype(BF16)
    for l0 in range(0, n_keys, LANES):
        kc = key_ref[:, l0:l0 + LANES]
        eq = kc == thr
        eqf = eq.astype(F32)
        before = _dot(eqf.astype(BF16), strict) + run
        sel = ((kc > thr) | (eq & (before < need))) & valid_fn(l0)
        sel_ref[:, l0:l0 + LANES] = sel.astype(F32).astype(BF16)
        run = run + jnp.sum(eqf, axis=1, keepdims=True)


def _bias_tiles_body(tab_ref, bucket_ref, o_ref):
    for d in range(bucket_ref.shape[0]):
        bk = bucket_ref[d]
        for h in range(N_HEADS):
            acc = jnp.zeros(bk.shape, F32)
            for bb in range(N_BUCKETS):
                acc = jnp.where(bk == bb, tab_ref[bb, h], acc)
            o_ref[h, d] = (acc - tab_ref[N_BUCKETS - 1, h]) * LOG2E


def _attn_prompt_body(bias_ref, q_ref, k_ref, v_ref, qi_ref, ki_ref, sm_ref, *rest, nk, topk, j0, emit_transposed):
    key_scr, hi_scr, lo_scr, msk_scr, lg_scr, ot_scr, ka_scr = rest[-7:]
    if emit_transposed:
        k32_ref, v32_ref, o_ref, kt_ref, vt_ref, kit_ref = rest[-13:-7]

        @pl.when(pl.program_id(1) == 0)
        def _():
            kt_ref[0] = k32_ref[...].T
            vt_ref[0] = v32_ref[...].T
            kit_ref[0] = ki_ref[...].T[0:IDX_DIM, :]
    else:
        o_ref = rest[-8]
    j = j0 + pl.program_id(1)
    QB = LANES

    def valid(r0, rows):
        kp = r0 + lax.broadcasted_iota(I32, (rows, QB), 0)
        qp = j * QB + lax.broadcasted_iota(I32, (rows, QB), 1)
        return kp <= qp

    if nk <= topk:
        msk_scr[...] = jnp.where(valid(0, nk), 0.0, NEG_INF)
    else:
        @pl.when(pl.program_id(1) == 0)
        def _():
            x = ki_ref[0:nk, :]
            hi, lo = _hi_lo(x)
            first = lax.broadcasted_iota(I32, (nk, LANES), 1) < IDX_DIM
            ka_scr[:, 0:LANES] = jnp.where(first, hi, lo)
            ka_scr[:, LANES:2 * LANES] = jnp.where(first, hi, jnp.zeros_like(hi))

        wt = sm_ref[...].T
        ka = ka_scr[...]
        score = jnp.zeros((nk, QB), F32)
        pad = jnp.zeros((QB, 2 * LANES - 3 * IDX_DIM), BF16)
        for h2 in range(IDX_HEADS // 2):
            blocks = []
            for h in (2 * h2, 2 * h2 + 1):
                qh, ql = _hi_lo(qi_ref[:, h * IDX_DIM:(h + 1) * IDX_DIM] * (IDX_DIM ** -0.5))
                blocks.append(jnp.concatenate([ql, qh, qh, pad], axis=1))
            s = _dot_nt(ka, jnp.concatenate(blocks, axis=0))
            for i, h in enumerate((2 * h2, 2 * h2 + 1)):
                score = score + jnp.maximum(s[:, i * QB:(i + 1) * QB], 0.0) * wt[SM_WI + h:SM_WI + h + 1, :]
        score = score * (IDX_HEADS ** -0.5)
        kpos = lax.broadcasted_iota(I32, (nk, QB), 0)
        qpos = j * QB + lax.broadcasted_iota(I32, (nk, QB), 1)
        score = jnp.where(kpos <= qpos, score, NEG_INF)
        key_scr[...] = _sortable_key(score)

        _topk_mask_cols(key_scr, hi_scr, lo_scr, msk_scr, nk, topk, valid)


    near = min(2 * QB, nk)
    start = pl.multiple_of(jnp.clip((j - 1) * QB, 0, nk - near), QB)
    msk_near = msk_scr[pl.ds(start, near), :]
    rows = lax.broadcasted_iota(I32, (nk, QB), 0)
    lg_scr[...] = jnp.where((rows >= start) & (rows < start + near), NEG_INF, msk_scr[...])
    first_is_diag = j == 0
    for g in range(N_KV_HEADS):
        kcols = slice(g * HEAD_DIM, (g + 1) * HEAD_DIM)
        vcols = slice(g * LANES, (g + 1) * LANES)
        heads = range(g * Q_PER_KV, (g + 1) * Q_PER_KV)
        qg = jnp.concatenate([(q_ref[:, h * HEAD_DIM:(h + 1) * HEAD_DIM] * (HEAD_DIM ** -0.5 * LOG2E)).astype(BF16)
                              for h in heads], axis=0)
        lg_main = _dot_nt(k_ref[0:nk, kcols], qg)
        lg_near = _dot_nt(k_ref[pl.ds(start, near), kcols], qg)
        p_main, p_near = [], []
        for r, h in enumerate(heads):
            sl = slice(r * QB, (r + 1) * QB)
            bias = jnp.where(first_is_diag, bias_ref[h, 0], bias_ref[h, 1])
            if near > QB:
                bias = jnp.concatenate([bias, bias_ref[h, 0]], axis=0)
            lg_n = (lg_near[:, sl] + bias) + msk_near
            lg_m = lg_main[:, sl] + lg_scr[...]
            m = jnp.maximum(_col_max(lg_m), _col_max(lg_n))
            p_main.append(jnp.exp2(lg_m - m).astype(BF16))
            p_near.append(jnp.exp2(lg_n - m).astype(BF16))
        pv = (_dot_tn(v_ref[0:nk, vcols], jnp.concatenate(p_main, axis=1))
              + _dot_tn(v_ref[pl.ds(start, near), vcols], jnp.concatenate(p_near, axis=1)))
        for r, h in enumerate(heads):
            sl = slice(r * QB, (r + 1) * QB)
            ot_scr[h * HEAD_DIM:(h + 1) * HEAD_DIM, :] = pv[0:HEAD_DIM, sl] * (1.0 / pv[HEAD_DIM:HEAD_DIM + 1, sl])
    o_ref[...] = ot_scr[...].T.astype(o_ref.dtype)


def attn_prompt(proj, bias_table, batch, seq):
    QB = LANES
    nq = seq // QB
    assert seq % QB == 0
    topk = min(TOPK_MAX, seq // 4)
    ts = np.arange(QB)[:, None]
    tq = np.arange(QB)[None, :]
    bucket = jnp.asarray(np.stack([_t5_bucket_np(tq - ts), _t5_bucket_np(QB + tq - ts)]))
    n_groups = math.gcd(nq, PROMPT_KEY_GROUPS)
    per_group = nq // n_groups
    kvw = N_KV_HEADS * HEAD_DIM
    kv32 = lax.optimization_barrier(proj[:, COL_K:COL_K + 2 * kvw])
    k_bf = kv32[:, :kvw].astype(BF16)
    v4 = kv32[:, kvw:].astype(BF16).reshape(batch * seq, N_KV_HEADS, HEAD_DIM)
    v_aug = jnp.concatenate([v4, jnp.ones(v4.shape[:2] + (1,), BF16),
                             jnp.zeros(v4.shape[:2] + (LANES - HEAD_DIM - 1,), BF16)], axis=-1)
    v_aug = v_aug.reshape(batch * seq, N_KV_HEADS * LANES)
    bias = pl.pallas_call(
        _bias_tiles_body,
        in_specs=[pl.BlockSpec(memory_space=pltpu.SMEM), pl.BlockSpec(memory_space=pltpu.VMEM)],
        out_specs=pl.BlockSpec(memory_space=pltpu.VMEM),
        out_shape=jax.ShapeDtypeStruct((N_HEADS, 2, QB, QB), F32),
        name="bias_tiles",
    )(bias_table, bucket)
    out = None
    for grp in range(n_groups):
        j0 = grp * per_group
        nk = (j0 + per_group) * QB
        rowblk = lambda width, colblk: pl.BlockSpec((QB, width), lambda b, j: (b * nq + j0 + j, colblk))
        seqblk = lambda width, colblk: pl.BlockSpec((seq, width), lambda b, j: (b, colblk))
        prev = [] if out is None else [out]
        last = grp == n_groups - 1
        extra_in = [seqblk(kvw, COL_K // kvw), seqblk(kvw, COL_V // kvw)] if last else []
        tblk = lambda rows: pl.BlockSpec((1, rows, seq), lambda b, j: (b, 0, 0))
        tshape = lambda rows: jax.ShapeDtypeStruct((batch, rows, seq), F32)
        res = pl.pallas_call(
            functools.partial(_attn_prompt_body, nk=nk, topk=topk, j0=j0, emit_transposed=last),
            grid=(batch, per_group),
            in_specs=[
                pl.BlockSpec((N_HEADS, 2, QB, QB), lambda b, j: (0, 0, 0, 0)),
                rowblk(N_HEADS * HEAD_DIM, COL_Q // (N_HEADS * HEAD_DIM)),
                seqblk(kvw, 0),
                seqblk(N_KV_HEADS * LANES, 0),
                rowblk(IDX_HEADS * IDX_DIM, COL_QI // (IDX_HEADS * IDX_DIM)),
                seqblk(LANES, COL_KI // LANES),
                rowblk(LANES, COL_SM // LANES),
            ] + [pl.BlockSpec(memory_space=pl.ANY) for _ in prev] + extra_in,
            out_specs=[pl.BlockSpec((QB, N_HEADS * HEAD_DIM), lambda b, j: (b * nq + j0 + j, 0))]
                      + ([tblk(kvw), tblk(kvw), tblk(IDX_DIM)] if last else []),
            out_shape=[jax.ShapeDtypeStruct((batch * seq, N_HEADS * HEAD_DIM), BF16)]
                      + ([tshape(kvw), tshape(kvw), tshape(IDX_DIM)] if last else []),
            input_output_aliases={7: 0} if prev else {},
            scratch_shapes=[
                pltpu.VMEM((nk, QB), I32),
                pltpu.VMEM((nk, QB), jnp.int16),
                pltpu.VMEM((nk, QB), jnp.int16),
                pltpu.VMEM((nk, QB), F32),
                pltpu.VMEM((nk, QB), F32),
                pltpu.VMEM((N_HEADS * HEAD_DIM, QB), F32),
                pltpu.VMEM((nk, 2 * LANES), BF16),
            ],
            compiler_params=_cparams(("arbitrary", "arbitrary")),
            name=f"attn_prompt_g{grp}",
        )(bias, proj, k_bf, v_aug, proj, proj, proj, *prev, *([proj, proj] if last else []))
        out = res[0]
    assert nk == seq
    return out, res[1:]


def _attn_sample_body(pt_ref, q_ref, qi_ref, w_ref, kn_ref, vn_ref, kin_ref, tabt_ref, bucket_ref, gsum_ref, xpand_ref,
                      *rest, n_seq, n_pages, page, t_new, topk):
    n_pg = n_seq * n_pages
    ck = rest[0:n_pg]
    cv = rest[n_pg:2 * n_pg]
    cki = rest[2 * n_pg:3 * n_pg]
    o_ref, sc_scr, key_scr, sel_scr, lg_scr, p_scr, bias_scr, ki_st, k_st, v_st = rest[3 * n_pg:]
    past = n_pages * page
    n_tiles = n_pages + 1
    rows_q = t_new * N_HEADS
    rows_g = t_new * Q_PER_KV

    @pl.when(pl.program_id(0) == 0)
    def _():
        bk = bucket_ref[...]
        acc = jnp.zeros(bk.shape, F32)
        for bb in range(N_BUCKETS):
            acc = jnp.where(bk == bb, tabt_ref[:, bb:bb + 1], acc)
        bias_scr[...] = acc - tabt_ref[:, N_BUCKETS - 1:N_BUCKETS]

    for s in range(n_seq):
        for c in range(n_pages):
            sl = slice(c * page, (c + 1) * page)
            ki_st[s, :, sl] = cki[s * n_pages + c][0]
            k_st[s, :, :, sl] = ck[s * n_pages + c][0].astype(BF16)
            v_st[s, :, :, sl] = cv[s * n_pages + c][0].astype(BF16)

    for s in range(n_seq):
        qh, ql = _hi_lo(qi_ref[s])
        wcol = w_ref[s]
        gsum = gsum_ref[s]

        def idx_scores(kt_h, kt_l, dot):
            sc = (dot(qh, kt_l) + dot(ql, kt_h)) + dot(qh, kt_h)
            sc = jnp.maximum(sc * (IDX_DIM ** -0.5), 0.0) * wcol
            return _sel_dot_exact(gsum, sc) * (IDX_HEADS ** -0.5)

        main = idx_scores(*_hi_lo(ki_st[s]), _dot)
        tail = idx_scores(*_hi_lo(kin_ref[s]), _dot_nt)
        sc_scr[:, 0:past] = main if s == 0 else sc_scr[:, 0:past] + main
        sc_scr[:, past:] = tail if s == 0 else sc_scr[:, past:] + tail

    n_keys = n_tiles * page
    n_rows = n_seq * t_new

    def valid(l0):
        kp = l0 + lax.broadcasted_iota(I32, (n_rows, LANES), 1)
        row = lax.broadcasted_iota(I32, (n_rows, LANES), 0)
        tok = row
        for s in range(1, n_seq):
            tok = jnp.where(row >= s * t_new, row - s * t_new, tok)
        return kp <= past + tok

    for c in range(n_tiles):
        sl = slice(c * page, (c + 1) * page)
        key_scr[:, sl] = _sortable_key(jnp.where(valid(c * page), sc_scr[:, sl], NEG_INF))
    _topk_select_rows(key_scr, sel_scr, topk, valid)

    for s in range(n_seq):
        for g in range(N_KV_HEADS):
            rs = slice(g * rows_g, (g + 1) * rows_g)
            cols = slice(g * HEAD_DIM, (g + 1) * HEAD_DIM)
            qg = (q_ref[s, rs, :] * (HEAD_DIM ** -0.5)).astype(BF16)
            lg_scr[rs, 0:past] = _dot(qg, k_st[s, g])
            lg_scr[rs, past:] = _dot_nt(qg, kn_ref[s, :, cols].astype(BF16))
        near = slice(past - page, n_keys)
        lg_scr[:, near] = lg_scr[:, near] + bias_scr[...]
        picked = _dot(xpand_ref[s], sel_scr[...]) > 0.5
        lg = jnp.where(picked, lg_scr[...], NEG_INF)
        m = jnp.max(lg, axis=1, keepdims=True)
        p = jnp.exp(lg - m)
        p_scr[...] = (p * (1.0 / jnp.sum(p, axis=1, keepdims=True))).astype(BF16)
        for g in range(N_KV_HEADS):
            rs = slice(g * rows_g, (g + 1) * rows_g)
            cols = slice(g * HEAD_DIM, (g + 1) * HEAD_DIM)
            acc = _dot_nt(p_scr[rs, 0:past], v_st[s, g]) + _dot(p_scr[rs, past:], vn_ref[s, :, cols].astype(BF16))
            o_ref[s, rs, :] = acc.astype(o_ref.dtype)


def attn_sample(proj_s, cache_k, cache_v, cache_kidx, page_table, bias_table, t_new):
    nb, n_pages = page_table.shape
    n_pool, page = cache_k.shape[0], cache_k.shape[1]
    past = n_pages * page
    total = past + t_new
    topk = min(TOPK_MAX, total // 4)
    kvw = N_KV_HEADS * HEAD_DIM
    n_seq = SAMPLE_SEQS_PER_STEP if nb % SAMPLE_SEQS_PER_STEP == 0 else 1
    rows_q = t_new * N_HEADS
    rows_i = t_new * IDX_HEADS
    n_rows = n_seq * t_new
    n_keys = past + page
    assert page == LANES and page >= MAX_DISTANCE and t_new <= page
    ck_t = jnp.transpose(cache_k, (0, 2, 3, 1))
    cv_t = jnp.transpose(cache_v, (0, 2, 3, 1))
    cki_t = jnp.transpose(cache_kidx, (0, 2, 1))
    seg = lambda col, width: proj_s[:, col:col + width]
    q3 = seg(COL_Q, N_HEADS * HEAD_DIM).reshape(nb, t_new, N_KV_HEADS, Q_PER_KV, HEAD_DIM)
    q3 = q3.transpose(0, 2, 1, 3, 4).reshape(nb, rows_q, HEAD_DIM)
    qi3 = seg(COL_QI, IDX_HEADS * IDX_DIM).reshape(nb, rows_i, IDX_DIM)
    w3 = seg(COL_SM + SM_WI, IDX_HEADS).reshape(nb, rows_i, 1)
    pad_rows = lambda a: jnp.pad(a.reshape(nb, t_new, -1), ((0, 0), (0, page - t_new), (0, 0)))
    kn3, vn3, kin3 = pad_rows(seg(COL_K, kvw)), pad_rows(seg(COL_V, kvw)), pad_rows(seg(COL_KI, IDX_DIM))
    row = np.arange(rows_q)
    row_tok = (row // Q_PER_KV) % t_new
    row_head = (row // (t_new * Q_PER_KV)) * Q_PER_KV + row % Q_PER_KV
    tab_t = jnp.pad(bias_table.T[row_head], ((0, 0), (0, LANES - N_BUCKETS)))
    kpos = np.concatenate([past - page + np.arange(page), past + np.arange(page)])[None, :]
    bucket = jnp.asarray(_t5_bucket_np(past + row_tok[:, None] - kpos))
    gsum = np.zeros((n_seq, n_rows, rows_i), np.float32)
    xpand = np.zeros((n_seq, rows_q, n_rows), np.float32)
    for s in range(n_seq):
        gsum[s, s * t_new + np.arange(rows_i) // IDX_HEADS, np.arange(rows_i)] = 1.0
        xpand[s, row, s * t_new + row_tok] = 1.0
    per_step = lambda shape: pl.BlockSpec((n_seq,) + shape, lambda b, pt: (b,) + tuple(0 for _ in shape))
    const = lambda shape: pl.BlockSpec(shape, lambda b, pt: tuple(0 for _ in shape))

    def page_specs(shape):
        return [pl.BlockSpec((1,) + shape, lambda b, pt, s=s, c=c: (pt[b * n_seq + s, c],) + tuple(0 for _ in shape))
                for s in range(n_seq) for c in range(n_pages)]

    grid_spec = pltpu.PrefetchScalarGridSpec(
        num_scalar_prefetch=1,
        grid=(nb // n_seq,),
        in_specs=[per_step((rows_q, HEAD_DIM)), per_step((rows_i, IDX_DIM)), per_step((rows_i, 1)),
                  per_step((page, kvw)), per_step((page, kvw)), per_step((page, IDX_DIM)),
                  const((rows_q, LANES)), const((rows_q, 2 * page)),
                  const((n_seq, n_rows, rows_i)), const((n_seq, rows_q, n_rows))]
                 + page_specs((N_KV_HEADS, HEAD_DIM, page)) + page_specs((N_KV_HEADS, HEAD_DIM, page))
                 + page_specs((IDX_DIM, page)),
        out_specs=pl.BlockSpec((n_seq, rows_q, HEAD_DIM), lambda b, pt: (b, 0, 0)),
        scratch_shapes=[
            pltpu.VMEM((n_rows, n_keys), F32),
            pltpu.VMEM((n_rows, n_keys), I32),
            pltpu.VMEM((n_rows, n_keys), BF16),
            pltpu.VMEM((rows_q, n_keys), F32),
            pltpu.VMEM((rows_q, n_keys), BF16),
            pltpu.VMEM((rows_q, 2 * page), F32),
            pltpu.VMEM((n_seq, IDX_DIM, past), F32),
            pltpu.VMEM((n_seq, N_KV_HEADS, HEAD_DIM, past), BF16),
            pltpu.VMEM((n_seq, N_KV_HEADS, HEAD_DIM, past), BF16),
        ],
    )
    n_pg = n_seq * n_pages
    out = pl.pallas_call(
        functools.partial(_attn_sample_body, n_seq=n_seq, n_pages=n_pages, page=page, t_new=t_new, topk=topk),
        grid_spec=grid_spec,
        out_shape=jax.ShapeDtypeStruct((nb, rows_q, HEAD_DIM), BF16),
        compiler_params=_cparams(("arbitrary",)),
        name="attn_sample",
    )(page_table, q3, qi3, w3, kn3, vn3, kin3, tab_t, bucket, jnp.asarray(gsum, BF16), jnp.asarray(xpand, BF16),
      *([ck_t] * n_pg), *([cv_t] * n_pg), *([cki_t] * n_pg))
    out = out.reshape(nb, N_KV_HEADS, t_new, Q_PER_KV, HEAD_DIM).transpose(0, 2, 1, 3, 4)
    return out.reshape(nb * t_new, N_HEADS * HEAD_DIM)


def _merge_body(ys_ref, ya_ref, g_ref, h_ref, wbs_ref, wba_ref, wo_ref, lg_ref, lb_ref, o_ref):
    a = _dot(ys_ref[...], wbs_ref[...])
    b = _dot(ya_ref[...], wba_ref[...])
    gates = jax.nn.sigmoid(g_ref[...])
    merged = gates[:, :D_MODEL] * a + gates[:, D_MODEL:] * b
    r = ALPHA * h_ref[...] + _dot(merged.astype(BF16), wo_ref[...])
    o_ref[...] = _layer_norm(r, lg_ref[...], lb_ref[...])


def merge_out(y_ssm, y_att, proj, h1, wbs, wba, wo, g, b, *, tm=512):
    m = h1.shape[0]
    tm = min(tm, m)
    assert m % tm == 0
    row = lambda width, colblk=0: pl.BlockSpec((tm, width), lambda i: (i, colblk))
    full = lambda shape: pl.BlockSpec(shape, lambda i: tuple(0 for _ in shape))
    return pl.pallas_call(
        _merge_body,
        grid=(m // tm,),
        in_specs=[row(D_INNER), row(N_HEADS * HEAD_DIM), row(2 * D_MODEL, COL_G // (2 * D_MODEL)), row(D_MODEL),
                  full(wbs.shape), full(wba.shape), full(wo.shape), full((1, D_MODEL)), full((1, D_MODEL))],
        out_specs=row(D_MODEL),
        out_shape=jax.ShapeDtypeStruct((m, D_MODEL), F32),
        compiler_params=_cparams(("parallel",)),
        name="merge_out",
    )(y_ssm, y_att, proj, h1, wbs, wba, wo, g, b)


def kernel(x_prompt, x_sample, cache_k, cache_v, cache_kidx, state_conv, state_ssm, page_table, bias_table, w_in, conv_w, conv_b, dt_bias, a_log, d_skip, ssm_norm_g, w_branch_ssm, w_branch_attn, w_out, ffn1_wi, ffn1_wo, ffn2_wi, ffn2_wo, ln1_g, ln1_b, ln2_g, ln2_b, ln3_g, ln3_b):
    assert w_in.shape[0] == DEPTH
    batch, seq, d = x_prompt.shape
    nb, t_new, _ = x_sample.shape
    xs = (x_prompt.reshape(batch * seq, d), x_sample.reshape(nb * t_new, d))
    outs_p, outs_s = [], []
    for i in range(DEPTH):
        w_pack = pack_w_in(w_in[i])
        f1i, f1o = ffn1_wi[i].astype(BF16), ffn1_wo[i].astype(BF16)
        f2i, f2o = ffn2_wi[i].astype(BF16), ffn2_wo[i].astype(BF16)
        wbs, wba, wo = w_branch_ssm[i].astype(BF16), w_branch_attn[i].astype(BF16), w_out[i].astype(BF16)
        row = lambda v: v.reshape(1, -1)
        pad_heads = lambda v: jnp.concatenate([v, jnp.zeros((LANES - SSM_HEADS,), v.dtype)]).reshape(1, LANES)
        dtb_pad, alog_pad = pad_heads(dt_bias[i]), pad_heads(a_log[i])
        dskip_e = jnp.repeat(d_skip[i], SSM_HEAD_DIM).reshape(1, D_INNER)
        ng = row(ssm_norm_g[i])
        cw, cb = conv_w[i], row(conv_b[i])

        h1, proj = [], []
        for x in xs:
            hf, hb = ffn_ln(x, f1i, f1o, row(ln1_g[i]), row(ln1_b[i]), emit_bf16=True)
            h1.append(hf)
            proj.append(matmul(hb, w_pack, tm=2048, tn=PROJ_TN))
        proj_p, proj_s = proj

        ys_p, conv_p, ssm_p = ssd_prompt(proj_p, batch, seq, cw, cb, dtb_pad, alog_pad, dskip_e, ng)
        ya_p, (kt_p, vt_p, kit_p) = attn_prompt(proj_p, bias_table, batch, seq)

        raw_s = jnp.concatenate([proj_s[:, COL_X:COL_X + D_INNER], proj_s[:, COL_BC:COL_BC + 2 * GN]], axis=1)
        u_cat = jnp.concatenate([state_conv[i], raw_s.reshape(nb, t_new, CONV_DIM)], axis=1)
        expand = np.zeros((LANES, D_INNER), np.float32)
        expand[np.arange(D_INNER) // SSM_HEAD_DIM, np.arange(D_INNER)] = 1.0
        ys_s, conv_s, ssm_s = ssd_sample(u_cat, proj_s.reshape(nb, t_new, PROJ_N),
                                         state_ssm[i].reshape(nb, SSM_HEADS * SSM_HEAD_DIM, SSM_STATE),
                                         cw, cb, dtb_pad, alog_pad, dskip_e, ng, jnp.asarray(expand, BF16))
        ya_s = attn_sample(proj_s, cache_k[i], cache_v[i], cache_kidx[i], page_table, bias_table, t_new)

        new_xs = []
        for hf, pj, ys, ya in ((h1[0], proj_p, ys_p, ya_p), (h1[1], proj_s, ys_s.reshape(nb * t_new, D_INNER), ya_s)):
            h2 = merge_out(ys, ya, pj, hf, wbs, wba, wo, row(ln2_g[i]), row(ln2_b[i]))
            new_xs.append(ffn_ln(h2, f2i, f2o, row(ln3_g[i]), row(ln3_b[i]), emit_bf16=False))
        xs = tuple(new_xs)

        kvw = N_KV_HEADS * HEAD_DIM
        shp = lambda nb_, l_: (nb_, l_, N_KV_HEADS, HEAD_DIM)
        head_major = lambda a: a.reshape(batch, N_KV_HEADS, HEAD_DIM, seq).transpose(0, 3, 1, 2)
        outs_p.append((head_major(kt_p), head_major(vt_p), kit_p.transpose(0, 2, 1),
                       conv_p, ssm_p.reshape(batch, SSM_HEADS, SSM_HEAD_DIM, SSM_STATE)))
        outs_s.append((proj_s[:, COL_K:COL_K + kvw].reshape(shp(nb, t_new)),
                       proj_s[:, COL_V:COL_V + kvw].reshape(shp(nb, t_new)),
                       proj_s[:, COL_KI:COL_KI + IDX_DIM].reshape(nb, t_new, IDX_DIM),
                       conv_s, ssm_s.reshape(nb, SSM_HEADS, SSM_HEAD_DIM, SSM_STATE)))
    k_p, v_p, kidx_p, conv_pp, ssm_pp = [jnp.stack(a) for a in zip(*outs_p)]
    k_s, v_s, kidx_s, conv_ss, ssm_ss = [jnp.stack(a) for a in zip(*outs_s)]
    return (xs[0].reshape(batch, seq, d), xs[1].reshape(nb, t_new, d),
            k_p, v_p, kidx_p, conv_pp, ssm_pp, k_s, v_s, kidx_s, conv_ss, ssm_ss)
```

```python
import functools
import math

import numpy as np
import jax
import jax.numpy as jnp
from jax import lax
from jax.experimental import pallas as pl
from jax.experimental.pallas import tpu as pltpu

F32 = jnp.float32
BF16 = jnp.bfloat16
I32 = jnp.int32

D_MODEL = 1024
D_INNER = 2 * D_MODEL
SSM_HEAD_DIM = 64
SSM_HEADS = D_INNER // SSM_HEAD_DIM
SSM_GROUPS = 4
SSM_STATE = 128
CONV_W = 4
CONV_DIM = D_INNER + 2 * SSM_GROUPS * SSM_STATE
SSD_CHUNK = 128
N_HEADS = 16
HEAD_DIM = 64
N_KV_HEADS = 4
Q_PER_KV = N_HEADS // N_KV_HEADS
IDX_HEADS = 8
IDX_DIM = 64
TOPK_MAX = 256
N_BUCKETS = 32
MAX_DISTANCE = 128
DEPTH = 1
ALPHA = (2 * DEPTH) ** 0.25
LN_EPS = 1e-5
IN_SPLITS = (D_INNER, CONV_DIM, SSM_HEADS, N_HEADS * HEAD_DIM, N_KV_HEADS * HEAD_DIM,
             N_KV_HEADS * HEAD_DIM, IDX_HEADS * IDX_DIM, IDX_DIM, IDX_HEADS, 2 * D_MODEL)

LANES = 128
SUBLANES = 8
VMEM_LIMIT_BYTES = 56 * 1024 * 1024

GN = SSM_GROUPS * SSM_STATE
COL_Z = 0
COL_X = COL_Z + D_INNER
COL_G = COL_X + D_INNER
COL_BC = COL_G + 2 * D_MODEL
COL_Q = COL_BC + 2 * GN
COL_K = COL_Q + N_HEADS * HEAD_DIM
COL_V = COL_K + N_KV_HEADS * HEAD_DIM
COL_QI = COL_V + N_KV_HEADS * HEAD_DIM
COL_SM = COL_QI + IDX_HEADS * IDX_DIM
COL_KI = COL_SM + LANES
PROJ_TN = 512
PROJ_N = -(-(COL_KI + LANES) // PROJ_TN) * PROJ_TN
SM_WI = SSM_HEADS

INT_MIN = -(2 ** 31)
LOG2E = math.log2(math.e)
NEG_INF = float("-inf")
SAMPLE_SEQS_PER_STEP = 4
SSD_SAMPLE_SEQS_PER_STEP = 4
RADIX_BITS = 4


def _cparams(sem):
    return pltpu.CompilerParams(dimension_semantics=sem, vmem_limit_bytes=VMEM_LIMIT_BYTES)


def _layer_norm(r, g, b):
    mu = jnp.mean(r, axis=-1, keepdims=True)
    d = r - mu
    var = jnp.mean(d * d, axis=-1, keepdims=True)
    return d * lax.rsqrt(var + LN_EPS) * g + b


def _silu(x):
    h = 0.5 * x
    return h + h * jnp.tanh(h)


def _softplus(x):
    return jnp.maximum(x, 0.0) + jnp.log1p(jnp.exp(-jnp.abs(x)))


def _dot(a, b):
    return jnp.dot(a, b, preferred_element_type=F32)


def _dot_nt(a, b):
    return lax.dot_general(a, b, (((1,), (1,)), ((), ())), preferred_element_type=F32)


def _dot_tn(a, b):
    return lax.dot_general(a, b, (((0,), (0,)), ((), ())), preferred_element_type=F32)


def _split3(v):
    hi = v.astype(BF16)
    r1 = v - hi.astype(F32)
    mid = r1.astype(BF16)
    lo = (r1 - mid.astype(F32)).astype(BF16)
    return hi, mid, lo


def _dot_exact_sel(v, sel_bf16):
    hi, mid, lo = _split3(v)
    return (_dot(hi, sel_bf16) + _dot(mid, sel_bf16)) + _dot(lo, sel_bf16)


def _sel_dot_exact(sel_bf16, v):
    hi, mid, lo = _split3(v)
    return (_dot(sel_bf16, hi) + _dot(sel_bf16, mid)) + _dot(sel_bf16, lo)


def _hi_lo(a):
    ah = a.astype(BF16)
    return ah, (a - ah.astype(F32)).astype(BF16)


def _col_reduce(x, op2, op):
    parts = []
    for r0 in range(0, x.shape[0], LANES):
        y = x[r0:r0 + LANES]
        n = y.shape[0]
        while n > SUBLANES and n % (2 * SUBLANES) == 0:
            n //= 2
            y = op2(y[:n], y[n:])
        parts.append(y)
    while len(parts) > 1:
        parts = [op2(parts[i], parts[i + 1]) if i + 1 < len(parts) else parts[i] for i in range(0, len(parts), 2)]
    return op(parts[0], axis=0, keepdims=True)


def _col_sum(x):
    return _col_reduce(x, jnp.add, jnp.sum)


def _col_max(x):
    return _col_reduce(x, jnp.maximum, jnp.max)


def _ffn_ln_body(x_ref, wi_ref, wo_ref, g_ref, b_ref, o_ref, *maybe_ob_ref, tf):
    x = x_ref[...]
    xb = x.astype(BF16)
    dff = wo_ref.shape[0]
    acc = None
    for c0 in range(0, dff, tf):
        gate = _dot(xb, wi_ref[:, c0:c0 + tf])
        up = _dot(xb, wi_ref[:, dff + c0:dff + c0 + tf])
        part = _dot((_silu(gate) * up).astype(BF16), wo_ref[c0:c0 + tf, :])
        acc = part if acc is None else acc + part
    y = _layer_norm(ALPHA * x + 0.5 * acc, g_ref[...], b_ref[...])
    o_ref[...] = y
    for ob_ref in maybe_ob_ref:
        ob_ref[...] = y.astype(BF16)


def ffn_ln(x, wi_bf, wo_bf, g, b, *, emit_bf16, tm=512, tf=1408):
    m, d = x.shape
    dff = wo_bf.shape[0]
    tm = min(tm, m)
    assert m % tm == 0 and dff % tf == 0
    row = pl.BlockSpec((tm, d), lambda i: (i, 0))
    resident = lambda shape: pl.BlockSpec(shape, lambda i: (0, 0), pipeline_mode=pl.Buffered(1))
    n_out = 2 if emit_bf16 else 1
    res = pl.pallas_call(
        functools.partial(_ffn_ln_body, tf=tf),
        grid=(m // tm,),
        in_specs=[row, resident(wi_bf.shape), resident(wo_bf.shape), resident((1, d)), resident((1, d))],
        out_specs=[row] * n_out,
        out_shape=[jax.ShapeDtypeStruct((m, d), F32), jax.ShapeDtypeStruct((m, d), BF16)][:n_out],
        compiler_params=_cparams(("parallel",)),
        name="ffn_ln",
    )(x, wi_bf, wo_bf, g, b)
    return res if emit_bf16 else res[0]


def _matmul_body(x_ref, w_ref, o_ref):
    o_ref[...] = _dot(x_ref[...], w_ref[...])


def matmul(x_bf, w_bf, *, tm, tn):
    m, k = x_bf.shape
    n = w_bf.shape[1]
    tm = min(tm, m)
    assert m % tm == 0 and n % tn == 0
    return pl.pallas_call(
        _matmul_body,
        grid=(m // tm, n // tn),
        in_specs=[pl.BlockSpec((tm, k), lambda i, j: (i, 0)),
                  pl.BlockSpec((k, tn), lambda i, j: (0, j))],
        out_specs=pl.BlockSpec((tm, tn), lambda i, j: (i, j)),
        out_shape=jax.ShapeDtypeStruct((m, n), F32),
        compiler_params=_cparams(("parallel", "arbitrary")),
        name="in_proj",
    )(x_bf, w_bf)


def pack_w_in(w_in):
    offs = np.cumsum((0,) + IN_SPLITS)
    z, xbc, dt, q, k, v, qi, ki, wi, gates = [w_in[:, offs[i]:offs[i + 1]] for i in range(len(IN_SPLITS))]
    d = w_in.shape[0]
    zeros = lambda n: jnp.zeros((d, n), w_in.dtype)
    small = jnp.concatenate([dt, wi, zeros(LANES - SSM_HEADS - IDX_HEADS)], axis=1)
    kib = jnp.concatenate([ki, ki], axis=1)
    packed = jnp.concatenate([z, xbc[:, :D_INNER], gates, xbc[:, D_INNER:], q, k, v, qi, small, kib,
                              zeros(PROJ_N - COL_KI - LANES)], axis=1)
    return packed.astype(BF16)


def _ssd_prompt_body(z_ref, xr_ref, bc_ref, sm_ref, cwx_ref, cwbc_ref, cbx_ref, cbbc_ref,
                     dtb_ref, alog_ref, dskip_ref, ng_ref,
                     y_ref, cs_ref, hs_ref,
                     xbuf, bcbuf, h_scr, xc_scr, bcc_scr, y_scr):
    c = pl.program_id(1)
    nc = pl.num_programs(1)
    L = SSD_CHUNK
    halo = SUBLANES
    n_pairs = SSM_HEADS // 2
    pairs_per_group = n_pairs // SSM_GROUPS

    @pl.when(c == 0)
    def _():
        xbuf[...] = jnp.zeros_like(xbuf)
        bcbuf[...] = jnp.zeros_like(bcbuf)
        h_scr[...] = jnp.zeros_like(h_scr)

    assert CONV_W == 4

    def conv(halo_ref, x_ref, w_ref, b_ref):
        full = jnp.concatenate([halo_ref[...], x_ref[...]], axis=0)
        prev = pltpu.roll(full, 1, axis=0)
        late = (w_ref[1:2, :] * full + w_ref[0:1, :] * prev)
        early = w_ref[3:4, :] * full[halo:] + w_ref[2:3, :] * prev[halo:]
        acc = (b_ref[...] + early) + pltpu.roll(late, 2, axis=0)[halo:]
        halo_ref[...] = x_ref[L - halo:L, :]
        return _silu(acc)

    xc_scr[...] = conv(xbuf, xr_ref, cwx_ref, cbx_ref)
    bcc_scr[...] = conv(bcbuf, bc_ref, cwbc_ref, cbbc_ref)

    dt = _softplus(sm_ref[...] + dtb_ref[...])
    a = -jnp.exp(alog_ref[...])
    row = lax.broadcasted_iota(I32, (L, L), 0)
    col = lax.broadcasted_iota(I32, (L, L), 1)
    tril = (row >= col).astype(F32)
    cum = jnp.dot(tril, dt * a, precision=lax.Precision.HIGHEST, preferred_element_type=F32)
    w = jnp.exp(cum[L - 1:L, :] - cum) * dt
    cum_t = cum.T
    dt_t = dt.T
    causal = row >= col
    lane = lax.broadcasted_iota(I32, (L, LANES), 1)
    lo_half = lane < SSM_HEAD_DIM
    sub = lax.broadcasted_iota(I32, (2 * SSM_HEAD_DIM, SSM_STATE), 0)
    lo_rows = sub < SSM_HEAD_DIM

    for i in range(n_pairs):
        g = i // pairs_per_group
        h0, h1 = 2 * i, 2 * i + 1
        sl = slice(i * LANES, (i + 1) * LANES)
        if i % pairs_per_group == 0:
            bg = bcc_scr[:, g * SSM_STATE:(g + 1) * SSM_STATE].astype(BF16)
            cg = bcc_scr[:, GN + g * SSM_STATE:GN + (g + 1) * SSM_STATE].astype(BF16)
            cb = _dot_nt(cg, bg)
        xp = xc_scr[:, sl]
        c0 = jnp.broadcast_to(cum[:, h0:h0 + 1], (L, LANES))
        c1 = jnp.broadcast_to(cum[:, h1:h1 + 1], (L, LANES))
        w0 = jnp.broadcast_to(w[:, h0:h0 + 1], (L, LANES))
        w1 = jnp.broadcast_to(w[:, h1:h1 + 1], (L, LANES))

        def mix(ct, h):
            seg = jnp.exp(jnp.where(causal, ct - cum_t[h:h + 1, :], NEG_INF))
            return (cb * seg * dt_t[h:h + 1, :]).astype(BF16)

        zero = jnp.zeros_like(xp)
        yd = (_dot(mix(c0, h0), jnp.where(lo_half, xp, zero).astype(BF16))
              + _dot(mix(c1, h1), jnp.where(lo_half, zero, xp).astype(BF16)))
        hp = h_scr[i]
        yo = _dot_nt(cg, hp.astype(BF16)) * jnp.where(lo_half, jnp.exp(c0), jnp.exp(c1))
        y_scr[:, sl] = yd + yo + dskip_ref[:, sl] * xp
        xw = (xp * jnp.where(lo_half, w0, w1)).astype(BF16)
        st = _dot_tn(xw, bg)
        dec = jnp.where(lo_rows, jnp.exp(c0[L - 1:L, :]), jnp.exp(c1[L - 1:L, :]))
        h_scr[i] = dec * hp + st

    gw = D_INNER // SSM_GROUPS
    for g in range(SSM_GROUPS):
        sl = slice(g * gw, (g + 1) * gw)
        u = y_scr[:, sl] * _silu(z_ref[:, sl])
        ms = jnp.mean(u * u, axis=-1, keepdims=True)
        y_ref[:, sl] = (u * lax.rsqrt(ms + LN_EPS) * ng_ref[:, sl]).astype(y_ref.dtype)

    @pl.when(c == nc - 1)
    def _():
        cs_ref[0, :, 0:D_INNER] = xr_ref[L - (CONV_W - 1):L, :]
        cs_ref[0, :, D_INNER:CONV_DIM] = bc_ref[L - (CONV_W - 1):L, :]
        for i in range(n_pairs):
            hs_ref[0, i * LANES:(i + 1) * LANES, :] = h_scr[i]


def ssd_prompt(proj, batch, seq, conv_w, conv_b, dtb_pad, alog_pad, dskip_e, norm_g):
    L = SSD_CHUNK
    nc = seq // L
    assert seq % L == 0
    blk = lambda width, colblk: pl.BlockSpec((L, width), lambda b, c: (b * nc + c, colblk))
    full = lambda shape: pl.BlockSpec(shape, lambda b, c: tuple(0 for _ in shape))
    n_pairs = SSM_HEADS // 2
    return pl.pallas_call(
        _ssd_prompt_body,
        grid=(batch, nc),
        in_specs=[
            blk(D_INNER, COL_Z // D_INNER),
            blk(D_INNER, COL_X // D_INNER),
            blk(2 * GN, COL_BC // (2 * GN)),
            blk(LANES, COL_SM // LANES),
            full((CONV_W, D_INNER)), full((CONV_W, 2 * GN)), full((1, D_INNER)), full((1, 2 * GN)),
            full((1, LANES)), full((1, LANES)), full((1, D_INNER)), full((1, D_INNER)),
        ],
        out_specs=[
            pl.BlockSpec((L, D_INNER), lambda b, c: (b * nc + c, 0)),
            pl.BlockSpec((1, CONV_W - 1, CONV_DIM), lambda b, c: (b, 0, 0)),
            pl.BlockSpec((1, SSM_HEADS * SSM_HEAD_DIM, SSM_STATE), lambda b, c: (b, 0, 0)),
        ],
        out_shape=[
            jax.ShapeDtypeStruct((batch * seq, D_INNER), BF16),
            jax.ShapeDtypeStruct((batch, CONV_W - 1, CONV_DIM), F32),
            jax.ShapeDtypeStruct((batch, SSM_HEADS * SSM_HEAD_DIM, SSM_STATE), F32),
        ],
        scratch_shapes=[
            pltpu.VMEM((SUBLANES, D_INNER), F32),
            pltpu.VMEM((SUBLANES, 2 * GN), F32),
            pltpu.VMEM((n_pairs, 2 * SSM_HEAD_DIM, SSM_STATE), F32),
            pltpu.VMEM((L, D_INNER), F32),
            pltpu.VMEM((L, 2 * GN), F32),
            pltpu.VMEM((L, D_INNER), F32),
        ],
        compiler_params=_cparams(("parallel", "arbitrary")),
        name="ssd_prompt",
    )(proj, proj, proj, proj, conv_w[:, :D_INNER], conv_w[:, D_INNER:], conv_b[:, :D_INNER], conv_b[:, D_INNER:],
      dtb_pad, alog_pad, dskip_e, norm_g)


def _ssd_sample_body(u_ref, z_ref, sm_ref, h0_ref, cw_ref, cb_ref, dtb_ref, alog_ref, dskip_ref, ng_ref, exp_ref,
                     y_ref, cs_ref, hn_ref, *, t_new, n_seq):
    for s in range(n_seq):
        _ssd_sample_one(s, u_ref, z_ref, sm_ref, h0_ref, cw_ref, cb_ref, dtb_ref, alog_ref, dskip_ref, ng_ref, exp_ref,
                        y_ref, cs_ref, hn_ref, t_new)


def _ssd_sample_one(s, u_ref, z_ref, sm_ref, h0_ref, cw_ref, cb_ref, dtb_ref, alog_ref, dskip_ref, ng_ref, exp_ref,
                    y_ref, cs_ref, hn_ref, t_new):
    T = t_new
    n_pairs = SSM_HEADS // 2
    pairs_per_group = n_pairs // SSM_GROUPS
    acc = cb_ref[...] + cw_ref[0:1, :] * u_ref[s, 0:T, :]
    for k in range(1, CONV_W):
        acc = acc + cw_ref[k:k + 1, :] * u_ref[s, k:k + T, :]
    xbc = _silu(acc)
    cs_ref[s] = u_ref[s, T:T + CONV_W - 1, :]
    x = xbc[:, :D_INNER]

    dt = _softplus(sm_ref[s] + dtb_ref[...])
    da = dt * (-jnp.exp(alog_ref[...]))
    rows = [da[0:1, :]]
    for t in range(1, T):
        rows.append(rows[-1] + da[t:t + 1, :])
    cum = jnp.concatenate(rows, axis=0)
    both = _dot_exact_sel(jnp.concatenate([cum, dt], axis=0), exp_ref[...])
    cum_e = both[0:T, :]
    dt_e = both[T:2 * T, :]
    t_idx = lax.broadcasted_iota(I32, (T, GN), 0)

    gw = D_INNER // SSM_GROUPS
    y_parts = []
    for g in range(SSM_GROUPS):
        sl = slice(g * gw, (g + 1) * gw)
        bg = xbc[:, D_INNER + g * SSM_STATE:D_INNER + (g + 1) * SSM_STATE]
        cg = xbc[:, D_INNER + GN + g * SSM_STATE:D_INNER + GN + (g + 1) * SSM_STATE]
        cbg = _dot_nt(cg.astype(BF16), bg.astype(BF16))
        yg = dskip_ref[:, sl] * x[:, sl]
        for s_tok in range(T):
            seg = jnp.exp(jnp.where(t_idx >= s_tok, cum_e[:, sl] - cum_e[s_tok:s_tok + 1, sl], NEG_INF))
            coef = seg * dt_e[s_tok:s_tok + 1, sl] * jnp.broadcast_to(cbg[:, s_tok:s_tok + 1], (T, gw))
            yg = yg + coef * x[s_tok:s_tok + 1, sl]
        y_parts.append(yg)

    xw = x * dt_e * jnp.exp(cum_e[T - 1:T, :] - cum_e)
    ecum = jnp.exp(cum_e)
    sub = lax.broadcasted_iota(I32, (2 * SSM_HEAD_DIM, SSM_STATE), 0)
    lo_rows = sub < SSM_HEAD_DIM
    yo_parts = []
    for i in range(n_pairs):
        g = i // pairs_per_group
        sl = slice(i * LANES, (i + 1) * LANES)
        bg = xbc[:, D_INNER + g * SSM_STATE:D_INNER + (g + 1) * SSM_STATE].astype(BF16)
        cg = xbc[:, D_INNER + GN + g * SSM_STATE:D_INNER + GN + (g + 1) * SSM_STATE].astype(BF16)
        hp = h0_ref[s, i * LANES:(i + 1) * LANES, :]
        yo_parts.append(_dot_nt(cg, hp.astype(BF16)) * ecum[:, sl])
        st = _dot_tn(xw[:, sl].astype(BF16), bg)
        d0 = jnp.exp(cum[T - 1:T, 2 * i:2 * i + 1])
        d1 = jnp.exp(cum[T - 1:T, 2 * i + 1:2 * i + 2])
        dec = jnp.where(lo_rows, jnp.broadcast_to(d0, sub.shape), jnp.broadcast_to(d1, sub.shape))
        hn_ref[s, i * LANES:(i + 1) * LANES, :] = dec * hp + st

    for g in range(SSM_GROUPS):
        sl = slice(g * gw, (g + 1) * gw)
        yo = jnp.concatenate(yo_parts[g * pairs_per_group:(g + 1) * pairs_per_group], axis=1)
        u = (y_parts[g] + yo) * _silu(z_ref[s, :, sl])
        ms = jnp.mean(u * u, axis=-1, keepdims=True)
        y_ref[s, :, sl] = (u * lax.rsqrt(ms + LN_EPS) * ng_ref[:, sl]).astype(y_ref.dtype)


def ssd_sample(u_cat, proj3, h0, conv_w, conv_b, dtb_pad, alog_pad, dskip_e, norm_g, expand):
    nb, t_new, _ = proj3.shape
    full = lambda shape: pl.BlockSpec(shape, lambda b: tuple(0 for _ in shape))
    hp = SSM_HEADS * SSM_HEAD_DIM
    n_seq = SSD_SAMPLE_SEQS_PER_STEP if nb % SSD_SAMPLE_SEQS_PER_STEP == 0 else 1
    return pl.pallas_call(
        functools.partial(_ssd_sample_body, t_new=t_new, n_seq=n_seq),
        grid=(nb // n_seq,),
        in_specs=[
            pl.BlockSpec((n_seq, t_new + CONV_W - 1, CONV_DIM), lambda b: (b, 0, 0)),
            pl.BlockSpec((n_seq, t_new, D_INNER), lambda b: (b, 0, COL_Z // D_INNER)),
            pl.BlockSpec((n_seq, t_new, LANES), lambda b: (b, 0, COL_SM // LANES)),
            pl.BlockSpec((n_seq, hp, SSM_STATE), lambda b: (b, 0, 0)),
            full((CONV_W, CONV_DIM)), full((1, CONV_DIM)), full((1, LANES)), full((1, LANES)),
            full((1, D_INNER)), full((1, D_INNER)), full((LANES, D_INNER)),
        ],
        out_specs=[
            pl.BlockSpec((n_seq, t_new, D_INNER), lambda b: (b, 0, 0)),
            pl.BlockSpec((n_seq, CONV_W - 1, CONV_DIM), lambda b: (b, 0, 0)),
            pl.BlockSpec((n_seq, hp, SSM_STATE), lambda b: (b, 0, 0)),
        ],
        out_shape=[
            jax.ShapeDtypeStruct((nb, t_new, D_INNER), BF16),
            jax.ShapeDtypeStruct((nb, CONV_W - 1, CONV_DIM), F32),
            jax.ShapeDtypeStruct((nb, hp, SSM_STATE), F32),
        ],
        compiler_params=_cparams(("parallel",)),
        name="ssd_sample",
    )(u_cat, proj3, proj3, h0, conv_w, conv_b, dtb_pad, alog_pad, dskip_e, norm_g, expand)


def _sortable_key(score):
    score = jnp.where(score == 0.0, 0.0, score)
    bits = lax.bitcast_convert_type(score, I32)
    return bits ^ ((bits >> 31) & jnp.int32(0x7FFFFFFF))


def _t5_bucket_np(rel):
    n = np.maximum(rel, 0)
    max_exact = N_BUCKETS // 2
    nf = np.maximum(n, 1).astype(np.float32)
    large = max_exact + (np.log(nf / max_exact) / math.log(MAX_DISTANCE / max_exact)
                         * (N_BUCKETS - max_exact)).astype(np.int32)
    large = np.minimum(large, N_BUCKETS - 1)
    return np.where(n < max_exact, n, large).astype(np.int32)


def _count_cols_i16(ref, n_rows, pred):
    packed_rows = 2 * SUBLANES
    parts = []
    for r0 in range(0, n_rows, LANES):
        x = jnp.where(pred(ref[r0:r0 + LANES, :]), jnp.int16(1), jnp.int16(0))
        n = x.shape[0]
        while n > packed_rows:
            n //= 2
            x = x[:n] + x[n:]
        parts.append(x)
    while len(parts) > 1:
        parts = [parts[i] + parts[i + 1] if i + 1 < len(parts) else parts[i] for i in range(0, len(parts), 2)]
    return jnp.sum(parts[0].astype(I32), axis=0, keepdims=True)


def _kth_largest_i16(ref, n_rows, k):
    i16_min = -(2 ** 15)

    def try_cand(r, cand):
        cnt = _count_cols_i16(ref, n_rows, lambda v: v >= cand.astype(jnp.int16))
        return jnp.where(cnt >= k, cand, r)

    r0 = try_cand(jnp.full((1, LANES), i16_min, I32), jnp.zeros((1, LANES), I32))
    return lax.fori_loop(0, 15, lambda it, r: try_cand(r, r + (jnp.int32(1) << (14 - it))), r0)


def _kth_largest_cols(key_ref, hi_ref, lo_ref, n_rows, k):
    i16_min = -(2 ** 15)
    for r0 in range(0, n_rows, LANES):
        key = key_ref[r0:r0 + LANES, :]
        hi_ref[r0:r0 + LANES, :] = (key >> 16).astype(jnp.int16)
        lo_ref[r0:r0 + LANES, :] = ((key & 0xFFFF) + i16_min).astype(jnp.int16)
    t_hi = _kth_largest_i16(hi_ref, n_rows, k)
    t_hi16 = t_hi.astype(jnp.int16)
    k_lo = k - _count_cols_i16(hi_ref, n_rows, lambda v: v > t_hi16)
    for r0 in range(0, n_rows, LANES):
        sl = slice(r0, r0 + LANES)
        lo_ref[sl, :] = jnp.where(hi_ref[sl, :] == t_hi16, lo_ref[sl, :], jnp.int16(i16_min))
    t_lo = _kth_largest_i16(lo_ref, n_rows, k_lo)
    return (t_hi << 16) | ((t_lo - i16_min) & 0xFFFF)


def _topk_mask_cols(key_ref, hi_ref, lo_ref, msk_ref, n_rows, k, valid_fn):
    thr = _kth_largest_cols(key_ref, hi_ref, lo_ref, n_rows, k)
    n_gt = _col_sum(jnp.where(key_ref[0:n_rows, :] > thr, 1.0, 0.0))
    need = k - n_gt
    run = jnp.zeros((1, LANES), F32)
    ri = lax.broadcasted_iota(I32, (LANES, LANES), 0)
    ci = lax.broadcasted_iota(I32, (LANES, LANES), 1)
    strict = (ri > ci).astype(BF16)
    for r0 in range(0, n_rows, LANES):
        kc = key_ref[r0:r0 + LANES, :]
        eq = kc == thr
        eqf = eq.astype(F32)
        before = _dot(strict, eqf.astype(BF16)) + run
        sel = ((kc > thr) | (eq & (before < need))) & valid_fn(r0, LANES)
        msk_ref[r0:r0 + LANES, :] = jnp.where(sel, 0.0, NEG_INF)
        run = run + jnp.sum(eqf, axis=0, keepdims=True)


def _kth_largest_rows(key_ref, k):
    rows = key_ref.shape[0]
    digits = 2 ** RADIX_BITS
    n_rounds = 32 // RADIX_BITS

    def body(rnd, r):
        shift = (32 - RADIX_BITS) - RADIX_BITS * rnd
        n_ok = jnp.zeros((rows, 1), I32)
        for i in range(1, digits):
            cand = r + (jnp.int32(i) << shift)
            cnt = jnp.sum((key_ref[...] >= cand).astype(I32), axis=1, keepdims=True)
            n_ok = n_ok + (cnt >= k).astype(I32)
        return r + (n_ok << shift)

    return lax.fori_loop(0, n_rounds, body, jnp.full((rows, 1), INT_MIN, I32))


def _topk_select_rows(key_ref, sel_ref, k, valid_fn):
    n_keys = key_ref.shape[1]
    thr = _kth_largest_rows(key_ref, k)
    n_gt = jnp.sum((key_ref[...] > thr).astype(I32), axis=1, keepdims=True)
    need = (k - n_gt).astype(F32)
    run = jnp.zeros((key_ref.shape[0], 1), F32)
    ri = lax.broadcasted_iota(I32, (LANES, LANES), 0)
    ci = lax.broadcasted_iota(I32, (LANES, LANES), 1)
    strict = (ri < ci).astype(BF16)
    for l0 in range(0, n_keys, LANES):
        kc = key_ref[:, l0:l0 + LANES]
        eq = kc == thr
        eqf = eq.astype(F32)
        before = _dot(eqf.astype(BF16), strict) + run
        sel = ((kc > thr) | (eq & (before < need))) & valid_fn(l0)
        sel_ref[:, l0:l0 + LANES] = sel.astype(F32).astype(BF16)
        run = run + jnp.sum(eqf, axis=1, keepdims=True)


def _bias_tiles_body(tab_ref, bucket_ref, o_ref):
    for d in range(bucket_ref.shape[0]):
        bk = bucket_ref[d]
        for h in range(N_HEADS):
            acc = jnp.zeros(bk.shape, F32)
            for bb in range(N_BUCKETS):
                acc = jnp.where(bk == bb, tab_ref[bb, h], acc)
            o_ref[h, d] = (acc - tab_ref[N_BUCKETS - 1, h]) * LOG2E


def _attn_prompt_body(bias_ref, q_ref, k_ref, v_ref, qi_ref, ki_ref, sm_ref, *rest, j, topk, emit_transposed):
    key_scr, hi_scr, lo_scr, msk_scr, ot_scr, ka_scr = rest[-6:]
    QB = LANES
    nk = (j + 1) * QB
    if emit_transposed:
        k32_ref, v32_ref, o_ref, kt_ref, vt_ref, kit_ref = rest[-12:-6]
        kt_ref[0] = k32_ref[...].T
        vt_ref[0] = v32_ref[...].T
        kit_ref[0] = ki_ref[...].T[0:IDX_DIM, :]
    else:
        o_ref = rest[-7]

    def valid(r0, rows):
        kp = r0 + lax.broadcasted_iota(I32, (rows, QB), 0)
        qp = j * QB + lax.broadcasted_iota(I32, (rows, QB), 1)
        return kp <= qp

    if nk <= topk:
        msk_scr[...] = jnp.where(valid(0, nk), 0.0, NEG_INF)
    else:
        x = ki_ref[0:nk, :]
        hi, lo = _hi_lo(x)
        first = lax.broadcasted_iota(I32, (nk, LANES), 1) < IDX_DIM
        ka_scr[:, 0:LANES] = jnp.where(first, hi, lo)
        ka_scr[:, LANES:2 * LANES] = jnp.where(first, hi, jnp.zeros_like(hi))

        wt = sm_ref[...].T
        ka = ka_scr[...]
        score = jnp.zeros((nk, QB), F32)
        pad = jnp.zeros((QB, 2 * LANES - 3 * IDX_DIM), BF16)
        for h2 in range(IDX_HEADS // 2):
            blocks = []
            for h in (2 * h2, 2 * h2 + 1):
                qh, ql = _hi_lo(qi_ref[:, h * IDX_DIM:(h + 1) * IDX_DIM] * (IDX_DIM ** -0.5))
                blocks.append(jnp.concatenate([ql, qh, qh, pad], axis=1))
            s = _dot_nt(ka, jnp.concatenate(blocks, axis=0))
            for i, h in enumerate((2 * h2, 2 * h2 + 1)):
                score = score + jnp.maximum(s[:, i * QB:(i + 1) * QB], 0.0) * wt[SM_WI + h:SM_WI + h + 1, :]
        score = jnp.where(valid(0, nk), score * (IDX_HEADS ** -0.5), NEG_INF)
        key_scr[...] = _sortable_key(score)
        _topk_mask_cols(key_scr, hi_scr, lo_scr, msk_scr, nk, topk, valid)

    near0 = max(j - 1, 0) * QB
    for g in range(N_KV_HEADS):
        kcols = slice(g * HEAD_DIM, (g + 1) * HEAD_DIM)
        vcols = slice(g * LANES, (g + 1) * LANES)
        heads = range(g * Q_PER_KV, (g + 1) * Q_PER_KV)
        qg = jnp.concatenate([(q_ref[:, h * HEAD_DIM:(h + 1) * HEAD_DIM] * (HEAD_DIM ** -0.5 * LOG2E)).astype(BF16)
                              for h in heads], axis=0)
        lg_all = _dot_nt(k_ref[0:nk, kcols], qg)
        probs = []
        for r, h in enumerate(heads):
            lg = lg_all[:, r * QB:(r + 1) * QB] + msk_scr[...]
            bias = bias_ref[h, 0] if j == 0 else jnp.concatenate([bias_ref[h, 1], bias_ref[h, 0]], axis=0)
            lg = jnp.concatenate([lg[:near0], lg[near0:] + bias], axis=0) if near0 else lg + bias
            probs.append(jnp.exp2(lg - _col_max(lg)).astype(BF16))
        pv = _dot_tn(v_ref[0:nk, vcols], jnp.concatenate(probs, axis=1))
        for r, h in enumerate(heads):
            sl = slice(r * QB, (r + 1) * QB)
            ot_scr[h * HEAD_DIM:(h + 1) * HEAD_DIM, :] = pv[0:HEAD_DIM, sl] * (1.0 / pv[HEAD_DIM:HEAD_DIM + 1, sl])
    o_ref[...] = ot_scr[...].T.astype(o_ref.dtype)


def attn_prompt(proj, bias_table, batch, seq):
    QB = LANES
    nq = seq // QB
    assert seq % QB == 0
    topk = min(TOPK_MAX, seq // 4)
    ts = np.arange(QB)[:, None]
    tq = np.arange(QB)[None, :]
    bucket = jnp.asarray(np.stack([_t5_bucket_np(tq - ts), _t5_bucket_np(QB + tq - ts)]))
    kvw = N_KV_HEADS * HEAD_DIM
    kv32 = lax.optimization_barrier(proj[:, COL_K:COL_K + 2 * kvw])
    k_bf = kv32[:, :kvw].astype(BF16)
    v4 = kv32[:, kvw:].astype(BF16).reshape(batch * seq, N_KV_HEADS, HEAD_DIM)
    v_aug = jnp.concatenate([v4, jnp.ones(v4.shape[:2] + (1,), BF16),
                             jnp.zeros(v4.shape[:2] + (LANES - HEAD_DIM - 1,), BF16)], axis=-1)
    v_aug = v_aug.reshape(batch * seq, N_KV_HEADS * LANES)
    bias = pl.pallas_call(
        _bias_tiles_body,
        in_specs=[pl.BlockSpec(memory_space=pltpu.SMEM), pl.BlockSpec(memory_space=pltpu.VMEM)],
        out_specs=pl.BlockSpec(memory_space=pltpu.VMEM),
        out_shape=jax.ShapeDtypeStruct((N_HEADS, 2, QB, QB), F32),
        name="bias_tiles",
    )(bias_table, bucket)
    out = None
    for j in range(nq):
        nk = (j + 1) * QB
        rows = nk if seq % nk == 0 else seq
        rowblk = lambda width, colblk: pl.BlockSpec((QB, width), lambda b: (b * nq + j, colblk))
        keyblk = lambda width, colblk: pl.BlockSpec((rows, width), lambda b: (b * (seq // rows), colblk))
        prev = [] if out is None else [out]
        last = j == nq - 1
        extra_in = [keyblk(kvw, COL_K // kvw), keyblk(kvw, COL_V // kvw)] if last else []
        tblk = lambda r: pl.BlockSpec((1, r, seq), lambda b: (b, 0, 0))
        tshape = lambda r: jax.ShapeDtypeStruct((batch, r, seq), F32)
        res = pl.pallas_call(
            functools.partial(_attn_prompt_body, j=j, topk=topk, emit_transposed=last),
            grid=(batch,),
            in_specs=[
                pl.BlockSpec((N_HEADS, 2, QB, QB), lambda b: (0, 0, 0, 0)),
                rowblk(N_HEADS * HEAD_DIM, COL_Q // (N_HEADS * HEAD_DIM)),
                keyblk(kvw, 0),
                keyblk(N_KV_HEADS * LANES, 0),
                rowblk(IDX_HEADS * IDX_DIM, COL_QI // (IDX_HEADS * IDX_DIM)),
                keyblk(LANES, COL_KI // LANES),
                rowblk(LANES, COL_SM // LANES),
            ] + [pl.BlockSpec(memory_space=pl.ANY) for _ in prev] + extra_in,
            out_specs=[pl.BlockSpec((QB, N_HEADS * HEAD_DIM), lambda b: (b * nq + j, 0))]
                      + ([tblk(kvw), tblk(kvw), tblk(IDX_DIM)] if last else []),
            out_shape=[jax.ShapeDtypeStruct((batch * seq, N_HEADS * HEAD_DIM), BF16)]
                      + ([tshape(kvw), tshape(kvw), tshape(IDX_DIM)] if last else []),
            input_output_aliases={7: 0} if prev else {},
            scratch_shapes=[
                pltpu.VMEM((nk, QB), I32),
                pltpu.VMEM((nk, QB), jnp.int16),
                pltpu.VMEM((nk, QB), jnp.int16),
                pltpu.VMEM((nk, QB), F32),
                pltpu.VMEM((N_HEADS * HEAD_DIM, QB), F32),
                pltpu.VMEM((nk, 2 * LANES), BF16),
            ],
            compiler_params=_cparams(("parallel",)),
            name=f"attn_prompt_q{j}",
        )(bias, proj, k_bf, v_aug, proj, proj, proj, *prev, *([proj, proj] if last else []))
        out = res[0]
    return out, res[1:]


def _attn_sample_body(pt_ref, q_ref, qi_ref, w_ref, kn_ref, vn_ref, kin_ref, tabt_ref, bucket_ref, gsum_ref, xpand_ref,
                      *rest, n_seq, n_pages, page, t_new, topk):
    n_pg = n_seq * n_pages
    ck = rest[0:n_pg]
    cv = rest[n_pg:2 * n_pg]
    cki = rest[2 * n_pg:3 * n_pg]
    o_ref, sc_scr, key_scr, sel_scr, lg_scr, p_scr, bias_scr, ki_st, k_st, v_st = rest[3 * n_pg:]
    past = n_pages * page
    n_tiles = n_pages + 1
    rows_q = t_new * N_HEADS
    rows_g = t_new * Q_PER_KV

    @pl.when(pl.program_id(0) == 0)
    def _():
        bk = bucket_ref[...]
        acc = jnp.zeros(bk.shape, F32)
        for bb in range(N_BUCKETS):
            acc = jnp.where(bk == bb, tabt_ref[:, bb:bb + 1], acc)
        bias_scr[...] = acc - tabt_ref[:, N_BUCKETS - 1:N_BUCKETS]

    for s in range(n_seq):
        for c in range(n_pages):
            sl = slice(c * page, (c + 1) * page)
            ki_st[s, :, sl] = cki[s * n_pages + c][0]
            k_st[s, :, :, sl] = ck[s * n_pages + c][0].astype(BF16)
            v_st[s, :, :, sl] = cv[s * n_pages + c][0].astype(BF16)

    for s in range(n_seq):
        qh, ql = _hi_lo(qi_ref[s])
        wcol = w_ref[s]
        gsum = gsum_ref[s]

        def idx_scores(kt_h, kt_l, dot):
            sc = (dot(qh, kt_l) + dot(ql, kt_h)) + dot(qh, kt_h)
            sc = jnp.maximum(sc * (IDX_DIM ** -0.5), 0.0) * wcol
            return _sel_dot_exact(gsum, sc) * (IDX_HEADS ** -0.5)

        main = idx_scores(*_hi_lo(ki_st[s]), _dot)
        tail = idx_scores(*_hi_lo(kin_ref[s]), _dot_nt)
        sc_scr[:, 0:past] = main if s == 0 else sc_scr[:, 0:past] + main
        sc_scr[:, past:] = tail if s == 0 else sc_scr[:, past:] + tail

    n_keys = n_tiles * page
    n_rows = n_seq * t_new

    def valid(l0):
        kp = l0 + lax.broadcasted_iota(I32, (n_rows, LANES), 1)
        row = lax.broadcasted_iota(I32, (n_rows, LANES), 0)
        tok = row
        for s in range(1, n_seq):
            tok = jnp.where(row >= s * t_new, row - s * t_new, tok)
        return kp <= past + tok

    for c in range(n_tiles):
        sl = slice(c * page, (c + 1) * page)
        key_scr[:, sl] = _sortable_key(jnp.where(valid(c * page), sc_scr[:, sl], NEG_INF))
    _topk_select_rows(key_scr, sel_scr, topk, valid)

    for s in range(n_seq):
        for g in range(N_KV_HEADS):
            rs = slice(g * rows_g, (g + 1) * rows_g)
            cols = slice(g * HEAD_DIM, (g + 1) * HEAD_DIM)
            qg = (q_ref[s, rs, :] * (HEAD_DIM ** -0.5)).astype(BF16)
            lg_scr[rs, 0:past] = _dot(qg, k_st[s, g])
            lg_scr[rs, past:] = _dot_nt(qg, kn_ref[s, :, cols].astype(BF16))
        near = slice(past - page, n_keys)
        lg_scr[:, near] = lg_scr[:, near] + bias_scr[...]
        picked = _dot(xpand_ref[s], sel_scr[...]) > 0.5
        lg = jnp.where(picked, lg_scr[...], NEG_INF)
        m = jnp.max(lg, axis=1, keepdims=True)
        p = jnp.exp(lg - m)
        p_scr[...] = (p * (1.0 / jnp.sum(p, axis=1, keepdims=True))).astype(BF16)
        for g in range(N_KV_HEADS):
            rs = slice(g * rows_g, (g + 1) * rows_g)
            cols = slice(g * HEAD_DIM, (g + 1) * HEAD_DIM)
            acc = _dot_nt(p_scr[rs, 0:past], v_st[s, g]) + _dot(p_scr[rs, past:], vn_ref[s, :, cols].astype(BF16))
            o_ref[s, rs, :] = acc.astype(o_ref.dtype)


def attn_sample(proj_s, cache_k, cache_v, cache_kidx, page_table, bias_table, t_new):
    nb, n_pages = page_table.shape
    n_pool, page = cache_k.shape[0], cache_k.shape[1]
    past = n_pages * page
    total = past + t_new
    topk = min(TOPK_MAX, total // 4)
    kvw = N_KV_HEADS * HEAD_DIM
    n_seq = SAMPLE_SEQS_PER_STEP if nb % SAMPLE_SEQS_PER_STEP == 0 else 1
    rows_q = t_new * N_HEADS
    rows_i = t_new * IDX_HEADS
    n_rows = n_seq * t_new
    n_keys = past + page
    assert page == LANES and page >= MAX_DISTANCE and t_new <= page
    ck_t = jnp.transpose(cache_k, (0, 2, 3, 1))
    cv_t = jnp.transpose(cache_v, (0, 2, 3, 1))
    cki_t = jnp.transpose(cache_kidx, (0, 2, 1))
    seg = lambda col, width: proj_s[:, col:col + width]
    q3 = seg(COL_Q, N_HEADS * HEAD_DIM).reshape(nb, t_new, N_KV_HEADS, Q_PER_KV, HEAD_DIM)
    q3 = q3.transpose(0, 2, 1, 3, 4).reshape(nb, rows_q, HEAD_DIM)
    qi3 = seg(COL_QI, IDX_HEADS * IDX_DIM).reshape(nb, rows_i, IDX_DIM)
    w3 = seg(COL_SM + SM_WI, IDX_HEADS).reshape(nb, rows_i, 1)
    pad_rows = lambda a: jnp.pad(a.reshape(nb, t_new, -1), ((0, 0), (0, page - t_new), (0, 0)))
    kn3, vn3, kin3 = pad_rows(seg(COL_K, kvw)), pad_rows(seg(COL_V, kvw)), pad_rows(seg(COL_KI, IDX_DIM))
    row = np.arange(rows_q)
    row_tok = (row // Q_PER_KV) % t_new
    row_head = (row // (t_new * Q_PER_KV)) * Q_PER_KV + row % Q_PER_KV
    tab_t = jnp.pad(bias_table.T[row_head], ((0, 0), (0, LANES - N_BUCKETS)))
    kpos = np.concatenate([past - page + np.arange(page), past + np.arange(page)])[None, :]
    bucket = jnp.asarray(_t5_bucket_np(past + row_tok[:, None] - kpos))
    gsum = np.zeros((n_seq, n_rows, rows_i), np.float32)
    xpand = np.zeros((n_seq, rows_q, n_rows), np.float32)
    for s in range(n_seq):
        gsum[s, s * t_new + np.arange(rows_i) // IDX_HEADS, np.arange(rows_i)] = 1.0
        xpand[s, row, s * t_new + row_tok] = 1.0
    per_step = lambda shape: pl.BlockSpec((n_seq,) + shape, lambda b, pt: (b,) + tuple(0 for _ in shape))
    const = lambda shape: pl.BlockSpec(shape, lambda b, pt: tuple(0 for _ in shape))

    def page_specs(shape):
        return [pl.BlockSpec((1,) + shape, lambda b, pt, s=s, c=c: (pt[b * n_seq + s, c],) + tuple(0 for _ in shape))
                for s in range(n_seq) for c in range(n_pages)]

    grid_spec = pltpu.PrefetchScalarGridSpec(
        num_scalar_prefetch=1,
        grid=(nb // n_seq,),
        in_specs=[per_step((rows_q, HEAD_DIM)), per_step((rows_i, IDX_DIM)), per_step((rows_i, 1)),
                  per_step((page, kvw)), per_step((page, kvw)), per_step((page, IDX_DIM)),
                  const((rows_q, LANES)), const((rows_q, 2 * page)),
                  const((n_seq, n_rows, rows_i)), const((n_seq, rows_q, n_rows))]
                 + page_specs((N_KV_HEADS, HEAD_DIM, page)) + page_specs((N_KV_HEADS, HEAD_DIM, page))
                 + page_specs((IDX_DIM, page)),
        out_specs=pl.BlockSpec((n_seq, rows_q, HEAD_DIM), lambda b, pt: (b, 0, 0)),
        scratch_shapes=[
            pltpu.VMEM((n_rows, n_keys), F32),
            pltpu.VMEM((n_rows, n_keys), I32),
            pltpu.VMEM((n_rows, n_keys), BF16),
            pltpu.VMEM((rows_q, n_keys), F32),
            pltpu.VMEM((rows_q, n_keys), BF16),
            pltpu.VMEM((rows_q, 2 * page), F32),
            pltpu.VMEM((n_seq, IDX_DIM, past), F32),
            pltpu.VMEM((n_seq, N_KV_HEADS, HEAD_DIM, past), BF16),
            pltpu.VMEM((n_seq, N_KV_HEADS, HEAD_DIM, past), BF16),
        ],
    )
    n_pg = n_seq * n_pages
    out = pl.pallas_call(
        functools.partial(_attn_sample_body, n_seq=n_seq, n_pages=n_pages, page=page, t_new=t_new, topk=topk),
        grid_spec=grid_spec,
        out_shape=jax.ShapeDtypeStruct((nb, rows_q, HEAD_DIM), BF16),
        compiler_params=_cparams(("arbitrary",)),
        name="attn_sample",
    )(page_table, q3, qi3, w3, kn3, vn3, kin3, tab_t, bucket, jnp.asarray(gsum, BF16), jnp.asarray(xpand, BF16),
      *([ck_t] * n_pg), *([cv_t] * n_pg), *([cki_t] * n_pg))
    out = out.reshape(nb, N_KV_HEADS, t_new, Q_PER_KV, HEAD_DIM).transpose(0, 2, 1, 3, 4)
    return out.reshape(nb * t_new, N_HEADS * HEAD_DIM)


def _merge_body(ys_ref, ya_ref, g_ref, h_ref, wbs_ref, wba_ref, wo_ref, lg_ref, lb_ref, o_ref):
    a = _dot(ys_ref[...], wbs_ref[...])
    b = _dot(ya_ref[...], wba_ref[...])
    gates = jax.nn.sigmoid(g_ref[...])
    merged = gates[:, :D_MODEL] * a + gates[:, D_MODEL:] * b
    r = ALPHA * h_ref[...] + _dot(merged.astype(BF16), wo_ref[...])
    o_ref[...] = _layer_norm(r, lg_ref[...], lb_ref[...])


def merge_out(y_ssm, y_att, proj, h1, wbs, wba, wo, g, b, *, tm=512):
    m = h1.shape[0]
    tm = min(tm, m)
    assert m % tm == 0
    row = lambda width, colblk=0: pl.BlockSpec((tm, width), lambda i: (i, colblk))
    full = lambda shape: pl.BlockSpec(shape, lambda i: tuple(0 for _ in shape))
    return pl.pallas_call(
        _merge_body,
        grid=(m // tm,),
        in_specs=[row(D_INNER), row(N_HEADS * HEAD_DIM), row(2 * D_MODEL, COL_G // (2 * D_MODEL)), row(D_MODEL),
                  full(wbs.shape), full(wba.shape), full(wo.shape), full((1, D_MODEL)), full((1, D_MODEL))],
        out_specs=row(D_MODEL),
        out_shape=jax.ShapeDtypeStruct((m, D_MODEL), F32),
        compiler_params=_cparams(("parallel",)),
        name="merge_out",
    )(y_ssm, y_att, proj, h1, wbs, wba, wo, g, b)


def kernel(x_prompt, x_sample, cache_k, cache_v, cache_kidx, state_conv, state_ssm, page_table, bias_table, w_in, conv_w, conv_b, dt_bias, a_log, d_skip, ssm_norm_g, w_branch_ssm, w_branch_attn, w_out, ffn1_wi, ffn1_wo, ffn2_wi, ffn2_wo, ln1_g, ln1_b, ln2_g, ln2_b, ln3_g, ln3_b):
    assert w_in.shape[0] == DEPTH
    batch, seq, d = x_prompt.shape
    nb, t_new, _ = x_sample.shape
    xs = (x_prompt.reshape(batch * seq, d), x_sample.reshape(nb * t_new, d))
    outs_p, outs_s = [], []
    for i in range(DEPTH):
        w_pack = pack_w_in(w_in[i])
        f1i, f1o = ffn1_wi[i].astype(BF16), ffn1_wo[i].astype(BF16)
        f2i, f2o = ffn2_wi[i].astype(BF16), ffn2_wo[i].astype(BF16)
        wbs, wba, wo = w_branch_ssm[i].astype(BF16), w_branch_attn[i].astype(BF16), w_out[i].astype(BF16)
        row = lambda v: v.reshape(1, -1)
        pad_heads = lambda v: jnp.concatenate([v, jnp.zeros((LANES - SSM_HEADS,), v.dtype)]).reshape(1, LANES)
        dtb_pad, alog_pad = pad_heads(dt_bias[i]), pad_heads(a_log[i])
        dskip_e = jnp.repeat(d_skip[i], SSM_HEAD_DIM).reshape(1, D_INNER)
        ng = row(ssm_norm_g[i])
        cw, cb = conv_w[i], row(conv_b[i])

        h1, proj = [], []
        for x in xs:
            hf, hb = ffn_ln(x, f1i, f1o, row(ln1_g[i]), row(ln1_b[i]), emit_bf16=True)
            h1.append(hf)
            proj.append(matmul(hb, w_pack, tm=2048, tn=PROJ_TN))
        proj_p, proj_s = proj

        ys_p, conv_p, ssm_p = ssd_prompt(proj_p, batch, seq, cw, cb, dtb_pad, alog_pad, dskip_e, ng)
        ya_p, (kt_p, vt_p, kit_p) = attn_prompt(proj_p, bias_table, batch, seq)

        raw_s = jnp.concatenate([proj_s[:, COL_X:COL_X + D_INNER], proj_s[:, COL_BC:COL_BC + 2 * GN]], axis=1)
        u_cat = jnp.concatenate([state_conv[i], raw_s.reshape(nb, t_new, CONV_DIM)], axis=1)
        expand = np.zeros((LANES, D_INNER), np.float32)
        expand[np.arange(D_INNER) // SSM_HEAD_DIM, np.arange(D_INNER)] = 1.0
        ys_s, conv_s, ssm_s = ssd_sample(u_cat, proj_s.reshape(nb, t_new, PROJ_N),
                                         state_ssm[i].reshape(nb, SSM_HEADS * SSM_HEAD_DIM, SSM_STATE),
                                         cw, cb, dtb_pad, alog_pad, dskip_e, ng, jnp.asarray(expand, BF16))
        ya_s = attn_sample(proj_s, cache_k[i], cache_v[i], cache_kidx[i], page_table, bias_table, t_new)

        new_xs = []
        for hf, pj, ys, ya in ((h1[0], proj_p, ys_p, ya_p), (h1[1], proj_s, ys_s.reshape(nb * t_new, D_INNER), ya_s)):
            h2 = merge_out(ys, ya, pj, hf, wbs, wba, wo, row(ln2_g[i]), row(ln2_b[i]))
            new_xs.append(ffn_ln(h2, f2i, f2o, row(ln3_g[i]), row(ln3_b[i]), emit_bf16=False))
        xs = tuple(new_xs)

        kvw = N_KV_HEADS * HEAD_DIM
        shp = lambda nb_, l_: (nb_, l_, N_KV_HEADS, HEAD_DIM)
        head_major = lambda a: a.reshape(batch, N_KV_HEADS, HEAD_DIM, seq).transpose(0, 3, 1, 2)
        outs_p.append((head_major(kt_p), head_major(vt_p), kit_p.transpose(0, 2, 1),
                       conv_p, ssm_p.reshape(batch, SSM_HEADS, SSM_HEAD_DIM, SSM_STATE)))
        outs_s.append((proj_s[:, COL_K:COL_K + kvw].reshape(shp(nb, t_new)),
                       proj_s[:, COL_V:COL_V + kvw].reshape(shp(nb, t_new)),
                       proj_s[:, COL_KI:COL_KI + IDX_DIM].reshape(nb, t_new, IDX_DIM),
                       conv_s, ssm_s.reshape(nb, SSM_HEADS, SSM_HEAD_DIM, SSM_STATE)))
    k_p, v_p, kidx_p, conv_pp, ssm_pp = [jnp.stack(a) for a in zip(*outs_p)]
    k_s, v_s, kidx_s, conv_ss, ssm_ss = [jnp.stack(a) for a in zip(*outs_s)]
    return (xs[0].reshape(batch, seq, d), xs[1].reshape(nb, t_new, d),
            k_p, v_p, kidx_p, conv_pp, ssm_pp, k_s, v_s, kidx_s, conv_ss, ssm_ss)
```

```python
import functools
import math

import numpy as np
import jax
import jax.numpy as jnp
from jax import lax
from jax.experimental import pallas as pl
from jax.experimental.pallas import tpu as pltpu

F32 = jnp.float32
BF16 = jnp.bfloat16
I32 = jnp.int32

D_MODEL = 1024
D_INNER = 2 * D_MODEL
SSM_HEAD_DIM = 64
SSM_HEADS = D_INNER // SSM_HEAD_DIM
SSM_GROUPS = 4
SSM_STATE = 128
CONV_W = 4
CONV_DIM = D_INNER + 2 * SSM_GROUPS * SSM_STATE
SSD_CHUNK = 128
N_HEADS = 16
HEAD_DIM = 64
N_KV_HEADS = 4
Q_PER_KV = N_HEADS // N_KV_HEADS
IDX_HEADS = 8
IDX_DIM = 64
TOPK_MAX = 256
N_BUCKETS = 32
MAX_DISTANCE = 128
DEPTH = 1
ALPHA = (2 * DEPTH) ** 0.25
LN_EPS = 1e-5
IN_SPLITS = (D_INNER, CONV_DIM, SSM_HEADS, N_HEADS * HEAD_DIM, N_KV_HEADS * HEAD_DIM,
             N_KV_HEADS * HEAD_DIM, IDX_HEADS * IDX_DIM, IDX_DIM, IDX_HEADS, 2 * D_MODEL)

LANES = 128
SUBLANES = 8
VMEM_LIMIT_BYTES = 56 * 1024 * 1024

GN = SSM_GROUPS * SSM_STATE
COL_Z = 0
COL_X = COL_Z + D_INNER
COL_G = COL_X + D_INNER
COL_BC = COL_G + 2 * D_MODEL
COL_Q = COL_BC + 2 * GN
COL_K = COL_Q + N_HEADS * HEAD_DIM
COL_V = COL_K + N_KV_HEADS * HEAD_DIM
COL_QI = COL_V + N_KV_HEADS * HEAD_DIM
COL_SM = COL_QI + IDX_HEADS * IDX_DIM
COL_KI = COL_SM + LANES
PROJ_TN = 512
PROJ_N = -(-(COL_KI + LANES) // PROJ_TN) * PROJ_TN
SM_WI = SSM_HEADS

INT_MIN = -(2 ** 31)
LOG2E = math.log2(math.e)
NEG_INF = float("-inf")
SAMPLE_SEQS_PER_STEP = 4
SSD_SAMPLE_SEQS_PER_STEP = 4
RADIX_BITS = 4


def _cparams(sem):
    return pltpu.CompilerParams(dimension_semantics=sem, vmem_limit_bytes=VMEM_LIMIT_BYTES)


def _layer_norm(r, g, b):
    mu = jnp.mean(r, axis=-1, keepdims=True)
    d = r - mu
    var = jnp.mean(d * d, axis=-1, keepdims=True)
    return d * lax.rsqrt(var + LN_EPS) * g + b


def _silu(x):
    h = 0.5 * x
    return h + h * jnp.tanh(h)


def _softplus(x):
    return jnp.maximum(x, 0.0) + jnp.log1p(jnp.exp(-jnp.abs(x)))


def _dot(a, b):
    return jnp.dot(a, b, preferred_element_type=F32)


def _dot_nt(a, b):
    return lax.dot_general(a, b, (((1,), (1,)), ((), ())), preferred_element_type=F32)


def _dot_tn(a, b):
    return lax.dot_general(a, b, (((0,), (0,)), ((), ())), preferred_element_type=F32)


def _split3(v):
    hi = v.astype(BF16)
    r1 = v - hi.astype(F32)
    mid = r1.astype(BF16)
    lo = (r1 - mid.astype(F32)).astype(BF16)
    return hi, mid, lo


def _dot_exact_sel(v, sel_bf16):
    hi, mid, lo = _split3(v)
    return (_dot(hi, sel_bf16) + _dot(mid, sel_bf16)) + _dot(lo, sel_bf16)


def _sel_dot_exact(sel_bf16, v):
    hi, mid, lo = _split3(v)
    return (_dot(sel_bf16, hi) + _dot(sel_bf16, mid)) + _dot(sel_bf16, lo)


def _hi_lo(a):
    ah = a.astype(BF16)
    return ah, (a - ah.astype(F32)).astype(BF16)


def _col_reduce(x, op2, op):
    parts = []
    for r0 in range(0, x.shape[0], LANES):
        y = x[r0:r0 + LANES]
        n = y.shape[0]
        while n > SUBLANES and n % (2 * SUBLANES) == 0:
            n //= 2
            y = op2(y[:n], y[n:])
        parts.append(y)
    while len(parts) > 1:
        parts = [op2(parts[i], parts[i + 1]) if i + 1 < len(parts) else parts[i] for i in range(0, len(parts), 2)]
    return op(parts[0], axis=0, keepdims=True)


def _col_sum(x):
    return _col_reduce(x, jnp.add, jnp.sum)


def _col_max(x):
    return _col_reduce(x, jnp.maximum, jnp.max)


def _ffn_ln_body(x_ref, wi_ref, wo_ref, g_ref, b_ref, o_ref, *maybe_ob_ref, tf):
    x = x_ref[...]
    xb = x.astype(BF16)
    dff = wo_ref.shape[0]
    acc = None
    for c0 in range(0, dff, tf):
        gate = _dot(xb, wi_ref[:, c0:c0 + tf])
        up = _dot(xb, wi_ref[:, dff + c0:dff + c0 + tf])
        part = _dot((_silu(gate) * up).astype(BF16), wo_ref[c0:c0 + tf, :])
        acc = part if acc is None else acc + part
    y = _layer_norm(ALPHA * x + 0.5 * acc, g_ref[...], b_ref[...])
    o_ref[...] = y
    for ob_ref in maybe_ob_ref:
        ob_ref[...] = y.astype(BF16)


def ffn_ln(x, wi_bf, wo_bf, g, b, *, emit_bf16, tm=512, tf=1408):
    m, d = x.shape
    dff = wo_bf.shape[0]
    tm = min(tm, m)
    assert m % tm == 0 and dff % tf == 0
    row = pl.BlockSpec((tm, d), lambda i: (i, 0))
    resident = lambda shape: pl.BlockSpec(shape, lambda i: (0, 0), pipeline_mode=pl.Buffered(1))
    n_out = 2 if emit_bf16 else 1
    res = pl.pallas_call(
        functools.partial(_ffn_ln_body, tf=tf),
        grid=(m // tm,),
        in_specs=[row, resident(wi_bf.shape), resident(wo_bf.shape), resident((1, d)), resident((1, d))],
        out_specs=[row] * n_out,
        out_shape=[jax.ShapeDtypeStruct((m, d), F32), jax.ShapeDtypeStruct((m, d), BF16)][:n_out],
        compiler_params=_cparams(("parallel",)),
        name="ffn_ln",
    )(x, wi_bf, wo_bf, g, b)
    return res if emit_bf16 else res[0]


def _matmul_body(x_ref, w_ref, o_ref):
    o_ref[...] = _dot(x_ref[...], w_ref[...])


def matmul(x_bf, w_bf, *, tm, tn):
    m, k = x_bf.shape
    n = w_bf.shape[1]
    tm = min(tm, m)
    assert m % tm == 0 and n % tn == 0
    return pl.pallas_call(
        _matmul_body,
        grid=(m // tm, n // tn),
        in_specs=[pl.BlockSpec((tm, k), lambda i, j: (i, 0)),
                  pl.BlockSpec((k, tn), lambda i, j: (0, j))],
        out_specs=pl.BlockSpec((tm, tn), lambda i, j: (i, j)),
        out_shape=jax.ShapeDtypeStruct((m, n), F32),
        compiler_params=_cparams(("parallel", "arbitrary")),
        name="in_proj",
    )(x_bf, w_bf)


def pack_w_in(w_in):
    offs = np.cumsum((0,) + IN_SPLITS)
    z, xbc, dt, q, k, v, qi, ki, wi, gates = [w_in[:, offs[i]:offs[i + 1]] for i in range(len(IN_SPLITS))]
    d = w_in.shape[0]
    zeros = lambda n: jnp.zeros((d, n), w_in.dtype)
    small = jnp.concatenate([dt, wi, zeros(LANES - SSM_HEADS - IDX_HEADS)], axis=1)
    kib = jnp.concatenate([ki, ki], axis=1)
    packed = jnp.concatenate([z, xbc[:, :D_INNER], gates, xbc[:, D_INNER:], q, k, v, qi, small, kib,
                              zeros(PROJ_N - COL_KI - LANES)], axis=1)
    return packed.astype(BF16)


def _ssd_prompt_body(z_ref, xr_ref, bc_ref, sm_ref, cwx_ref, cwbc_ref, cbx_ref, cbbc_ref,
                     dtb_ref, alog_ref, dskip_ref, ng_ref,
                     y_ref, cs_ref, hs_ref,
                     xbuf, bcbuf, h_scr, xc_scr, bcc_scr, y_scr):
    c = pl.program_id(1)
    nc = pl.num_programs(1)
    L = SSD_CHUNK
    halo = SUBLANES
    n_pairs = SSM_HEADS // 2
    pairs_per_group = n_pairs // SSM_GROUPS

    @pl.when(c == 0)
    def _():
        xbuf[...] = jnp.zeros_like(xbuf)
        bcbuf[...] = jnp.zeros_like(bcbuf)
        h_scr[...] = jnp.zeros_like(h_scr)

    assert CONV_W == 4

    def conv(halo_ref, x_ref, w_ref, b_ref):
        full = jnp.concatenate([halo_ref[...], x_ref[...]], axis=0)
        prev = pltpu.roll(full, 1, axis=0)
        late = (w_ref[1:2, :] * full + w_ref[0:1, :] * prev)
        early = w_ref[3:4, :] * full[halo:] + w_ref[2:3, :] * prev[halo:]
        acc = (b_ref[...] + early) + pltpu.roll(late, 2, axis=0)[halo:]
        halo_ref[...] = x_ref[L - halo:L, :]
        return _silu(acc)

    xc_scr[...] = conv(xbuf, xr_ref, cwx_ref, cbx_ref)
    bcc_scr[...] = conv(bcbuf, bc_ref, cwbc_ref, cbbc_ref)

    dt = _softplus(sm_ref[...] + dtb_ref[...])
    a = -jnp.exp(alog_ref[...])
    row = lax.broadcasted_iota(I32, (L, L), 0)
    col = lax.broadcasted_iota(I32, (L, L), 1)
    tril = (row >= col).astype(F32)
    cum = jnp.dot(tril, dt * a, precision=lax.Precision.HIGHEST, preferred_element_type=F32)
    w = jnp.exp(cum[L - 1:L, :] - cum) * dt
    cum_t = cum.T
    dt_t = dt.T
    causal = row >= col
    lane = lax.broadcasted_iota(I32, (L, LANES), 1)
    lo_half = lane < SSM_HEAD_DIM
    sub = lax.broadcasted_iota(I32, (2 * SSM_HEAD_DIM, SSM_STATE), 0)
    lo_rows = sub < SSM_HEAD_DIM

    for i in range(n_pairs):
        g = i // pairs_per_group
        h0, h1 = 2 * i, 2 * i + 1
        sl = slice(i * LANES, (i + 1) * LANES)
        if i % pairs_per_group == 0:
            bg = bcc_scr[:, g * SSM_STATE:(g + 1) * SSM_STATE].astype(BF16)
            cg = bcc_scr[:, GN + g * SSM_STATE:GN + (g + 1) * SSM_STATE].astype(BF16)
            cb = _dot_nt(cg, bg)
        xp = xc_scr[:, sl]
        c0 = jnp.broadcast_to(cum[:, h0:h0 + 1], (L, LANES))
        c1 = jnp.broadcast_to(cum[:, h1:h1 + 1], (L, LANES))
        w0 = jnp.broadcast_to(w[:, h0:h0 + 1], (L, LANES))
        w1 = jnp.broadcast_to(w[:, h1:h1 + 1], (L, LANES))

        def mix(ct, h):
            seg = jnp.exp(jnp.where(causal, ct - cum_t[h:h + 1, :], NEG_INF))
            return (cb * seg * dt_t[h:h + 1, :]).astype(BF16)

        zero = jnp.zeros_like(xp)
        yd = (_dot(mix(c0, h0), jnp.where(lo_half, xp, zero).astype(BF16))
              + _dot(mix(c1, h1), jnp.where(lo_half, zero, xp).astype(BF16)))
        hp = h_scr[i]
        yo = _dot_nt(cg, hp.astype(BF16)) * jnp.where(lo_half, jnp.exp(c0), jnp.exp(c1))
        y_scr[:, sl] = yd + yo + dskip_ref[:, sl] * xp
        xw = (xp * jnp.where(lo_half, w0, w1)).astype(BF16)
        st = _dot_tn(xw, bg)
        dec = jnp.where(lo_rows, jnp.exp(c0[L - 1:L, :]), jnp.exp(c1[L - 1:L, :]))
        h_scr[i] = dec * hp + st

    gw = D_INNER // SSM_GROUPS
    for g in range(SSM_GROUPS):
        sl = slice(g * gw, (g + 1) * gw)
        u = y_scr[:, sl] * _silu(z_ref[:, sl])
        ms = jnp.mean(u * u, axis=-1, keepdims=True)
        y_ref[:, sl] = (u * lax.rsqrt(ms + LN_EPS) * ng_ref[:, sl]).astype(y_ref.dtype)

    @pl.when(c == nc - 1)
    def _():
        cs_ref[0, :, 0:D_INNER] = xr_ref[L - (CONV_W - 1):L, :]
        cs_ref[0, :, D_INNER:CONV_DIM] = bc_ref[L - (CONV_W - 1):L, :]
        for i in range(n_pairs):
            hs_ref[0, i * LANES:(i + 1) * LANES, :] = h_scr[i]


def ssd_prompt(proj, batch, seq, conv_w, conv_b, dtb_pad, alog_pad, dskip_e, norm_g):
    L = SSD_CHUNK
    nc = seq // L
    assert seq % L == 0
    blk = lambda width, colblk: pl.BlockSpec((L, width), lambda b, c: (b * nc + c, colblk))
    full = lambda shape: pl.BlockSpec(shape, lambda b, c: tuple(0 for _ in shape))
    n_pairs = SSM_HEADS // 2
    return pl.pallas_call(
        _ssd_prompt_body,
        grid=(batch, nc),
        in_specs=[
            blk(D_INNER, COL_Z // D_INNER),
            blk(D_INNER, COL_X // D_INNER),
            blk(2 * GN, COL_BC // (2 * GN)),
            blk(LANES, COL_SM // LANES),
            full((CONV_W, D_INNER)), full((CONV_W, 2 * GN)), full((1, D_INNER)), full((1, 2 * GN)),
            full((1, LANES)), full((1, LANES)), full((1, D_INNER)), full((1, D_INNER)),
        ],
        out_specs=[
            pl.BlockSpec((L, D_INNER), lambda b, c: (b * nc + c, 0)),
            pl.BlockSpec((1, CONV_W - 1, CONV_DIM), lambda b, c: (b, 0, 0)),
            pl.BlockSpec((1, SSM_HEADS * SSM_HEAD_DIM, SSM_STATE), lambda b, c: (b, 0, 0)),
        ],
        out_shape=[
            jax.ShapeDtypeStruct((batch * seq, D_INNER), BF16),
            jax.ShapeDtypeStruct((batch, CONV_W - 1, CONV_DIM), F32),
            jax.ShapeDtypeStruct((batch, SSM_HEADS * SSM_HEAD_DIM, SSM_STATE), F32),
        ],
        scratch_shapes=[
            pltpu.VMEM((SUBLANES, D_INNER), F32),
            pltpu.VMEM((SUBLANES, 2 * GN), F32),
            pltpu.VMEM((n_pairs, 2 * SSM_HEAD_DIM, SSM_STATE), F32),
            pltpu.VMEM((L, D_INNER), F32),
            pltpu.VMEM((L, 2 * GN), F32),
            pltpu.VMEM((L, D_INNER), F32),
        ],
        compiler_params=_cparams(("parallel", "arbitrary")),
        name="ssd_prompt",
    )(proj, proj, proj, proj, conv_w[:, :D_INNER], conv_w[:, D_INNER:], conv_b[:, :D_INNER], conv_b[:, D_INNER:],
      dtb_pad, alog_pad, dskip_e, norm_g)


def _ssd_sample_body(u_ref, z_ref, sm_ref, h0_ref, cw_ref, cb_ref, dtb_ref, alog_ref, dskip_ref, ng_ref, exp_ref,
                     y_ref, cs_ref, hn_ref, *, t_new, n_seq):
    for s in range(n_seq):
        _ssd_sample_one(s, u_ref, z_ref, sm_ref, h0_ref, cw_ref, cb_ref, dtb_ref, alog_ref, dskip_ref, ng_ref, exp_ref,
                        y_ref, cs_ref, hn_ref, t_new)


def _ssd_sample_one(s, u_ref, z_ref, sm_ref, h0_ref, cw_ref, cb_ref, dtb_ref, alog_ref, dskip_ref, ng_ref, exp_ref,
                    y_ref, cs_ref, hn_ref, t_new):
    T = t_new
    n_pairs = SSM_HEADS // 2
    pairs_per_group = n_pairs // SSM_GROUPS
    acc = cb_ref[...] + cw_ref[0:1, :] * u_ref[s, 0:T, :]
    for k in range(1, CONV_W):
        acc = acc + cw_ref[k:k + 1, :] * u_ref[s, k:k + T, :]
    xbc = _silu(acc)
    cs_ref[s] = u_ref[s, T:T + CONV_W - 1, :]
    x = xbc[:, :D_INNER]

    dt = _softplus(sm_ref[s] + dtb_ref[...])
    da = dt * (-jnp.exp(alog_ref[...]))
    rows = [da[0:1, :]]
    for t in range(1, T):
        rows.append(rows[-1] + da[t:t + 1, :])
    cum = jnp.concatenate(rows, axis=0)
    both = _dot_exact_sel(jnp.concatenate([cum, dt], axis=0), exp_ref[...])
    cum_e = both[0:T, :]
    dt_e = both[T:2 * T, :]
    t_idx = lax.broadcasted_iota(I32, (T, GN), 0)

    gw = D_INNER // SSM_GROUPS
    y_parts = []
    for g in range(SSM_GROUPS):
        sl = slice(g * gw, (g + 1) * gw)
        bg = xbc[:, D_INNER + g * SSM_STATE:D_INNER + (g + 1) * SSM_STATE]
        cg = xbc[:, D_INNER + GN + g * SSM_STATE:D_INNER + GN + (g + 1) * SSM_STATE]
        cbg = _dot_nt(cg.astype(BF16), bg.astype(BF16))
        yg = dskip_ref[:, sl] * x[:, sl]
        for s_tok in range(T):
            seg = jnp.exp(jnp.where(t_idx >= s_tok, cum_e[:, sl] - cum_e[s_tok:s_tok + 1, sl], NEG_INF))
            coef = seg * dt_e[s_tok:s_tok + 1, sl] * jnp.broadcast_to(cbg[:, s_tok:s_tok + 1], (T, gw))
            yg = yg + coef * x[s_tok:s_tok + 1, sl]
        y_parts.append(yg)

    xw = x * dt_e * jnp.exp(cum_e[T - 1:T, :] - cum_e)
    ecum = jnp.exp(cum_e)
    sub = lax.broadcasted_iota(I32, (2 * SSM_HEAD_DIM, SSM_STATE), 0)
    lo_rows = sub < SSM_HEAD_DIM
    yo_parts = []
    for i in range(n_pairs):
        g = i // pairs_per_group
        sl = slice(i * LANES, (i + 1) * LANES)
        bg = xbc[:, D_INNER + g * SSM_STATE:D_INNER + (g + 1) * SSM_STATE].astype(BF16)
        cg = xbc[:, D_INNER + GN + g * SSM_STATE:D_INNER + GN + (g + 1) * SSM_STATE].astype(BF16)
        hp = h0_ref[s, i * LANES:(i + 1) * LANES, :]
        yo_parts.append(_dot_nt(cg, hp.astype(BF16)) * ecum[:, sl])
        st = _dot_tn(xw[:, sl].astype(BF16), bg)
        d0 = jnp.exp(cum[T - 1:T, 2 * i:2 * i + 1])
        d1 = jnp.exp(cum[T - 1:T, 2 * i + 1:2 * i + 2])
        dec = jnp.where(lo_rows, jnp.broadcast_to(d0, sub.shape), jnp.broadcast_to(d1, sub.shape))
        hn_ref[s, i * LANES:(i + 1) * LANES, :] = dec * hp + st

    for g in range(SSM_GROUPS):
        sl = slice(g * gw, (g + 1) * gw)
        yo = jnp.concatenate(yo_parts[g * pairs_per_group:(g + 1) * pairs_per_group], axis=1)
        u = (y_parts[g] + yo) * _silu(z_ref[s, :, sl])
        ms = jnp.mean(u * u, axis=-1, keepdims=True)
        y_ref[s, :, sl] = (u * lax.rsqrt(ms + LN_EPS) * ng_ref[:, sl]).astype(y_ref.dtype)


def ssd_sample(u_cat, proj3, h0, conv_w, conv_b, dtb_pad, alog_pad, dskip_e, norm_g, expand):
    nb, t_new, _ = proj3.shape
    full = lambda shape: pl.BlockSpec(shape, lambda b: tuple(0 for _ in shape))
    hp = SSM_HEADS * SSM_HEAD_DIM
    n_seq = SSD_SAMPLE_SEQS_PER_STEP if nb % SSD_SAMPLE_SEQS_PER_STEP == 0 else 1
    return pl.pallas_call(
        functools.partial(_ssd_sample_body, t_new=t_new, n_seq=n_seq),
        grid=(nb // n_seq,),
        in_specs=[
            pl.BlockSpec((n_seq, t_new + CONV_W - 1, CONV_DIM), lambda b: (b, 0, 0)),
            pl.BlockSpec((n_seq, t_new, D_INNER), lambda b: (b, 0, COL_Z // D_INNER)),
            pl.BlockSpec((n_seq, t_new, LANES), lambda b: (b, 0, COL_SM // LANES)),
            pl.BlockSpec((n_seq, hp, SSM_STATE), lambda b: (b, 0, 0)),
            full((CONV_W, CONV_DIM)), full((1, CONV_DIM)), full((1, LANES)), full((1, LANES)),
            full((1, D_INNER)), full((1, D_INNER)), full((LANES, D_INNER)),
        ],
        out_specs=[
            pl.BlockSpec((n_seq, t_new, D_INNER), lambda b: (b, 0, 0)),
            pl.BlockSpec((n_seq, CONV_W - 1, CONV_DIM), lambda b: (b, 0, 0)),
            pl.BlockSpec((n_seq, hp, SSM_STATE), lambda b: (b, 0, 0)),
        ],
        out_shape=[
            jax.ShapeDtypeStruct((nb, t_new, D_INNER), BF16),
            jax.ShapeDtypeStruct((nb, CONV_W - 1, CONV_DIM), F32),
            jax.ShapeDtypeStruct((nb, hp, SSM_STATE), F32),
        ],
        compiler_params=_cparams(("parallel",)),
        name="ssd_sample",
    )(u_cat, proj3, proj3, h0, conv_w, conv_b, dtb_pad, alog_pad, dskip_e, norm_g, expand)


def _sortable_key(score):
    score = jnp.where(score == 0.0, 0.0, score)
    bits = lax.bitcast_convert_type(score, I32)
    return bits ^ ((bits >> 31) & jnp.int32(0x7FFFFFFF))


def _t5_bucket_np(rel):
    n = np.maximum(rel, 0)
    max_exact = N_BUCKETS // 2
    nf = np.maximum(n, 1).astype(np.float32)
    large = max_exact + (np.log(nf / max_exact) / math.log(MAX_DISTANCE / max_exact)
                         * (N_BUCKETS - max_exact)).astype(np.int32)
    large = np.minimum(large, N_BUCKETS - 1)
    return np.where(n < max_exact, n, large).astype(np.int32)


def _count_cols_i16(ref, n_rows, pred):
    packed_rows = 2 * SUBLANES
    parts = []
    for r0 in range(0, n_rows, LANES):
        x = jnp.where(pred(ref[r0:r0 + LANES, :]), jnp.int16(1), jnp.int16(0))
        n = x.shape[0]
        while n > packed_rows:
            n //= 2
            x = x[:n] + x[n:]
        parts.append(x)
    while len(parts) > 1:
        parts = [parts[i] + parts[i + 1] if i + 1 < len(parts) else parts[i] for i in range(0, len(parts), 2)]
    return jnp.sum(parts[0].astype(I32), axis=0, keepdims=True)


def _kth_largest_i16(ref, n_rows, k):
    i16_min = -(2 ** 15)

    def try_cand(r, cand):
        cnt = _count_cols_i16(ref, n_rows, lambda v: v >= cand.astype(jnp.int16))
        return jnp.where(cnt >= k, cand, r)

    r0 = try_cand(jnp.full((1, LANES), i16_min, I32), jnp.zeros((1, LANES), I32))
    return lax.fori_loop(0, 15, lambda it, r: try_cand(r, r + (jnp.int32(1) << (14 - it))), r0)


def _kth_largest_cols(key_ref, hi_ref, lo_ref, n_rows, k):
    i16_min = -(2 ** 15)
    for r0 in range(0, n_rows, LANES):
        key = key_ref[r0:r0 + LANES, :]
        hi_ref[r0:r0 + LANES, :] = (key >> 16).astype(jnp.int16)
        lo_ref[r0:r0 + LANES, :] = ((key & 0xFFFF) + i16_min).astype(jnp.int16)
    t_hi = _kth_largest_i16(hi_ref, n_rows, k)
    t_hi16 = t_hi.astype(jnp.int16)
    k_lo = k - _count_cols_i16(hi_ref, n_rows, lambda v: v > t_hi16)
    for r0 in range(0, n_rows, LANES):
        sl = slice(r0, r0 + LANES)
        lo_ref[sl, :] = jnp.where(hi_ref[sl, :] == t_hi16, lo_ref[sl, :], jnp.int16(i16_min))
    t_lo = _kth_largest_i16(lo_ref, n_rows, k_lo)
    return (t_hi << 16) | ((t_lo - i16_min) & 0xFFFF)


def _topk_mask_cols(key_ref, hi_ref, lo_ref, msk_ref, n_rows, k, valid_fn):
    thr = _kth_largest_cols(key_ref, hi_ref, lo_ref, n_rows, k)
    n_gt = _col_sum(jnp.where(key_ref[0:n_rows, :] > thr, 1.0, 0.0))
    need = k - n_gt
    run = jnp.zeros((1, LANES), F32)
    ri = lax.broadcasted_iota(I32, (LANES, LANES), 0)
    ci = lax.broadcasted_iota(I32, (LANES, LANES), 1)
    strict = (ri > ci).astype(BF16)
    for r0 in range(0, n_rows, LANES):
        kc = key_ref[r0:r0 + LANES, :]
        eq = kc == thr
        eqf = eq.astype(F32)
        before = _dot(strict, eqf.astype(BF16)) + run
        sel = ((kc > thr) | (eq & (before < need))) & valid_fn(r0, LANES)
        msk_ref[r0:r0 + LANES, :] = jnp.where(sel, 0.0, NEG_INF)
        run = run + jnp.sum(eqf, axis=0, keepdims=True)


def _kth_largest_rows(key_ref, k):
    rows = key_ref.shape[0]
    digits = 2 ** RADIX_BITS
    n_rounds = 32 // RADIX_BITS

    def body(rnd, r):
        shift = (32 - RADIX_BITS) - RADIX_BITS * rnd
        n_ok = jnp.zeros((rows, 1), I32)
        for i in range(1, digits):
            cand = r + (jnp.int32(i) << shift)
            cnt = jnp.sum((key_ref[...] >= cand).astype(I32), axis=1, keepdims=True)
            n_ok = n_ok + (cnt >= k).astype(I32)
        return r + (n_ok << shift)

    return lax.fori_loop(0, n_rounds, body, jnp.full((rows, 1), INT_MIN, I32))


def _topk_select_rows(key_ref, sel_ref, k, valid_fn):
    n_keys = key_ref.shape[1]
    thr = _kth_largest_rows(key_ref, k)
    n_gt = jnp.sum((key_ref[...] > thr).astype(I32), axis=1, keepdims=True)
    need = (k - n_gt).astype(F32)
    run = jnp.zeros((key_ref.shape[0], 1), F32)
    ri = lax.broadcasted_iota(I32, (LANES, LANES), 0)
    ci = lax.broadcasted_iota(I32, (LANES, LANES), 1)
    strict = (ri < ci).astype(BF16)
    for l0 in range(0, n_keys, LANES):
        kc = key_ref[:, l0:l0 + LANES]
        eq = kc == thr
        eqf = eq.astype(F32)
        before = _dot(eqf.astype(BF16), strict) + run
        sel = ((kc > thr) | (eq & (before < need))) & valid_fn(l0)
        sel_ref[:, l0:l0 + LANES] = sel.astype(F32).astype(BF16)
        run = run + jnp.sum(eqf, axis=1, keepdims=True)


def _bias_tiles_body(tab_ref, bucket_ref, o_ref):
    for d in range(bucket_ref.shape[0]):
        bk = bucket_ref[d]
        for h in range(N_HEADS):
            acc = jnp.zeros(bk.shape, F32)
            for bb in range(N_BUCKETS):
                acc = jnp.where(bk == bb, tab_ref[bb, h], acc)
            o_ref[h, d] = (acc - tab_ref[N_BUCKETS - 1, h]) * LOG2E


def _attn_prompt_body(bias_ref, q_ref, k_ref, v_ref, qi_ref, ki_ref, sm_ref, *rest, j, topk, emit_transposed):
    key_scr, hi_scr, lo_scr, msk_scr, ot_scr, ka_scr = rest[-6:]
    QB = LANES
    nk = (j + 1) * QB
    if emit_transposed:
        k32_ref, v32_ref, o_ref, kt_ref, vt_ref, kit_ref = rest[-12:-6]
        kt_ref[0] = k32_ref[...].T
        vt_ref[0] = v32_ref[...].T
        kit_ref[0] = ki_ref[...].T[0:IDX_DIM, :]
    else:
        o_ref = rest[-7]

    def valid(r0, rows):
        kp = r0 + lax.broadcasted_iota(I32, (rows, QB), 0)
        qp = j * QB + lax.broadcasted_iota(I32, (rows, QB), 1)
        return kp <= qp

    if nk <= topk:
        msk_scr[...] = jnp.where(valid(0, nk), 0.0, NEG_INF)
    else:
        x = ki_ref[0:nk, :]
        hi, lo = _hi_lo(x)
        first = lax.broadcasted_iota(I32, (nk, LANES), 1) < IDX_DIM
        ka_scr[:, 0:LANES] = jnp.where(first, hi, lo)
        ka_scr[:, LANES:2 * LANES] = jnp.where(first, hi, jnp.zeros_like(hi))

        wt = sm_ref[...].T
        ka = ka_scr[...]
        score = jnp.zeros((nk, QB), F32)
        pad = jnp.zeros((QB, 2 * LANES - 3 * IDX_DIM), BF16)
        for h2 in range(IDX_HEADS // 2):
            blocks = []
            for h in (2 * h2, 2 * h2 + 1):
                qh, ql = _hi_lo(qi_ref[:, h * IDX_DIM:(h + 1) * IDX_DIM] * (IDX_DIM ** -0.5))
                blocks.append(jnp.concatenate([ql, qh, qh, pad], axis=1))
            s = _dot_nt(ka, jnp.concatenate(blocks, axis=0))
            for i, h in enumerate((2 * h2, 2 * h2 + 1)):
                score = score + jnp.maximum(s[:, i * QB:(i + 1) * QB], 0.0) * wt[SM_WI + h:SM_WI + h + 1, :]
        score = jnp.where(valid(0, nk), score * (IDX_HEADS ** -0.5), NEG_INF)
        key_scr[...] = _sortable_key(score)
        _topk_mask_cols(key_scr, hi_scr, lo_scr, msk_scr, nk, topk, valid)

    near0 = max(j - 1, 0) * QB
    for g in range(N_KV_HEADS):
        kcols = slice(g * HEAD_DIM, (g + 1) * HEAD_DIM)
        vcols = slice(g * LANES, (g + 1) * LANES)
        heads = range(g * Q_PER_KV, (g + 1) * Q_PER_KV)
        qg = jnp.concatenate([(q_ref[:, h * HEAD_DIM:(h + 1) * HEAD_DIM] * (HEAD_DIM ** -0.5 * LOG2E)).astype(BF16)
                              for h in heads], axis=0)
        lg_all = _dot_nt(k_ref[0:nk, kcols], qg)
        probs = []
        for r, h in enumerate(heads):
            lg = lg_all[:, r * QB:(r + 1) * QB] + msk_scr[...]
            bias = bias_ref[h, 0] if j == 0 else jnp.concatenate([bias_ref[h, 1], bias_ref[h, 0]], axis=0)
            lg = jnp.concatenate([lg[:near0], lg[near0:] + bias], axis=0) if near0 else lg + bias
            probs.append(jnp.exp2(lg - _col_max(lg)).astype(BF16))
        pv = _dot_tn(v_ref[0:nk, vcols], jnp.concatenate(probs, axis=1))
        for r, h in enumerate(heads):
            sl = slice(r * QB, (r + 1) * QB)
            ot_scr[h * HEAD_DIM:(h + 1) * HEAD_DIM, :] = pv[0:HEAD_DIM, sl] * (1.0 / pv[HEAD_DIM:HEAD_DIM + 1, sl])
    o_ref[...] = ot_scr[...].T.astype(o_ref.dtype)


def attn_prompt(proj, bias_table, batch, seq):
    QB = LANES
    nq = seq // QB
    assert seq % QB == 0
    topk = min(TOPK_MAX, seq // 4)
    ts = np.arange(QB)[:, None]
    tq = np.arange(QB)[None, :]
    bucket = jnp.asarray(np.stack([_t5_bucket_np(tq - ts), _t5_bucket_np(QB + tq - ts)]))
    kvw = N_KV_HEADS * HEAD_DIM
    kv32 = lax.optimization_barrier(proj[:, COL_K:COL_K + 2 * kvw])
    k_bf = kv32[:, :kvw].astype(BF16)
    v4 = kv32[:, kvw:].astype(BF16).reshape(batch * seq, N_KV_HEADS, HEAD_DIM)
    v_aug = jnp.concatenate([v4, jnp.ones(v4.shape[:2] + (1,), BF16),
                             jnp.zeros(v4.shape[:2] + (LANES - HEAD_DIM - 1,), BF16)], axis=-1)
    v_aug = v_aug.reshape(batch * seq, N_KV_HEADS * LANES)
    bias = pl.pallas_call(
        _bias_tiles_body,
        in_specs=[pl.BlockSpec(memory_space=pltpu.SMEM), pl.BlockSpec(memory_space=pltpu.VMEM)],
        out_specs=pl.BlockSpec(memory_space=pltpu.VMEM),
        out_shape=jax.ShapeDtypeStruct((N_HEADS, 2, QB, QB), F32),
        name="bias_tiles",
    )(bias_table, bucket)
    out = jnp.zeros((batch * seq, N_HEADS * HEAD_DIM), BF16)
    for j in range(nq):
        nk = (j + 1) * QB
        rows = min(d for d in range(nk, seq + 1, QB) if seq % d == 0)
        rowblk = lambda width, colblk: pl.BlockSpec((QB, width), lambda b: (b * nq + j, colblk))
        keyblk = lambda width, colblk: pl.BlockSpec((rows, width), lambda b: (b * (seq // rows), colblk))
        last = j == nq - 1
        extra_in = [keyblk(kvw, COL_K // kvw), keyblk(kvw, COL_V // kvw)] if last else []
        tblk = lambda r: pl.BlockSpec((1, r, seq), lambda b: (b, 0, 0))
        tshape = lambda r: jax.ShapeDtypeStruct((batch, r, seq), F32)
        res = pl.pallas_call(
            functools.partial(_attn_prompt_body, j=j, topk=topk, emit_transposed=last),
            grid=(batch,),
            in_specs=[
                pl.BlockSpec((N_HEADS, 2, QB, QB), lambda b: (0, 0, 0, 0)),
                rowblk(N_HEADS * HEAD_DIM, COL_Q // (N_HEADS * HEAD_DIM)),
                keyblk(kvw, 0),
                keyblk(N_KV_HEADS * LANES, 0),
                rowblk(IDX_HEADS * IDX_DIM, COL_QI // (IDX_HEADS * IDX_DIM)),
                keyblk(LANES, COL_KI // LANES),
                rowblk(LANES, COL_SM // LANES),
                pl.BlockSpec(memory_space=pl.ANY),
            ] + extra_in,
            out_specs=[pl.BlockSpec((QB, N_HEADS * HEAD_DIM), lambda b: (b * nq + j, 0))]
                      + ([tblk(kvw), tblk(kvw), tblk(IDX_DIM)] if last else []),
            out_shape=[jax.ShapeDtypeStruct((batch * seq, N_HEADS * HEAD_DIM), BF16)]
                      + ([tshape(kvw), tshape(kvw), tshape(IDX_DIM)] if last else []),
            input_output_aliases={7: 0},
            scratch_shapes=[
                pltpu.VMEM((nk, QB), I32),
                pltpu.VMEM((nk, QB), jnp.int16),
                pltpu.VMEM((nk, QB), jnp.int16),
                pltpu.VMEM((nk, QB), F32),
                pltpu.VMEM((N_HEADS * HEAD_DIM, QB), F32),
                pltpu.VMEM((nk, 2 * LANES), BF16),
            ],
            compiler_params=_cparams(("parallel",)),
            name=f"attn_prompt_q{j}",
        )(bias, proj, k_bf, v_aug, proj, proj, proj, out, *([proj, proj] if last else []))
        out = res[0]
    return out, res[1:]


def _attn_sample_body(pt_ref, q_ref, qi_ref, w_ref, kn_ref, vn_ref, kin_ref, tabt_ref, bucket_ref, gsum_ref, xpand_ref,
                      *rest, n_seq, n_pages, page, t_new, topk):
    n_pg = n_seq * n_pages
    ck = rest[0:n_pg]
    cv = rest[n_pg:2 * n_pg]
    cki = rest[2 * n_pg:3 * n_pg]
    o_ref, sc_scr, key_scr, sel_scr, lg_scr, p_scr, bias_scr, ki_st, k_st, v_st = rest[3 * n_pg:]
    past = n_pages * page
    n_tiles = n_pages + 1
    rows_q = t_new * N_HEADS
    rows_g = t_new * Q_PER_KV

    @pl.when(pl.program_id(0) == 0)
    def _():
        bk = bucket_ref[...]
        acc = jnp.zeros(bk.shape, F32)
        for bb in range(N_BUCKETS):
            acc = jnp.where(bk == bb, tabt_ref[:, bb:bb + 1], acc)
        bias_scr[...] = acc - tabt_ref[:, N_BUCKETS - 1:N_BUCKETS]

    for s in range(n_seq):
        for c in range(n_pages):
            sl = slice(c * page, (c + 1) * page)
            ki_st[s, :, sl] = cki[s * n_pages + c][0]
            k_st[s, :, :, sl] = ck[s * n_pages + c][0].astype(BF16)
            v_st[s, :, :, sl] = cv[s * n_pages + c][0].astype(BF16)

    for s in range(n_seq):
        qh, ql = _hi_lo(qi_ref[s])
        wcol = w_ref[s]
        gsum = gsum_ref[s]

        def idx_scores(kt_h, kt_l, dot):
            sc = (dot(qh, kt_l) + dot(ql, kt_h)) + dot(qh, kt_h)
            sc = jnp.maximum(sc * (IDX_DIM ** -0.5), 0.0) * wcol
            return _sel_dot_exact(gsum, sc) * (IDX_HEADS ** -0.5)

        main = idx_scores(*_hi_lo(ki_st[s]), _dot)
        tail = idx_scores(*_hi_lo(kin_ref[s]), _dot_nt)
        sc_scr[:, 0:past] = main if s == 0 else sc_scr[:, 0:past] + main
        sc_scr[:, past:] = tail if s == 0 else sc_scr[:, past:] + tail

    n_keys = n_tiles * page
    n_rows = n_seq * t_new

    def valid(l0):
        kp = l0 + lax.broadcasted_iota(I32, (n_rows, LANES), 1)
        row = lax.broadcasted_iota(I32, (n_rows, LANES), 0)
        tok = row
        for s in range(1, n_seq):
            tok = jnp.where(row >= s * t_new, row - s * t_new, tok)
        return kp <= past + tok

    for c in range(n_tiles):
        sl = slice(c * page, (c + 1) * page)
        key_scr[:, sl] = _sortable_key(jnp.where(valid(c * page), sc_scr[:, sl], NEG_INF))
    _topk_select_rows(key_scr, sel_scr, topk, valid)

    for s in range(n_seq):
        for g in range(N_KV_HEADS):
            rs = slice(g * rows_g, (g + 1) * rows_g)
            cols = slice(g * HEAD_DIM, (g + 1) * HEAD_DIM)
            qg = (q_ref[s, rs, :] * (HEAD_DIM ** -0.5)).astype(BF16)
            lg_scr[rs, 0:past] = _dot(qg, k_st[s, g])
            lg_scr[rs, past:] = _dot_nt(qg, kn_ref[s, :, cols].astype(BF16))
        near = slice(past - page, n_keys)
        lg_scr[:, near] = lg_scr[:, near] + bias_scr[...]
        picked = _dot(xpand_ref[s], sel_scr[...]) > 0.5
        lg = jnp.where(picked, lg_scr[...], NEG_INF)
        m = jnp.max(lg, axis=1, keepdims=True)
        p = jnp.exp(lg - m)
        p_scr[...] = (p * (1.0 / jnp.sum(p, axis=1, keepdims=True))).astype(BF16)
        for g in range(N_KV_HEADS):
            rs = slice(g * rows_g, (g + 1) * rows_g)
            cols = slice(g * HEAD_DIM, (g + 1) * HEAD_DIM)
            acc = _dot_nt(p_scr[rs, 0:past], v_st[s, g]) + _dot(p_scr[rs, past:], vn_ref[s, :, cols].astype(BF16))
            o_ref[s, rs, :] = acc.astype(o_ref.dtype)


def attn_sample(proj_s, cache_k, cache_v, cache_kidx, page_table, bias_table, t_new):
    nb, n_pages = page_table.shape
    n_pool, page = cache_k.shape[0], cache_k.shape[1]
    past = n_pages * page
    total = past + t_new
    topk = min(TOPK_MAX, total // 4)
    kvw = N_KV_HEADS * HEAD_DIM
    n_seq = SAMPLE_SEQS_PER_STEP if nb % SAMPLE_SEQS_PER_STEP == 0 else 1
    rows_q = t_new * N_HEADS
    rows_i = t_new * IDX_HEADS
    n_rows = n_seq * t_new
    n_keys = past + page
    assert page == LANES and page >= MAX_DISTANCE and t_new <= page
    ck_t = jnp.transpose(cache_k, (0, 2, 3, 1))
    cv_t = jnp.transpose(cache_v, (0, 2, 3, 1))
    cki_t = jnp.transpose(cache_kidx, (0, 2, 1))
    seg = lambda col, width: proj_s[:, col:col + width]
    q3 = seg(COL_Q, N_HEADS * HEAD_DIM).reshape(nb, t_new, N_KV_HEADS, Q_PER_KV, HEAD_DIM)
    q3 = q3.transpose(0, 2, 1, 3, 4).reshape(nb, rows_q, HEAD_DIM)
    qi3 = seg(COL_QI, IDX_HEADS * IDX_DIM).reshape(nb, rows_i, IDX_DIM)
    w3 = seg(COL_SM + SM_WI, IDX_HEADS).reshape(nb, rows_i, 1)
    pad_rows = lambda a: jnp.pad(a.reshape(nb, t_new, -1), ((0, 0), (0, page - t_new), (0, 0)))
    kn3, vn3, kin3 = pad_rows(seg(COL_K, kvw)), pad_rows(seg(COL_V, kvw)), pad_rows(seg(COL_KI, IDX_DIM))
    row = np.arange(rows_q)
    row_tok = (row // Q_PER_KV) % t_new
    row_head = (row // (t_new * Q_PER_KV)) * Q_PER_KV + row % Q_PER_KV
    tab_t = jnp.pad(bias_table.T[row_head], ((0, 0), (0, LANES - N_BUCKETS)))
    kpos = np.concatenate([past - page + np.arange(page), past + np.arange(page)])[None, :]
    bucket = jnp.asarray(_t5_bucket_np(past + row_tok[:, None] - kpos))
    gsum = np.zeros((n_seq, n_rows, rows_i), np.float32)
    xpand = np.zeros((n_seq, rows_q, n_rows), np.float32)
    for s in range(n_seq):
        gsum[s, s * t_new + np.arange(rows_i) // IDX_HEADS, np.arange(rows_i)] = 1.0
        xpand[s, row, s * t_new + row_tok] = 1.0
    per_step = lambda shape: pl.BlockSpec((n_seq,) + shape, lambda b, pt: (b,) + tuple(0 for _ in shape))
    const = lambda shape: pl.BlockSpec(shape, lambda b, pt: tuple(0 for _ in shape))

    def page_specs(shape):
        return [pl.BlockSpec((1,) + shape, lambda b, pt, s=s, c=c: (pt[b * n_seq + s, c],) + tuple(0 for _ in shape))
                for s in range(n_seq) for c in range(n_pages)]

    grid_spec = pltpu.PrefetchScalarGridSpec(
        num_scalar_prefetch=1,
        grid=(nb // n_seq,),
        in_specs=[per_step((rows_q, HEAD_DIM)), per_step((rows_i, IDX_DIM)), per_step((rows_i, 1)),
                  per_step((page, kvw)), per_step((page, kvw)), per_step((page, IDX_DIM)),
                  const((rows_q, LANES)), const((rows_q, 2 * page)),
                  const((n_seq, n_rows, rows_i)), const((n_seq, rows_q, n_rows))]
                 + page_specs((N_KV_HEADS, HEAD_DIM, page)) + page_specs((N_KV_HEADS, HEAD_DIM, page))
                 + page_specs((IDX_DIM, page)),
        out_specs=pl.BlockSpec((n_seq, rows_q, HEAD_DIM), lambda b, pt: (b, 0, 0)),
        scratch_shapes=[
            pltpu.VMEM((n_rows, n_keys), F32),
            pltpu.VMEM((n_rows, n_keys), I32),
            pltpu.VMEM((n_rows, n_keys), BF16),
            pltpu.VMEM((rows_q, n_keys), F32),
            pltpu.VMEM((rows_q, n_keys), BF16),
            pltpu.VMEM((rows_q, 2 * page), F32),
            pltpu.VMEM((n_seq, IDX_DIM, past), F32),
            pltpu.VMEM((n_seq, N_KV_HEADS, HEAD_DIM, past), BF16),
            pltpu.VMEM((n_seq, N_KV_HEADS, HEAD_DIM, past), BF16),
        ],
    )
    n_pg = n_seq * n_pages
    out = pl.pallas_call(
        functools.partial(_attn_sample_body, n_seq=n_seq, n_pages=n_pages, page=page, t_new=t_new, topk=topk),
        grid_spec=grid_spec,
        out_shape=jax.ShapeDtypeStruct((nb, rows_q, HEAD_DIM), BF16),
        compiler_params=_cparams(("arbitrary",)),
        name="attn_sample",
    )(page_table, q3, qi3, w3, kn3, vn3, kin3, tab_t, bucket, jnp.asarray(gsum, BF16), jnp.asarray(xpand, BF16),
      *([ck_t] * n_pg), *([cv_t] * n_pg), *([cki_t] * n_pg))
    out = out.reshape(nb, N_KV_HEADS, t_new, Q_PER_KV, HEAD_DIM).transpose(0, 2, 1, 3, 4)
    return out.reshape(nb * t_new, N_HEADS * HEAD_DIM)


def _merge_body(ys_ref, ya_ref, g_ref, h_ref, wbs_ref, wba_ref, wo_ref, lg_ref, lb_ref, o_ref):
    a = _dot(ys_ref[...], wbs_ref[...])
    b = _dot(ya_ref[...], wba_ref[...])
    gates = jax.nn.sigmoid(g_ref[...])
    merged = gates[:, :D_MODEL] * a + gates[:, D_MODEL:] * b
    r = ALPHA * h_ref[...] + _dot(merged.astype(BF16), wo_ref[...])
    o_ref[...] = _layer_norm(r, lg_ref[...], lb_ref[...])


def merge_out(y_ssm, y_att, proj, h1, wbs, wba, wo, g, b, *, tm=512):
    m = h1.shape[0]
    tm = min(tm, m)
    assert m % tm == 0
    row = lambda width, colblk=0: pl.BlockSpec((tm, width), lambda i: (i, colblk))
    full = lambda shape: pl.BlockSpec(shape, lambda i: tuple(0 for _ in shape))
    return pl.pallas_call(
        _merge_body,
        grid=(m // tm,),
        in_specs=[row(D_INNER), row(N_HEADS * HEAD_DIM), row(2 * D_MODEL, COL_G // (2 * D_MODEL)), row(D_MODEL),
                  full(wbs.shape), full(wba.shape), full(wo.shape), full((1, D_MODEL)), full((1, D_MODEL))],
        out_specs=row(D_MODEL),
        out_shape=jax.ShapeDtypeStruct((m, D_MODEL), F32),
        compiler_params=_cparams(("parallel",)),
        name="merge_out",
    )(y_ssm, y_att, proj, h1, wbs, wba, wo, g, b)


def kernel(x_prompt, x_sample, cache_k, cache_v, cache_kidx, state_conv, state_ssm, page_table, bias_table, w_in, conv_w, conv_b, dt_bias, a_log, d_skip, ssm_norm_g, w_branch_ssm, w_branch_attn, w_out, ffn1_wi, ffn1_wo, ffn2_wi, ffn2_wo, ln1_g, ln1_b, ln2_g, ln2_b, ln3_g, ln3_b):
    assert w_in.shape[0] == DEPTH
    batch, seq, d = x_prompt.shape
    nb, t_new, _ = x_sample.shape
    xs = (x_prompt.reshape(batch * seq, d), x_sample.reshape(nb * t_new, d))
    outs_p, outs_s = [], []
    for i in range(DEPTH):
        w_pack = pack_w_in(w_in[i])
        f1i, f1o = ffn1_wi[i].astype(BF16), ffn1_wo[i].astype(BF16)
        f2i, f2o = ffn2_wi[i].astype(BF16), ffn2_wo[i].astype(BF16)
        wbs, wba, wo = w_branch_ssm[i].astype(BF16), w_branch_attn[i].astype(BF16), w_out[i].astype(BF16)
        row = lambda v: v.reshape(1, -1)
        pad_heads = lambda v: jnp.concatenate([v, jnp.zeros((LANES - SSM_HEADS,), v.dtype)]).reshape(1, LANES)
        dtb_pad, alog_pad = pad_heads(dt_bias[i]), pad_heads(a_log[i])
        dskip_e = jnp.repeat(d_skip[i], SSM_HEAD_DIM).reshape(1, D_INNER)
        ng = row(ssm_norm_g[i])
        cw, cb = conv_w[i], row(conv_b[i])

        h1, proj = [], []
        for x in xs:
            hf, hb = ffn_ln(x, f1i, f1o, row(ln1_g[i]), row(ln1_b[i]), emit_bf16=True)
            h1.append(hf)
            proj.append(matmul(hb, w_pack, tm=2048, tn=PROJ_TN))
        proj_p, proj_s = proj

        ys_p, conv_p, ssm_p = ssd_prompt(proj_p, batch, seq, cw, cb, dtb_pad, alog_pad, dskip_e, ng)
        ya_p, (kt_p, vt_p, kit_p) = attn_prompt(proj_p, bias_table, batch, seq)

        raw_s = jnp.concatenate([proj_s[:, COL_X:COL_X + D_INNER], proj_s[:, COL_BC:COL_BC + 2 * GN]], axis=1)
        u_cat = jnp.concatenate([state_conv[i], raw_s.reshape(nb, t_new, CONV_DIM)], axis=1)
        expand = np.zeros((LANES, D_INNER), np.float32)
        expand[np.arange(D_INNER) // SSM_HEAD_DIM, np.arange(D_INNER)] = 1.0
        ys_s, conv_s, ssm_s = ssd_sample(u_cat, proj_s.reshape(nb, t_new, PROJ_N),
                                         state_ssm[i].reshape(nb, SSM_HEADS * SSM_HEAD_DIM, SSM_STATE),
                                         cw, cb, dtb_pad, alog_pad, dskip_e, ng, jnp.asarray(expand, BF16))
        ya_s = attn_sample(proj_s, cache_k[i], cache_v[i], cache_kidx[i], page_table, bias_table, t_new)

        new_xs = []
        for hf, pj, ys, ya in ((h1[0], proj_p, ys_p, ya_p), (h1[1], proj_s, ys_s.reshape(nb * t_new, D_INNER), ya_s)):
            h2 = merge_out(ys, ya, pj, hf, wbs, wba, wo, row(ln2_g[i]), row(ln2_b[i]))
            new_xs.append(ffn_ln(h2, f2i, f2o, row(ln3_g[i]), row(ln3_b[i]), emit_bf16=False))
        xs = tuple(new_xs)

        kvw = N_KV_HEADS * HEAD_DIM
        shp = lambda nb_, l_: (nb_, l_, N_KV_HEADS, HEAD_DIM)
        head_major = lambda a: a.reshape(batch, N_KV_HEADS, HEAD_DIM, seq).transpose(0, 3, 1, 2)
        outs_p.append((head_major(kt_p), head_major(vt_p), kit_p.transpose(0, 2, 1),
                       conv_p, ssm_p.reshape(batch, SSM_HEADS, SSM_HEAD_DIM, SSM_STATE)))
        outs_s.append((proj_s[:, COL_K:COL_K + kvw].reshape(shp(nb, t_new)),
                       proj_s[:, COL_V:COL_V + kvw].reshape(shp(nb, t_new)),
                       proj_s[:, COL_KI:COL_KI + IDX_DIM].reshape(nb, t_new, IDX_DIM),
                       conv_s, ssm_s.reshape(nb, SSM_HEADS, SSM_HEAD_DIM, SSM_STATE)))
    k_p, v_p, kidx_p, conv_pp, ssm_pp = [jnp.stack(a) for a in zip(*outs_p)]
    k_s, v_s, kidx_s, conv_ss, ssm_ss = [jnp.stack(a) for a in zip(*outs_s)]
    return (xs[0].reshape(batch, seq, d), xs[1].reshape(nb, t_new, d),
            k_p, v_p, kidx_p, conv_pp, ssm_pp, k_s, v_s, kidx_s, conv_ss, ssm_ss)
```

```python
import functools
import math

import numpy as np
import jax
import jax.numpy as jnp
from jax import lax
from jax.experimental import pallas as pl
from jax.experimental.pallas import tpu as pltpu

F32 = jnp.float32
BF16 = jnp.bfloat16
I32 = jnp.int32

D_MODEL = 1024
D_INNER = 2 * D_MODEL
SSM_HEAD_DIM = 64
SSM_HEADS = D_INNER // SSM_HEAD_DIM
SSM_GROUPS = 4
SSM_STATE = 128
CONV_W = 4
CONV_DIM = D_INNER + 2 * SSM_GROUPS * SSM_STATE
SSD_CHUNK = 128
N_HEADS = 16
HEAD_DIM = 64
N_KV_HEADS = 4
Q_PER_KV = N_HEADS // N_KV_HEADS
IDX_HEADS = 8
IDX_DIM = 64
TOPK_MAX = 256
N_BUCKETS = 32
MAX_DISTANCE = 128
DEPTH = 1
ALPHA = (2 * DEPTH) ** 0.25
LN_EPS = 1e-5
IN_SPLITS = (D_INNER, CONV_DIM, SSM_HEADS, N_HEADS * HEAD_DIM, N_KV_HEADS * HEAD_DIM,
             N_KV_HEADS * HEAD_DIM, IDX_HEADS * IDX_DIM, IDX_DIM, IDX_HEADS, 2 * D_MODEL)

LANES = 128
SUBLANES = 8
VMEM_LIMIT_BYTES = 56 * 1024 * 1024

GN = SSM_GROUPS * SSM_STATE
COL_Z = 0
COL_X = COL_Z + D_INNER
COL_G = COL_X + D_INNER
COL_BC = COL_G + 2 * D_MODEL
COL_Q = COL_BC + 2 * GN
COL_K = COL_Q + N_HEADS * HEAD_DIM
COL_V = COL_K + N_KV_HEADS * HEAD_DIM
COL_QI = COL_V + N_KV_HEADS * HEAD_DIM
COL_SM = COL_QI + IDX_HEADS * IDX_DIM
COL_KI = COL_SM + LANES
PROJ_TN = 512
PROJ_N = -(-(COL_KI + LANES) // PROJ_TN) * PROJ_TN
SM_WI = SSM_HEADS

INT_MIN = -(2 ** 31)
LOG2E = math.log2(math.e)
NEG_INF = float("-inf")
SAMPLE_SEQS_PER_STEP = 4
SSD_SAMPLE_SEQS_PER_STEP = 8
RADIX_BITS = 4


def _cparams(sem):
    return pltpu.CompilerParams(dimension_semantics=sem, vmem_limit_bytes=VMEM_LIMIT_BYTES)


def _layer_norm(r, g, b):
    mu = jnp.mean(r, axis=-1, keepdims=True)
    d = r - mu
    var = jnp.mean(d * d, axis=-1, keepdims=True)
    return d * lax.rsqrt(var + LN_EPS) * g + b


def _silu(x):
    h = 0.5 * x
    return h + h * jnp.tanh(h)


def _softplus(x):
    return jnp.maximum(x, 0.0) + jnp.log1p(jnp.exp(-jnp.abs(x)))


def _dot(a, b):
    return jnp.dot(a, b, preferred_element_type=F32)


def _dot_nt(a, b):
    return lax.dot_general(a, b, (((1,), (1,)), ((), ())), preferred_element_type=F32)


def _dot_tn(a, b):
    return lax.dot_general(a, b, (((0,), (0,)), ((), ())), preferred_element_type=F32)


def _split3(v):
    hi = v.astype(BF16)
    r1 = v - hi.astype(F32)
    mid = r1.astype(BF16)
    lo = (r1 - mid.astype(F32)).astype(BF16)
    return hi, mid, lo


def _dot_exact_sel(v, sel_bf16):
    hi, mid, lo = _split3(v)
    return (_dot(hi, sel_bf16) + _dot(mid, sel_bf16)) + _dot(lo, sel_bf16)


def _sel_dot_exact(sel_bf16, v):
    hi, mid, lo = _split3(v)
    return (_dot(sel_bf16, hi) + _dot(sel_bf16, mid)) + _dot(sel_bf16, lo)


def _hi_lo(a):
    ah = a.astype(BF16)
    return ah, (a - ah.astype(F32)).astype(BF16)


def _col_reduce(x, op2, op):
    parts = []
    for r0 in range(0, x.shape[0], LANES):
        y = x[r0:r0 + LANES]
        n = y.shape[0]
        while n > SUBLANES and n % (2 * SUBLANES) == 0:
            n //= 2
            y = op2(y[:n], y[n:])
        parts.append(y)
    while len(parts) > 1:
        parts = [op2(parts[i], parts[i + 1]) if i + 1 < len(parts) else parts[i] for i in range(0, len(parts), 2)]
    return op(parts[0], axis=0, keepdims=True)


def _col_sum(x):
    return _col_reduce(x, jnp.add, jnp.sum)


def _col_max(x):
    return _col_reduce(x, jnp.maximum, jnp.max)


def _ffn_ln_body(x_ref, wi_ref, wo_ref, g_ref, b_ref, o_ref, *maybe_ob_ref, tf):
    x = x_ref[...]
    xb = x.astype(BF16)
    dff = wo_ref.shape[0]
    acc = None
    for c0 in range(0, dff, tf):
        gate = _dot(xb, wi_ref[:, c0:c0 + tf])
        up = _dot(xb, wi_ref[:, dff + c0:dff + c0 + tf])
        part = _dot((_silu(gate) * up).astype(BF16), wo_ref[c0:c0 + tf, :])
        acc = part if acc is None else acc + part
    y = _layer_norm(ALPHA * x + 0.5 * acc, g_ref[...], b_ref[...])
    o_ref[...] = y
    for ob_ref in maybe_ob_ref:
        ob_ref[...] = y.astype(BF16)


def ffn_ln(x, wi_bf, wo_bf, g, b, *, emit_bf16, tm=512, tf=2816):
    m, d = x.shape
    dff = wo_bf.shape[0]
    tm = min(tm, m)
    assert m % tm == 0 and dff % tf == 0
    row = pl.BlockSpec((tm, d), lambda i: (i, 0))
    resident = lambda shape: pl.BlockSpec(shape, lambda i: (0, 0), pipeline_mode=pl.Buffered(1))
    n_out = 2 if emit_bf16 else 1
    res = pl.pallas_call(
        functools.partial(_ffn_ln_body, tf=tf),
        grid=(m // tm,),
        in_specs=[row, resident(wi_bf.shape), resident(wo_bf.shape), resident((1, d)), resident((1, d))],
        out_specs=[row] * n_out,
        out_shape=[jax.ShapeDtypeStruct((m, d), F32), jax.ShapeDtypeStruct((m, d), BF16)][:n_out],
        compiler_params=_cparams(("parallel",)),
        name="ffn_ln",
    )(x, wi_bf, wo_bf, g, b)
    return res if emit_bf16 else res[0]


def _matmul_body(x_ref, w_ref, o_ref):
    o_ref[...] = _dot(x_ref[...], w_ref[...])


def matmul(x_bf, w_bf, *, tm, tn):
    m, k = x_bf.shape
    n = w_bf.shape[1]
    tm = min(tm, m)
    assert m % tm == 0 and n % tn == 0
    return pl.pallas_call(
        _matmul_body,
        grid=(m // tm, n // tn),
        in_specs=[pl.BlockSpec((tm, k), lambda i, j: (i, 0)),
                  pl.BlockSpec((k, tn), lambda i, j: (0, j))],
        out_specs=pl.BlockSpec((tm, tn), lambda i, j: (i, j)),
        out_shape=jax.ShapeDtypeStruct((m, n), F32),
        compiler_params=_cparams(("parallel", "arbitrary")),
        name="in_proj",
    )(x_bf, w_bf)


def pack_w_in(w_in):
    offs = np.cumsum((0,) + IN_SPLITS)
    z, xbc, dt, q, k, v, qi, ki, wi, gates = [w_in[:, offs[i]:offs[i + 1]] for i in range(len(IN_SPLITS))]
    d = w_in.shape[0]
    zeros = lambda n: jnp.zeros((d, n), w_in.dtype)
    small = jnp.concatenate([dt, wi, zeros(LANES - SSM_HEADS - IDX_HEADS)], axis=1)
    kib = jnp.concatenate([ki, ki], axis=1)
    packed = jnp.concatenate([z, xbc[:, :D_INNER], gates, xbc[:, D_INNER:], q, k, v, qi, small, kib,
                              zeros(PROJ_N - COL_KI - LANES)], axis=1)
    return packed.astype(BF16)


def _ssd_prompt_body(z_ref, xr_ref, bc_ref, sm_ref, cwx_ref, cwbc_ref, cbx_ref, cbbc_ref,
                     dtb_ref, alog_ref, dskip_ref, ng_ref,
                     y_ref, cs_ref, hs_ref,
                     xbuf, bcbuf, h_scr, xc_scr, bcc_scr, y_scr):
    c = pl.program_id(1)
    nc = pl.num_programs(1)
    L = SSD_CHUNK
    halo = SUBLANES
    n_pairs = SSM_HEADS // 2
    pairs_per_group = n_pairs // SSM_GROUPS

    @pl.when(c == 0)
    def _():
        xbuf[...] = jnp.zeros_like(xbuf)
        bcbuf[...] = jnp.zeros_like(bcbuf)
        h_scr[...] = jnp.zeros_like(h_scr)

    assert CONV_W == 4

    def conv(halo_ref, x_ref, w_ref, b_ref):
        full = jnp.concatenate([halo_ref[...], x_ref[...]], axis=0)
        prev = pltpu.roll(full, 1, axis=0)
        late = (w_ref[1:2, :] * full + w_ref[0:1, :] * prev)
        early = w_ref[3:4, :] * full[halo:] + w_ref[2:3, :] * prev[halo:]
        acc = (b_ref[...] + early) + pltpu.roll(late, 2, axis=0)[halo:]
        halo_ref[...] = x_ref[L - halo:L, :]
        return _silu(acc)

    xc_scr[...] = conv(xbuf, xr_ref, cwx_ref, cbx_ref)
    bcc_scr[...] = conv(bcbuf, bc_ref, cwbc_ref, cbbc_ref)

    dt = _softplus(sm_ref[...] + dtb_ref[...])
    a = -jnp.exp(alog_ref[...])
    row = lax.broadcasted_iota(I32, (L, L), 0)
    col = lax.broadcasted_iota(I32, (L, L), 1)
    tril = (row >= col).astype(F32)
    cum = jnp.dot(tril, dt * a, precision=lax.Precision.HIGHEST, preferred_element_type=F32)
    w = jnp.exp(cum[L - 1:L, :] - cum) * dt
    cum_t = cum.T
    dt_t = dt.T
    causal = row >= col
    lane = lax.broadcasted_iota(I32, (L, LANES), 1)
    lo_half = lane < SSM_HEAD_DIM
    sub = lax.broadcasted_iota(I32, (2 * SSM_HEAD_DIM, SSM_STATE), 0)
    lo_rows = sub < SSM_HEAD_DIM

    for i in range(n_pairs):
        g = i // pairs_per_group
        h0, h1 = 2 * i, 2 * i + 1
        sl = slice(i * LANES, (i + 1) * LANES)
        if i % pairs_per_group == 0:
            bg = bcc_scr[:, g * SSM_STATE:(g + 1) * SSM_STATE].astype(BF16)
            cg = bcc_scr[:, GN + g * SSM_STATE:GN + (g + 1) * SSM_STATE].astype(BF16)
            cb = _dot_nt(cg, bg)
        xp = xc_scr[:, sl]
        c0 = jnp.broadcast_to(cum[:, h0:h0 + 1], (L, LANES))
        c1 = jnp.broadcast_to(cum[:, h1:h1 + 1], (L, LANES))
        w0 = jnp.broadcast_to(w[:, h0:h0 + 1], (L, LANES))
        w1 = jnp.broadcast_to(w[:, h1:h1 + 1], (L, LANES))

        def mix(ct, h):
            seg = jnp.exp(jnp.where(causal, ct - cum_t[h:h + 1, :], NEG_INF))
            return (cb * seg * dt_t[h:h + 1, :]).astype(BF16)

        zero = jnp.zeros_like(xp)
        yd = (_dot(mix(c0, h0), jnp.where(lo_half, xp, zero).astype(BF16))
              + _dot(mix(c1, h1), jnp.where(lo_half, zero, xp).astype(BF16)))
        hp = h_scr[i]
        yo = _dot_nt(cg, hp.astype(BF16)) * jnp.where(lo_half, jnp.exp(c0), jnp.exp(c1))
        y_scr[:, sl] = yd + yo + dskip_ref[:, sl] * xp
        xw = (xp * jnp.where(lo_half, w0, w1)).astype(BF16)
        st = _dot_tn(xw, bg)
        dec = jnp.where(lo_rows, jnp.exp(c0[L - 1:L, :]), jnp.exp(c1[L - 1:L, :]))
        h_scr[i] = dec * hp + st

    gw = D_INNER // SSM_GROUPS
    for g in range(SSM_GROUPS):
        sl = slice(g * gw, (g + 1) * gw)
        u = y_scr[:, sl] * _silu(z_ref[:, sl])
        ms = jnp.mean(u * u, axis=-1, keepdims=True)
        y_ref[:, sl] = (u * lax.rsqrt(ms + LN_EPS) * ng_ref[:, sl]).astype(y_ref.dtype)

    @pl.when(c == nc - 1)
    def _():
        cs_ref[0, :, 0:D_INNER] = xr_ref[L - (CONV_W - 1):L, :]
        cs_ref[0, :, D_INNER:CONV_DIM] = bc_ref[L - (CONV_W - 1):L, :]
        for i in range(n_pairs):
            hs_ref[0, i * LANES:(i + 1) * LANES, :] = h_scr[i]


def ssd_prompt(proj, batch, seq, conv_w, conv_b, dtb_pad, alog_pad, dskip_e, norm_g):
    L = SSD_CHUNK
    nc = seq // L
    assert seq % L == 0
    blk = lambda width, colblk: pl.BlockSpec((L, width), lambda b, c: (b * nc + c, colblk))
    full = lambda shape: pl.BlockSpec(shape, lambda b, c: tuple(0 for _ in shape))
    n_pairs = SSM_HEADS // 2
    return pl.pallas_call(
        _ssd_prompt_body,
        grid=(batch, nc),
        in_specs=[
            blk(D_INNER, COL_Z // D_INNER),
            blk(D_INNER, COL_X // D_INNER),
            blk(2 * GN, COL_BC // (2 * GN)),
            blk(LANES, COL_SM // LANES),
            full((CONV_W, D_INNER)), full((CONV_W, 2 * GN)), full((1, D_INNER)), full((1, 2 * GN)),
            full((1, LANES)), full((1, LANES)), full((1, D_INNER)), full((1, D_INNER)),
        ],
        out_specs=[
            pl.BlockSpec((L, D_INNER), lambda b, c: (b * nc + c, 0)),
            pl.BlockSpec((1, CONV_W - 1, CONV_DIM), lambda b, c: (b, 0, 0)),
            pl.BlockSpec((1, SSM_HEADS * SSM_HEAD_DIM, SSM_STATE), lambda b, c: (b, 0, 0)),
        ],
        out_shape=[
            jax.ShapeDtypeStruct((batch * seq, D_INNER), BF16),
            jax.ShapeDtypeStruct((batch, CONV_W - 1, CONV_DIM), F32),
            jax.ShapeDtypeStruct((batch, SSM_HEADS * SSM_HEAD_DIM, SSM_STATE), F32),
        ],
        scratch_shapes=[
            pltpu.VMEM((SUBLANES, D_INNER), F32),
            pltpu.VMEM((SUBLANES, 2 * GN), F32),
            pltpu.VMEM((n_pairs, 2 * SSM_HEAD_DIM, SSM_STATE), F32),
            pltpu.VMEM((L, D_INNER), F32),
            pltpu.VMEM((L, 2 * GN), F32),
            pltpu.VMEM((L, D_INNER), F32),
        ],
        compiler_params=_cparams(("parallel", "arbitrary")),
        name="ssd_prompt",
    )(proj, proj, proj, proj, conv_w[:, :D_INNER], conv_w[:, D_INNER:], conv_b[:, :D_INNER], conv_b[:, D_INNER:],
      dtb_pad, alog_pad, dskip_e, norm_g)


def _ssd_sample_body(u_ref, z_ref, sm_ref, h0_ref, cw_ref, cb_ref, dtb_ref, alog_ref, dskip_ref, ng_ref, exp_ref,
                     y_ref, cs_ref, hn_ref, *, t_new, n_seq):
    for s in range(n_seq):
        _ssd_sample_one(s, u_ref, z_ref, sm_ref, h0_ref, cw_ref, cb_ref, dtb_ref, alog_ref, dskip_ref, ng_ref, exp_ref,
                        y_ref, cs_ref, hn_ref, t_new)


def _ssd_sample_one(s, u_ref, z_ref, sm_ref, h0_ref, cw_ref, cb_ref, dtb_ref, alog_ref, dskip_ref, ng_ref, exp_ref,
                    y_ref, cs_ref, hn_ref, t_new):
    T = t_new
    n_pairs = SSM_HEADS // 2
    pairs_per_group = n_pairs // SSM_GROUPS
    acc = cb_ref[...] + cw_ref[0:1, :] * u_ref[s, 0:T, :]
    for k in range(1, CONV_W):
        acc = acc + cw_ref[k:k + 1, :] * u_ref[s, k:k + T, :]
    xbc = _silu(acc)
    cs_ref[s] = u_ref[s, T:T + CONV_W - 1, :]
    x = xbc[:, :D_INNER]

    dt = _softplus(sm_ref[s] + dtb_ref[...])
    da = dt * (-jnp.exp(alog_ref[...]))
    rows = [da[0:1, :]]
    for t in range(1, T):
        rows.append(rows[-1] + da[t:t + 1, :])
    cum = jnp.concatenate(rows, axis=0)
    both = _dot_exact_sel(jnp.concatenate([cum, dt], axis=0), exp_ref[...])
    cum_e = both[0:T, :]
    dt_e = both[T:2 * T, :]
    t_idx = lax.broadcasted_iota(I32, (T, GN), 0)

    gw = D_INNER // SSM_GROUPS
    y_parts = []
    for g in range(SSM_GROUPS):
        sl = slice(g * gw, (g + 1) * gw)
        bg = xbc[:, D_INNER + g * SSM_STATE:D_INNER + (g + 1) * SSM_STATE]
        cg = xbc[:, D_INNER + GN + g * SSM_STATE:D_INNER + GN + (g + 1) * SSM_STATE]
        cbg = _dot_nt(cg.astype(BF16), bg.astype(BF16))
        yg = dskip_ref[:, sl] * x[:, sl]
        for s_tok in range(T):
            seg = jnp.exp(jnp.where(t_idx >= s_tok, cum_e[:, sl] - cum_e[s_tok:s_tok + 1, sl], NEG_INF))
            coef = seg * dt_e[s_tok:s_tok + 1, sl] * jnp.broadcast_to(cbg[:, s_tok:s_tok + 1], (T, gw))
            yg = yg + coef * x[s_tok:s_tok + 1, sl]
        y_parts.append(yg)

    xw = x * dt_e * jnp.exp(cum_e[T - 1:T, :] - cum_e)
    ecum = jnp.exp(cum_e)
    sub = lax.broadcasted_iota(I32, (2 * SSM_HEAD_DIM, SSM_STATE), 0)
    lo_rows = sub < SSM_HEAD_DIM
    yo_parts = []
    for i in range(n_pairs):
        g = i // pairs_per_group
        sl = slice(i * LANES, (i + 1) * LANES)
        bg = xbc[:, D_INNER + g * SSM_STATE:D_INNER + (g + 1) * SSM_STATE].astype(BF16)
        cg = xbc[:, D_INNER + GN + g * SSM_STATE:D_INNER + GN + (g + 1) * SSM_STATE].astype(BF16)
        hp = h0_ref[s, i * LANES:(i + 1) * LANES, :]
        yo_parts.append(_dot_nt(cg, hp.astype(BF16)) * ecum[:, sl])
        st = _dot_tn(xw[:, sl].astype(BF16), bg)
        d0 = jnp.exp(cum[T - 1:T, 2 * i:2 * i + 1])
        d1 = jnp.exp(cum[T - 1:T, 2 * i + 1:2 * i + 2])
        dec = jnp.where(lo_rows, jnp.broadcast_to(d0, sub.shape), jnp.broadcast_to(d1, sub.shape))
        hn_ref[s, i * LANES:(i + 1) * LANES, :] = dec * hp + st

    for g in range(SSM_GROUPS):
        sl = slice(g * gw, (g + 1) * gw)
        yo = jnp.concatenate(yo_parts[g * pairs_per_group:(g + 1) * pairs_per_group], axis=1)
        u = (y_parts[g] + yo) * _silu(z_ref[s, :, sl])
        ms = jnp.mean(u * u, axis=-1, keepdims=True)
        y_ref[s, :, sl] = (u * lax.rsqrt(ms + LN_EPS) * ng_ref[:, sl]).astype(y_ref.dtype)


def ssd_sample(u_cat, proj3, h0, conv_w, conv_b, dtb_pad, alog_pad, dskip_e, norm_g, expand):
    nb, t_new, _ = proj3.shape
    full = lambda shape: pl.BlockSpec(shape, lambda b: tuple(0 for _ in shape))
    hp = SSM_HEADS * SSM_HEAD_DIM
    n_seq = SSD_SAMPLE_SEQS_PER_STEP if nb % SSD_SAMPLE_SEQS_PER_STEP == 0 else 1
    return pl.pallas_call(
        functools.partial(_ssd_sample_body, t_new=t_new, n_seq=n_seq),
        grid=(nb // n_seq,),
        in_specs=[
            pl.BlockSpec((n_seq, t_new + CONV_W - 1, CONV_DIM), lambda b: (b, 0, 0)),
            pl.BlockSpec((n_seq, t_new, D_INNER), lambda b: (b, 0, COL_Z // D_INNER)),
            pl.BlockSpec((n_seq, t_new, LANES), lambda b: (b, 0, COL_SM // LANES)),
            pl.BlockSpec((n_seq, hp, SSM_STATE), lambda b: (b, 0, 0)),
            full((CONV_W, CONV_DIM)), full((1, CONV_DIM)), full((1, LANES)), full((1, LANES)),
            full((1, D_INNER)), full((1, D_INNER)), full((LANES, D_INNER)),
        ],
        out_specs=[
            pl.BlockSpec((n_seq, t_new, D_INNER), lambda b: (b, 0, 0)),
            pl.BlockSpec((n_seq, CONV_W - 1, CONV_DIM), lambda b: (b, 0, 0)),
            pl.BlockSpec((n_seq, hp, SSM_STATE), lambda b: (b, 0, 0)),
        ],
        out_shape=[
            jax.ShapeDtypeStruct((nb, t_new, D_INNER), BF16),
            jax.ShapeDtypeStruct((nb, CONV_W - 1, CONV_DIM), F32),
            jax.ShapeDtypeStruct((nb, hp, SSM_STATE), F32),
        ],
        compiler_params=_cparams(("parallel",)),
        name="ssd_sample",
    )(u_cat, proj3, proj3, h0, conv_w, conv_b, dtb_pad, alog_pad, dskip_e, norm_g, expand)


def _sortable_key(score):
    score = jnp.where(score == 0.0, 0.0, score)
    bits = lax.bitcast_convert_type(score, I32)
    return bits ^ ((bits >> 31) & jnp.int32(0x7FFFFFFF))


def _t5_bucket_np(rel):
    n = np.maximum(rel, 0)
    max_exact = N_BUCKETS // 2
    nf = np.maximum(n, 1).astype(np.float32)
    large = max_exact + (np.log(nf / max_exact) / math.log(MAX_DISTANCE / max_exact)
                         * (N_BUCKETS - max_exact)).astype(np.int32)
    large = np.minimum(large, N_BUCKETS - 1)
    return np.where(n < max_exact, n, large).astype(np.int32)


def _count_cols_i16(ref, n_rows, pred):
    packed_rows = 2 * SUBLANES
    parts = []
    for r0 in range(0, n_rows, LANES):
        x = jnp.where(pred(ref[r0:r0 + LANES, :]), jnp.int16(1), jnp.int16(0))
        n = x.shape[0]
        while n > packed_rows:
            n //= 2
            x = x[:n] + x[n:]
        parts.append(x)
    while len(parts) > 1:
        parts = [parts[i] + parts[i + 1] if i + 1 < len(parts) else parts[i] for i in range(0, len(parts), 2)]
    return jnp.sum(parts[0].astype(I32), axis=0, keepdims=True)


def _kth_largest_i16(ref, n_rows, k):
    i16_min = -(2 ** 15)

    def try_cand(r, cand):
        cnt = _count_cols_i16(ref, n_rows, lambda v: v >= cand.astype(jnp.int16))
        return jnp.where(cnt >= k, cand, r)

    r0 = try_cand(jnp.full((1, LANES), i16_min, I32), jnp.zeros((1, LANES), I32))
    return lax.fori_loop(0, 15, lambda it, r: try_cand(r, r + (jnp.int32(1) << (14 - it))), r0)


def _kth_largest_cols(key_ref, hi_ref, lo_ref, n_rows, k):
    i16_min = -(2 ** 15)
    for r0 in range(0, n_rows, LANES):
        key = key_ref[r0:r0 + LANES, :]
        hi_ref[r0:r0 + LANES, :] = (key >> 16).astype(jnp.int16)
        lo_ref[r0:r0 + LANES, :] = ((key & 0xFFFF) + i16_min).astype(jnp.int16)
    t_hi = _kth_largest_i16(hi_ref, n_rows, k)
    t_hi16 = t_hi.astype(jnp.int16)
    k_lo = k - _count_cols_i16(hi_ref, n_rows, lambda v: v > t_hi16)
    for r0 in range(0, n_rows, LANES):
        sl = slice(r0, r0 + LANES)
        lo_ref[sl, :] = jnp.where(hi_ref[sl, :] == t_hi16, lo_ref[sl, :], jnp.int16(i16_min))
    t_lo = _kth_largest_i16(lo_ref, n_rows, k_lo)
    return (t_hi << 16) | ((t_lo - i16_min) & 0xFFFF)


def _topk_mask_cols(key_ref, hi_ref, lo_ref, msk_ref, n_rows, k, valid_fn):
    thr = _kth_largest_cols(key_ref, hi_ref, lo_ref, n_rows, k)
    n_gt = _col_sum(jnp.where(key_ref[0:n_rows, :] > thr, 1.0, 0.0))
    need = k - n_gt
    run = jnp.zeros((1, LANES), F32)
    ri = lax.broadcasted_iota(I32, (LANES, LANES), 0)
    ci = lax.broadcasted_iota(I32, (LANES, LANES), 1)
    strict = (ri > ci).astype(BF16)
    for r0 in range(0, n_rows, LANES):
        kc = key_ref[r0:r0 + LANES, :]
        eq = kc == thr
        eqf = eq.astype(F32)
        before = _dot(strict, eqf.astype(BF16)) + run
        sel = ((kc > thr) | (eq & (before < need))) & valid_fn(r0, LANES)
        msk_ref[r0:r0 + LANES, :] = jnp.where(sel, 0.0, NEG_INF)
        run = run + jnp.sum(eqf, axis=0, keepdims=True)


def _kth_largest_rows(key_ref, k):
    rows = key_ref.shape[0]
    digits = 2 ** RADIX_BITS
    n_rounds = 32 // RADIX_BITS

    def body(rnd, r):
        shift = (32 - RADIX_BITS) - RADIX_BITS * rnd
        n_ok = jnp.zeros((rows, 1), I32)
        for i in range(1, digits):
            cand = r + (jnp.int32(i) << shift)
            cnt = jnp.sum((key_ref[...] >= cand).astype(I32), axis=1, keepdims=True)
            n_ok = n_ok + (cnt >= k).astype(I32)
        return r + (n_ok << shift)

    return lax.fori_loop(0, n_rounds, body, jnp.full((rows, 1), INT_MIN, I32))


def _topk_select_rows(key_ref, sel_ref, k, valid_fn):
    n_keys = key_ref.shape[1]
    thr = _kth_largest_rows(key_ref, k)
    n_gt = jnp.sum((key_ref[...] > thr).astype(I32), axis=1, keepdims=True)
    need = (k - n_gt).astype(F32)
    run = jnp.zeros((key_ref.shape[0], 1), F32)
    ri = lax.broadcasted_iota(I32, (LANES, LANES), 0)
    ci = lax.broadcasted_iota(I32, (LANES, LANES), 1)
    strict = (ri < ci).astype(BF16)
    for l0 in range(0, n_keys, LANES):
        kc = key_ref[:, l0:l0 + LANES]
        eq = kc == thr
        eqf = eq.astype(F32)
        before = _dot(eqf.astype(BF16), strict) + run
        sel = ((kc > thr) | (eq & (before < need))) & valid_fn(l0)
        sel_ref[:, l0:l0 + LANES] = sel.astype(F32).astype(BF16)
        run = run + jnp.sum(eqf, axis=1, keepdims=True)


def _bias_tiles_body(tab_ref, bucket_ref, o_ref):
    for d in range(bucket_ref.shape[0]):
        bk = bucket_ref[d]
        for h in range(N_HEADS):
            acc = jnp.zeros(bk.shape, F32)
            for bb in range(N_BUCKETS):
                acc = jnp.where(bk == bb, tab_ref[bb, h], acc)
            o_ref[h, d] = (acc - tab_ref[N_BUCKETS - 1, h]) * LOG2E


def _attn_prompt_body(bias_ref, q_ref, k_ref, v_ref, qi_ref, ki_ref, sm_ref, *rest, j, topk, emit_transposed):
    key_scr, hi_scr, lo_scr, msk_scr, ot_scr, ka_scr = rest[-6:]
    QB = LANES
    nk = (j + 1) * QB
    if emit_transposed:
        k32_ref, v32_ref, o_ref, kt_ref, vt_ref, kit_ref = rest[-12:-6]
        kt_ref[0] = k32_ref[...].T
        vt_ref[0] = v32_ref[...].T
        kit_ref[0] = ki_ref[...].T[0:IDX_DIM, :]
    else:
        o_ref = rest[-7]

    def valid(r0, rows):
        kp = r0 + lax.broadcasted_iota(I32, (rows, QB), 0)
        qp = j * QB + lax.broadcasted_iota(I32, (rows, QB), 1)
        return kp <= qp

    if nk <= topk:
        msk_scr[...] = jnp.where(valid(0, nk), 0.0, NEG_INF)
    else:
        x = ki_ref[0:nk, :]
        hi, lo = _hi_lo(x)
        first = lax.broadcasted_iota(I32, (nk, LANES), 1) < IDX_DIM
        ka_scr[:, 0:LANES] = jnp.where(first, hi, lo)
        ka_scr[:, LANES:2 * LANES] = jnp.where(first, hi, jnp.zeros_like(hi))

        wt = sm_ref[...].T
        ka = ka_scr[...]
        score = jnp.zeros((nk, QB), F32)
        pad = jnp.zeros((QB, 2 * LANES - 3 * IDX_DIM), BF16)
        for h2 in range(IDX_HEADS // 2):
            blocks = []
            for h in (2 * h2, 2 * h2 + 1):
                qh, ql = _hi_lo(qi_ref[:, h * IDX_DIM:(h + 1) * IDX_DIM] * (IDX_DIM ** -0.5))
                blocks.append(jnp.concatenate([ql, qh, qh, pad], axis=1))
            s = _dot_nt(ka, jnp.concatenate(blocks, axis=0))
            for i, h in enumerate((2 * h2, 2 * h2 + 1)):
                score = score + jnp.maximum(s[:, i * QB:(i + 1) * QB], 0.0) * wt[SM_WI + h:SM_WI + h + 1, :]
        score = jnp.where(valid(0, nk), score * (IDX_HEADS ** -0.5), NEG_INF)
        key_scr[...] = _sortable_key(score)
        _topk_mask_cols(key_scr, hi_scr, lo_scr, msk_scr, nk, topk, valid)

    near0 = max(j - 1, 0) * QB
    for g in range(N_KV_HEADS):
        kcols = slice(g * HEAD_DIM, (g + 1) * HEAD_DIM)
        vcols = slice(g * LANES, (g + 1) * LANES)
        heads = range(g * Q_PER_KV, (g + 1) * Q_PER_KV)
        qg = jnp.concatenate([(q_ref[:, h * HEAD_DIM:(h + 1) * HEAD_DIM] * (HEAD_DIM ** -0.5 * LOG2E)).astype(BF16)
                              for h in heads], axis=0)
        lg_all = _dot_nt(k_ref[0:nk, kcols], qg)
        probs = []
        for r, h in enumerate(heads):
            lg = lg_all[:, r * QB:(r + 1) * QB] + msk_scr[...]
            bias = bias_ref[h, 0] if j == 0 else jnp.concatenate([bias_ref[h, 1], bias_ref[h, 0]], axis=0)
            lg = jnp.concatenate([lg[:near0], lg[near0:] + bias], axis=0) if near0 else lg + bias
            probs.append(jnp.exp2(lg - _col_max(lg)).astype(BF16))
        pv = _dot_tn(v_ref[0:nk, vcols], jnp.concatenate(probs, axis=1))
        for r, h in enumerate(heads):
            sl = slice(r * QB, (r + 1) * QB)
            ot_scr[h * HEAD_DIM:(h + 1) * HEAD_DIM, :] = pv[0:HEAD_DIM, sl] * (1.0 / pv[HEAD_DIM:HEAD_DIM + 1, sl])
    o_ref[...] = ot_scr[...].T.astype(o_ref.dtype)


def attn_prompt(proj, bias_table, batch, seq):
    QB = LANES
    nq = seq // QB
    assert seq % QB == 0
    topk = min(TOPK_MAX, seq // 4)
    ts = np.arange(QB)[:, None]
    tq = np.arange(QB)[None, :]
    bucket = jnp.asarray(np.stack([_t5_bucket_np(tq - ts), _t5_bucket_np(QB + tq - ts)]))
    kvw = N_KV_HEADS * HEAD_DIM
    kv32 = lax.optimization_barrier(proj[:, COL_K:COL_K + 2 * kvw])
    k_bf = kv32[:, :kvw].astype(BF16)
    v4 = kv32[:, kvw:].astype(BF16).reshape(batch * seq, N_KV_HEADS, HEAD_DIM)
    v_aug = jnp.concatenate([v4, jnp.ones(v4.shape[:2] + (1,), BF16),
                             jnp.zeros(v4.shape[:2] + (LANES - HEAD_DIM - 1,), BF16)], axis=-1)
    v_aug = v_aug.reshape(batch * seq, N_KV_HEADS * LANES)
    bias = pl.pallas_call(
        _bias_tiles_body,
        in_specs=[pl.BlockSpec(memory_space=pltpu.SMEM), pl.BlockSpec(memory_space=pltpu.VMEM)],
        out_specs=pl.BlockSpec(memory_space=pltpu.VMEM),
        out_shape=jax.ShapeDtypeStruct((N_HEADS, 2, QB, QB), F32),
        name="bias_tiles",
    )(bias_table, bucket)
    out = jnp.zeros((batch * seq, N_HEADS * HEAD_DIM), BF16)
    for j in range(nq):
        nk = (j + 1) * QB
        rows = min(d for d in range(nk, seq + 1, QB) if seq % d == 0)
        rowblk = lambda width, colblk: pl.BlockSpec((QB, width), lambda b: (b * nq + j, colblk))
        keyblk = lambda width, colblk: pl.BlockSpec((rows, width), lambda b: (b * (seq // rows), colblk))
        last = j == nq - 1
        extra_in = [keyblk(kvw, COL_K // kvw), keyblk(kvw, COL_V // kvw)] if last else []
        tblk = lambda r: pl.BlockSpec((1, r, seq), lambda b: (b, 0, 0))
        tshape = lambda r: jax.ShapeDtypeStruct((batch, r, seq), F32)
        res = pl.pallas_call(
            functools.partial(_attn_prompt_body, j=j, topk=topk, emit_transposed=last),
            grid=(batch,),
            in_specs=[
                pl.BlockSpec((N_HEADS, 2, QB, QB), lambda b: (0, 0, 0, 0)),
                rowblk(N_HEADS * HEAD_DIM, COL_Q // (N_HEADS * HEAD_DIM)),
                keyblk(kvw, 0),
                keyblk(N_KV_HEADS * LANES, 0),
                rowblk(IDX_HEADS * IDX_DIM, COL_QI // (IDX_HEADS * IDX_DIM)),
                keyblk(LANES, COL_KI // LANES),
                rowblk(LANES, COL_SM // LANES),
                pl.BlockSpec(memory_space=pl.ANY),
            ] + extra_in,
            out_specs=[pl.BlockSpec((QB, N_HEADS * HEAD_DIM), lambda b: (b * nq + j, 0))]
                      + ([tblk(kvw), tblk(kvw), tblk(IDX_DIM)] if last else []),
            out_shape=[jax.ShapeDtypeStruct((batch * seq, N_HEADS * HEAD_DIM), BF16)]
                      + ([tshape(kvw), tshape(kvw), tshape(IDX_DIM)] if last else []),
            input_output_aliases={7: 0},
            scratch_shapes=[
                pltpu.VMEM((nk, QB), I32),
                pltpu.VMEM((nk, QB), jnp.int16),
                pltpu.VMEM((nk, QB), jnp.int16),
                pltpu.VMEM((nk, QB), F32),
                pltpu.VMEM((N_HEADS * HEAD_DIM, QB), F32),
                pltpu.VMEM((nk, 2 * LANES), BF16),
            ],
            compiler_params=_cparams(("parallel",)),
            name=f"attn_prompt_q{j}",
        )(bias, proj, k_bf, v_aug, proj, proj, proj, out, *([proj, proj] if last else []))
        out = res[0]
    return out, res[1:]


def _attn_sample_body(pt_ref, q_ref, qi_ref, w_ref, kn_ref, vn_ref, kin_ref, tabt_ref, bucket_ref, gsum_ref, xpand_ref,
                      *rest, n_seq, n_pages, page, t_new, topk):
    n_pg = n_seq * n_pages
    ck = rest[0:n_pg]
    cv = rest[n_pg:2 * n_pg]
    cki = rest[2 * n_pg:3 * n_pg]
    o_ref, sc_scr, key_scr, sel_scr, lg_scr, p_scr, bias_scr, ki_st, k_st, v_st = rest[3 * n_pg:]
    past = n_pages * page
    n_tiles = n_pages + 1
    rows_q = t_new * N_HEADS
    rows_g = t_new * Q_PER_KV

    @pl.when(pl.program_id(0) == 0)
    def _():
        bk = bucket_ref[...]
        acc = jnp.zeros(bk.shape, F32)
        for bb in range(N_BUCKETS):
            acc = jnp.where(bk == bb, tabt_ref[:, bb:bb + 1], acc)
        bias_scr[...] = acc - tabt_ref[:, N_BUCKETS - 1:N_BUCKETS]

    for s in range(n_seq):
        for c in range(n_pages):
            sl = slice(c * page, (c + 1) * page)
            ki_st[s, :, sl] = cki[s * n_pages + c][0]
            k_st[s, :, :, sl] = ck[s * n_pages + c][0].astype(BF16)
            v_st[s, :, :, sl] = cv[s * n_pages + c][0].astype(BF16)

    for s in range(n_seq):
        qh, ql = _hi_lo(qi_ref[s])
        wcol = w_ref[s]
        gsum = gsum_ref[s]

        def idx_scores(kt_h, kt_l, dot):
            sc = (dot(qh, kt_l) + dot(ql, kt_h)) + dot(qh, kt_h)
            sc = jnp.maximum(sc * (IDX_DIM ** -0.5), 0.0) * wcol
            return _sel_dot_exact(gsum, sc) * (IDX_HEADS ** -0.5)

        main = idx_scores(*_hi_lo(ki_st[s]), _dot)
        tail = idx_scores(*_hi_lo(kin_ref[s]), _dot_nt)
        sc_scr[:, 0:past] = main if s == 0 else sc_scr[:, 0:past] + main
        sc_scr[:, past:] = tail if s == 0 else sc_scr[:, past:] + tail

    n_keys = n_tiles * page
    n_rows = n_seq * t_new

    def valid(l0):
        kp = l0 + lax.broadcasted_iota(I32, (n_rows, LANES), 1)
        row = lax.broadcasted_iota(I32, (n_rows, LANES), 0)
        tok = row
        for s in range(1, n_seq):
            tok = jnp.where(row >= s * t_new, row - s * t_new, tok)
        return kp <= past + tok

    for c in range(n_tiles):
        sl = slice(c * page, (c + 1) * page)
        key_scr[:, sl] = _sortable_key(jnp.where(valid(c * page), sc_scr[:, sl], NEG_INF))
    _topk_select_rows(key_scr, sel_scr, topk, valid)

    for s in range(n_seq):
        for g in range(N_KV_HEADS):
            rs = slice(g * rows_g, (g + 1) * rows_g)
            cols = slice(g * HEAD_DIM, (g + 1) * HEAD_DIM)
            qg = (q_ref[s, rs, :] * (HEAD_DIM ** -0.5)).astype(BF16)
            lg_scr[rs, 0:past] = _dot(qg, k_st[s, g])
            lg_scr[rs, past:] = _dot_nt(qg, kn_ref[s, :, cols].astype(BF16))
        near = slice(past - page, n_keys)
        lg_scr[:, near] = lg_scr[:, near] + bias_scr[...]
        picked = _dot(xpand_ref[s], sel_scr[...]) > 0.5
        lg = jnp.where(picked, lg_scr[...], NEG_INF)
        m = jnp.max(lg, axis=1, keepdims=True)
        p = jnp.exp(lg - m)
        p_scr[...] = (p * (1.0 / jnp.sum(p, axis=1, keepdims=True))).astype(BF16)
        for g in range(N_KV_HEADS):
            rs = slice(g * rows_g, (g + 1) * rows_g)
            cols = slice(g * HEAD_DIM, (g + 1) * HEAD_DIM)
            acc = _dot_nt(p_scr[rs, 0:past], v_st[s, g]) + _dot(p_scr[rs, past:], vn_ref[s, :, cols].astype(BF16))
            o_ref[s, rs, :] = acc.astype(o_ref.dtype)


def attn_sample(proj_s, cache_k, cache_v, cache_kidx, page_table, bias_table, t_new):
    nb, n_pages = page_table.shape
    n_pool, page = cache_k.shape[0], cache_k.shape[1]
    past = n_pages * page
    total = past + t_new
    topk = min(TOPK_MAX, total // 4)
    kvw = N_KV_HEADS * HEAD_DIM
    n_seq = SAMPLE_SEQS_PER_STEP if nb % SAMPLE_SEQS_PER_STEP == 0 else 1
    rows_q = t_new * N_HEADS
    rows_i = t_new * IDX_HEADS
    n_rows = n_seq * t_new
    n_keys = past + page
    assert page == LANES and page >= MAX_DISTANCE and t_new <= page
    ck_t = jnp.transpose(cache_k, (0, 2, 3, 1))
    cv_t = jnp.transpose(cache_v, (0, 2, 3, 1))
    cki_t = jnp.transpose(cache_kidx, (0, 2, 1))
    seg = lambda col, width: proj_s[:, col:col + width]
    q3 = seg(COL_Q, N_HEADS * HEAD_DIM).reshape(nb, t_new, N_KV_HEADS, Q_PER_KV, HEAD_DIM)
    q3 = q3.transpose(0, 2, 1, 3, 4).reshape(nb, rows_q, HEAD_DIM)
    qi3 = seg(COL_QI, IDX_HEADS * IDX_DIM).reshape(nb, rows_i, IDX_DIM)
    w3 = seg(COL_SM + SM_WI, IDX_HEADS).reshape(nb, rows_i, 1)
    pad_rows = lambda a: jnp.pad(a.reshape(nb, t_new, -1), ((0, 0), (0, page - t_new), (0, 0)))
    kn3, vn3, kin3 = pad_rows(seg(COL_K, kvw)), pad_rows(seg(COL_V, kvw)), pad_rows(seg(COL_KI, IDX_DIM))
    row = np.arange(rows_q)
    row_tok = (row // Q_PER_KV) % t_new
    row_head = (row // (t_new * Q_PER_KV)) * Q_PER_KV + row % Q_PER_KV
    tab_t = jnp.pad(bias_table.T[row_head], ((0, 0), (0, LANES - N_BUCKETS)))
    kpos = np.concatenate([past - page + np.arange(page), past + np.arange(page)])[None, :]
    bucket = jnp.asarray(_t5_bucket_np(past + row_tok[:, None] - kpos))
    gsum = np.zeros((n_seq, n_rows, rows_i), np.float32)
    xpand = np.zeros((n_seq, rows_q, n_rows), np.float32)
    for s in range(n_seq):
        gsum[s, s * t_new + np.arange(rows_i) // IDX_HEADS, np.arange(rows_i)] = 1.0
        xpand[s, row, s * t_new + row_tok] = 1.0
    per_step = lambda shape: pl.BlockSpec((n_seq,) + shape, lambda b, pt: (b,) + tuple(0 for _ in shape))
    const = lambda shape: pl.BlockSpec(shape, lambda b, pt: tuple(0 for _ in shape))

    def page_specs(shape):
        return [pl.BlockSpec((1,) + shape, lambda b, pt, s=s, c=c: (pt[b * n_seq + s, c],) + tuple(0 for _ in shape))
                for s in range(n_seq) for c in range(n_pages)]

    grid_spec = pltpu.PrefetchScalarGridSpec(
        num_scalar_prefetch=1,
        grid=(nb // n_seq,),
        in_specs=[per_step((rows_q, HEAD_DIM)), per_step((rows_i, IDX_DIM)), per_step((rows_i, 1)),
                  per_step((page, kvw)), per_step((page, kvw)), per_step((page, IDX_DIM)),
                  const((rows_q, LANES)), const((rows_q, 2 * page)),
                  const((n_seq, n_rows, rows_i)), const((n_seq, rows_q, n_rows))]
                 + page_specs((N_KV_HEADS, HEAD_DIM, page)) + page_specs((N_KV_HEADS, HEAD_DIM, page))
                 + page_specs((IDX_DIM, page)),
        out_specs=pl.BlockSpec((n_seq, rows_q, HEAD_DIM), lambda b, pt: (b, 0, 0)),
        scratch_shapes=[
            pltpu.VMEM((n_rows, n_keys), F32),
            pltpu.VMEM((n_rows, n_keys), I32),
            pltpu.VMEM((n_rows, n_keys), BF16),
            pltpu.VMEM((rows_q, n_keys), F32),
            pltpu.VMEM((rows_q, n_keys), BF16),
            pltpu.VMEM((rows_q, 2 * page), F32),
            pltpu.VMEM((n_seq, IDX_DIM, past), F32),
            pltpu.VMEM((n_seq, N_KV_HEADS, HEAD_DIM, past), BF16),
            pltpu.VMEM((n_seq, N_KV_HEADS, HEAD_DIM, past), BF16),
        ],
    )
    n_pg = n_seq * n_pages
    out = pl.pallas_call(
        functools.partial(_attn_sample_body, n_seq=n_seq, n_pages=n_pages, page=page, t_new=t_new, topk=topk),
        grid_spec=grid_spec,
        out_shape=jax.ShapeDtypeStruct((nb, rows_q, HEAD_DIM), BF16),
        compiler_params=_cparams(("arbitrary",)),
        name="attn_sample",
    )(page_table, q3, qi3, w3, kn3, vn3, kin3, tab_t, bucket, jnp.asarray(gsum, BF16), jnp.asarray(xpand, BF16),
      *([ck_t] * n_pg), *([cv_t] * n_pg), *([cki_t] * n_pg))
    out = out.reshape(nb, N_KV_HEADS, t_new, Q_PER_KV, HEAD_DIM).transpose(0, 2, 1, 3, 4)
    return out.reshape(nb * t_new, N_HEADS * HEAD_DIM)


def _merge_body(ys_ref, ya_ref, g_ref, h_ref, wbs_ref, wba_ref, wo_ref, lg_ref, lb_ref, o_ref):
    a = _dot(ys_ref[...], wbs_ref[...])
    b = _dot(ya_ref[...], wba_ref[...])
    gates = jax.nn.sigmoid(g_ref[...])
    merged = gates[:, :D_MODEL] * a + gates[:, D_MODEL:] * b
    r = ALPHA * h_ref[...] + _dot(merged.astype(BF16), wo_ref[...])
    o_ref[...] = _layer_norm(r, lg_ref[...], lb_ref[...])


def merge_out(y_ssm, y_att, proj, h1, wbs, wba, wo, g, b, *, tm=512):
    m = h1.shape[0]
    tm = min(tm, m)
    assert m % tm == 0
    row = lambda width, colblk=0: pl.BlockSpec((tm, width), lambda i: (i, colblk))
    full = lambda shape: pl.BlockSpec(shape, lambda i: tuple(0 for _ in shape))
    return pl.pallas_call(
        _merge_body,
        grid=(m // tm,),
        in_specs=[row(D_INNER), row(N_HEADS * HEAD_DIM), row(2 * D_MODEL, COL_G // (2 * D_MODEL)), row(D_MODEL),
                  full(wbs.shape), full(wba.shape), full(wo.shape), full((1, D_MODEL)), full((1, D_MODEL))],
        out_specs=row(D_MODEL),
        out_shape=jax.ShapeDtypeStruct((m, D_MODEL), F32),
        compiler_params=_cparams(("parallel",)),
        name="merge_out",
    )(y_ssm, y_att, proj, h1, wbs, wba, wo, g, b)


def kernel(x_prompt, x_sample, cache_k, cache_v, cache_kidx, state_conv, state_ssm, page_table, bias_table, w_in, conv_w, conv_b, dt_bias, a_log, d_skip, ssm_norm_g, w_branch_ssm, w_branch_attn, w_out, ffn1_wi, ffn1_wo, ffn2_wi, ffn2_wo, ln1_g, ln1_b, ln2_g, ln2_b, ln3_g, ln3_b):
    assert w_in.shape[0] == DEPTH
    batch, seq, d = x_prompt.shape
    nb, t_new, _ = x_sample.shape
    xs = (x_prompt.reshape(batch * seq, d), x_sample.reshape(nb * t_new, d))
    outs_p, outs_s = [], []
    for i in range(DEPTH):
        w_pack = pack_w_in(w_in[i])
        f1i, f1o = ffn1_wi[i].astype(BF16), ffn1_wo[i].astype(BF16)
        f2i, f2o = ffn2_wi[i].astype(BF16), ffn2_wo[i].astype(BF16)
        wbs, wba, wo = w_branch_ssm[i].astype(BF16), w_branch_attn[i].astype(BF16), w_out[i].astype(BF16)
        row = lambda v: v.reshape(1, -1)
        pad_heads = lambda v: jnp.concatenate([v, jnp.zeros((LANES - SSM_HEADS,), v.dtype)]).reshape(1, LANES)
        dtb_pad, alog_pad = pad_heads(dt_bias[i]), pad_heads(a_log[i])
        dskip_e = jnp.repeat(d_skip[i], SSM_HEAD_DIM).reshape(1, D_INNER)
        ng = row(ssm_norm_g[i])
        cw, cb = conv_w[i], row(conv_b[i])

        h1, proj = [], []
        for x in xs:
            hf, hb = ffn_ln(x, f1i, f1o, row(ln1_g[i]), row(ln1_b[i]), emit_bf16=True)
            h1.append(hf)
            proj.append(matmul(hb, w_pack, tm=4096, tn=PROJ_TN))
        proj_p, proj_s = proj

        ys_p, conv_p, ssm_p = ssd_prompt(proj_p, batch, seq, cw, cb, dtb_pad, alog_pad, dskip_e, ng)
        ya_p, (kt_p, vt_p, kit_p) = attn_prompt(proj_p, bias_table, batch, seq)

        raw_s = jnp.concatenate([proj_s[:, COL_X:COL_X + D_INNER], proj_s[:, COL_BC:COL_BC + 2 * GN]], axis=1)
        u_cat = jnp.concatenate([state_conv[i], raw_s.reshape(nb, t_new, CONV_DIM)], axis=1)
        expand = np.zeros((LANES, D_INNER), np.float32)
        expand[np.arange(D_INNER) // SSM_HEAD_DIM, np.arange(D_INNER)] = 1.0
        ys_s, conv_s, ssm_s = ssd_sample(u_cat, proj_s.reshape(nb, t_new, PROJ_N),
                                         state_ssm[i].reshape(nb, SSM_HEADS * SSM_HEAD_DIM, SSM_STATE),
                                         cw, cb, dtb_pad, alog_pad, dskip_e, ng, jnp.asarray(expand, BF16))
        ya_s = attn_sample(proj_s, cache_k[i], cache_v[i], cache_kidx[i], page_table, bias_table, t_new)

        new_xs = []
        for hf, pj, ys, ya in ((h1[0], proj_p, ys_p, ya_p), (h1[1], proj_s, ys_s.reshape(nb * t_new, D_INNER), ya_s)):
            h2 = merge_out(ys, ya, pj, hf, wbs, wba, wo, row(ln2_g[i]), row(ln2_b[i]))
            new_xs.append(ffn_ln(h2, f2i, f2o, row(ln3_g[i]), row(ln3_b[i]), emit_bf16=False))
        xs = tuple(new_xs)

        kvw = N_KV_HEADS * HEAD_DIM
        shp = lambda nb_, l_: (nb_, l_, N_KV_HEADS, HEAD_DIM)
        head_major = lambda a: a.reshape(batch, N_KV_HEADS, HEAD_DIM, seq).transpose(0, 3, 1, 2)
        outs_p.append((head_major(kt_p), head_major(vt_p), kit_p.transpose(0, 2, 1),
                       conv_p, ssm_p.reshape(batch, SSM_HEADS, SSM_HEAD_DIM, SSM_STATE)))
        outs_s.append((proj_s[:, COL_K:COL_K + kvw].reshape(shp(nb, t_new)),
                       proj_s[:, COL_V:COL_V + kvw].reshape(shp(nb, t_new)),
                       proj_s[:, COL_KI:COL_KI + IDX_DIM].reshape(nb, t_new, IDX_DIM),
                       conv_s, ssm_s.reshape(nb, SSM_HEADS, SSM_HEAD_DIM, SSM_STATE)))
    k_p, v_p, kidx_p, conv_pp, ssm_pp = [jnp.stack(a) for a in zip(*outs_p)]
    k_s, v_s, kidx_s, conv_ss, ssm_ss = [jnp.stack(a) for a in zip(*outs_s)]
    return (xs[0].reshape(batch, seq, d), xs[1].reshape(nb, t_new, d),
            k_p, v_p, kidx_p, conv_pp, ssm_pp, k_s, v_s, kidx_s, conv_ss, ssm_ss)
```
